```python
import jax
import jax.numpy as jnp
from jax import lax
import numpy as np

D_MODEL = 1024
BATCH = 4
SEQ = 4096
DEPTH = 1

CTX_LEN = 256
GRID_W = 64
EPS = 1e-6

HG_HEADS = 4
HG_DK = 128
HG_DV = 128
HG_KEY_WIDTH = HG_HEADS * HG_DK
HG_WIDTH = HG_HEADS * HG_DV
HG_CHUNK = 16

CM_CHUNK = 128
CM_GROUPS = 4
CM_WIDTH = 512
CM_GROUP_DIM = CM_WIDTH // CM_GROUPS
ROWS_PER_CHUNK = CM_CHUNK // GRID_W

SPLIT_IDX = (HG_KEY_WIDTH, 2 * HG_KEY_WIDTH, 3 * HG_KEY_WIDTH, 3 * HG_KEY_WIDTH + HG_WIDTH,
             3 * HG_KEY_WIDTH + 2 * HG_WIDTH, 3 * HG_KEY_WIDTH + 2 * HG_WIDTH + 2 * CM_WIDTH)
D_IN = 3 * HG_KEY_WIDTH + 2 * HG_WIDTH + 2 * CM_WIDTH + 2 * D_MODEL

N_EXPERTS = 64
TOP_K = 8
N_EXPERT_GROUPS = 8
TOPK_GROUPS = 4
D_EXPERT = 256
D_SHARED = 256
ROUTED_SCALE = 2.5
MOE_BLOCK = 256

kernel_name = 'hybrid_hgrn2_chunkmlp_moe_prefix_dit'


def rms_norm(x, g):
    x32 = x.astype(jnp.float32)
    y = x32 * lax.rsqrt(jnp.mean(x32 * x32, axis=-1, keepdims=True) + EPS)
    return (y * g.astype(jnp.float32)).astype(x.dtype)


def layer_norm(x, g, b):
    x32 = x.astype(jnp.float32)
    xc = x32 - jnp.mean(x32, axis=-1, keepdims=True)
    y = xc * lax.rsqrt(jnp.mean(xc * xc, axis=-1, keepdims=True) + EPS)
    return (y * g.astype(jnp.float32) + b.astype(jnp.float32)).astype(x.dtype)


def hgrn2_gates(z, lb):
    B, T, _ = z.shape
    z = z.astype(jnp.float32).reshape(B, T, HG_HEADS, HG_DK)
    lb = lb.reshape(HG_HEADS, HG_DK)
    log_f = jnp.log(lb + (1.0 - lb) * jax.nn.sigmoid(z))
    k = (1.0 - lb) * jax.nn.sigmoid(-z)
    return k, log_f


def gla_chunk_scan(q, k, v, log_f, s0, readout):
    B, T, H, DK = k.shape
    DV = v.shape[-1]
    n = T // HG_CHUNK

    def to_chunks(a):
        return a.astype(jnp.float32).reshape(B, n, HG_CHUNK, H, a.shape[-1]).transpose(1, 0, 3, 2, 4)

    kc, vc = to_chunks(k), to_chunks(v)
    bc = jnp.cumsum(to_chunks(log_f), axis=3)
    tri = jnp.tril(jnp.ones((HG_CHUNK, HG_CHUNK), dtype=bool))
    mid = HG_CHUNK // 2 - 1

    def step(S, xs):
        k_, v_, b_ = xs[0], xs[1], xs[2]
        b_last = b_[:, :, -1:, :]
        S_new = jnp.exp(b_last[:, :, 0, :, None]) * S + jnp.einsum('bhsd,bhsv->bhdv', k_ * jnp.exp(b_last - b_), v_)
        if not readout:
            return S_new, None
        q_ = xs[3]
        b_mid = b_[:, :, mid:mid + 1, :]
        A = jnp.einsum('bhtd,bhsd->bhts', q_ * jnp.exp(b_ - b_mid), k_ * jnp.exp(b_mid - b_))
        A = jnp.where(tri, A, 0.0)
        o = jnp.einsum('bhts,bhsv->bhtv', A, v_) + jnp.einsum('bhtd,bhdv->bhtv', q_ * jnp.exp(b_), S)
        return S_new, o

    xs = (kc, vc, bc, to_chunks(q)) if readout else (kc, vc, bc)
    S_fin, o = lax.scan(step, s0, xs)
    if readout:
        o = o.transpose(1, 0, 3, 2, 4).reshape(B, T, H, DV)
    return o, S_fin


def chunk_spatial_gating(p, n_chunks, ln_g, ln_b, w_s, b_s):
    a = jax.nn.gelu(p)
    u, v = jnp.split(a, 2, axis=-1)
    v = layer_norm(v, ln_g, ln_b)
    B, T, _ = v.shape
    v = v.reshape(B, n_chunks, CM_CHUNK, CM_GROUPS, CM_GROUP_DIM)
    z = jnp.einsum('gpq,bnqgc->bnpgc', w_s, v) + b_s[None, None, :, :, None]
    return u * z.reshape(B, T, CM_WIDTH)


def token_mixer(h_lat, h_ctx, w_in, lb, g_out, ln_g, ln_b, w_s, b_s, w_a, w_b, w_o,
                n_lat_chunks, n_ctx_chunks, need_ctx):
    B, T, _ = h_lat.shape
    L = h_ctx.shape[1]
    p_lat = h_lat @ w_in
    p_ctx = h_ctx @ w_in
    ql, fl_fwd, fl_bwd, il, gl, cml, gtl = jnp.split(p_lat, SPLIT_IDX, axis=-1)
    qc, fc_fwd, fc_bwd, ic, gc, cmc, gtc = jnp.split(p_ctx, SPLIT_IDX, axis=-1)

    q_lat = jax.nn.silu(ql).reshape(B, T, HG_HEADS, HG_DK)
    v_lat = il.reshape(B, T, HG_HEADS, HG_DV)
    q_ctx = jax.nn.silu(qc).reshape(B, L, HG_HEADS, HG_DK) if need_ctx else None
    v_ctx = ic.reshape(B, L, HG_HEADS, HG_DV)
    s0 = jnp.zeros((B, HG_HEADS, HG_DK, HG_DV), jnp.float32)

    lat_outs, ctx_outs = [], []
    for d, (f_lat, f_ctx) in enumerate(((fl_fwd, fc_fwd), (fl_bwd, fc_bwd))):
        rev = (lambda a: a[:, ::-1]) if d == 1 else (lambda a: a)
        k_l, lf_l = hgrn2_gates(f_lat, lb[d])
        k_c, lf_c = hgrn2_gates(f_ctx, lb[d])
        q_c = rev(q_ctx) if need_ctx else None
        oc, s_ctx = gla_chunk_scan(q_c, rev(k_c), rev(v_ctx), rev(lf_c), s0, need_ctx)
        ol, _ = gla_chunk_scan(rev(q_lat), rev(k_l), rev(v_lat), rev(lf_l), s_ctx, True)
        lat_outs.append(rev(ol))
        if need_ctx:
            ctx_outs.append(rev(oc))

    def merge(o, g, cm, gt, n_chunks, dtype):
        n_tok = o.shape[1]
        g = g.reshape(B, n_tok, HG_HEADS, HG_DV).astype(jnp.float32)
        a = (rms_norm(o, g_out) * jax.nn.silu(g)).reshape(B, n_tok, HG_WIDTH).astype(dtype)
        bm = chunk_spatial_gating(cm, n_chunks, ln_g, ln_b, w_s, b_s)
        ga, gb = jnp.split(gt, 2, axis=-1)
        y = jax.nn.sigmoid(ga) * (a @ w_a) + jax.nn.sigmoid(gb) * (bm @ w_b)
        return y @ w_o

    y_lat = merge(lat_outs[0] + lat_outs[1], gl, cml, gtl, n_lat_chunks, h_lat.dtype)
    y_ctx = merge(ctx_outs[0] + ctx_outs[1], gc, cmc, gtc, n_ctx_chunks, h_ctx.dtype) if need_ctx else None
    return y_lat, y_ctx


def swiglu(h, w_gu, w_down):
    gate, up = jnp.split(h @ w_gu, 2, axis=-1)
    return (jax.nn.silu(gate) * up) @ w_down


def moe_ffn(h, w_router, b_router, w_gu, w_down, w_sh_gu, w_sh_down):
    N, D = h.shape
    scores = jax.nn.sigmoid((h @ w_router).astype(jnp.float32))
    sel = scores + b_router.astype(jnp.float32)
    grp = sel.reshape(N, N_EXPERT_GROUPS, N_EXPERTS // N_EXPERT_GROUPS)
    grp_score = jnp.sum(lax.top_k(grp, 2)[0], axis=-1)
    _, top_g = lax.top_k(grp_score, TOPK_GROUPS)
    gmask = jnp.any(top_g[:, :, None] == jnp.arange(N_EXPERT_GROUPS)[None, None, :], axis=1)
    sel = jnp.where(jnp.repeat(gmask, N_EXPERTS // N_EXPERT_GROUPS, axis=1), sel, -jnp.inf)
    _, top_e = lax.top_k(sel, TOP_K)
    w = jnp.take_along_axis(scores, top_e, axis=1)
    w = w / jnp.sum(w, axis=-1, keepdims=True) * ROUTED_SCALE

    M = N * TOP_K
    flat_e = top_e.reshape(M).astype(jnp.int32)
    flat_tok = jnp.repeat(jnp.arange(N, dtype=jnp.int32), TOP_K)
    flat_w = w.reshape(M)
    counts = jnp.bincount(flat_e, length=N_EXPERTS).astype(jnp.int32)
    padded = (counts + MOE_BLOCK - 1) // MOE_BLOCK * MOE_BLOCK
    pad_end = jnp.cumsum(padded)
    pad_start = pad_end - padded
    start = jnp.cumsum(counts) - counts
    order = jnp.argsort(flat_e)
    se, stok, sw = flat_e[order], flat_tok[order], flat_w[order]
    dest = pad_start[se] + jnp.arange(M, dtype=jnp.int32) - start[se]
    NB = -(-M // MOE_BLOCK) + N_EXPERTS
    row_tok = jnp.full((NB * MOE_BLOCK,), N, jnp.int32).at[dest].set(stok)
    row_w = jnp.zeros((NB * MOE_BLOCK,), jnp.float32).at[dest].set(sw)
    block_e = jnp.clip(jnp.searchsorted(pad_end, jnp.arange(NB, dtype=jnp.int32) * MOE_BLOCK, side='right'),
                       0, N_EXPERTS - 1)
    h_pad = jnp.concatenate([h, jnp.zeros((1, D), h.dtype)], axis=0)

    def expert_block(acc, blk):
        tok, wts, e = blk
        out = swiglu(h_pad[tok], w_gu[e], w_down[e])
        return acc.at[tok].add(out * wts[:, None].astype(out.dtype)), None

    acc, _ = lax.scan(expert_block, jnp.zeros((N + 1, D), h.dtype),
                      (row_tok.reshape(NB, MOE_BLOCK), row_w.reshape(NB, MOE_BLOCK), block_e))
    return acc[:N] + swiglu(h, w_sh_gu, w_sh_down)


def setup_inputs(seed: int = 0) -> dict:
    key = jax.random.key(seed)
    ks = jax.random.split(key, 26)

    def nrm(k, shape, scale):
        return jax.random.normal(k, shape, jnp.float32) * scale

    def gain(k, shape):
        return 1.0 + nrm(k, shape, 0.05)

    D = D_MODEL
    return {
        'x': nrm(ks[0], (BATCH, SEQ, D), 1.0),
        'c': nrm(ks[1], (BATCH, D), 1.0),
        'ctx': nrm(ks[2], (BATCH, CTX_LEN, D), 1.0),
        'c_ctx': nrm(ks[3], (D,), 1.0),
        'w_ada': nrm(ks[4], (DEPTH, D, 6 * D), 0.5 * D ** -0.5),
        'b_ada': nrm(ks[5], (DEPTH, 6 * D), 0.02),
        'g_pre_mix': gain(ks[6], (DEPTH, D)),
        'g_post_mix': gain(ks[7], (DEPTH, D)),
        'g_pre_ffn': gain(ks[8], (DEPTH, D)),
        'g_post_ffn': gain(ks[9], (DEPTH, D)),
        'w_in': nrm(ks[10], (DEPTH, D, D_IN), D ** -0.5),
        'lb_logits': nrm(ks[11], (2, DEPTH + 1, HG_KEY_WIDTH), 0.1),
        'g_hgrn_out': gain(ks[12], (DEPTH, HG_DV)),
        'cm_ln_g': gain(ks[13], (DEPTH, CM_WIDTH)),
        'cm_ln_b': nrm(ks[14], (DEPTH, CM_WIDTH), 0.02),
        'w_spatial': nrm(ks[15], (DEPTH, CM_GROUPS, CM_CHUNK, CM_CHUNK), CM_CHUNK ** -0.5),
        'b_spatial': 1.0 + nrm(ks[16], (DEPTH, CM_CHUNK, CM_GROUPS), 0.1),
        'w_branch_a': nrm(ks[17], (DEPTH, HG_WIDTH, D), HG_WIDTH ** -0.5),
        'w_branch_b': nrm(ks[18], (DEPTH, CM_WIDTH, D), CM_WIDTH ** -0.5),
        'w_out': nrm(ks[19], (DEPTH, D, D), D ** -0.5),
        'w_router': nrm(ks[20], (DEPTH, D, N_EXPERTS), D ** -0.5),
        'b_router': nrm(ks[21], (DEPTH, N_EXPERTS), 0.01),
        'w_expert_gu': nrm(ks[22], (DEPTH, N_EXPERTS, D, 2 * D_EXPERT), D ** -0.5),
        'w_expert_down': nrm(ks[23], (DEPTH, N_EXPERTS, D_EXPERT, D), D_EXPERT ** -0.5),
        'w_shared_gu': nrm(ks[24], (DEPTH, D, 2 * D_SHARED), D ** -0.5),
        'w_shared_down': nrm(ks[25], (DEPTH, D_SHARED, D), D_SHARED ** -0.5),
    }


def reference(x, c, ctx, c_ctx, w_ada, b_ada, g_pre_mix, g_post_mix, g_pre_ffn, g_post_ffn, w_in,
              lb_logits, g_hgrn_out, cm_ln_g, cm_ln_b, w_spatial, b_spatial, w_branch_a, w_branch_b,
              w_out, w_router, b_router, w_expert_gu, w_expert_down, w_shared_gu, w_shared_down):
    B, T, D = x.shape
    L = ctx.shape[1]
    rows = T // GRID_W
    n_lat_chunks = rows // ROWS_PER_CHUNK
    n_ctx_chunks = L // CM_CHUNK
    lower_bounds = jnp.cumsum(jax.nn.softmax(lb_logits.astype(jnp.float32), axis=1), axis=1)
    silu_c = jax.nn.silu(c)
    silu_cc = jax.nn.silu(c_ctx)
    x_lat, x_ctx = x, ctx
    for l in range(DEPTH):
        need_ctx = l < DEPTH - 1
        mod_lat = (silu_c @ w_ada[l] + b_ada[l])[:, None, :]
        mod_ctx = silu_cc @ w_ada[l] + b_ada[l]
        sh1, sc1, gt1, sh2, sc2, gt2 = jnp.split(mod_lat, 6, axis=-1)
        csh1, csc1, cgt1, csh2, csc2, cgt2 = jnp.split(mod_ctx, 6, axis=-1)

        h_lat = rms_norm(x_lat, g_pre_mix[l]) * (1.0 + sc1) + sh1
        h_ctx = rms_norm(x_ctx, g_pre_mix[l]) * (1.0 + csc1) + csh1
        y_lat, y_ctx = token_mixer(h_lat, h_ctx, w_in[l], lower_bounds[:, l], g_hgrn_out[l], cm_ln_g[l],
                                   cm_ln_b[l], w_spatial[l], b_spatial[l], w_branch_a[l], w_branch_b[l],
                                   w_out[l], n_lat_chunks, n_ctx_chunks, need_ctx)
        x_lat = x_lat + gt1 * rms_norm(y_lat, g_post_mix[l])
        if need_ctx:
            x_ctx = x_ctx + cgt1 * rms_norm(y_ctx, g_post_mix[l])

        h2_lat = (rms_norm(x_lat, g_pre_ffn[l]) * (1.0 + sc2) + sh2).reshape(B * T, D)
        if need_ctx:
            h2_ctx = (rms_norm(x_ctx, g_pre_ffn[l]) * (1.0 + csc2) + csh2).reshape(B * L, D)
            f_all = moe_ffn(jnp.concatenate([h2_lat, h2_ctx], axis=0), w_router[l], b_router[l],
                            w_expert_gu[l], w_expert_down[l], w_shared_gu[l], w_shared_down[l])
            f_lat, f_ctx = f_all[:B * T], f_all[B * T:]
            x_ctx = x_ctx + cgt2 * rms_norm(f_ctx.reshape(B, L, D), g_post_ffn[l])
        else:
            f_lat = moe_ffn(h2_lat, w_router[l], b_router[l], w_expert_gu[l], w_expert_down[l],
                            w_shared_gu[l], w_shared_down[l])
        x_lat = x_lat + gt2 * rms_norm(f_lat.reshape(B, T, D), g_post_ffn[l])
    return x_lat
```

```python
import functools

import jax
import jax.numpy as jnp
from jax import lax
from jax.experimental import pallas as pl
from jax.experimental.pallas import tpu as pltpu

F32 = jnp.float32
BF16 = jnp.bfloat16

D_MODEL = 1024
EPS = 1e-6
HG_HEADS = 4
HG_DK = 128
HG_W = HG_HEADS * HG_DK
CM_W = 512
CM_CHUNK = 128
CM_GROUPS = 4
D_IN = 5 * HG_W + 2 * CM_W + 2 * D_MODEL
N_EXPERTS = 64
TOP_K = 8
N_GROUPS = 8
GROUP_SIZE = N_EXPERTS // N_GROUPS
TOPK_GROUPS = 4
D_EXPERT = 256
ROUTED_SCALE = 2.5
SCAN_CHUNK = 128
SUB = 16
VMEM_LIMIT = 56 * 1024 * 1024


def _params(sem):
    return pltpu.CompilerParams(dimension_semantics=sem, vmem_limit_bytes=VMEM_LIMIT)


def _dot(a, b):
    return jnp.dot(a, b, preferred_element_type=F32)


def _dot_nt(a, b):
    return lax.dot_general(a, b, (((1,), (1,)), ((), ())), preferred_element_type=F32)


def _dot_tn(a, b):
    return lax.dot_general(a, b, (((0,), (0,)), ((), ())), preferred_element_type=F32)


def _rms(x, g):
    return x * lax.rsqrt(jnp.mean(x * x, axis=-1, keepdims=True) + EPS) * g


def _ada_kernel(c_ref, w_ref, b_ref, o_ref):
    c = c_ref[...]
    s = c * jax.nn.sigmoid(c)
    o_ref[...] = _dot(s.astype(BF16), w_ref[...].astype(BF16)) + b_ref[...]


def _ada_mod(cs, w_ada, b_ada):
    rows = cs.shape[0]
    n_out = w_ada.shape[1]
    return pl.pallas_call(
        _ada_kernel,
        grid=(n_out // D_MODEL,),
        in_specs=[
            pl.BlockSpec((rows, D_MODEL), lambda j: (0, 0)),
            pl.BlockSpec((D_MODEL, D_MODEL), lambda j: (0, j)),
            pl.BlockSpec((1, D_MODEL), lambda j: (0, j)),
        ],
        out_specs=pl.BlockSpec((rows, D_MODEL), lambda j: (0, j)),
        out_shape=jax.ShapeDtypeStruct((rows, n_out), F32),
        compiler_params=_params(("parallel",)),
        name="ada_mod",
    )(cs, w_ada, b_ada)


def _lower_bounds(lbl):
    out = []
    for d in range(2):
        l0, l1 = lbl[2 * d:2 * d + 1], lbl[2 * d + 1:2 * d + 2]
        m = jnp.maximum(l0, l1)
        e0, e1 = jnp.exp(l0 - m), jnp.exp(l1 - m)
        out.append(e0 / (e0 + e1))
    return out


def _prenorm(x_ref, sh_ref, sc_ref, g_ref):
    return (_rms(x_ref[...], g_ref[...]) * (1.0 + sc_ref[...]) + sh_ref[...]).astype(BF16)


def _gates(z, lb, k_ref, lf_ref, d):
    k_ref[d] = ((1.0 - lb) * jax.nn.sigmoid(-z)).astype(k_ref.dtype)
    lf_ref[d] = jnp.log(lb + (1.0 - lb) * jax.nn.sigmoid(z))


def _proj_lat_kernel(x_ref, sh_ref, sc_ref, g_ref, w_ref, lbl_ref, lng_ref, lnb_ref,
                     q_ref, k_ref, lf_ref, v_ref, sg_ref, u_ref, vn_ref, sga_ref, sgb_ref):
    hb = _prenorm(x_ref, sh_ref, sc_ref, g_ref)
    lbs = _lower_bounds(lbl_ref[...])

    def mm(lo, width):
        return _dot(hb, w_ref[:, lo:lo + width])

    z = mm(0, HG_W)
    q_ref[...] = (z * jax.nn.sigmoid(z)).astype(q_ref.dtype)
    for d in range(2):
        _gates(mm((1 + d) * HG_W, HG_W), lbs[d], k_ref, lf_ref, d)
    v_ref[...] = mm(3 * HG_W, HG_W).astype(v_ref.dtype)
    z = mm(4 * HG_W, HG_W)
    sg_ref[...] = (z * jax.nn.sigmoid(z)).astype(sg_ref.dtype)
    u_ref[...] = jax.nn.gelu(mm(5 * HG_W, CM_W)).astype(u_ref.dtype)
    vv = jax.nn.gelu(mm(5 * HG_W + CM_W, CM_W))
    vc = vv - jnp.mean(vv, axis=-1, keepdims=True)
    vn = vc * lax.rsqrt(jnp.mean(vc * vc, axis=-1, keepdims=True) + EPS)
    vn_ref[...] = (vn * lng_ref[...] + lnb_ref[...]).astype(vn_ref.dtype)
    base = 5 * HG_W + 2 * CM_W
    sga_ref[...] = jax.nn.sigmoid(mm(base, D_MODEL)).astype(sga_ref.dtype)
    sgb_ref[...] = jax.nn.sigmoid(mm(base + D_MODEL, D_MODEL)).astype(sgb_ref.dtype)


def _mod_spec(rows_per_batch_tiles, col):
    return pl.BlockSpec((None, 1, D_MODEL), lambda i: (i // rows_per_batch_tiles, 0, col))


def _proj_lat(x2, mod3, g_pre, w_in_b, lbl, ln_g, ln_b, seq, tm):
    n = x2.shape[0]
    tpb = seq // tm
    row = lambda w: pl.BlockSpec((tm, w), lambda i: (i, 0))
    row2 = pl.BlockSpec((2, tm, HG_W), lambda i: (0, i, 0))
    full = lambda a: pl.BlockSpec(a.shape, lambda i: (0,) * a.ndim)
    outs = [
        (row(HG_W), jax.ShapeDtypeStruct((n, HG_W), BF16)),
        (row2, jax.ShapeDtypeStruct((2, n, HG_W), BF16)),
        (row2, jax.ShapeDtypeStruct((2, n, HG_W), F32)),
        (row(HG_W), jax.ShapeDtypeStruct((n, HG_W), BF16)),
        (row(HG_W), jax.ShapeDtypeStruct((n, HG_W), BF16)),
        (row(CM_W), jax.ShapeDtypeStruct((n, CM_W), BF16)),
        (row(CM_W), jax.ShapeDtypeStruct((n, CM_W), BF16)),
        (row(D_MODEL), jax.ShapeDtypeStruct((n, D_MODEL), BF16)),
        (row(D_MODEL), jax.ShapeDtypeStruct((n, D_MODEL), BF16)),
    ]
    return pl.pallas_call(
        _proj_lat_kernel,
        grid=(n // tm,),
        in_specs=[row(D_MODEL), _mod_spec(tpb, 0), _mod_spec(tpb, 1), full(g_pre), full(w_in_b),
                  full(lbl), full(ln_g), full(ln_b)],
        out_specs=[o[0] for o in outs],
        out_shape=[o[1] for o in outs],
        compiler_params=_params(("parallel",)),
        name="proj_lat",
    )(x2, mod3, mod3, g_pre, w_in_b, lbl, ln_g, ln_b)


def _proj_ctx_kernel(x_ref, sh_ref, sc_ref, g_ref, w_ref, lbl_ref, k_ref, lf_ref, v_ref):
    hb = _prenorm(x_ref, sh_ref, sc_ref, g_ref)
    lbs = _lower_bounds(lbl_ref[...])
    for d in range(2):
        _gates(_dot(hb, w_ref[:, d * HG_W:(d + 1) * HG_W]), lbs[d], k_ref, lf_ref, d)
    v_ref[...] = _dot(hb, w_ref[:, 2 * HG_W:3 * HG_W]).astype(v_ref.dtype)


def _proj_ctx(c2, mod3, ctx_row, g_pre, w_ctx_b, lbl, tm):
    n = c2.shape[0]
    row = lambda w: pl.BlockSpec((tm, w), lambda i: (i, 0))
    row2 = pl.BlockSpec((2, tm, HG_W), lambda i: (0, i, 0))
    full = lambda a: pl.BlockSpec(a.shape, lambda i: (0,) * a.ndim)
    mod = lambda col: pl.BlockSpec((None, 1, D_MODEL), lambda i: (ctx_row, 0, col))
    return pl.pallas_call(
        _proj_ctx_kernel,
        grid=(n // tm,),
        in_specs=[row(D_MODEL), mod(0), mod(1), full(g_pre), full(w_ctx_b), full(lbl)],
        out_specs=[row2, row2, row(HG_W)],
        out_shape=[jax.ShapeDtypeStruct((2, n, HG_W), BF16), jax.ShapeDtypeStruct((2, n, HG_W), F32),
                   jax.ShapeDtypeStruct((n, HG_W), BF16)],
        compiler_params=_params(("parallel",)),
        name="proj_ctx",
    )(c2, mod3, mod3, g_pre, w_ctx_b, lbl)


def _scan_step(d, k_ref, lf_ref, v_ref, st_ref, b_scr, q_ref=None, o_ref=None):
    C = SCAN_CHUNK
    t_i = lax.broadcasted_iota(jnp.int32, (C, C), 0)
    s_i = lax.broadcasted_iota(jnp.int32, (C, C), 1)
    pt = jnp.where(d == 0, t_i, C - 1 - t_i)
    ps = jnp.where(d == 0, s_i, C - 1 - s_i)

    lf = lf_ref[...]
    hi = lf.astype(BF16)
    r1 = lf - hi.astype(F32)
    mid = r1.astype(BF16)
    lo = (r1 - mid.astype(F32)).astype(BF16)
    lmat = (ps <= pt).astype(BF16)
    b_scr[...] = _dot(jnp.concatenate([lmat, lmat, lmat], axis=1), jnp.concatenate([hi, mid, lo], axis=0))
    b = b_scr[...]

    def row(i):
        return b_scr[pl.ds(i, 1), :]

    b_last = row(jnp.where(d == 0, C - 1, 0))
    k = k_ref[...].astype(F32)
    v = v_ref[...]
    khat = (k * jnp.exp(b_last - b)).astype(BF16)
    decay = jnp.exp(b_last)

    if q_ref is not None:
        q = q_ref[...].astype(F32)
        qhat = (q * jnp.exp(b)).astype(BF16)
        levels = []
        half = C // 2
        while half >= SUB:
            span = 2 * half
            e = jnp.concatenate([b[m * span:(m + 1) * span] - row(m * span + half - 1 + d)
                                 for m in range(C // span)], axis=0)
            w = jnp.exp(-jnp.abs(e))
            mask = (pt // span == ps // span) & ((pt // half) % 2 == 1) & ((ps // half) % 2 == 0)
            levels.append((w, w, mask))
            half //= 2
        e0 = jnp.concatenate([b[m * SUB:(m + 1) * SUB] - row(m * SUB + SUB // 2 - 1 + d)
                              for m in range(C // SUB)], axis=0)
        levels.append((jnp.exp(e0), jnp.exp(-e0), (pt // SUB == ps // SUB) & (ps <= pt)))

    for h in range(HG_HEADS):
        sl = slice(h * HG_DK, (h + 1) * HG_DK)
        st = st_ref[:, sl]
        if q_ref is not None:
            a = jnp.zeros((C, C), F32)
            for wq, wk, mask in levels:
                al = _dot_nt((q[:, sl] * wq[:, sl]).astype(BF16), (k[:, sl] * wk[:, sl]).astype(BF16))
                a = a + jnp.where(mask, al, 0.0)
            o = _dot(a.astype(BF16), v[:, sl]) + _dot_nt(qhat[:, sl], st.astype(BF16))
            o_ref[:, sl] = o.astype(o_ref.dtype)
        st_ref[:, sl] = decay[:, sl] * st + _dot_tn(v[:, sl], khat[:, sl])


def _scan_kernel(n_ctx_steps, q_ref, k_ref, lf_ref, v_ref, kc_ref, lfc_ref, vc_ref, o_ref, st_ref, b_scr):
    d = pl.program_id(1)
    s = pl.program_id(2)

    @pl.when(s == 0)
    def _():
        st_ref[...] = jnp.zeros_like(st_ref)

    @pl.when(s < n_ctx_steps)
    def _():
        _scan_step(d, kc_ref, lfc_ref, vc_ref, st_ref, b_scr)

    @pl.when(s >= n_ctx_steps)
    def _():
        _scan_step(d, k_ref, lf_ref, v_ref, st_ref, b_scr, q_ref, o_ref)


def _hgrn_scan(q, k2, lf2, v, kc2, lfc2, vc, batch, seq, ctx_len):
    C = SCAN_CHUNK
    n_lat, n_ctx = seq // C, ctx_len // C

    def lat_blk(b, d, s):
        j = jnp.maximum(s - n_ctx, 0)
        return b * n_lat + jnp.where(d == 0, j, n_lat - 1 - j)

    def ctx_blk(b, d, s):
        i = jnp.minimum(s, n_ctx - 1)
        return b * n_ctx + jnp.where(d == 0, i, n_ctx - 1 - i)

    lat = pl.BlockSpec((C, HG_W), lambda b, d, s: (lat_blk(b, d, s), 0))
    lat_d = pl.BlockSpec((None, C, HG_W), lambda b, d, s: (d, lat_blk(b, d, s), 0))
    ctx = pl.BlockSpec((C, HG_W), lambda b, d, s: (ctx_blk(b, d, s), 0))
    ctx_d = pl.BlockSpec((None, C, HG_W), lambda b, d, s: (d, ctx_blk(b, d, s), 0))
    return pl.pallas_call(
        functools.partial(_scan_kernel, n_ctx),
        grid=(batch, 2, n_ctx + n_lat),
        in_specs=[lat, lat_d, lat_d, lat, ctx_d, ctx_d, ctx],
        out_specs=lat_d,
        out_shape=jax.ShapeDtypeStruct((2, batch * seq, HG_W), BF16),
        scratch_shapes=[pltpu.VMEM((HG_DK, HG_W), F32), pltpu.VMEM((C, HG_W), F32)],
        compiler_params=_params(("parallel", "parallel", "arbitrary")),
        name="hgrn_scan",
    )(q, k2, lf2, v, kc2, lfc2, vc)


def _merge_kernel(o_ref, sg_ref, u_ref, vn_ref, sga_ref, sgb_ref, x_ref, gt1_ref, sh2_ref, sc2_ref,
                  gout_ref, ws_ref, bs_ref, wa_ref, wb_ref, wo_ref, gpost_ref, gffn_ref, wr_ref,
                  x1_ref, h2_ref, lg_ref):
    tm = x_ref.shape[0]
    o = o_ref[0].astype(F32) + o_ref[1].astype(F32)
    sg = sg_ref[...].astype(F32)
    gout = gout_ref[...]
    a = jnp.concatenate(
        [_rms(o[:, h * HG_DK:(h + 1) * HG_DK], gout) * sg[:, h * HG_DK:(h + 1) * HG_DK] for h in range(HG_HEADS)],
        axis=1).astype(BF16)
    vn = vn_ref[...]
    gw = CM_W // CM_GROUPS
    z = jnp.concatenate(
        [jnp.concatenate([_dot(ws_ref[g], vn[c * CM_CHUNK:(c + 1) * CM_CHUNK, g * gw:(g + 1) * gw])
                          for g in range(CM_GROUPS)], axis=1) + bs_ref[...]
         for c in range(tm // CM_CHUNK)], axis=0)
    bm = (u_ref[...].astype(F32) * z).astype(BF16)
    y = sga_ref[...].astype(F32) * _dot(a, wa_ref[...]) + sgb_ref[...].astype(F32) * _dot(bm, wb_ref[...])
    yo = _dot(y.astype(BF16), wo_ref[...])
    x1 = x_ref[...] + gt1_ref[...] * _rms(yo, gpost_ref[...])
    x1_ref[...] = x1
    h2 = (_rms(x1, gffn_ref[...]) * (1.0 + sc2_ref[...]) + sh2_ref[...]).astype(BF16)
    h2_ref[...] = h2
    lg_ref[...] = _dot(h2, wr_ref[...])


def _merge(o2, sg, u, vn, sga, sgb, x2, mod3, g_out, ws_b, bs_full, wa_b, wb_b, wo_b, g_post, g_ffn, wr_b,
           seq, tm):
    n = x2.shape[0]
    tpb = seq // tm
    row = lambda w: pl.BlockSpec((tm, w), lambda i: (i, 0))
    full = lambda a: pl.BlockSpec(a.shape, lambda i: (0,) * a.ndim)
    return pl.pallas_call(
        _merge_kernel,
        grid=(n // tm,),
        in_specs=[pl.BlockSpec((2, tm, HG_W), lambda i: (0, i, 0)), row(HG_W), row(CM_W), row(CM_W),
                  row(D_MODEL), row(D_MODEL), row(D_MODEL), _mod_spec(tpb, 2), _mod_spec(tpb, 3),
                  _mod_spec(tpb, 4), full(g_out), full(ws_b), full(bs_full), full(wa_b), full(wb_b),
                  full(wo_b), full(g_post), full(g_ffn), full(wr_b)],
        out_specs=[row(D_MODEL), row(D_MODEL), row(N_EXPERTS)],
        out_shape=[jax.ShapeDtypeStruct((n, D_MODEL), F32), jax.ShapeDtypeStruct((n, D_MODEL), BF16),
                   jax.ShapeDtypeStruct((n, N_EXPERTS), F32)],
        compiler_params=_params(("parallel",)),
        name="merge",
    )(o2, sg, u, vn, sga, sgb, x2, mod3, mod3, mod3, g_out, ws_b, bs_full, wa_b, wb_b, wo_b, g_post, g_ffn,
      wr_b)


def _router_kernel(lg_ref, br_ref, w_ref):
    scores = jax.nn.sigmoid(lg_ref[...])
    sel = scores + br_ref[...]
    lane = lax.broadcasted_iota(jnp.int32, sel.shape, 1)
    grp = lane // GROUP_SIZE
    neg = -jnp.inf

    def first_max(x, ids, sentinel):
        m = jnp.max(x, axis=1, keepdims=True)
        return m, jnp.min(jnp.where(x == m, ids, sentinel), axis=1, keepdims=True)

    gscore = jnp.zeros_like(sel)
    for g in range(N_GROUPS):
        x = jnp.where(grp == g, sel, neg)
        m1, i1 = first_max(x, lane, N_EXPERTS)
        m2 = jnp.max(jnp.where(lane == i1, neg, x), axis=1, keepdims=True)
        gscore = jnp.where(grp == g, m1 + m2, gscore)
    gmask = jnp.zeros(sel.shape, jnp.bool_)
    for _ in range(TOPK_GROUPS):
        _, gi = first_max(gscore, grp, N_GROUPS)
        gmask = gmask | (grp == gi)
        gscore = jnp.where(grp == gi, neg, gscore)
    x = jnp.where(gmask, sel, neg)
    chosen = jnp.zeros(sel.shape, jnp.bool_)
    for _ in range(TOP_K):
        _, ei = first_max(x, lane, N_EXPERTS)
        chosen = chosen | (lane == ei)
        x = jnp.where(lane == ei, neg, x)
    w = jnp.where(chosen, scores, 0.0)
    w_ref[...] = w / jnp.sum(w, axis=1, keepdims=True) * ROUTED_SCALE


def _router(logits, b_router, tm):
    n = logits.shape[0]
    return pl.pallas_call(
        _router_kernel,
        grid=(n // tm,),
        in_specs=[pl.BlockSpec((tm, N_EXPERTS), lambda i: (i, 0)), pl.BlockSpec((1, N_EXPERTS), lambda i: (0, 0))],
        out_specs=pl.BlockSpec((tm, N_EXPERTS), lambda i: (i, 0)),
        out_shape=jax.ShapeDtypeStruct((n, N_EXPERTS), F32),
        compiler_params=_params(("parallel",)),
        name="router",
    )(logits, b_router)


def _swiglu_act(h, w_gu):
    gu = _dot(h, w_gu)
    de = gu.shape[1] // 2
    g = gu[:, :de]
    return g * jax.nn.sigmoid(g) * gu[:, de:]


def _moe_kernel(h_ref, cw_ref, x1_ref, gt2_ref, gpost_ref, wgu_ref, wdn_ref, wsgu_ref, wsdn_ref, o_ref, acc_ref):
    e = pl.program_id(1)
    h = h_ref[...]

    @pl.when(e == 0)
    def _():
        acc_ref[...] = _dot(_swiglu_act(h, wsgu_ref[...]).astype(BF16), wsdn_ref[...])

    cw = cw_ref[...]
    lane = lax.broadcasted_iota(jnp.int32, cw.shape, 1)
    w_e = jnp.sum(jnp.where(lane == e, cw, 0.0), axis=1, keepdims=True)
    act = _swiglu_act(h, wgu_ref[...])
    act = jnp.where(w_e > 0.0, act * w_e, 0.0)
    acc_ref[...] += _dot(act.astype(BF16), wdn_ref[...])

    @pl.when(e == pl.num_programs(1) - 1)
    def _():
        o_ref[...] = x1_ref[...] + gt2_ref[...] * _rms(acc_ref[...], gpost_ref[...])


def _moe(h2, cw, x1, mod3, g_post, wgu_b, wdn_b, wsgu_b, wsdn_b, seq, tm):
    n = h2.shape[0]
    tpb = seq // tm
    row = lambda w: pl.BlockSpec((tm, w), lambda i, e: (i, 0))
    full = lambda a: pl.BlockSpec(a.shape, lambda i, e: (0,) * a.ndim)
    return pl.pallas_call(
        _moe_kernel,
        grid=(n // tm, N_EXPERTS),
        in_specs=[row(D_MODEL), row(N_EXPERTS), row(D_MODEL),
                  pl.BlockSpec((None, 1, D_MODEL), lambda i, e: (i // tpb, 0, 5)), full(g_post),
                  pl.BlockSpec((None, D_MODEL, 2 * D_EXPERT), lambda i, e: (e, 0, 0)),
                  pl.BlockSpec((None, D_EXPERT, D_MODEL), lambda i, e: (e, 0, 0)),
                  full(wsgu_b), full(wsdn_b)],
        out_specs=row(D_MODEL),
        out_shape=jax.ShapeDtypeStruct((n, D_MODEL), F32),
        scratch_shapes=[pltpu.VMEM((tm, D_MODEL), F32)],
        compiler_params=_params(("parallel", "arbitrary")),
        name="moe",
    )(h2, cw, x1, mod3, g_post, wgu_b, wdn_b, wsgu_b, wsdn_b)


def _tile(n, pref):
    t = pref
    while n % t:
        t //= 2
    return t


def kernel(x, c, ctx, c_ctx, w_ada, b_ada, g_pre_mix, g_post_mix, g_pre_ffn, g_post_ffn, w_in, lb_logits, g_hgrn_out, cm_ln_g, cm_ln_b, w_spatial, b_spatial, w_branch_a, w_branch_b, w_out, w_router, b_router, w_expert_gu, w_expert_down, w_shared_gu, w_shared_down):
    B, T, D = x.shape
    L = ctx.shape[1]
    assert D == D_MODEL and w_ada.shape[0] == 1 and T % SCAN_CHUNK == 0 and L % SCAN_CHUNK == 0
    l = 0
    row = lambda a: a[l].reshape(1, -1)

    n_rows = -(-(B + 1) // 16) * 16
    cs = jnp.zeros((n_rows, D), F32).at[:B].set(c).at[B].set(c_ctx)
    mod3 = _ada_mod(cs, w_ada[l], row(b_ada)).reshape(n_rows, 1, 6 * D)

    w_in_b = w_in[l].astype(BF16)
    lbl = lb_logits[:, l:l + 2].reshape(4, HG_W)
    x2 = x.reshape(B * T, D)
    q, k2, lf2, v, sg, u, vn, sga, sgb = _proj_lat(
        x2, mod3, row(g_pre_mix), w_in_b, lbl, row(cm_ln_g), row(cm_ln_b), T, _tile(T, 256))
    kc2, lfc2, vc = _proj_ctx(ctx.reshape(B * L, D), mod3, B, row(g_pre_mix), w_in_b[:, HG_W:4 * HG_W], lbl,
                              _tile(B * L, 256))

    o2 = _hgrn_scan(q, k2, lf2, v, kc2, lfc2, vc, B, T, L)

    bs_full = jnp.repeat(b_spatial[l], CM_W // CM_GROUPS, axis=1)
    x1, h2, logits = _merge(
        o2, sg, u, vn, sga, sgb, x2, mod3, row(g_hgrn_out), w_spatial[l].astype(BF16), bs_full,
        w_branch_a[l].astype(BF16), w_branch_b[l].astype(BF16), w_out[l].astype(BF16), row(g_post_mix),
        row(g_pre_ffn), w_router[l].astype(BF16), T, _tile(T, 512))

    cw = _router(logits, row(b_router), _tile(B * T, 512))
    out = _moe(h2, cw, x1, mod3, row(g_post_ffn), w_expert_gu[l].astype(BF16), w_expert_down[l].astype(BF16),
               w_shared_gu[l].astype(BF16), w_shared_down[l].astype(BF16), T, _tile(T, 1024))
    return out.reshape(B, T, D)
```

```python
import functools

import numpy as np
import jax
import jax.numpy as jnp
from jax import lax
from jax.experimental import pallas as pl
from jax.experimental.pallas import tpu as pltpu

F32 = jnp.float32
BF16 = jnp.bfloat16

D_MODEL = 1024
EPS = 1e-6
HG_HEADS = 4
HG_DK = 128
HG_W = HG_HEADS * HG_DK
CM_W = 512
CM_CHUNK = 128
CM_GROUPS = 4
D_IN = 5 * HG_W + 2 * CM_W + 2 * D_MODEL
N_EXPERTS = 64
TOP_K = 8
N_GROUPS = 8
GROUP_BITS = 3
TOPK_GROUPS = 4
D_EXPERT = 256
ROUTED_SCALE = 2.5
SCAN_CHUNK = 128
SUB = 16
VMEM_LIMIT = 56 * 1024 * 1024


def _params(sem):
    return pltpu.CompilerParams(dimension_semantics=sem, vmem_limit_bytes=VMEM_LIMIT)


def _dot(a, b):
    return jnp.dot(a, b, preferred_element_type=F32)


def _dot_nt(a, b):
    return lax.dot_general(a, b, (((1,), (1,)), ((), ())), preferred_element_type=F32)


def _dot_tn(a, b):
    return lax.dot_general(a, b, (((0,), (0,)), ((), ())), preferred_element_type=F32)


def _rms(x, g):
    return x * lax.rsqrt(jnp.mean(x * x, axis=-1, keepdims=True) + EPS) * g


def _ada_kernel(c_ref, w_ref, b_ref, o_ref):
    c = c_ref[...]
    s = c * jax.nn.sigmoid(c)
    o_ref[...] = _dot(s.astype(BF16), w_ref[...].astype(BF16)) + b_ref[...]


def _ada_mod(cs, w_ada, b_ada):
    rows = cs.shape[0]
    n_out = w_ada.shape[1]
    return pl.pallas_call(
        _ada_kernel,
        grid=(n_out // D_MODEL,),
        in_specs=[
            pl.BlockSpec((rows, D_MODEL), lambda j: (0, 0)),
            pl.BlockSpec((D_MODEL, D_MODEL), lambda j: (0, j)),
            pl.BlockSpec((1, D_MODEL), lambda j: (0, j)),
        ],
        out_specs=pl.BlockSpec((rows, D_MODEL), lambda j: (0, j)),
        out_shape=jax.ShapeDtypeStruct((rows, n_out), F32),
        compiler_params=_params(("parallel",)),
        name="ada_mod",
    )(cs, w_ada, b_ada)


def _lower_bounds(lbl):
    out = []
    for d in range(2):
        l0, l1 = lbl[2 * d:2 * d + 1], lbl[2 * d + 1:2 * d + 2]
        m = jnp.maximum(l0, l1)
        e0, e1 = jnp.exp(l0 - m), jnp.exp(l1 - m)
        out.append(e0 / (e0 + e1))
    return out


def _prenorm(x_ref, sh_ref, sc_ref, g_ref):
    return (_rms(x_ref[...], g_ref[...]) * (1.0 + sc_ref[...]) + sh_ref[...]).astype(BF16)


def _gates(z, lb, k_ref, lf_ref, d):
    k_ref[d] = ((1.0 - lb) * jax.nn.sigmoid(-z)).astype(k_ref.dtype)
    lf_ref[d] = jnp.log(lb + (1.0 - lb) * jax.nn.sigmoid(z))


def _proj_lat_kernel(x_ref, sh_ref, sc_ref, g_ref, w_ref, lbl_ref, lng_ref, lnb_ref,
                     q_ref, k_ref, lf_ref, v_ref, sg_ref, u_ref, vn_ref, sga_ref, sgb_ref):
    hb = _prenorm(x_ref, sh_ref, sc_ref, g_ref)
    lbs = _lower_bounds(lbl_ref[...])

    def mm(lo, width):
        return _dot(hb, w_ref[:, lo:lo + width])

    z = mm(0, HG_W)
    q_ref[...] = (z * jax.nn.sigmoid(z)).astype(q_ref.dtype)
    for d in range(2):
        _gates(mm((1 + d) * HG_W, HG_W), lbs[d], k_ref, lf_ref, d)
    v_ref[...] = mm(3 * HG_W, HG_W).astype(v_ref.dtype)
    z = mm(4 * HG_W, HG_W)
    sg_ref[...] = (z * jax.nn.sigmoid(z)).astype(sg_ref.dtype)
    u_ref[...] = jax.nn.gelu(mm(5 * HG_W, CM_W)).astype(u_ref.dtype)
    vv = jax.nn.gelu(mm(5 * HG_W + CM_W, CM_W))
    vc = vv - jnp.mean(vv, axis=-1, keepdims=True)
    vn = vc * lax.rsqrt(jnp.mean(vc * vc, axis=-1, keepdims=True) + EPS)
    vn_ref[...] = (vn * lng_ref[...] + lnb_ref[...]).astype(vn_ref.dtype)
    base = 5 * HG_W + 2 * CM_W
    sga_ref[...] = jax.nn.sigmoid(mm(base, D_MODEL)).astype(sga_ref.dtype)
    sgb_ref[...] = jax.nn.sigmoid(mm(base + D_MODEL, D_MODEL)).astype(sgb_ref.dtype)


def _mod_spec(rows_per_batch_tiles, col):
    return pl.BlockSpec((None, 1, D_MODEL), lambda i: (i // rows_per_batch_tiles, 0, col))


def _proj_lat(x2, mod3, g_pre, w_in_b, lbl, ln_g, ln_b, seq, tm):
    n = x2.shape[0]
    tpb = seq // tm
    row = lambda w: pl.BlockSpec((tm, w), lambda i: (i, 0))
    row2 = pl.BlockSpec((2, tm, HG_W), lambda i: (0, i, 0))
    full = lambda a: pl.BlockSpec(a.shape, lambda i: (0,) * a.ndim)
    outs = [
        (row(HG_W), jax.ShapeDtypeStruct((n, HG_W), BF16)),
        (row2, jax.ShapeDtypeStruct((2, n, HG_W), BF16)),
        (row2, jax.ShapeDtypeStruct((2, n, HG_W), F32)),
        (row(HG_W), jax.ShapeDtypeStruct((n, HG_W), BF16)),
        (row(HG_W), jax.ShapeDtypeStruct((n, HG_W), BF16)),
        (row(CM_W), jax.ShapeDtypeStruct((n, CM_W), BF16)),
        (row(CM_W), jax.ShapeDtypeStruct((n, CM_W), BF16)),
        (row(D_MODEL), jax.ShapeDtypeStruct((n, D_MODEL), BF16)),
        (row(D_MODEL), jax.ShapeDtypeStruct((n, D_MODEL), BF16)),
    ]
    return pl.pallas_call(
        _proj_lat_kernel,
        grid=(n // tm,),
        in_specs=[row(D_MODEL), _mod_spec(tpb, 0), _mod_spec(tpb, 1), full(g_pre), full(w_in_b),
                  full(lbl), full(ln_g), full(ln_b)],
        out_specs=[o[0] for o in outs],
        out_shape=[o[1] for o in outs],
        compiler_params=_params(("parallel",)),
        name="proj_lat",
    )(x2, mod3, mod3, g_pre, w_in_b, lbl, ln_g, ln_b)


def _proj_ctx_kernel(x_ref, sh_ref, sc_ref, g_ref, w_ref, lbl_ref, k_ref, lf_ref, v_ref):
    hb = _prenorm(x_ref, sh_ref, sc_ref, g_ref)
    lbs = _lower_bounds(lbl_ref[...])
    for d in range(2):
        _gates(_dot(hb, w_ref[:, d * HG_W:(d + 1) * HG_W]), lbs[d], k_ref, lf_ref, d)
    v_ref[...] = _dot(hb, w_ref[:, 2 * HG_W:3 * HG_W]).astype(v_ref.dtype)


def _proj_ctx(c2, mod3, ctx_row, g_pre, w_ctx_b, lbl, tm):
    n = c2.shape[0]
    row = lambda w: pl.BlockSpec((tm, w), lambda i: (i, 0))
    row2 = pl.BlockSpec((2, tm, HG_W), lambda i: (0, i, 0))
    full = lambda a: pl.BlockSpec(a.shape, lambda i: (0,) * a.ndim)
    mod = lambda col: pl.BlockSpec((None, 1, D_MODEL), lambda i: (ctx_row, 0, col))
    return pl.pallas_call(
        _proj_ctx_kernel,
        grid=(n // tm,),
        in_specs=[row(D_MODEL), mod(0), mod(1), full(g_pre), full(w_ctx_b), full(lbl)],
        out_specs=[row2, row2, row(HG_W)],
        out_shape=[jax.ShapeDtypeStruct((2, n, HG_W), BF16), jax.ShapeDtypeStruct((2, n, HG_W), F32),
                   jax.ShapeDtypeStruct((n, HG_W), BF16)],
        compiler_params=_params(("parallel",)),
        name="proj_ctx",
    )(c2, mod3, mod3, g_pre, w_ctx_b, lbl)


def _scan_step(d, k_ref, lf_ref, v_ref, st_ref, b_scr, q_ref=None, o_ref=None):
    C = SCAN_CHUNK
    t_i = lax.broadcasted_iota(jnp.int32, (C, C), 0)
    s_i = lax.broadcasted_iota(jnp.int32, (C, C), 1)
    pt = jnp.where(d == 0, t_i, C - 1 - t_i)
    ps = jnp.where(d == 0, s_i, C - 1 - s_i)

    lf = lf_ref[...]
    hi = lf.astype(BF16)
    r1 = lf - hi.astype(F32)
    mid = r1.astype(BF16)
    lo = (r1 - mid.astype(F32)).astype(BF16)
    lmat = (ps <= pt).astype(BF16)
    b_scr[...] = _dot(jnp.concatenate([lmat, lmat, lmat], axis=1), jnp.concatenate([hi, mid, lo], axis=0))
    b = b_scr[...]

    def row(i):
        return b_scr[pl.ds(i, 1), :]

    b_last = row(jnp.where(d == 0, C - 1, 0))
    k = k_ref[...].astype(F32)
    v = v_ref[...]
    khat = (k * jnp.exp(b_last - b)).astype(BF16)
    decay = jnp.exp(b_last)

    if q_ref is not None:
        q = q_ref[...].astype(F32)
        qhat = (q * jnp.exp(b)).astype(BF16)
        levels = []
        half = C // 2
        while half >= SUB:
            span = 2 * half
            e = jnp.concatenate([b[m * span:(m + 1) * span] - row(m * span + half - 1 + d)
                                 for m in range(C // span)], axis=0)
            w = jnp.exp(-jnp.abs(e))
            mask = (pt // span == ps // span) & ((pt // half) % 2 == 1) & ((ps // half) % 2 == 0)
            levels.append((w, w, mask))
            half //= 2
        e0 = jnp.concatenate([b[m * SUB:(m + 1) * SUB] - row(m * SUB + SUB // 2 - 1 + d)
                              for m in range(C // SUB)], axis=0)
        levels.append((jnp.exp(e0), jnp.exp(-e0), (pt // SUB == ps // SUB) & (ps <= pt)))

    for h in range(HG_HEADS):
        sl = slice(h * HG_DK, (h + 1) * HG_DK)
        st = st_ref[:, sl]
        if q_ref is not None:
            a = jnp.zeros((C, C), F32)
            for wq, wk, mask in levels:
                al = _dot_nt((q[:, sl] * wq[:, sl]).astype(BF16), (k[:, sl] * wk[:, sl]).astype(BF16))
                a = a + jnp.where(mask, al, 0.0)
            o = _dot(a.astype(BF16), v[:, sl]) + _dot_nt(qhat[:, sl], st.astype(BF16))
            o_ref[:, sl] = o.astype(o_ref.dtype)
        st_ref[:, sl] = decay[:, sl] * st + _dot_tn(v[:, sl], khat[:, sl])


def _scan_kernel(n_ctx_steps, q_ref, k_ref, lf_ref, v_ref, kc_ref, lfc_ref, vc_ref, o_ref, st_ref, b_scr):
    d = pl.program_id(1)
    s = pl.program_id(2)

    @pl.when(s == 0)
    def _():
        st_ref[...] = jnp.zeros_like(st_ref)

    @pl.when(s < n_ctx_steps)
    def _():
        _scan_step(d, kc_ref, lfc_ref, vc_ref, st_ref, b_scr)

    @pl.when(s >= n_ctx_steps)
    def _():
        _scan_step(d, k_ref, lf_ref, v_ref, st_ref, b_scr, q_ref, o_ref)


def _hgrn_scan(q, k2, lf2, v, kc2, lfc2, vc, batch, seq, ctx_len):
    C = SCAN_CHUNK
    n_lat, n_ctx = seq // C, ctx_len // C

    def lat_blk(b, d, s):
        j = jnp.maximum(s - n_ctx, 0)
        return b * n_lat + jnp.where(d == 0, j, n_lat - 1 - j)

    def ctx_blk(b, d, s):
        i = jnp.minimum(s, n_ctx - 1)
        return b * n_ctx + jnp.where(d == 0, i, n_ctx - 1 - i)

    lat = pl.BlockSpec((C, HG_W), lambda b, d, s: (lat_blk(b, d, s), 0))
    lat_d = pl.BlockSpec((None, C, HG_W), lambda b, d, s: (d, lat_blk(b, d, s), 0))
    ctx = pl.BlockSpec((C, HG_W), lambda b, d, s: (ctx_blk(b, d, s), 0))
    ctx_d = pl.BlockSpec((None, C, HG_W), lambda b, d, s: (d, ctx_blk(b, d, s), 0))
    return pl.pallas_call(
        functools.partial(_scan_kernel, n_ctx),
        grid=(batch, 2, n_ctx + n_lat),
        in_specs=[lat, lat_d, lat_d, lat, ctx_d, ctx_d, ctx],
        out_specs=lat_d,
        out_shape=jax.ShapeDtypeStruct((2, batch * seq, HG_W), BF16),
        scratch_shapes=[pltpu.VMEM((HG_DK, HG_W), F32), pltpu.VMEM((C, HG_W), F32)],
        compiler_params=_params(("parallel", "parallel", "arbitrary")),
        name="hgrn_scan",
    )(q, k2, lf2, v, kc2, lfc2, vc)


def _merge_kernel(o_ref, sg_ref, u_ref, vn_ref, sga_ref, sgb_ref, x_ref, gt1_ref, sh2_ref, sc2_ref,
                  gout_ref, ws_ref, bs_ref, wa_ref, wb_ref, wo_ref, gpost_ref, gffn_ref, wr_ref,
                  x1_ref, h2_ref, lg_ref):
    tm = x_ref.shape[0]
    o = o_ref[0].astype(F32) + o_ref[1].astype(F32)
    sg = sg_ref[...].astype(F32)
    gout = gout_ref[...]
    a = jnp.concatenate(
        [_rms(o[:, h * HG_DK:(h + 1) * HG_DK], gout) * sg[:, h * HG_DK:(h + 1) * HG_DK] for h in range(HG_HEADS)],
        axis=1).astype(BF16)
    vn = vn_ref[...]
    gw = CM_W // CM_GROUPS
    z = jnp.concatenate(
        [jnp.concatenate([_dot(ws_ref[g], vn[c * CM_CHUNK:(c + 1) * CM_CHUNK, g * gw:(g + 1) * gw])
                          for g in range(CM_GROUPS)], axis=1) + bs_ref[...]
         for c in range(tm // CM_CHUNK)], axis=0)
    bm = (u_ref[...].astype(F32) * z).astype(BF16)
    y = sga_ref[...].astype(F32) * _dot(a, wa_ref[...]) + sgb_ref[...].astype(F32) * _dot(bm, wb_ref[...])
    yo = _dot(y.astype(BF16), wo_ref[...])
    x1 = x_ref[...] + gt1_ref[...] * _rms(yo, gpost_ref[...])
    x1_ref[...] = x1
    h2 = (_rms(x1, gffn_ref[...]) * (1.0 + sc2_ref[...]) + sh2_ref[...]).astype(BF16)
    h2_ref[...] = h2
    lg_ref[...] = _dot(h2, wr_ref[...])


def _merge(o2, sg, u, vn, sga, sgb, x2, mod3, g_out, ws_b, bs_full, wa_b, wb_b, wo_b, g_post, g_ffn, wr_b,
           seq, tm):
    n = x2.shape[0]
    tpb = seq // tm
    row = lambda w: pl.BlockSpec((tm, w), lambda i: (i, 0))
    full = lambda a: pl.BlockSpec(a.shape, lambda i: (0,) * a.ndim)
    return pl.pallas_call(
        _merge_kernel,
        grid=(n // tm,),
        in_specs=[pl.BlockSpec((2, tm, HG_W), lambda i: (0, i, 0)), row(HG_W), row(CM_W), row(CM_W),
                  row(D_MODEL), row(D_MODEL), row(D_MODEL), _mod_spec(tpb, 2), _mod_spec(tpb, 3),
                  _mod_spec(tpb, 4), full(g_out), full(ws_b), full(bs_full), full(wa_b), full(wb_b),
                  full(wo_b), full(g_post), full(g_ffn), full(wr_b)],
        out_specs=[row(D_MODEL), row(D_MODEL), row(wr_b.shape[1])],
        out_shape=[jax.ShapeDtypeStruct((n, D_MODEL), F32), jax.ShapeDtypeStruct((n, D_MODEL), BF16),
                   jax.ShapeDtypeStruct((n, wr_b.shape[1]), F32)],
        compiler_params=_params(("parallel",)),
        name="merge",
    )(o2, sg, u, vn, sga, sgb, x2, mod3, mod3, mod3, g_out, ws_b, bs_full, wa_b, wb_b, wo_b, g_post, g_ffn,
      wr_b)


def _router_kernel(lg_ref, br_ref, w_ref):
    scores = jax.nn.sigmoid(lg_ref[...])
    lane = lax.broadcasted_iota(jnp.int32, scores.shape, 1)
    grp = jnp.right_shift(lane, GROUP_BITS)
    neg = -jnp.inf
    sel = jnp.where(lane < N_EXPERTS, scores + br_ref[...], neg)
    n_lanes = scores.shape[1]

    def first_max(x, ids):
        m = jnp.max(x, axis=1, keepdims=True)
        return m, jnp.min(jnp.where(x == m, ids, n_lanes), axis=1, keepdims=True)

    gscore = jnp.full(sel.shape, neg, F32)
    for g in range(N_GROUPS):
        x = jnp.where(grp == g, sel, neg)
        m1, i1 = first_max(x, lane)
        m2 = jnp.max(jnp.where(lane == i1, neg, x), axis=1, keepdims=True)
        gscore = jnp.where(grp == g, m1 + m2, gscore)
    gmask = jnp.zeros(sel.shape, jnp.bool_)
    for _ in range(TOPK_GROUPS):
        _, gi = first_max(gscore, grp)
        gmask = gmask | (grp == gi)
        gscore = jnp.where(grp == gi, neg, gscore)
    x = jnp.where(gmask, sel, neg)
    chosen = jnp.zeros(sel.shape, jnp.bool_)
    for _ in range(TOP_K):
        _, ei = first_max(x, lane)
        chosen = chosen | (lane == ei)
        x = jnp.where(lane == ei, neg, x)
    w = jnp.where(chosen, scores, 0.0)
    w_ref[...] = w / jnp.sum(w, axis=1, keepdims=True) * ROUTED_SCALE


def _router(logits, b_router, tm):
    n, lanes = logits.shape
    return pl.pallas_call(
        _router_kernel,
        grid=(n // tm,),
        in_specs=[pl.BlockSpec((tm, lanes), lambda i: (i, 0)), pl.BlockSpec((1, lanes), lambda i: (0, 0))],
        out_specs=pl.BlockSpec((tm, lanes), lambda i: (i, 0)),
        out_shape=jax.ShapeDtypeStruct((n, lanes), F32),
        compiler_params=_params(("parallel",)),
        name="router",
    )(logits, b_router)


MOE_TILE = 256
MOE_UNIT = 16
MOE_BLOCK = 256
UNITS_PER_BLOCK = MOE_BLOCK // MOE_UNIT
TILE_ROWS = 3072
TILE_UNITS = TILE_ROWS // MOE_UNIT
ROW_CHUNK = 512
KEY_W = 128
DIGIT_BITS = 6
DIGIT = 1 << DIGIT_BITS


def _swiglu_act(h, w_gu):
    gu = _dot(h, w_gu)
    de = gu.shape[1] // 2
    g = gu[:, :de]
    return g * jax.nn.sigmoid(g) * gu[:, de:]


def _token_keys(cw, starts_row):
    t = cw.shape[0]
    routed = cw > 0.0
    t_i = lax.broadcasted_iota(jnp.int32, (t, t), 0)
    s_i = lax.broadcasted_iota(jnp.int32, (t, t), 1)
    rank = _dot((s_i < t_i).astype(BF16), routed.astype(BF16))
    pos = (starts_row + rank).astype(jnp.int32)
    lane = lax.broadcasted_iota(jnp.int32, cw.shape, 1)
    hi = jnp.where(routed, jnp.right_shift(pos, DIGIT_BITS), -1)
    lo = jnp.where(routed, jnp.bitwise_and(pos, DIGIT - 1), -1)
    key_hi = jnp.where(lane < N_EXPERTS, hi, jnp.where(lane == N_EXPERTS, -1, 0))
    key_lo = jnp.where(lane < N_EXPERTS, lo, jnp.where(lane == N_EXPERTS + 1, -1, 0))
    return key_hi.astype(F32).astype(BF16), key_lo.astype(F32).astype(BF16)


def _segment_units(counts):
    return jnp.floor((counts + (MOE_UNIT - 1)) * (1.0 / MOE_UNIT))


def _dispatch_kernel(h_ref, cw_ref, digits_ref, xs_ref, cnt_ref):
    cw = cw_ref[...]
    t = cw.shape[0]
    routed = (cw > 0.0).astype(BF16)
    counts = _dot(jnp.ones((8, t), BF16), routed)
    cnt_ref[...] = counts.astype(jnp.int32)
    units = _segment_units(counts)
    e_i = lax.broadcasted_iota(jnp.int32, (KEY_W, KEY_W), 0)
    f_i = lax.broadcasted_iota(jnp.int32, (KEY_W, KEY_W), 1)
    starts = _dot(units.astype(BF16), (e_i < f_i).astype(BF16)) * MOE_UNIT
    ends = starts + units * MOE_UNIT
    key_hi, key_lo = _token_keys(cw, starts[:1])
    h = h_ref[...]
    lane = lax.broadcasted_iota(jnp.int32, (ROW_CHUNK, KEY_W), 1)
    for c in range(TILE_ROWS // ROW_CHUNK):
        rows = slice(c * ROW_CHUNK, (c + 1) * ROW_CHUNK)
        r = (lax.broadcasted_iota(jnp.int32, (ROW_CHUNK, KEY_W), 0) + c * ROW_CHUNK).astype(F32)
        in_seg = (r >= starts[:1]) & (r < ends[:1])
        rmap = jnp.where(lane < N_EXPERTS, in_seg.astype(F32), digits_ref[rows, :].astype(F32)).astype(BF16)
        hit = (_dot_nt(rmap, key_hi) == 0.0) & (_dot_nt(rmap, key_lo) == 0.0)
        xs_ref[rows, :] = _dot(hit.astype(BF16), h).astype(xs_ref.dtype)


def _dispatch(h2, cw, digits):
    n = h2.shape[0]
    n_tiles = n // MOE_TILE
    return pl.pallas_call(
        _dispatch_kernel,
        grid=(n_tiles,),
        in_specs=[pl.BlockSpec((MOE_TILE, D_MODEL), lambda i: (i, 0)),
                  pl.BlockSpec((MOE_TILE, KEY_W), lambda i: (i, 0)),
                  pl.BlockSpec(digits.shape, lambda i: (0, 0))],
        out_specs=[pl.BlockSpec((TILE_ROWS, D_MODEL), lambda i: (i, 0)),
                   pl.BlockSpec((8, KEY_W), lambda i: (i, 0))],
        out_shape=[jax.ShapeDtypeStruct((n_tiles * TILE_ROWS, D_MODEL), BF16),
                   jax.ShapeDtypeStruct((n_tiles * 8, KEY_W), jnp.int32)],
        compiler_params=_params(("parallel",)),
        name="moe_dispatch",
    )(h2, cw, digits)


def _unit_copy(src_hbm, unit, dst, slot, pos, sem):
    return pltpu.make_async_copy(
        src_hbm.at[pl.ds(pl.multiple_of(unit * MOE_UNIT, MOE_UNIT), MOE_UNIT)],
        dst.at[slot, pl.ds(pos * MOE_UNIT, MOE_UNIT)], sem.at[slot])


def _experts_kernel(be_ref, src_ref, nb_ref, xs_hbm, wgu_ref, wdn_ref, ys_ref, xbuf, sem, wgu_b, wdn_b):
    j = pl.program_id(0)
    nb = nb_ref[0]

    def copies(blk, slot):
        return [_unit_copy(xs_hbm, src_ref[blk * UNITS_PER_BLOCK + u], xbuf, slot, u, sem)
                for u in range(UNITS_PER_BLOCK)]

    def fetch(blk, slot):
        for cp in copies(blk, slot):
            cp.start()

    @pl.when(j == 0)
    def _():
        fetch(0, 0)

    @pl.when(j + 1 < nb)
    def _():
        fetch(j + 1, (j + 1) % 2)

    @pl.when((j == 0) | (be_ref[j] != be_ref[jnp.maximum(j - 1, 0)]))
    def _():
        wgu_b[...] = wgu_ref[...].astype(BF16)
        wdn_b[...] = wdn_ref[...].astype(BF16)

    @pl.when(j < nb)
    def _():
        slot = j % 2
        for cp in copies(j, slot):
            cp.wait()
        act = _swiglu_act(xbuf[slot], wgu_b[...])
        ys_ref[...] = _dot(act.astype(BF16), wdn_b[...]).astype(ys_ref.dtype)

    @pl.when(j >= nb)
    def _():
        ys_ref[...] = jnp.zeros_like(ys_ref)


def _experts(xs, block_expert, src_units, n_blocks_used, w_gu, w_dn):
    nb_max = block_expert.shape[0]
    grid_spec = pltpu.PrefetchScalarGridSpec(
        num_scalar_prefetch=3,
        grid=(nb_max,),
        in_specs=[pl.BlockSpec(memory_space=pl.ANY),
                  pl.BlockSpec((None, D_MODEL, 2 * D_EXPERT), lambda j, be, src, nb: (be[j], 0, 0)),
                  pl.BlockSpec((None, D_EXPERT, D_MODEL), lambda j, be, src, nb: (be[j], 0, 0))],
        out_specs=pl.BlockSpec((MOE_BLOCK, D_MODEL), lambda j, be, src, nb: (j, 0)),
        scratch_shapes=[pltpu.VMEM((2, MOE_BLOCK, D_MODEL), BF16), pltpu.SemaphoreType.DMA((2,)),
                        pltpu.VMEM((D_MODEL, 2 * D_EXPERT), BF16), pltpu.VMEM((D_EXPERT, D_MODEL), BF16)],
    )
    return pl.pallas_call(
        _experts_kernel,
        grid_spec=grid_spec,
        out_shape=jax.ShapeDtypeStruct((nb_max * MOE_BLOCK, D_MODEL), BF16),
        compiler_params=_params(("arbitrary",)),
        name="moe_experts",
    )(block_expert, src_units, n_blocks_used, xs, w_gu, w_dn)


def _combine_kernel(src_ref, ys_hbm, cw_ref, h_ref, x1_ref, gt2_ref, gpost_ref, digits_t_ref, wsgu_ref,
                    wsdn_ref, o_ref, ybuf, sem):
    i = pl.program_id(0)

    def copies(tile, slot):
        return [_unit_copy(ys_hbm, src_ref[tile * TILE_UNITS + u], ybuf, slot, u, sem)
                for u in range(TILE_UNITS)]

    def fetch(tile, slot):
        for cp in copies(tile, slot):
            cp.start()

    @pl.when(i == 0)
    def _():
        fetch(0, 0)

    @pl.when(i + 1 < pl.num_programs(0))
    def _():
        fetch(i + 1, (i + 1) % 2)

    cw = cw_ref[...]
    t = cw.shape[0]
    routed = (cw > 0.0).astype(BF16)
    e_i = lax.broadcasted_iota(jnp.int32, (KEY_W, KEY_W), 0)
    f_i = lax.broadcasted_iota(jnp.int32, (KEY_W, KEY_W), 1)
    units = _segment_units(_dot_tn(routed, jnp.ones((t, KEY_W), BF16)))
    starts = _dot((f_i < e_i).astype(BF16), units.astype(BF16)) * MOE_UNIT
    ends = starts + units * MOE_UNIT
    units_row = _segment_units(_dot(jnp.ones((8, t), BF16), routed))
    starts_row = _dot(units_row.astype(BF16), (e_i < f_i).astype(BF16)) * MOE_UNIT
    key_hi, key_lo = _token_keys(cw, starts_row[:1])
    wb = cw.astype(BF16)

    f = _dot(_swiglu_act(h_ref[...], wsgu_ref[...]).astype(BF16), wsdn_ref[...])
    slot = i % 2
    for cp in copies(i, slot):
        cp.wait()
    reps = ROW_CHUNK // KEY_W
    starts_c = jnp.concatenate([starts] * reps, axis=1)
    ends_c = jnp.concatenate([ends] * reps, axis=1)
    sub = lax.broadcasted_iota(jnp.int32, (KEY_W, ROW_CHUNK), 0)
    for c in range(TILE_ROWS // ROW_CHUNK):
        rows = slice(c * ROW_CHUNK, (c + 1) * ROW_CHUNK)
        r = (lax.broadcasted_iota(jnp.int32, (KEY_W, ROW_CHUNK), 1) + c * ROW_CHUNK).astype(F32)
        in_seg = (r >= starts_c) & (r < ends_c)
        rmap_t = jnp.where(sub < N_EXPERTS, in_seg.astype(F32), digits_t_ref[:, rows].astype(F32)).astype(BF16)
        hit = (_dot(key_hi, rmap_t) == 0.0) & (_dot(key_lo, rmap_t) == 0.0)
        w = _dot(wb, rmap_t)
        f = f + _dot(jnp.where(hit, w, 0.0).astype(BF16), ybuf[slot, rows, :])
    o_ref[...] = x1_ref[...] + gt2_ref[...] * _rms(f, gpost_ref[...])


def _combine(ys, src_units, cw, h2, x1, mod3, g_post, digits_t, wsgu_b, wsdn_b, seq):
    n = h2.shape[0]
    tpb = seq // MOE_TILE
    row = lambda w: pl.BlockSpec((MOE_TILE, w), lambda i, src: (i, 0))
    full = lambda a: pl.BlockSpec(a.shape, lambda i, src: (0,) * a.ndim)
    grid_spec = pltpu.PrefetchScalarGridSpec(
        num_scalar_prefetch=1,
        grid=(n // MOE_TILE,),
        in_specs=[pl.BlockSpec(memory_space=pl.ANY), row(KEY_W), row(D_MODEL), row(D_MODEL),
                  pl.BlockSpec((None, 1, D_MODEL), lambda i, src: (i // tpb, 0, 5)), full(g_post),
                  full(digits_t), full(wsgu_b), full(wsdn_b)],
        out_specs=row(D_MODEL),
        scratch_shapes=[pltpu.VMEM((2, TILE_ROWS, D_MODEL), BF16), pltpu.SemaphoreType.DMA((2,))],
    )
    return pl.pallas_call(
        _combine_kernel,
        grid_spec=grid_spec,
        out_shape=jax.ShapeDtypeStruct((n, D_MODEL), F32),
        compiler_params=_params(("arbitrary",)),
        name="moe_combine",
    )(src_units, ys, cw, h2, x1, mod3, g_post, digits_t, wsgu_b, wsdn_b)


def _row_digits():
    r = np.arange(TILE_ROWS)
    d = np.zeros((TILE_ROWS, KEY_W), np.float32)
    d[:, N_EXPERTS] = r // DIGIT
    d[:, N_EXPERTS + 1] = r % DIGIT
    return jnp.asarray(d, dtype=BF16)


def _moe_plan(counts, nb_max):
    n_tiles = counts.shape[0]
    s = (counts + (MOE_UNIT - 1)) // MOE_UNIT
    local = jnp.cumsum(s, axis=1) - s
    cs = jnp.cumsum(s, axis=0)
    per_expert = cs[-1]
    padded = (per_expert + UNITS_PER_BLOCK - 1) // UNITS_PER_BLOCK * UNITS_PER_BLOCK
    g_end = jnp.cumsum(padded)
    g_start = g_end - padded
    seg_start = g_start[None, :] + cs - s
    n_blocks_used = (g_end[-1] // UNITS_PER_BLOCK).astype(jnp.int32).reshape(1)
    block_expert = jnp.clip(
        jnp.searchsorted(g_end // UNITS_PER_BLOCK, jnp.arange(nb_max, dtype=jnp.int32), side="right"),
        0, N_EXPERTS - 1).astype(jnp.int32)
    p = jnp.arange(nb_max * UNITS_PER_BLOCK, dtype=jnp.int32)
    e = block_expert[p // UNITS_PER_BLOCK]
    q = p - g_start[e]
    cs_e = cs[:, e]
    tile = jnp.minimum(jnp.sum(cs_e <= q[None, :], axis=0), n_tiles - 1).astype(jnp.int32)
    before = cs[tile, e] - s[tile, e]
    src = tile * TILE_UNITS + local[tile, e] + q - before
    src_units = jnp.where(q < per_expert[e], src, 0).astype(jnp.int32)
    u = jnp.arange(TILE_UNITS, dtype=jnp.int32)
    seg_end = local + s
    eu = jnp.minimum(jnp.sum(seg_end[:, None, :] <= u[None, :, None], axis=2), N_EXPERTS - 1)
    rows = jnp.arange(n_tiles)[:, None]
    back = seg_start[rows, eu] + u[None, :] - local[rows, eu]
    back_units = jnp.where(u[None, :] < seg_end[:, -1:], back, 0).astype(jnp.int32).reshape(-1)
    return block_expert, src_units, n_blocks_used, back_units


def _tile(n, pref):
    t = pref
    while n % t:
        t //= 2
    return t


def kernel(x, c, ctx, c_ctx, w_ada, b_ada, g_pre_mix, g_post_mix, g_pre_ffn, g_post_ffn, w_in, lb_logits, g_hgrn_out, cm_ln_g, cm_ln_b, w_spatial, b_spatial, w_branch_a, w_branch_b, w_out, w_router, b_router, w_expert_gu, w_expert_down, w_shared_gu, w_shared_down):
    B, T, D = x.shape
    L = ctx.shape[1]
    assert D == D_MODEL and w_ada.shape[0] == 1 and T % SCAN_CHUNK == 0 and L % SCAN_CHUNK == 0
    l = 0
    row = lambda a: a[l].reshape(1, -1)

    n_rows = -(-(B + 1) // 16) * 16
    cs = jnp.zeros((n_rows, D), F32).at[:B].set(c).at[B].set(c_ctx)
    mod3 = _ada_mod(cs, w_ada[l], row(b_ada)).reshape(n_rows, 1, 6 * D)

    w_in_b = w_in[l].astype(BF16)
    lbl = lb_logits[:, l:l + 2].reshape(4, HG_W)
    x2 = x.reshape(B * T, D)
    q, k2, lf2, v, sg, u, vn, sga, sgb = _proj_lat(
        x2, mod3, row(g_pre_mix), w_in_b, lbl, row(cm_ln_g), row(cm_ln_b), T, _tile(T, 256))
    kc2, lfc2, vc = _proj_ctx(ctx.reshape(B * L, D), mod3, B, row(g_pre_mix), w_in_b[:, HG_W:4 * HG_W], lbl,
                              _tile(B * L, 256))

    o2 = _hgrn_scan(q, k2, lf2, v, kc2, lfc2, vc, B, T, L)

    bs_full = jnp.repeat(b_spatial[l], CM_W // CM_GROUPS, axis=1)
    x1, h2, logits = _merge(
        o2, sg, u, vn, sga, sgb, x2, mod3, row(g_hgrn_out), w_spatial[l].astype(BF16), bs_full,
        w_branch_a[l].astype(BF16), w_branch_b[l].astype(BF16), w_out[l].astype(BF16), row(g_post_mix),
        row(g_pre_ffn), jnp.pad(w_router[l], ((0, 0), (0, KEY_W - N_EXPERTS))).astype(BF16), T, _tile(T, 512))

    pad = KEY_W - N_EXPERTS
    cw = _router(logits, jnp.pad(row(b_router), ((0, 0), (0, pad))), _tile(B * T, 512))

    n_tok = B * T
    n_tiles = n_tok // MOE_TILE
    digits = _row_digits()
    xs, cnt = _dispatch(h2, cw, digits)
    counts = cnt.reshape(n_tiles, 8, KEY_W)[:, 0, :N_EXPERTS]
    max_units = (n_tok * TOP_K + n_tiles * N_EXPERTS * (MOE_UNIT - 1)) // MOE_UNIT + N_EXPERTS * (UNITS_PER_BLOCK - 1)
    nb_max = -(-max_units // UNITS_PER_BLOCK)
    block_expert, src_units, n_blocks_used, back_units = _moe_plan(counts, nb_max)
    ys = _experts(xs, block_expert, src_units, n_blocks_used, w_expert_gu[l], w_expert_down[l])
    out = _combine(ys, back_units, cw, h2, x1, mod3, row(g_post_ffn), digits.T,
                   w_shared_gu[l].astype(BF16), w_shared_down[l].astype(BF16), T)
    return out.reshape(B, T, D)
```

```python
import functools

import numpy as np
import jax
import jax.numpy as jnp
from jax import lax
from jax.experimental import pallas as pl
from jax.experimental.pallas import tpu as pltpu

F32 = jnp.float32
BF16 = jnp.bfloat16

D_MODEL = 1024
EPS = 1e-6
HG_HEADS = 4
HG_DK = 128
HG_W = HG_HEADS * HG_DK
CM_W = 512
CM_CHUNK = 128
CM_GROUPS = 4
D_IN = 5 * HG_W + 2 * CM_W + 2 * D_MODEL
N_EXPERTS = 64
TOP_K = 8
N_GROUPS = 8
GROUP_BITS = 3
TOPK_GROUPS = 4
D_EXPERT = 256
ROUTED_SCALE = 2.5
SCAN_CHUNK = 128
SUB = 16
VMEM_LIMIT = 56 * 1024 * 1024


def _params(sem):
    return pltpu.CompilerParams(dimension_semantics=sem, vmem_limit_bytes=VMEM_LIMIT)


def _dot(a, b):
    return jnp.dot(a, b, preferred_element_type=F32)


def _dot_nt(a, b):
    return lax.dot_general(a, b, (((1,), (1,)), ((), ())), preferred_element_type=F32)


def _dot_tn(a, b):
    return lax.dot_general(a, b, (((0,), (0,)), ((), ())), preferred_element_type=F32)


def _rms(x, g):
    return x * lax.rsqrt(jnp.mean(x * x, axis=-1, keepdims=True) + EPS) * g


def _ada_kernel(c_ref, w_ref, b_ref, o_ref):
    c = c_ref[...]
    s = c * jax.nn.sigmoid(c)
    o_ref[...] = _dot(s.astype(BF16), w_ref[...].astype(BF16)) + b_ref[...]


def _ada_mod(cs, w_ada, b_ada):
    rows = cs.shape[0]
    n_out = w_ada.shape[1]
    return pl.pallas_call(
        _ada_kernel,
        grid=(n_out // D_MODEL,),
        in_specs=[
            pl.BlockSpec((rows, D_MODEL), lambda j: (0, 0)),
            pl.BlockSpec((D_MODEL, D_MODEL), lambda j: (0, j)),
            pl.BlockSpec((1, D_MODEL), lambda j: (0, j)),
        ],
        out_specs=pl.BlockSpec((rows, D_MODEL), lambda j: (0, j)),
        out_shape=jax.ShapeDtypeStruct((rows, n_out), F32),
        compiler_params=_params(("parallel",)),
        name="ada_mod",
    )(cs, w_ada, b_ada)


def _lower_bounds(lbl):
    out = []
    for d in range(2):
        l0, l1 = lbl[2 * d:2 * d + 1], lbl[2 * d + 1:2 * d + 2]
        m = jnp.maximum(l0, l1)
        e0, e1 = jnp.exp(l0 - m), jnp.exp(l1 - m)
        out.append(e0 / (e0 + e1))
    return out


def _prenorm(x_ref, sh_ref, sc_ref, g_ref):
    return (_rms(x_ref[...], g_ref[...]) * (1.0 + sc_ref[...]) + sh_ref[...]).astype(BF16)


def _gates(z, lb, k_ref, lf_ref, d):
    k_ref[d] = ((1.0 - lb) * jax.nn.sigmoid(-z)).astype(k_ref.dtype)
    lf_ref[d] = jnp.log(lb + (1.0 - lb) * jax.nn.sigmoid(z))


def _proj_lat_kernel(x_ref, sh_ref, sc_ref, g_ref, w_ref, lbl_ref, lng_ref, lnb_ref,
                     q_ref, k_ref, lf_ref, v_ref, sg_ref, u_ref, vn_ref, sga_ref, sgb_ref):
    hb = _prenorm(x_ref, sh_ref, sc_ref, g_ref)
    lbs = _lower_bounds(lbl_ref[...])

    def mm(lo, width):
        return _dot(hb, w_ref[:, lo:lo + width])

    z = mm(0, HG_W)
    q_ref[...] = (z * jax.nn.sigmoid(z)).astype(q_ref.dtype)
    for d in range(2):
        _gates(mm((1 + d) * HG_W, HG_W), lbs[d], k_ref, lf_ref, d)
    v_ref[...] = mm(3 * HG_W, HG_W).astype(v_ref.dtype)
    z = mm(4 * HG_W, HG_W)
    sg_ref[...] = (z * jax.nn.sigmoid(z)).astype(sg_ref.dtype)
    u_ref[...] = jax.nn.gelu(mm(5 * HG_W, CM_W)).astype(u_ref.dtype)
    vv = jax.nn.gelu(mm(5 * HG_W + CM_W, CM_W))
    vc = vv - jnp.mean(vv, axis=-1, keepdims=True)
    vn = vc * lax.rsqrt(jnp.mean(vc * vc, axis=-1, keepdims=True) + EPS)
    vn_ref[...] = (vn * lng_ref[...] + lnb_ref[...]).astype(vn_ref.dtype)
    base = 5 * HG_W + 2 * CM_W
    sga_ref[...] = jax.nn.sigmoid(mm(base, D_MODEL)).astype(sga_ref.dtype)
    sgb_ref[...] = jax.nn.sigmoid(mm(base + D_MODEL, D_MODEL)).astype(sgb_ref.dtype)


def _mod_spec(rows_per_batch_tiles, col):
    return pl.BlockSpec((None, 1, D_MODEL), lambda i: (i // rows_per_batch_tiles, 0, col))


def _proj_lat(x2, mod3, g_pre, w_in_b, lbl, ln_g, ln_b, seq, tm):
    n = x2.shape[0]
    tpb = seq // tm
    row = lambda w: pl.BlockSpec((tm, w), lambda i: (i, 0))
    row2 = pl.BlockSpec((2, tm, HG_W), lambda i: (0, i, 0))
    full = lambda a: pl.BlockSpec(a.shape, lambda i: (0,) * a.ndim)
    outs = [
        (row(HG_W), jax.ShapeDtypeStruct((n, HG_W), BF16)),
        (row2, jax.ShapeDtypeStruct((2, n, HG_W), BF16)),
        (row2, jax.ShapeDtypeStruct((2, n, HG_W), F32)),
        (row(HG_W), jax.ShapeDtypeStruct((n, HG_W), BF16)),
        (row(HG_W), jax.ShapeDtypeStruct((n, HG_W), BF16)),
        (row(CM_W), jax.ShapeDtypeStruct((n, CM_W), BF16)),
        (row(CM_W), jax.ShapeDtypeStruct((n, CM_W), BF16)),
        (row(D_MODEL), jax.ShapeDtypeStruct((n, D_MODEL), BF16)),
        (row(D_MODEL), jax.ShapeDtypeStruct((n, D_MODEL), BF16)),
    ]
    return pl.pallas_call(
        _proj_lat_kernel,
        grid=(n // tm,),
        in_specs=[row(D_MODEL), _mod_spec(tpb, 0), _mod_spec(tpb, 1), full(g_pre), full(w_in_b),
                  full(lbl), full(ln_g), full(ln_b)],
        out_specs=[o[0] for o in outs],
        out_shape=[o[1] for o in outs],
        compiler_params=_params(("parallel",)),
        name="proj_lat",
    )(x2, mod3, mod3, g_pre, w_in_b, lbl, ln_g, ln_b)


def _proj_ctx_kernel(x_ref, sh_ref, sc_ref, g_ref, w_ref, lbl_ref, k_ref, lf_ref, v_ref):
    hb = _prenorm(x_ref, sh_ref, sc_ref, g_ref)
    lbs = _lower_bounds(lbl_ref[...])
    for d in range(2):
        _gates(_dot(hb, w_ref[:, d * HG_W:(d + 1) * HG_W]), lbs[d], k_ref, lf_ref, d)
    v_ref[...] = _dot(hb, w_ref[:, 2 * HG_W:3 * HG_W]).astype(v_ref.dtype)


def _proj_ctx(c2, mod3, ctx_row, g_pre, w_ctx_b, lbl, tm):
    n = c2.shape[0]
    row = lambda w: pl.BlockSpec((tm, w), lambda i: (i, 0))
    row2 = pl.BlockSpec((2, tm, HG_W), lambda i: (0, i, 0))
    full = lambda a: pl.BlockSpec(a.shape, lambda i: (0,) * a.ndim)
    mod = lambda col: pl.BlockSpec((None, 1, D_MODEL), lambda i: (ctx_row, 0, col))
    return pl.pallas_call(
        _proj_ctx_kernel,
        grid=(n // tm,),
        in_specs=[row(D_MODEL), mod(0), mod(1), full(g_pre), full(w_ctx_b), full(lbl)],
        out_specs=[row2, row2, row(HG_W)],
        out_shape=[jax.ShapeDtypeStruct((2, n, HG_W), BF16), jax.ShapeDtypeStruct((2, n, HG_W), F32),
                   jax.ShapeDtypeStruct((n, HG_W), BF16)],
        compiler_params=_params(("parallel",)),
        name="proj_ctx",
    )(c2, mod3, mod3, g_pre, w_ctx_b, lbl)


def _scan_step(d, k_ref, lf_ref, v_ref, st_ref, b_scr, q_ref=None, o_ref=None):
    C = SCAN_CHUNK
    t_i = lax.broadcasted_iota(jnp.int32, (C, C), 0)
    s_i = lax.broadcasted_iota(jnp.int32, (C, C), 1)
    pt = jnp.where(d == 0, t_i, C - 1 - t_i)
    ps = jnp.where(d == 0, s_i, C - 1 - s_i)

    lf = lf_ref[...]
    hi = lf.astype(BF16)
    r1 = lf - hi.astype(F32)
    mid = r1.astype(BF16)
    lo = (r1 - mid.astype(F32)).astype(BF16)
    lmat = (ps <= pt).astype(BF16)
    b_scr[...] = _dot(jnp.concatenate([lmat, lmat, lmat], axis=1), jnp.concatenate([hi, mid, lo], axis=0))
    b = b_scr[...]

    def row(i):
        return b_scr[pl.ds(i, 1), :]

    b_last = row(jnp.where(d == 0, C - 1, 0))
    k = k_ref[...].astype(F32)
    v = v_ref[...]
    khat = (k * jnp.exp(b_last - b)).astype(BF16)
    decay = jnp.exp(b_last)

    if q_ref is not None:
        q = q_ref[...].astype(F32)
        qhat = (q * jnp.exp(b)).astype(BF16)
        levels = []
        half = C // 2
        while half >= SUB:
            span = 2 * half
            e = jnp.concatenate([b[m * span:(m + 1) * span] - row(m * span + half - 1 + d)
                                 for m in range(C // span)], axis=0)
            w = jnp.exp(-jnp.abs(e))
            mask = (pt // span == ps // span) & ((pt // half) % 2 == 1) & ((ps // half) % 2 == 0)
            levels.append((w, w, mask))
            half //= 2
        e0 = jnp.concatenate([b[m * SUB:(m + 1) * SUB] - row(m * SUB + SUB // 2 - 1 + d)
                              for m in range(C // SUB)], axis=0)
        levels.append((jnp.exp(e0), jnp.exp(-e0), (pt // SUB == ps // SUB) & (ps <= pt)))

    for h in range(HG_HEADS):
        sl = slice(h * HG_DK, (h + 1) * HG_DK)
        st = st_ref[:, sl]
        if q_ref is not None:
            a = jnp.zeros((C, C), F32)
            for wq, wk, mask in levels:
                al = _dot_nt((q[:, sl] * wq[:, sl]).astype(BF16), (k[:, sl] * wk[:, sl]).astype(BF16))
                a = a + jnp.where(mask, al, 0.0)
            o = _dot(a.astype(BF16), v[:, sl]) + _dot_nt(qhat[:, sl], st.astype(BF16))
            o_ref[:, sl] = o.astype(o_ref.dtype)
        st_ref[:, sl] = decay[:, sl] * st + _dot_tn(v[:, sl], khat[:, sl])


def _scan_kernel(n_ctx_steps, q_ref, k_ref, lf_ref, v_ref, kc_ref, lfc_ref, vc_ref, o_ref, st_ref, b_scr):
    d = pl.program_id(1)
    s = pl.program_id(2)

    @pl.when(s == 0)
    def _():
        st_ref[...] = jnp.zeros_like(st_ref)

    @pl.when(s < n_ctx_steps)
    def _():
        _scan_step(d, kc_ref, lfc_ref, vc_ref, st_ref, b_scr)

    @pl.when(s >= n_ctx_steps)
    def _():
        _scan_step(d, k_ref, lf_ref, v_ref, st_ref, b_scr, q_ref, o_ref)


def _hgrn_scan(q, k2, lf2, v, kc2, lfc2, vc, batch, seq, ctx_len):
    C = SCAN_CHUNK
    n_lat, n_ctx = seq // C, ctx_len // C

    def lat_blk(b, d, s):
        j = jnp.maximum(s - n_ctx, 0)
        return b * n_lat + jnp.where(d == 0, j, n_lat - 1 - j)

    def ctx_blk(b, d, s):
        i = jnp.minimum(s, n_ctx - 1)
        return b * n_ctx + jnp.where(d == 0, i, n_ctx - 1 - i)

    lat = pl.BlockSpec((C, HG_W), lambda b, d, s: (lat_blk(b, d, s), 0))
    lat_d = pl.BlockSpec((None, C, HG_W), lambda b, d, s: (d, lat_blk(b, d, s), 0))
    ctx = pl.BlockSpec((C, HG_W), lambda b, d, s: (ctx_blk(b, d, s), 0))
    ctx_d = pl.BlockSpec((None, C, HG_W), lambda b, d, s: (d, ctx_blk(b, d, s), 0))
    return pl.pallas_call(
        functools.partial(_scan_kernel, n_ctx),
        grid=(batch, 2, n_ctx + n_lat),
        in_specs=[lat, lat_d, lat_d, lat, ctx_d, ctx_d, ctx],
        out_specs=lat_d,
        out_shape=jax.ShapeDtypeStruct((2, batch * seq, HG_W), BF16),
        scratch_shapes=[pltpu.VMEM((HG_DK, HG_W), F32), pltpu.VMEM((C, HG_W), F32)],
        compiler_params=_params(("parallel", "parallel", "arbitrary")),
        name="hgrn_scan",
    )(q, k2, lf2, v, kc2, lfc2, vc)


def _merge_kernel(o_ref, sg_ref, u_ref, vn_ref, sga_ref, sgb_ref, x_ref, gt1_ref, sh2_ref, sc2_ref,
                  gout_ref, ws_ref, bs_ref, wa_ref, wb_ref, wo_ref, gpost_ref, gffn_ref, wr_ref,
                  x1_ref, h2_ref, lg_ref):
    tm = x_ref.shape[0]
    o = o_ref[0].astype(F32) + o_ref[1].astype(F32)
    sg = sg_ref[...].astype(F32)
    gout = gout_ref[...]
    a = jnp.concatenate(
        [_rms(o[:, h * HG_DK:(h + 1) * HG_DK], gout) * sg[:, h * HG_DK:(h + 1) * HG_DK] for h in range(HG_HEADS)],
        axis=1).astype(BF16)
    vn = vn_ref[...]
    gw = CM_W // CM_GROUPS
    z = jnp.concatenate(
        [jnp.concatenate([_dot(ws_ref[g], vn[c * CM_CHUNK:(c + 1) * CM_CHUNK, g * gw:(g + 1) * gw])
                          for g in range(CM_GROUPS)], axis=1) + bs_ref[...]
         for c in range(tm // CM_CHUNK)], axis=0)
    bm = (u_ref[...].astype(F32) * z).astype(BF16)
    y = sga_ref[...].astype(F32) * _dot(a, wa_ref[...]) + sgb_ref[...].astype(F32) * _dot(bm, wb_ref[...])
    yo = _dot(y.astype(BF16), wo_ref[...])
    x1 = x_ref[...] + gt1_ref[...] * _rms(yo, gpost_ref[...])
    x1_ref[...] = x1
    h2 = (_rms(x1, gffn_ref[...]) * (1.0 + sc2_ref[...]) + sh2_ref[...]).astype(BF16)
    h2_ref[...] = h2
    lg_ref[...] = _dot(h2, wr_ref[...])


def _merge(o2, sg, u, vn, sga, sgb, x2, mod3, g_out, ws_b, bs_full, wa_b, wb_b, wo_b, g_post, g_ffn, wr_b,
           seq, tm):
    n = x2.shape[0]
    tpb = seq // tm
    row = lambda w: pl.BlockSpec((tm, w), lambda i: (i, 0))
    full = lambda a: pl.BlockSpec(a.shape, lambda i: (0,) * a.ndim)
    return pl.pallas_call(
        _merge_kernel,
        grid=(n // tm,),
        in_specs=[pl.BlockSpec((2, tm, HG_W), lambda i: (0, i, 0)), row(HG_W), row(CM_W), row(CM_W),
                  row(D_MODEL), row(D_MODEL), row(D_MODEL), _mod_spec(tpb, 2), _mod_spec(tpb, 3),
                  _mod_spec(tpb, 4), full(g_out), full(ws_b), full(bs_full), full(wa_b), full(wb_b),
                  full(wo_b), full(g_post), full(g_ffn), full(wr_b)],
        out_specs=[row(D_MODEL), row(D_MODEL), row(wr_b.shape[1])],
        out_shape=[jax.ShapeDtypeStruct((n, D_MODEL), F32), jax.ShapeDtypeStruct((n, D_MODEL), BF16),
                   jax.ShapeDtypeStruct((n, wr_b.shape[1]), F32)],
        compiler_params=_params(("parallel",)),
        name="merge",
    )(o2, sg, u, vn, sga, sgb, x2, mod3, mod3, mod3, g_out, ws_b, bs_full, wa_b, wb_b, wo_b, g_post, g_ffn,
      wr_b)


def _router_kernel(lg_ref, br_ref, w_ref):
    scores = jax.nn.sigmoid(lg_ref[...])
    lane = lax.broadcasted_iota(jnp.int32, scores.shape, 1)
    grp = jnp.right_shift(lane, GROUP_BITS)
    neg = -jnp.inf
    sel = jnp.where(lane < N_EXPERTS, scores + br_ref[...], neg)
    n_lanes = scores.shape[1]

    def first_max(x, ids):
        m = jnp.max(x, axis=1, keepdims=True)
        return m, jnp.min(jnp.where(x == m, ids, n_lanes), axis=1, keepdims=True)

    gscore = jnp.full(sel.shape, neg, F32)
    for g in range(N_GROUPS):
        x = jnp.where(grp == g, sel, neg)
        m1, i1 = first_max(x, lane)
        m2 = jnp.max(jnp.where(lane == i1, neg, x), axis=1, keepdims=True)
        gscore = jnp.where(grp == g, m1 + m2, gscore)
    gmask = jnp.zeros(sel.shape, jnp.bool_)
    for _ in range(TOPK_GROUPS):
        _, gi = first_max(gscore, grp)
        gmask = gmask | (grp == gi)
        gscore = jnp.where(grp == gi, neg, gscore)
    x = jnp.where(gmask, sel, neg)
    chosen = jnp.zeros(sel.shape, jnp.bool_)
    for _ in range(TOP_K):
        _, ei = first_max(x, lane)
        chosen = chosen | (lane == ei)
        x = jnp.where(lane == ei, neg, x)
    w = jnp.where(chosen, scores, 0.0)
    w_ref[...] = w / jnp.sum(w, axis=1, keepdims=True) * ROUTED_SCALE


def _router(logits, b_router, tm):
    n, lanes = logits.shape
    return pl.pallas_call(
        _router_kernel,
        grid=(n // tm,),
        in_specs=[pl.BlockSpec((tm, lanes), lambda i: (i, 0)), pl.BlockSpec((1, lanes), lambda i: (0, 0))],
        out_specs=pl.BlockSpec((tm, lanes), lambda i: (i, 0)),
        out_shape=jax.ShapeDtypeStruct((n, lanes), F32),
        compiler_params=_params(("parallel",)),
        name="router",
    )(logits, b_router)


MOE_TILE = 256
MOE_UNIT = 16
MOE_BLOCK = 512
MOE_MM_ROWS = 256
UNITS_PER_BLOCK = MOE_BLOCK // MOE_UNIT
TILE_ROWS = 3072
TILE_UNITS = TILE_ROWS // MOE_UNIT
ROW_CHUNK = 512
KEY_W = 128
DIGIT_BITS = 6
DIGIT = 1 << DIGIT_BITS


def _swiglu_act(h, w_gu):
    gu = _dot(h, w_gu)
    de = gu.shape[1] // 2
    g = gu[:, :de]
    return g * jax.nn.sigmoid(g) * gu[:, de:]


def _token_keys(cw, starts_row):
    t = cw.shape[0]
    routed = cw > 0.0
    t_i = lax.broadcasted_iota(jnp.int32, (t, t), 0)
    s_i = lax.broadcasted_iota(jnp.int32, (t, t), 1)
    rank = _dot((s_i < t_i).astype(BF16), routed.astype(BF16))
    pos = (starts_row + rank).astype(jnp.int32)
    lane = lax.broadcasted_iota(jnp.int32, cw.shape, 1)
    hi = jnp.where(routed, jnp.right_shift(pos, DIGIT_BITS), -1)
    lo = jnp.where(routed, jnp.bitwise_and(pos, DIGIT - 1), -1)
    key_hi = jnp.where(lane < N_EXPERTS, hi, jnp.where(lane == N_EXPERTS, -1, 0))
    key_lo = jnp.where(lane < N_EXPERTS, lo, jnp.where(lane == N_EXPERTS + 1, -1, 0))
    return key_hi.astype(F32).astype(BF16), key_lo.astype(F32).astype(BF16)


def _segment_units(counts):
    return jnp.floor((counts + (MOE_UNIT - 1)) * (1.0 / MOE_UNIT))


def _dispatch_kernel(h_ref, cw_ref, digits_ref, xs_ref, cnt_ref):
    cw = cw_ref[...]
    t = cw.shape[0]
    routed = (cw > 0.0).astype(BF16)
    counts = _dot(jnp.ones((8, t), BF16), routed)
    cnt_ref[...] = counts.astype(jnp.int32)
    units = _segment_units(counts)
    e_i = lax.broadcasted_iota(jnp.int32, (KEY_W, KEY_W), 0)
    f_i = lax.broadcasted_iota(jnp.int32, (KEY_W, KEY_W), 1)
    starts = _dot(units.astype(BF16), (e_i < f_i).astype(BF16)) * MOE_UNIT
    ends = starts + units * MOE_UNIT
    key_hi, key_lo = _token_keys(cw, starts[:1])
    h = h_ref[...]
    lane = lax.broadcasted_iota(jnp.int32, (ROW_CHUNK, KEY_W), 1)
    for c in range(TILE_ROWS // ROW_CHUNK):
        rows = slice(c * ROW_CHUNK, (c + 1) * ROW_CHUNK)
        r = (lax.broadcasted_iota(jnp.int32, (ROW_CHUNK, KEY_W), 0) + c * ROW_CHUNK).astype(F32)
        in_seg = (r >= starts[:1]) & (r < ends[:1])
        rmap = jnp.where(lane < N_EXPERTS, in_seg.astype(F32), digits_ref[rows, :].astype(F32)).astype(BF16)
        hit = (_dot_nt(rmap, key_hi) == 0.0) & (_dot_nt(rmap, key_lo) == 0.0)
        xs_ref[rows, :] = _dot(hit.astype(BF16), h).astype(xs_ref.dtype)


def _dispatch(h2, cw, digits):
    n = h2.shape[0]
    n_tiles = n // MOE_TILE
    return pl.pallas_call(
        _dispatch_kernel,
        grid=(n_tiles,),
        in_specs=[pl.BlockSpec((MOE_TILE, D_MODEL), lambda i: (i, 0)),
                  pl.BlockSpec((MOE_TILE, KEY_W), lambda i: (i, 0)),
                  pl.BlockSpec(digits.shape, lambda i: (0, 0))],
        out_specs=[pl.BlockSpec((TILE_ROWS, D_MODEL), lambda i: (i, 0)),
                   pl.BlockSpec((8, KEY_W), lambda i: (i, 0))],
        out_shape=[jax.ShapeDtypeStruct((n_tiles * TILE_ROWS, D_MODEL), BF16),
                   jax.ShapeDtypeStruct((n_tiles * 8, KEY_W), jnp.int32)],
        compiler_params=_params(("parallel",)),
        name="moe_dispatch",
    )(h2, cw, digits)


def _unit_copy(src_hbm, unit, dst, slot, pos, sem):
    return pltpu.make_async_copy(
        src_hbm.at[pl.ds(pl.multiple_of(unit * MOE_UNIT, MOE_UNIT), MOE_UNIT)],
        dst.at[slot, pl.ds(pos * MOE_UNIT, MOE_UNIT)], sem.at[slot])


def _experts_kernel(be_ref, src_ref, nb_ref, xs_hbm, wgu_ref, wdn_ref, ys_ref, xbuf, sem, wgu_b, wdn_b):
    j = pl.program_id(0)
    nb = nb_ref[0]

    def copies(blk, slot):
        return [_unit_copy(xs_hbm, src_ref[blk * UNITS_PER_BLOCK + u], xbuf, slot, u, sem)
                for u in range(UNITS_PER_BLOCK)]

    def fetch(blk, slot):
        for cp in copies(blk, slot):
            cp.start()

    @pl.when(j == 0)
    def _():
        fetch(0, 0)

    @pl.when(j + 1 < nb)
    def _():
        fetch(j + 1, (j + 1) % 2)

    @pl.when((j == 0) | (be_ref[j] != be_ref[jnp.maximum(j - 1, 0)]))
    def _():
        wgu_b[...] = wgu_ref[...].astype(BF16)
        wdn_b[...] = wdn_ref[...].astype(BF16)

    @pl.when(j < nb)
    def _():
        slot = j % 2
        for cp in copies(j, slot):
            cp.wait()
        for g in range(MOE_BLOCK // MOE_MM_ROWS):
            rows = pl.ds(g * MOE_MM_ROWS, MOE_MM_ROWS)
            act = _swiglu_act(xbuf[slot, rows, :], wgu_b[...])
            ys_ref[rows, :] = _dot(act.astype(BF16), wdn_b[...]).astype(ys_ref.dtype)

    @pl.when(j >= nb)
    def _():
        ys_ref[...] = jnp.zeros_like(ys_ref)


def _experts(xs, block_expert, src_units, n_blocks_used, w_gu, w_dn):
    nb_max = block_expert.shape[0]
    grid_spec = pltpu.PrefetchScalarGridSpec(
        num_scalar_prefetch=3,
        grid=(nb_max,),
        in_specs=[pl.BlockSpec(memory_space=pl.ANY),
                  pl.BlockSpec((None, D_MODEL, 2 * D_EXPERT), lambda j, be, src, nb: (be[j], 0, 0)),
                  pl.BlockSpec((None, D_EXPERT, D_MODEL), lambda j, be, src, nb: (be[j], 0, 0))],
        out_specs=pl.BlockSpec((MOE_BLOCK, D_MODEL), lambda j, be, src, nb: (j, 0)),
        scratch_shapes=[pltpu.VMEM((2, MOE_BLOCK, D_MODEL), BF16), pltpu.SemaphoreType.DMA((2,)),
                        pltpu.VMEM((D_MODEL, 2 * D_EXPERT), BF16), pltpu.VMEM((D_EXPERT, D_MODEL), BF16)],
    )
    return pl.pallas_call(
        _experts_kernel,
        grid_spec=grid_spec,
        out_shape=jax.ShapeDtypeStruct((nb_max * MOE_BLOCK, D_MODEL), BF16),
        compiler_params=_params(("arbitrary",)),
        name="moe_experts",
    )(block_expert, src_units, n_blocks_used, xs, w_gu, w_dn)


def _combine_kernel(src_ref, ys_hbm, cw_ref, h_ref, x1_ref, gt2_ref, gpost_ref, digits_t_ref, wsgu_ref,
                    wsdn_ref, o_ref, ybuf, sem):
    i = pl.program_id(0)

    def copies(tile, slot):
        return [_unit_copy(ys_hbm, src_ref[tile * TILE_UNITS + u], ybuf, slot, u, sem)
                for u in range(TILE_UNITS)]

    def fetch(tile, slot):
        for cp in copies(tile, slot):
            cp.start()

    @pl.when(i == 0)
    def _():
        fetch(0, 0)

    @pl.when(i + 1 < pl.num_programs(0))
    def _():
        fetch(i + 1, (i + 1) % 2)

    cw = cw_ref[...]
    t = cw.shape[0]
    routed = (cw > 0.0).astype(BF16)
    e_i = lax.broadcasted_iota(jnp.int32, (KEY_W, KEY_W), 0)
    f_i = lax.broadcasted_iota(jnp.int32, (KEY_W, KEY_W), 1)
    units = _segment_units(_dot_tn(routed, jnp.ones((t, KEY_W), BF16)))
    starts = _dot((f_i < e_i).astype(BF16), units.astype(BF16)) * MOE_UNIT
    ends = starts + units * MOE_UNIT
    units_row = _segment_units(_dot(jnp.ones((8, t), BF16), routed))
    starts_row = _dot(units_row.astype(BF16), (e_i < f_i).astype(BF16)) * MOE_UNIT
    key_hi, key_lo = _token_keys(cw, starts_row[:1])
    wb = cw.astype(BF16)

    f = _dot(_swiglu_act(h_ref[...], wsgu_ref[...]).astype(BF16), wsdn_ref[...])
    slot = i % 2
    for cp in copies(i, slot):
        cp.wait()
    reps = ROW_CHUNK // KEY_W
    starts_c = jnp.concatenate([starts] * reps, axis=1)
    ends_c = jnp.concatenate([ends] * reps, axis=1)
    sub = lax.broadcasted_iota(jnp.int32, (KEY_W, ROW_CHUNK), 0)
    for c in range(TILE_ROWS // ROW_CHUNK):
        rows = slice(c * ROW_CHUNK, (c + 1) * ROW_CHUNK)
        r = (lax.broadcasted_iota(jnp.int32, (KEY_W, ROW_CHUNK), 1) + c * ROW_CHUNK).astype(F32)
        in_seg = (r >= starts_c) & (r < ends_c)
        rmap_t = jnp.where(sub < N_EXPERTS, in_seg.astype(F32), digits_t_ref[:, rows].astype(F32)).astype(BF16)
        hit = (_dot(key_hi, rmap_t) == 0.0) & (_dot(key_lo, rmap_t) == 0.0)
        w = _dot(wb, rmap_t)
        f = f + _dot(jnp.where(hit, w, 0.0).astype(BF16), ybuf[slot, rows, :])
    o_ref[...] = x1_ref[...] + gt2_ref[...] * _rms(f, gpost_ref[...])


def _combine(ys, src_units, cw, h2, x1, mod3, g_post, digits_t, wsgu_b, wsdn_b, seq):
    n = h2.shape[0]
    tpb = seq // MOE_TILE
    row = lambda w: pl.BlockSpec((MOE_TILE, w), lambda i, src: (i, 0))
    full = lambda a: pl.BlockSpec(a.shape, lambda i, src: (0,) * a.ndim)
    grid_spec = pltpu.PrefetchScalarGridSpec(
        num_scalar_prefetch=1,
        grid=(n // MOE_TILE,),
        in_specs=[pl.BlockSpec(memory_space=pl.ANY), row(KEY_W), row(D_MODEL), row(D_MODEL),
                  pl.BlockSpec((None, 1, D_MODEL), lambda i, src: (i // tpb, 0, 5)), full(g_post),
                  full(digits_t), full(wsgu_b), full(wsdn_b)],
        out_specs=row(D_MODEL),
        scratch_shapes=[pltpu.VMEM((2, TILE_ROWS, D_MODEL), BF16), pltpu.SemaphoreType.DMA((2,))],
    )
    return pl.pallas_call(
        _combine_kernel,
        grid_spec=grid_spec,
        out_shape=jax.ShapeDtypeStruct((n, D_MODEL), F32),
        compiler_params=_params(("arbitrary",)),
        name="moe_combine",
    )(src_units, ys, cw, h2, x1, mod3, g_post, digits_t, wsgu_b, wsdn_b)


def _row_digits():
    r = np.arange(TILE_ROWS)
    d = np.zeros((TILE_ROWS, KEY_W), np.float32)
    d[:, N_EXPERTS] = r // DIGIT
    d[:, N_EXPERTS + 1] = r % DIGIT
    return jnp.asarray(d, dtype=BF16)


def _moe_plan(counts, nb_max):
    n_tiles = counts.shape[0]
    s = (counts + (MOE_UNIT - 1)) // MOE_UNIT
    local = jnp.cumsum(s, axis=1) - s
    cs = jnp.cumsum(s, axis=0)
    per_expert = cs[-1]
    padded = (per_expert + UNITS_PER_BLOCK - 1) // UNITS_PER_BLOCK * UNITS_PER_BLOCK
    g_end = jnp.cumsum(padded)
    g_start = g_end - padded
    seg_start = g_start[None, :] + cs - s
    n_blocks_used = (g_end[-1] // UNITS_PER_BLOCK).astype(jnp.int32).reshape(1)
    jb = jnp.arange(nb_max, dtype=jnp.int32)
    one_e = ((jb[:, None] >= (g_start // UNITS_PER_BLOCK)[None, :])
             & (jb[:, None] < (g_end // UNITS_PER_BLOCK)[None, :])).astype(jnp.int32)
    pick_e = lambda table: jnp.sum(one_e[:, :, None] * table.T[None, :, :], axis=1)
    block_expert = jnp.where(jb < n_blocks_used[0], jnp.sum(one_e * jnp.arange(N_EXPERTS, dtype=jnp.int32), axis=1),
                             N_EXPERTS - 1).astype(jnp.int32)
    cs_b, s_b, local_b = pick_e(cs), pick_e(s), pick_e(local)
    q = (jb * UNITS_PER_BLOCK - jnp.sum(one_e * g_start[None, :], axis=1))[:, None] \
        + jnp.arange(UNITS_PER_BLOCK, dtype=jnp.int32)[None, :]
    tile = jnp.minimum(jnp.sum(cs_b[:, None, :] <= q[:, :, None], axis=2), n_tiles - 1)
    one_t = (tile[:, :, None] == jnp.arange(n_tiles, dtype=jnp.int32)).astype(jnp.int32)
    src = tile * TILE_UNITS + q + jnp.sum(one_t * (local_b - cs_b + s_b)[:, None, :], axis=2)
    valid = q < jnp.sum(one_e * per_expert[None, :], axis=1)[:, None]
    src_units = jnp.where(valid, src, 0).astype(jnp.int32).reshape(-1)
    u = jnp.arange(TILE_UNITS, dtype=jnp.int32)
    seg_end = local + s
    eu = jnp.minimum(jnp.sum(seg_end[:, None, :] <= u[None, :, None], axis=2), N_EXPERTS - 1)
    one_u = (eu[:, :, None] == jnp.arange(N_EXPERTS, dtype=jnp.int32)).astype(jnp.int32)
    back = u[None, :] + jnp.sum(one_u * (seg_start - local)[:, None, :], axis=2)
    back_units = jnp.where(u[None, :] < seg_end[:, -1:], back, 0).astype(jnp.int32).reshape(-1)
    return block_expert, src_units, n_blocks_used, back_units


def _tile(n, pref):
    t = pref
    while n % t:
        t //= 2
    return t


def kernel(x, c, ctx, c_ctx, w_ada, b_ada, g_pre_mix, g_post_mix, g_pre_ffn, g_post_ffn, w_in, lb_logits, g_hgrn_out, cm_ln_g, cm_ln_b, w_spatial, b_spatial, w_branch_a, w_branch_b, w_out, w_router, b_router, w_expert_gu, w_expert_down, w_shared_gu, w_shared_down):
    B, T, D = x.shape
    L = ctx.shape[1]
    assert D == D_MODEL and w_ada.shape[0] == 1 and T % SCAN_CHUNK == 0 and L % SCAN_CHUNK == 0
    l = 0
    row = lambda a: a[l].reshape(1, -1)

    n_rows = -(-(B + 1) // 16) * 16
    cs = jnp.zeros((n_rows, D), F32).at[:B].set(c).at[B].set(c_ctx)
    mod3 = _ada_mod(cs, w_ada[l], row(b_ada)).reshape(n_rows, 1, 6 * D)

    w_in_b = w_in[l].astype(BF16)
    lbl = lb_logits[:, l:l + 2].reshape(4, HG_W)
    x2 = x.reshape(B * T, D)
    q, k2, lf2, v, sg, u, vn, sga, sgb = _proj_lat(
        x2, mod3, row(g_pre_mix), w_in_b, lbl, row(cm_ln_g), row(cm_ln_b), T, _tile(T, 256))
    kc2, lfc2, vc = _proj_ctx(ctx.reshape(B * L, D), mod3, B, row(g_pre_mix), w_in_b[:, HG_W:4 * HG_W], lbl,
                              _tile(B * L, 256))

    o2 = _hgrn_scan(q, k2, lf2, v, kc2, lfc2, vc, B, T, L)

    bs_full = jnp.repeat(b_spatial[l], CM_W // CM_GROUPS, axis=1)
    x1, h2, logits = _merge(
        o2, sg, u, vn, sga, sgb, x2, mod3, row(g_hgrn_out), w_spatial[l].astype(BF16), bs_full,
        w_branch_a[l].astype(BF16), w_branch_b[l].astype(BF16), w_out[l].astype(BF16), row(g_post_mix),
        row(g_pre_ffn), jnp.pad(w_router[l], ((0, 0), (0, KEY_W - N_EXPERTS))).astype(BF16), T, _tile(T, 512))

    pad = KEY_W - N_EXPERTS
    cw = _router(logits, jnp.pad(row(b_router), ((0, 0), (0, pad))), _tile(B * T, 512))

    n_tok = B * T
    n_tiles = n_tok // MOE_TILE
    digits = _row_digits()
    xs, cnt = _dispatch(h2, cw, digits)
    counts = cnt.reshape(n_tiles, 8, KEY_W)[:, 0, :N_EXPERTS]
    max_units = (n_tok * TOP_K + n_tiles * N_EXPERTS * (MOE_UNIT - 1)) // MOE_UNIT + N_EXPERTS * (UNITS_PER_BLOCK - 1)
    nb_max = -(-max_units // UNITS_PER_BLOCK)
    block_expert, src_units, n_blocks_used, back_units = _moe_plan(counts, nb_max)
    ys = _experts(xs, block_expert, src_units, n_blocks_used, w_expert_gu[l], w_expert_down[l])
    out = _combine(ys, back_units, cw, h2, x1, mod3, row(g_post_ffn), digits.T,
                   w_shared_gu[l].astype(BF16), w_shared_down[l].astype(BF16), T)
    return out.reshape(B, T, D)
```

```python
import functools

import numpy as np
import jax
import jax.numpy as jnp
from jax import lax
from jax.experimental import pallas as pl
from jax.experimental.pallas import tpu as pltpu

F32 = jnp.float32
BF16 = jnp.bfloat16

D_MODEL = 1024
EPS = 1e-6
HG_HEADS = 4
HG_DK = 128
HG_W = HG_HEADS * HG_DK
CM_W = 512
CM_CHUNK = 128
CM_GROUPS = 4
D_IN = 5 * HG_W + 2 * CM_W + 2 * D_MODEL
N_EXPERTS = 64
TOP_K = 8
N_GROUPS = 8
GROUP_BITS = 3
TOPK_GROUPS = 4
D_EXPERT = 256
ROUTED_SCALE = 2.5
SCAN_CHUNK = 128
SUB = 16
VMEM_LIMIT = 56 * 1024 * 1024


def _params(sem):
    return pltpu.CompilerParams(dimension_semantics=sem, vmem_limit_bytes=VMEM_LIMIT)


def _dot(a, b):
    return jnp.dot(a, b, preferred_element_type=F32)


def _dot_nt(a, b):
    return lax.dot_general(a, b, (((1,), (1,)), ((), ())), preferred_element_type=F32)


def _dot_tn(a, b):
    return lax.dot_general(a, b, (((0,), (0,)), ((), ())), preferred_element_type=F32)


def _rms(x, g):
    return x * lax.rsqrt(jnp.mean(x * x, axis=-1, keepdims=True) + EPS) * g


def _ada_kernel(c_ref, w_ref, b_ref, o_ref):
    c = c_ref[...]
    s = c * jax.nn.sigmoid(c)
    o_ref[...] = _dot(s.astype(BF16), w_ref[...].astype(BF16)) + b_ref[...]


def _ada_mod(cs, w_ada, b_ada):
    rows = cs.shape[0]
    n_out = w_ada.shape[1]
    return pl.pallas_call(
        _ada_kernel,
        grid=(n_out // D_MODEL,),
        in_specs=[
            pl.BlockSpec((rows, D_MODEL), lambda j: (0, 0)),
            pl.BlockSpec((D_MODEL, D_MODEL), lambda j: (0, j)),
            pl.BlockSpec((1, D_MODEL), lambda j: (0, j)),
        ],
        out_specs=pl.BlockSpec((rows, D_MODEL), lambda j: (0, j)),
        out_shape=jax.ShapeDtypeStruct((rows, n_out), F32),
        compiler_params=_params(("parallel",)),
        name="ada_mod",
    )(cs, w_ada, b_ada)


def _lower_bounds(lbl):
    out = []
    for d in range(2):
        l0, l1 = lbl[2 * d:2 * d + 1], lbl[2 * d + 1:2 * d + 2]
        m = jnp.maximum(l0, l1)
        e0, e1 = jnp.exp(l0 - m), jnp.exp(l1 - m)
        out.append(e0 / (e0 + e1))
    return out


def _prenorm(x_ref, sh_ref, sc_ref, g_ref):
    return (_rms(x_ref[...], g_ref[...]) * (1.0 + sc_ref[...]) + sh_ref[...]).astype(BF16)


def _gates(z, lb, k_ref, lf_ref, d):
    k_ref[d] = ((1.0 - lb) * jax.nn.sigmoid(-z)).astype(k_ref.dtype)
    lf_ref[d] = jnp.log(lb + (1.0 - lb) * jax.nn.sigmoid(z))


def _proj_lat_kernel(x_ref, sh_ref, sc_ref, g_ref, w_ref, lbl_ref, lng_ref, lnb_ref,
                     q_ref, k_ref, lf_ref, v_ref, sg_ref, u_ref, vn_ref, sga_ref, sgb_ref):
    hb = _prenorm(x_ref, sh_ref, sc_ref, g_ref)
    lbs = _lower_bounds(lbl_ref[...])

    def mm(lo, width):
        return _dot(hb, w_ref[:, lo:lo + width])

    z = mm(0, HG_W)
    q_ref[...] = (z * jax.nn.sigmoid(z)).astype(q_ref.dtype)
    for d in range(2):
        _gates(mm((1 + d) * HG_W, HG_W), lbs[d], k_ref, lf_ref, d)
    v_ref[...] = mm(3 * HG_W, HG_W).astype(v_ref.dtype)
    z = mm(4 * HG_W, HG_W)
    sg_ref[...] = (z * jax.nn.sigmoid(z)).astype(sg_ref.dtype)
    u_ref[...] = jax.nn.gelu(mm(5 * HG_W, CM_W)).astype(u_ref.dtype)
    vv = jax.nn.gelu(mm(5 * HG_W + CM_W, CM_W))
    vc = vv - jnp.mean(vv, axis=-1, keepdims=True)
    vn = vc * lax.rsqrt(jnp.mean(vc * vc, axis=-1, keepdims=True) + EPS)
    vn_ref[...] = (vn * lng_ref[...] + lnb_ref[...]).astype(vn_ref.dtype)
    base = 5 * HG_W + 2 * CM_W
    sga_ref[...] = jax.nn.sigmoid(mm(base, D_MODEL)).astype(sga_ref.dtype)
    sgb_ref[...] = jax.nn.sigmoid(mm(base + D_MODEL, D_MODEL)).astype(sgb_ref.dtype)


def _mod_spec(rows_per_batch_tiles, col):
    return pl.BlockSpec((None, 1, D_MODEL), lambda i: (i // rows_per_batch_tiles, 0, col))


def _proj_lat(x2, mod3, g_pre, w_in_b, lbl, ln_g, ln_b, seq, tm):
    n = x2.shape[0]
    tpb = seq // tm
    row = lambda w: pl.BlockSpec((tm, w), lambda i: (i, 0))
    row2 = pl.BlockSpec((2, tm, HG_W), lambda i: (0, i, 0))
    full = lambda a: pl.BlockSpec(a.shape, lambda i: (0,) * a.ndim)
    outs = [
        (row(HG_W), jax.ShapeDtypeStruct((n, HG_W), BF16)),
        (row2, jax.ShapeDtypeStruct((2, n, HG_W), BF16)),
        (row2, jax.ShapeDtypeStruct((2, n, HG_W), F32)),
        (row(HG_W), jax.ShapeDtypeStruct((n, HG_W), BF16)),
        (row(HG_W), jax.ShapeDtypeStruct((n, HG_W), BF16)),
        (row(CM_W), jax.ShapeDtypeStruct((n, CM_W), BF16)),
        (row(CM_W), jax.ShapeDtypeStruct((n, CM_W), BF16)),
        (row(D_MODEL), jax.ShapeDtypeStruct((n, D_MODEL), BF16)),
        (row(D_MODEL), jax.ShapeDtypeStruct((n, D_MODEL), BF16)),
    ]
    return pl.pallas_call(
        _proj_lat_kernel,
        grid=(n // tm,),
        in_specs=[row(D_MODEL), _mod_spec(tpb, 0), _mod_spec(tpb, 1), full(g_pre), full(w_in_b),
                  full(lbl), full(ln_g), full(ln_b)],
        out_specs=[o[0] for o in outs],
        out_shape=[o[1] for o in outs],
        compiler_params=_params(("parallel",)),
        name="proj_lat",
    )(x2, mod3, mod3, g_pre, w_in_b, lbl, ln_g, ln_b)


def _proj_ctx_kernel(x_ref, sh_ref, sc_ref, g_ref, w_ref, lbl_ref, k_ref, lf_ref, v_ref):
    hb = _prenorm(x_ref, sh_ref, sc_ref, g_ref)
    lbs = _lower_bounds(lbl_ref[...])
    for d in range(2):
        _gates(_dot(hb, w_ref[:, d * HG_W:(d + 1) * HG_W]), lbs[d], k_ref, lf_ref, d)
    v_ref[...] = _dot(hb, w_ref[:, 2 * HG_W:3 * HG_W]).astype(v_ref.dtype)


def _proj_ctx(c2, mod3, ctx_row, g_pre, w_ctx_b, lbl, tm):
    n = c2.shape[0]
    row = lambda w: pl.BlockSpec((tm, w), lambda i: (i, 0))
    row2 = pl.BlockSpec((2, tm, HG_W), lambda i: (0, i, 0))
    full = lambda a: pl.BlockSpec(a.shape, lambda i: (0,) * a.ndim)
    mod = lambda col: pl.BlockSpec((None, 1, D_MODEL), lambda i: (ctx_row, 0, col))
    return pl.pallas_call(
        _proj_ctx_kernel,
        grid=(n // tm,),
        in_specs=[row(D_MODEL), mod(0), mod(1), full(g_pre), full(w_ctx_b), full(lbl)],
        out_specs=[row2, row2, row(HG_W)],
        out_shape=[jax.ShapeDtypeStruct((2, n, HG_W), BF16), jax.ShapeDtypeStruct((2, n, HG_W), F32),
                   jax.ShapeDtypeStruct((n, HG_W), BF16)],
        compiler_params=_params(("parallel",)),
        name="proj_ctx",
    )(c2, mod3, mod3, g_pre, w_ctx_b, lbl)


def _scan_tables():
    C = SCAN_CHUNK
    t = np.arange(C)
    lmats, lvls = [], []
    for d in range(2):
        p = t if d == 0 else C - 1 - t
        pt, ps = p[:, None], p[None, :]
        lmat = (ps <= pt).astype(np.float32)
        lmats.append(np.concatenate([lmat, lmat], axis=1))
        lvl = np.full((C, C), -1, np.int32)
        lvl[(pt // SUB == ps // SUB) & (ps <= pt)] = 0
        half, idx = SUB, 1
        while half < C:
            span = 2 * half
            lvl[(pt // span == ps // span) & ((pt // half) % 2 == 1) & ((ps // half) % 2 == 0)] = idx
            half, idx = span, idx + 1
        lvls.append(lvl)
    return jnp.asarray(np.stack(lmats), dtype=BF16), jnp.asarray(np.stack(lvls))


def _scan_step(d, lmat_ref, lvl_ref, k_ref, lf_ref, v_ref, st_ref, b_scr, q_ref=None, o_ref=None):
    C = SCAN_CHUNK
    lf = lf_ref[...]
    hi = lf.astype(BF16)
    lo = (lf - hi.astype(F32)).astype(BF16)
    b_scr[d] = _dot(lmat_ref[d], jnp.concatenate([hi, lo], axis=0))
    b = b_scr[d]

    def row(i):
        return b_scr[d, pl.ds(i, 1), :]

    b_last = row(C - 1 if d == 0 else 0)
    k = k_ref[...]
    v = v_ref[...]
    khat = k * jnp.exp(b_last - b).astype(BF16)
    decay = jnp.exp(b_last)

    if q_ref is not None:
        q = q_ref[...]
        qhat = q * jnp.exp(b).astype(BF16)
        e0 = jnp.concatenate([b[m * SUB:(m + 1) * SUB] - row(m * SUB + SUB // 2 - 1 + d)
                              for m in range(C // SUB)], axis=0)
        factors = [(jnp.exp(e0).astype(BF16), jnp.exp(-e0).astype(BF16))]
        half = SUB
        while half < C:
            span = 2 * half
            e = jnp.concatenate([b[m * span:(m + 1) * span] - row(m * span + half - 1 + d)
                                 for m in range(C // span)], axis=0)
            w = jnp.exp(-jnp.abs(e)).astype(BF16)
            factors.append((w, w))
            half = span
        lvl = lvl_ref[d]
        masks = [lvl == i for i in range(len(factors))]

    for h in range(HG_HEADS):
        sl = slice(h * HG_DK, (h + 1) * HG_DK)
        st = st_ref[d, :, sl]
        if q_ref is not None:
            a = jnp.zeros((C, C), F32)
            for (wq, wk), mask in zip(factors, masks):
                a = jnp.where(mask, _dot_nt(q[:, sl] * wq[:, sl], k[:, sl] * wk[:, sl]), a)
            o = _dot(a.astype(BF16), v[:, sl]) + _dot_nt(qhat[:, sl], st.astype(BF16))
            o_ref[:, sl] = o.astype(o_ref.dtype)
        st_ref[d, :, sl] = decay[:, sl] * st + _dot_tn(v[:, sl], khat[:, sl])


def _scan_kernel(n_ctx_steps, lmat_ref, lvl_ref, q_f, k_f, lf_f, v_f, q_b, k_b, lf_b, v_b,
                 kc_f, lfc_f, vc_f, kc_b, lfc_b, vc_b, o_f, o_b, st_ref, b_scr):
    s = pl.program_id(1)

    @pl.when(s == 0)
    def _():
        st_ref[...] = jnp.zeros_like(st_ref)

    @pl.when(s < n_ctx_steps)
    def _():
        _scan_step(0, lmat_ref, lvl_ref, kc_f, lfc_f, vc_f, st_ref, b_scr)
        _scan_step(1, lmat_ref, lvl_ref, kc_b, lfc_b, vc_b, st_ref, b_scr)

    @pl.when(s >= n_ctx_steps)
    def _():
        _scan_step(0, lmat_ref, lvl_ref, k_f, lf_f, v_f, st_ref, b_scr, q_f, o_f)
        _scan_step(1, lmat_ref, lvl_ref, k_b, lf_b, v_b, st_ref, b_scr, q_b, o_b)


def _hgrn_scan(q, k2, lf2, v, kc2, lfc2, vc, batch, seq, ctx_len):
    C = SCAN_CHUNK
    n_lat, n_ctx = seq // C, ctx_len // C
    lmat, lvl = _scan_tables()

    def lat_blk(d):
        def blk(b, s):
            j = jnp.maximum(s - n_ctx, 0)
            return b * n_lat + (j if d == 0 else n_lat - 1 - j)
        return blk

    def ctx_blk(d):
        def blk(b, s):
            i = jnp.minimum(s, n_ctx - 1)
            return b * n_ctx + (i if d == 0 else n_ctx - 1 - i)
        return blk

    def specs(blk_of, with_q):
        out = []
        for d in range(2):
            blk = blk_of(d)
            plain = pl.BlockSpec((C, HG_W), lambda b, s, blk=blk: (blk(b, s), 0))
            per_dir = pl.BlockSpec((None, C, HG_W), lambda b, s, blk=blk, d=d: (d, blk(b, s), 0))
            out += ([plain] if with_q else []) + [per_dir, per_dir, plain]
        return out

    full = lambda a: pl.BlockSpec(a.shape, lambda b, s: (0,) * a.ndim)
    o_specs = [pl.BlockSpec((C, HG_W), lambda b, s, blk=lat_blk(d): (blk(b, s), 0)) for d in range(2)]
    o_shape = jax.ShapeDtypeStruct((batch * seq, HG_W), BF16)
    return pl.pallas_call(
        functools.partial(_scan_kernel, n_ctx),
        grid=(batch, n_ctx + n_lat),
        in_specs=[full(lmat), full(lvl)] + specs(lat_blk, True) + specs(ctx_blk, False),
        out_specs=o_specs,
        out_shape=[o_shape, o_shape],
        scratch_shapes=[pltpu.VMEM((2, HG_DK, HG_W), F32), pltpu.VMEM((2, C, HG_W), F32)],
        compiler_params=_params(("parallel", "arbitrary")),
        name="hgrn_scan",
    )(lmat, lvl, q, k2, lf2, v, q, k2, lf2, v, kc2, lfc2, vc, kc2, lfc2, vc)


def _merge_kernel(of_ref, ob_ref, sg_ref, u_ref, vn_ref, sga_ref, sgb_ref, x_ref, gt1_ref, sh2_ref, sc2_ref,
                  gout_ref, ws_ref, bs_ref, wa_ref, wb_ref, wo_ref, gpost_ref, gffn_ref, wr_ref,
                  x1_ref, h2_ref, lg_ref):
    tm = x_ref.shape[0]
    o = of_ref[...].astype(F32) + ob_ref[...].astype(F32)
    sg = sg_ref[...].astype(F32)
    gout = gout_ref[...]
    a = jnp.concatenate(
        [_rms(o[:, h * HG_DK:(h + 1) * HG_DK], gout) * sg[:, h * HG_DK:(h + 1) * HG_DK] for h in range(HG_HEADS)],
        axis=1).astype(BF16)
    vn = vn_ref[...]
    gw = CM_W // CM_GROUPS
    z = jnp.concatenate(
        [jnp.concatenate([_dot(ws_ref[g], vn[c * CM_CHUNK:(c + 1) * CM_CHUNK, g * gw:(g + 1) * gw])
                          for g in range(CM_GROUPS)], axis=1) + bs_ref[...]
         for c in range(tm // CM_CHUNK)], axis=0)
    bm = (u_ref[...].astype(F32) * z).astype(BF16)
    y = sga_ref[...].astype(F32) * _dot(a, wa_ref[...]) + sgb_ref[...].astype(F32) * _dot(bm, wb_ref[...])
    yo = _dot(y.astype(BF16), wo_ref[...])
    x1 = x_ref[...] + gt1_ref[...] * _rms(yo, gpost_ref[...])
    x1_ref[...] = x1
    h2 = (_rms(x1, gffn_ref[...]) * (1.0 + sc2_ref[...]) + sh2_ref[...]).astype(BF16)
    h2_ref[...] = h2
    lg_ref[...] = _dot_nt(wr_ref[...], h2)


def _merge(o_f, o_b, sg, u, vn, sga, sgb, x2, mod3, g_out, ws_b, bs_full, wa_b, wb_b, wo_b, g_post, g_ffn, wr_b,
           seq, tm):
    n = x2.shape[0]
    tpb = seq // tm
    row = lambda w: pl.BlockSpec((tm, w), lambda i: (i, 0))
    full = lambda a: pl.BlockSpec(a.shape, lambda i: (0,) * a.ndim)
    return pl.pallas_call(
        _merge_kernel,
        grid=(n // tm,),
        in_specs=[row(HG_W), row(HG_W), row(HG_W), row(CM_W), row(CM_W),
                  row(D_MODEL), row(D_MODEL), row(D_MODEL), _mod_spec(tpb, 2), _mod_spec(tpb, 3),
                  _mod_spec(tpb, 4), full(g_out), full(ws_b), full(bs_full), full(wa_b), full(wb_b),
                  full(wo_b), full(g_post), full(g_ffn), full(wr_b)],
        out_specs=[row(D_MODEL), row(D_MODEL), pl.BlockSpec((wr_b.shape[0], tm), lambda i: (0, i))],
        out_shape=[jax.ShapeDtypeStruct((n, D_MODEL), F32), jax.ShapeDtypeStruct((n, D_MODEL), BF16),
                   jax.ShapeDtypeStruct((wr_b.shape[0], n), F32)],
        compiler_params=_params(("parallel",)),
        name="merge",
    )(o_f, o_b, sg, u, vn, sga, sgb, x2, mod3, mod3, mod3, g_out, ws_b, bs_full, wa_b, wb_b, wo_b, g_post, g_ffn,
      wr_b)


def _router_kernel(lg_ref, br_ref, w_ref):
    tm = lg_ref.shape[1]
    gsz = N_EXPERTS // N_GROUPS
    scores = jax.nn.sigmoid(lg_ref[:N_EXPERTS, :])
    sel = scores + jnp.concatenate([br_ref[...]] * (tm // br_ref.shape[1]), axis=1)
    neg = -jnp.inf

    def first_max(x, ids, sentinel, axis):
        m = jnp.max(x, axis=axis, keepdims=True)
        return m, jnp.min(jnp.where(x == m, ids, sentinel), axis=axis, keepdims=True)

    sel3 = sel.reshape(N_GROUPS, gsz, tm)
    j3 = lax.broadcasted_iota(jnp.int32, sel3.shape, 1)
    m1, i1 = first_max(sel3, j3, gsz, 1)
    gscore = m1 + jnp.max(jnp.where(j3 == i1, neg, sel3), axis=1, keepdims=True)
    g3 = lax.broadcasted_iota(jnp.int32, gscore.shape, 0)
    keep = jnp.zeros(gscore.shape, F32)
    for _ in range(TOPK_GROUPS):
        _, gi = first_max(gscore, g3, N_GROUPS, 0)
        keep = jnp.where(g3 == gi, 1.0, keep)
        gscore = jnp.where(g3 == gi, neg, gscore)
    x = jnp.where(keep > 0.0, sel3, neg).reshape(N_EXPERTS, tm)
    e_i = lax.broadcasted_iota(jnp.int32, x.shape, 0)
    w = jnp.zeros(x.shape, F32)
    for _ in range(TOP_K):
        _, ei = first_max(x, e_i, N_EXPERTS, 0)
        w = jnp.where(e_i == ei, scores, w)
        x = jnp.where(e_i == ei, neg, x)
    w = w / jnp.sum(w, axis=0, keepdims=True) * ROUTED_SCALE
    w_ref[...] = jnp.concatenate([w, jnp.zeros_like(w)], axis=0).T


def _router(logits_t, b_router_cols, tm):
    rows, n = logits_t.shape
    return pl.pallas_call(
        _router_kernel,
        grid=(n // tm,),
        in_specs=[pl.BlockSpec((rows, tm), lambda i: (0, i)),
                  pl.BlockSpec(b_router_cols.shape, lambda i: (0, 0))],
        out_specs=pl.BlockSpec((tm, rows), lambda i: (i, 0)),
        out_shape=jax.ShapeDtypeStruct((n, rows), F32),
        compiler_params=_params(("parallel",)),
        name="router",
    )(logits_t, b_router_cols)


MOE_TILE = 256
MOE_UNIT = 16
MOE_BLOCK = 512
MOE_MM_ROWS = 256
UNITS_PER_BLOCK = MOE_BLOCK // MOE_UNIT
TILE_ROWS = 3072
TILE_UNITS = TILE_ROWS // MOE_UNIT
ROW_CHUNK = 512
KEY_W = 128
DIGIT_BITS = 6
DIGIT = 1 << DIGIT_BITS


def _swiglu_act(h, w_gu):
    gu = _dot(h, w_gu)
    de = gu.shape[1] // 2
    g = gu[:, :de]
    return g * jax.nn.sigmoid(g) * gu[:, de:]


def _token_keys(cw, starts_row):
    t = cw.shape[0]
    routed = cw > 0.0
    t_i = lax.broadcasted_iota(jnp.int32, (t, t), 0)
    s_i = lax.broadcasted_iota(jnp.int32, (t, t), 1)
    rank = _dot((s_i < t_i).astype(BF16), routed.astype(BF16))
    pos = (starts_row + rank).astype(jnp.int32)
    lane = lax.broadcasted_iota(jnp.int32, cw.shape, 1)
    hi = jnp.where(routed, jnp.right_shift(pos, DIGIT_BITS), -1)
    lo = jnp.where(routed, jnp.bitwise_and(pos, DIGIT - 1), -1)
    key_hi = jnp.where(lane < N_EXPERTS, hi, jnp.where(lane == N_EXPERTS, -1, 0))
    key_lo = jnp.where(lane < N_EXPERTS, lo, jnp.where(lane == N_EXPERTS + 1, -1, 0))
    return key_hi.astype(F32).astype(BF16), key_lo.astype(F32).astype(BF16)


def _segment_units(counts):
    return jnp.floor((counts + (MOE_UNIT - 1)) * (1.0 / MOE_UNIT))


def _dispatch_kernel(h_ref, cw_ref, digits_ref, xs_ref, cnt_ref):
    cw = cw_ref[...]
    t = cw.shape[0]
    routed = (cw > 0.0).astype(BF16)
    counts = _dot(jnp.ones((8, t), BF16), routed)
    cnt_ref[...] = counts.astype(jnp.int32)
    units = _segment_units(counts)
    e_i = lax.broadcasted_iota(jnp.int32, (KEY_W, KEY_W), 0)
    f_i = lax.broadcasted_iota(jnp.int32, (KEY_W, KEY_W), 1)
    starts = _dot(units.astype(BF16), (e_i < f_i).astype(BF16)) * MOE_UNIT
    ends = starts + units * MOE_UNIT
    key_hi, key_lo = _token_keys(cw, starts[:1])
    h = h_ref[...]
    lane = lax.broadcasted_iota(jnp.int32, (ROW_CHUNK, KEY_W), 1)
    for c in range(TILE_ROWS // ROW_CHUNK):
        rows = slice(c * ROW_CHUNK, (c + 1) * ROW_CHUNK)
        r = (lax.broadcasted_iota(jnp.int32, (ROW_CHUNK, KEY_W), 0) + c * ROW_CHUNK).astype(F32)
        in_seg = (r >= starts[:1]) & (r < ends[:1])
        rmap = jnp.where(lane < N_EXPERTS, in_seg.astype(F32), digits_ref[rows, :].astype(F32)).astype(BF16)
        hit = (_dot_nt(rmap, key_hi) == 0.0) & (_dot_nt(rmap, key_lo) == 0.0)
        xs_ref[rows, :] = _dot(hit.astype(BF16), h).astype(xs_ref.dtype)


def _dispatch(h2, cw, digits):
    n = h2.shape[0]
    n_tiles = n // MOE_TILE
    return pl.pallas_call(
        _dispatch_kernel,
        grid=(n_tiles,),
        in_specs=[pl.BlockSpec((MOE_TILE, D_MODEL), lambda i: (i, 0)),
                  pl.BlockSpec((MOE_TILE, KEY_W), lambda i: (i, 0)),
                  pl.BlockSpec(digits.shape, lambda i: (0, 0))],
        out_specs=[pl.BlockSpec((TILE_ROWS, D_MODEL), lambda i: (i, 0)),
                   pl.BlockSpec((8, KEY_W), lambda i: (i, 0))],
        out_shape=[jax.ShapeDtypeStruct((n_tiles * TILE_ROWS, D_MODEL), BF16),
                   jax.ShapeDtypeStruct((n_tiles * 8, KEY_W), jnp.int32)],
        compiler_params=_params(("parallel",)),
        name="moe_dispatch",
    )(h2, cw, digits)


def _unit_copy(src_hbm, unit, dst, slot, pos, sem):
    return pltpu.make_async_copy(
        src_hbm.at[pl.ds(pl.multiple_of(unit * MOE_UNIT, MOE_UNIT), MOE_UNIT)],
        dst.at[slot, pl.ds(pos * MOE_UNIT, MOE_UNIT)], sem.at[slot])


def _experts_kernel(be_ref, src_ref, nb_ref, xs_hbm, wgu_ref, wdn_ref, ys_ref, xbuf, sem, wgu_b, wdn_b):
    j = pl.program_id(0)
    nb = nb_ref[0]

    def copies(blk, slot):
        return [_unit_copy(xs_hbm, src_ref[blk * UNITS_PER_BLOCK + u], xbuf, slot, u, sem)
                for u in range(UNITS_PER_BLOCK)]

    def fetch(blk, slot):
        for cp in copies(blk, slot):
            cp.start()

    @pl.when(j == 0)
    def _():
        fetch(0, 0)

    @pl.when((j == 0) | (be_ref[j] != be_ref[jnp.maximum(j - 1, 0)]))
    def _():
        wgu_b[...] = wgu_ref[...].astype(BF16)
        wdn_b[...] = wdn_ref[...].astype(BF16)

    @pl.when(j < nb)
    def _():
        slot = j % 2
        for cp in copies(j, slot):
            cp.wait()
        fetch(jnp.minimum(j + 1, nb - 1), 1 - slot)
        for g in range(MOE_BLOCK // MOE_MM_ROWS):
            rows = pl.ds(g * MOE_MM_ROWS, MOE_MM_ROWS)
            act = _swiglu_act(xbuf[slot, rows, :], wgu_b[...])
            ys_ref[rows, :] = _dot(act.astype(BF16), wdn_b[...]).astype(ys_ref.dtype)

    @pl.when(j == nb - 1)
    def _():
        for cp in copies(j, (j + 1) % 2):
            cp.wait()

    @pl.when(j >= nb)
    def _():
        ys_ref[...] = jnp.zeros_like(ys_ref)


def _experts(xs, block_expert, src_units, n_blocks_used, w_gu, w_dn):
    nb_max = block_expert.shape[0]
    grid_spec = pltpu.PrefetchScalarGridSpec(
        num_scalar_prefetch=3,
        grid=(nb_max,),
        in_specs=[pl.BlockSpec(memory_space=pl.ANY),
                  pl.BlockSpec((None, D_MODEL, 2 * D_EXPERT), lambda j, be, src, nb: (be[j], 0, 0)),
                  pl.BlockSpec((None, D_EXPERT, D_MODEL), lambda j, be, src, nb: (be[j], 0, 0))],
        out_specs=pl.BlockSpec((MOE_BLOCK, D_MODEL), lambda j, be, src, nb: (j, 0)),
        scratch_shapes=[pltpu.VMEM((2, MOE_BLOCK, D_MODEL), BF16), pltpu.SemaphoreType.DMA((2,)),
                        pltpu.VMEM((D_MODEL, 2 * D_EXPERT), BF16), pltpu.VMEM((D_EXPERT, D_MODEL), BF16)],
    )
    return pl.pallas_call(
        _experts_kernel,
        grid_spec=grid_spec,
        out_shape=jax.ShapeDtypeStruct((nb_max * MOE_BLOCK, D_MODEL), BF16),
        compiler_params=_params(("arbitrary",)),
        name="moe_experts",
    )(block_expert, src_units, n_blocks_used, xs, w_gu, w_dn)


def _combine_kernel(src_ref, ys_hbm, cw_ref, h_ref, x1_ref, gt2_ref, gpost_ref, digits_t_ref, wsgu_ref,
                    wsdn_ref, o_ref, ybuf, sem):
    i = pl.program_id(0)

    def copies(tile, slot):
        return [_unit_copy(ys_hbm, src_ref[tile * TILE_UNITS + u], ybuf, slot, u, sem)
                for u in range(TILE_UNITS)]

    def fetch(tile, slot):
        for cp in copies(tile, slot):
            cp.start()

    @pl.when(i == 0)
    def _():
        fetch(0, 0)

    @pl.when(i + 1 < pl.num_programs(0))
    def _():
        fetch(i + 1, (i + 1) % 2)

    cw = cw_ref[...]
    t = cw.shape[0]
    routed = (cw > 0.0).astype(BF16)
    e_i = lax.broadcasted_iota(jnp.int32, (KEY_W, KEY_W), 0)
    f_i = lax.broadcasted_iota(jnp.int32, (KEY_W, KEY_W), 1)
    units = _segment_units(_dot_tn(routed, jnp.ones((t, KEY_W), BF16)))
    starts = _dot((f_i < e_i).astype(BF16), units.astype(BF16)) * MOE_UNIT
    ends = starts + units * MOE_UNIT
    units_row = _segment_units(_dot(jnp.ones((8, t), BF16), routed))
    starts_row = _dot(units_row.astype(BF16), (e_i < f_i).astype(BF16)) * MOE_UNIT
    key_hi, key_lo = _token_keys(cw, starts_row[:1])
    wb = cw.astype(BF16)

    f = _dot(_swiglu_act(h_ref[...], wsgu_ref[...]).astype(BF16), wsdn_ref[...])
    slot = i % 2
    for cp in copies(i, slot):
        cp.wait()
    reps = ROW_CHUNK // KEY_W
    starts_c = jnp.concatenate([starts] * reps, axis=1)
    ends_c = jnp.concatenate([ends] * reps, axis=1)
    sub = lax.broadcasted_iota(jnp.int32, (KEY_W, ROW_CHUNK), 0)
    for c in range(TILE_ROWS // ROW_CHUNK):
        rows = slice(c * ROW_CHUNK, (c + 1) * ROW_CHUNK)
        r = (lax.broadcasted_iota(jnp.int32, (KEY_W, ROW_CHUNK), 1) + c * ROW_CHUNK).astype(F32)
        in_seg = (r >= starts_c) & (r < ends_c)
        rmap_t = jnp.where(sub < N_EXPERTS, in_seg.astype(F32), digits_t_ref[:, rows].astype(F32)).astype(BF16)
        hit = (_dot(key_hi, rmap_t) == 0.0) & (_dot(key_lo, rmap_t) == 0.0)
        w = _dot(wb, rmap_t)
        f = f + _dot(jnp.where(hit, w, 0.0).astype(BF16), ybuf[slot, rows, :])
    o_ref[...] = x1_ref[...] + gt2_ref[...] * _rms(f, gpost_ref[...])


def _combine(ys, src_units, cw, h2, x1, mod3, g_post, digits_t, wsgu_b, wsdn_b, seq):
    n = h2.shape[0]
    tpb = seq // MOE_TILE
    row = lambda w: pl.BlockSpec((MOE_TILE, w), lambda i, src: (i, 0))
    full = lambda a: pl.BlockSpec(a.shape, lambda i, src: (0,) * a.ndim)
    grid_spec = pltpu.PrefetchScalarGridSpec(
        num_scalar_prefetch=1,
        grid=(n // MOE_TILE,),
        in_specs=[pl.BlockSpec(memory_space=pl.ANY), row(KEY_W), row(D_MODEL), row(D_MODEL),
                  pl.BlockSpec((None, 1, D_MODEL), lambda i, src: (i // tpb, 0, 5)), full(g_post),
                  full(digits_t), full(wsgu_b), full(wsdn_b)],
        out_specs=row(D_MODEL),
        scratch_shapes=[pltpu.VMEM((2, TILE_ROWS, D_MODEL), BF16), pltpu.SemaphoreType.DMA((2,))],
    )
    return pl.pallas_call(
        _combine_kernel,
        grid_spec=grid_spec,
        out_shape=jax.ShapeDtypeStruct((n, D_MODEL), F32),
        compiler_params=_params(("arbitrary",)),
        name="moe_combine",
    )(src_units, ys, cw, h2, x1, mod3, g_post, digits_t, wsgu_b, wsdn_b)


def _row_digits():
    r = np.arange(TILE_ROWS)
    d = np.zeros((TILE_ROWS, KEY_W), np.float32)
    d[:, N_EXPERTS] = r // DIGIT
    d[:, N_EXPERTS + 1] = r % DIGIT
    return jnp.asarray(d, dtype=BF16)


def _moe_plan(counts, nb_max):
    n_tiles = counts.shape[0]
    s = (counts + (MOE_UNIT - 1)) // MOE_UNIT
    local = jnp.cumsum(s, axis=1) - s
    cs = jnp.cumsum(s, axis=0)
    per_expert = cs[-1]
    padded = (per_expert + UNITS_PER_BLOCK - 1) // UNITS_PER_BLOCK * UNITS_PER_BLOCK
    g_end = jnp.cumsum(padded)
    g_start = g_end - padded
    seg_start = g_start[None, :] + cs - s
    n_blocks_used = (g_end[-1] // UNITS_PER_BLOCK).astype(jnp.int32).reshape(1)
    jb = jnp.arange(nb_max, dtype=jnp.int32)
    one_e = ((jb[:, None] >= (g_start // UNITS_PER_BLOCK)[None, :])
             & (jb[:, None] < (g_end // UNITS_PER_BLOCK)[None, :])).astype(jnp.int32)
    pick_e = lambda table: jnp.sum(one_e[:, :, None] * table.T[None, :, :], axis=1)
    block_expert = jnp.where(jb < n_blocks_used[0], jnp.sum(one_e * jnp.arange(N_EXPERTS, dtype=jnp.int32), axis=1),
                             N_EXPERTS - 1).astype(jnp.int32)
    cs_b, s_b, local_b = pick_e(cs), pick_e(s), pick_e(local)
    q = (jb * UNITS_PER_BLOCK - jnp.sum(one_e * g_start[None, :], axis=1))[:, None] \
        + jnp.arange(UNITS_PER_BLOCK, dtype=jnp.int32)[None, :]
    tile = jnp.minimum(jnp.sum(cs_b[:, None, :] <= q[:, :, None], axis=2), n_tiles - 1)
    one_t = (tile[:, :, None] == jnp.arange(n_tiles, dtype=jnp.int32)).astype(jnp.int32)
    src = tile * TILE_UNITS + q + jnp.sum(one_t * (local_b - cs_b + s_b)[:, None, :], axis=2)
    valid = q < jnp.sum(one_e * per_expert[None, :], axis=1)[:, None]
    src_units = jnp.where(valid, src, 0).astype(jnp.int32).reshape(-1)
    u = jnp.arange(TILE_UNITS, dtype=jnp.int32)
    seg_end = local + s
    eu = jnp.minimum(jnp.sum(seg_end[:, None, :] <= u[None, :, None], axis=2), N_EXPERTS - 1)
    one_u = (eu[:, :, None] == jnp.arange(N_EXPERTS, dtype=jnp.int32)).astype(jnp.int32)
    back = u[None, :] + jnp.sum(one_u * (seg_start - local)[:, None, :], axis=2)
    back_units = jnp.where(u[None, :] < seg_end[:, -1:], back, 0).astype(jnp.int32).reshape(-1)
    return block_expert, src_units, n_blocks_used, back_units


def _tile(n, pref):
    t = pref
    while n % t:
        t //= 2
    return t


def kernel(x, c, ctx, c_ctx, w_ada, b_ada, g_pre_mix, g_post_mix, g_pre_ffn, g_post_ffn, w_in, lb_logits, g_hgrn_out, cm_ln_g, cm_ln_b, w_spatial, b_spatial, w_branch_a, w_branch_b, w_out, w_router, b_router, w_expert_gu, w_expert_down, w_shared_gu, w_shared_down):
    B, T, D = x.shape
    L = ctx.shape[1]
    assert D == D_MODEL and w_ada.shape[0] == 1 and T % SCAN_CHUNK == 0 and L % SCAN_CHUNK == 0
    l = 0
    row = lambda a: a[l].reshape(1, -1)

    n_rows = -(-(B + 1) // 16) * 16
    cs = jnp.zeros((n_rows, D), F32).at[:B].set(c).at[B].set(c_ctx)
    mod3 = _ada_mod(cs, w_ada[l], row(b_ada)).reshape(n_rows, 1, 6 * D)

    w_in_b = w_in[l].astype(BF16)
    lbl = lb_logits[:, l:l + 2].reshape(4, HG_W)
    x2 = x.reshape(B * T, D)
    q, k2, lf2, v, sg, u, vn, sga, sgb = _proj_lat(
        x2, mod3, row(g_pre_mix), w_in_b, lbl, row(cm_ln_g), row(cm_ln_b), T, _tile(T, 256))
    kc2, lfc2, vc = _proj_ctx(ctx.reshape(B * L, D), mod3, B, row(g_pre_mix), w_in_b[:, HG_W:4 * HG_W], lbl,
                              _tile(B * L, 256))

    o_f, o_b = _hgrn_scan(q, k2, lf2, v, kc2, lfc2, vc, B, T, L)

    bs_full = jnp.repeat(b_spatial[l], CM_W // CM_GROUPS, axis=1)
    x1, h2, logits = _merge(
        o_f, o_b, sg, u, vn, sga, sgb, x2, mod3, row(g_hgrn_out), w_spatial[l].astype(BF16), bs_full,
        w_branch_a[l].astype(BF16), w_branch_b[l].astype(BF16), w_out[l].astype(BF16), row(g_post_mix),
        row(g_pre_ffn), jnp.pad(w_router[l].T, ((0, KEY_W - N_EXPERTS), (0, 0))).astype(BF16), T, _tile(T, 512))

    cw = _router(logits, jnp.broadcast_to(b_router[l][:, None], (N_EXPERTS, 128)), _tile(B * T, 512))

    n_tok = B * T
    n_tiles = n_tok // MOE_TILE
    digits = _row_digits()
    xs, cnt = _dispatch(h2, cw, digits)
    counts = cnt.reshape(n_tiles, 8, KEY_W)[:, 0, :N_EXPERTS]
    max_units = (n_tok * TOP_K + n_tiles * N_EXPERTS * (MOE_UNIT - 1)) // MOE_UNIT + N_EXPERTS * (UNITS_PER_BLOCK - 1)
    nb_max = -(-max_units // UNITS_PER_BLOCK)
    block_expert, src_units, n_blocks_used, back_units = _moe_plan(counts, nb_max)
    ys = _experts(xs, block_expert, src_units, n_blocks_used, w_expert_gu[l], w_expert_down[l])
    out = _combine(ys, back_units, cw, h2, x1, mod3, row(g_post_ffn), digits.T,
                   w_shared_gu[l].astype(BF16), w_shared_down[l].astype(BF16), T)
    return out.reshape(B, T, D)
```

```python
import functools

import numpy as np
import jax
import jax.numpy as jnp
from jax import lax
from jax.experimental import pallas as pl
from jax.experimental.pallas import tpu as pltpu

F32 = jnp.float32
BF16 = jnp.bfloat16

D_MODEL = 1024
EPS = 1e-6
HG_HEADS = 4
HG_DK = 128
HG_W = HG_HEADS * HG_DK
CM_W = 512
CM_CHUNK = 128
CM_GROUPS = 4
D_IN = 5 * HG_W + 2 * CM_W + 2 * D_MODEL
N_EXPERTS = 64
TOP_K = 8
N_GROUPS = 8
GROUP_BITS = 3
TOPK_GROUPS = 4
D_EXPERT = 256
ROUTED_SCALE = 2.5
SCAN_CHUNK = 128
SUB = 16
VMEM_LIMIT = 56 * 1024 * 1024


def _params(sem):
    return pltpu.CompilerParams(dimension_semantics=sem, vmem_limit_bytes=VMEM_LIMIT)


def _dot(a, b):
    return jnp.dot(a, b, preferred_element_type=F32)


def _dot_nt(a, b):
    return lax.dot_general(a, b, (((1,), (1,)), ((), ())), preferred_element_type=F32)


def _dot_tn(a, b):
    return lax.dot_general(a, b, (((0,), (0,)), ((), ())), preferred_element_type=F32)


def _rms(x, g):
    return x * lax.rsqrt(jnp.mean(x * x, axis=-1, keepdims=True) + EPS) * g


def _ada_kernel(c_ref, w_ref, b_ref, o_ref):
    c = c_ref[...]
    s = c * jax.nn.sigmoid(c)
    o_ref[...] = _dot(s.astype(BF16), w_ref[...].astype(BF16)) + b_ref[...]


def _ada_mod(cs, w_ada, b_ada):
    rows = cs.shape[0]
    n_out = w_ada.shape[1]
    return pl.pallas_call(
        _ada_kernel,
        grid=(n_out // D_MODEL,),
        in_specs=[
            pl.BlockSpec((rows, D_MODEL), lambda j: (0, 0)),
            pl.BlockSpec((D_MODEL, D_MODEL), lambda j: (0, j)),
            pl.BlockSpec((1, D_MODEL), lambda j: (0, j)),
        ],
        out_specs=pl.BlockSpec((rows, D_MODEL), lambda j: (0, j)),
        out_shape=jax.ShapeDtypeStruct((rows, n_out), F32),
        compiler_params=_params(("parallel",)),
        name="ada_mod",
    )(cs, w_ada, b_ada)


def _lower_bounds(lbl):
    out = []
    for d in range(2):
        l0, l1 = lbl[2 * d:2 * d + 1], lbl[2 * d + 1:2 * d + 2]
        m = jnp.maximum(l0, l1)
        e0, e1 = jnp.exp(l0 - m), jnp.exp(l1 - m)
        out.append(e0 / (e0 + e1))
    return out


def _prenorm(x_ref, sh_ref, sc_ref, g_ref):
    return (_rms(x_ref[...], g_ref[...]) * (1.0 + sc_ref[...]) + sh_ref[...]).astype(BF16)


def _gates(z, lb, k_ref, lf_ref, d):
    k_ref[d] = ((1.0 - lb) * jax.nn.sigmoid(-z)).astype(k_ref.dtype)
    lf_ref[d] = jnp.log2(lb + (1.0 - lb) * jax.nn.sigmoid(z))


def _proj_lat_kernel(x_ref, sh_ref, sc_ref, g_ref, w_ref, lbl_ref, lng_ref, lnb_ref,
                     q_ref, k_ref, lf_ref, v_ref, sg_ref, u_ref, vn_ref, sga_ref, sgb_ref):
    hb = _prenorm(x_ref, sh_ref, sc_ref, g_ref)
    lbs = _lower_bounds(lbl_ref[...])

    def mm(lo, width):
        return _dot(hb, w_ref[:, lo:lo + width])

    z = mm(0, HG_W)
    q_ref[...] = (z * jax.nn.sigmoid(z)).astype(q_ref.dtype)
    for d in range(2):
        _gates(mm((1 + d) * HG_W, HG_W), lbs[d], k_ref, lf_ref, d)
    v_ref[...] = mm(3 * HG_W, HG_W).astype(v_ref.dtype)
    z = mm(4 * HG_W, HG_W)
    sg_ref[...] = (z * jax.nn.sigmoid(z)).astype(sg_ref.dtype)
    u_ref[...] = jax.nn.gelu(mm(5 * HG_W, CM_W)).astype(u_ref.dtype)
    vv = jax.nn.gelu(mm(5 * HG_W + CM_W, CM_W))
    vc = vv - jnp.mean(vv, axis=-1, keepdims=True)
    vn = vc * lax.rsqrt(jnp.mean(vc * vc, axis=-1, keepdims=True) + EPS)
    vn_ref[...] = (vn * lng_ref[...] + lnb_ref[...]).astype(vn_ref.dtype)
    base = 5 * HG_W + 2 * CM_W
    sga_ref[...] = jax.nn.sigmoid(mm(base, D_MODEL)).astype(sga_ref.dtype)
    sgb_ref[...] = jax.nn.sigmoid(mm(base + D_MODEL, D_MODEL)).astype(sgb_ref.dtype)


def _mod_spec(rows_per_batch_tiles, col):
    return pl.BlockSpec((None, 1, D_MODEL), lambda i: (i // rows_per_batch_tiles, 0, col))


def _proj_lat(x2, mod3, g_pre, w_in_b, lbl, ln_g, ln_b, seq, tm):
    n = x2.shape[0]
    tpb = seq // tm
    row = lambda w: pl.BlockSpec((tm, w), lambda i: (i, 0))
    row2 = pl.BlockSpec((2, tm, HG_W), lambda i: (0, i, 0))
    full = lambda a: pl.BlockSpec(a.shape, lambda i: (0,) * a.ndim)
    outs = [
        (row(HG_W), jax.ShapeDtypeStruct((n, HG_W), BF16)),
        (row2, jax.ShapeDtypeStruct((2, n, HG_W), BF16)),
        (row2, jax.ShapeDtypeStruct((2, n, HG_W), F32)),
        (row(HG_W), jax.ShapeDtypeStruct((n, HG_W), BF16)),
        (row(HG_W), jax.ShapeDtypeStruct((n, HG_W), BF16)),
        (row(CM_W), jax.ShapeDtypeStruct((n, CM_W), BF16)),
        (row(CM_W), jax.ShapeDtypeStruct((n, CM_W), BF16)),
        (row(D_MODEL), jax.ShapeDtypeStruct((n, D_MODEL), BF16)),
        (row(D_MODEL), jax.ShapeDtypeStruct((n, D_MODEL), BF16)),
    ]
    return pl.pallas_call(
        _proj_lat_kernel,
        grid=(n // tm,),
        in_specs=[row(D_MODEL), _mod_spec(tpb, 0), _mod_spec(tpb, 1), full(g_pre), full(w_in_b),
                  full(lbl), full(ln_g), full(ln_b)],
        out_specs=[o[0] for o in outs],
        out_shape=[o[1] for o in outs],
        compiler_params=_params(("parallel",)),
        name="proj_lat",
    )(x2, mod3, mod3, g_pre, w_in_b, lbl, ln_g, ln_b)


def _proj_ctx_kernel(x_ref, sh_ref, sc_ref, g_ref, w_ref, lbl_ref, k_ref, lf_ref, v_ref):
    hb = _prenorm(x_ref, sh_ref, sc_ref, g_ref)
    lbs = _lower_bounds(lbl_ref[...])
    for d in range(2):
        _gates(_dot(hb, w_ref[:, d * HG_W:(d + 1) * HG_W]), lbs[d], k_ref, lf_ref, d)
    v_ref[...] = _dot(hb, w_ref[:, 2 * HG_W:3 * HG_W]).astype(v_ref.dtype)


def _proj_ctx(c2, mod3, ctx_row, g_pre, w_ctx_b, lbl, tm):
    n = c2.shape[0]
    row = lambda w: pl.BlockSpec((tm, w), lambda i: (i, 0))
    row2 = pl.BlockSpec((2, tm, HG_W), lambda i: (0, i, 0))
    full = lambda a: pl.BlockSpec(a.shape, lambda i: (0,) * a.ndim)
    mod = lambda col: pl.BlockSpec((None, 1, D_MODEL), lambda i: (ctx_row, 0, col))
    return pl.pallas_call(
        _proj_ctx_kernel,
        grid=(n // tm,),
        in_specs=[row(D_MODEL), mod(0), mod(1), full(g_pre), full(w_ctx_b), full(lbl)],
        out_specs=[row2, row2, row(HG_W)],
        out_shape=[jax.ShapeDtypeStruct((2, n, HG_W), BF16), jax.ShapeDtypeStruct((2, n, HG_W), F32),
                   jax.ShapeDtypeStruct((n, HG_W), BF16)],
        compiler_params=_params(("parallel",)),
        name="proj_ctx",
    )(c2, mod3, mod3, g_pre, w_ctx_b, lbl)


def _scan_tables():
    C = SCAN_CHUNK
    t = np.arange(C)
    lmats, lvls = [], []
    for d in range(2):
        p = t if d == 0 else C - 1 - t
        pt, ps = p[:, None], p[None, :]
        lmat = (ps <= pt).astype(np.float32)
        lmats.append(np.concatenate([lmat, lmat], axis=1))
        lvl = np.full((C, C), -1, np.int32)
        lvl[(pt // SUB == ps // SUB) & (ps <= pt)] = 0
        half, idx = SUB, 1
        while half < C:
            span = 2 * half
            lvl[(pt // span == ps // span) & ((pt // half) % 2 == 1) & ((ps // half) % 2 == 0)] = idx
            half, idx = span, idx + 1
        lvls.append(lvl)
    return jnp.asarray(np.stack(lmats), dtype=BF16), jnp.asarray(np.stack(lvls))


def _scan_step(d, lmat_ref, lvl_ref, k_ref, lf_ref, v_ref, st_ref, b_scr, q_ref=None, o_ref=None):
    C = SCAN_CHUNK
    lf = lf_ref[...]
    hi = lf.astype(BF16)
    lo = (lf - hi.astype(F32)).astype(BF16)
    b_scr[d] = _dot(lmat_ref[d], jnp.concatenate([hi, lo], axis=0))
    b = b_scr[d]

    def row(i):
        return b_scr[d, pl.ds(i, 1), :]

    b_last = row(C - 1 if d == 0 else 0)
    k = k_ref[...]
    v = v_ref[...]
    khat = k * jnp.exp2(b_last - b).astype(BF16)
    decay = jnp.exp2(b_last)

    if q_ref is not None:
        q = q_ref[...]
        qhat = q * jnp.exp2(b).astype(BF16)
        e0 = jnp.concatenate([b[m * SUB:(m + 1) * SUB] - row(m * SUB + SUB // 2 - 1 + d)
                              for m in range(C // SUB)], axis=0)
        factors = [(jnp.exp2(e0).astype(BF16), jnp.exp2(-e0).astype(BF16))]
        half = SUB
        while half < C:
            span = 2 * half
            e = jnp.concatenate([b[m * span:(m + 1) * span] - row(m * span + half - 1 + d)
                                 for m in range(C // span)], axis=0)
            w = jnp.exp2(-jnp.abs(e)).astype(BF16)
            factors.append((w, w))
            half = span
        lvl = lvl_ref[d]
        masks = [lvl == i for i in range(len(factors))]

    for h in range(HG_HEADS):
        sl = slice(h * HG_DK, (h + 1) * HG_DK)
        st = st_ref[d, :, sl]
        if q_ref is not None:
            a = jnp.zeros((C, C), F32)
            for (wq, wk), mask in zip(factors, masks):
                a = jnp.where(mask, _dot_nt(q[:, sl] * wq[:, sl], k[:, sl] * wk[:, sl]), a)
            o = _dot(a.astype(BF16), v[:, sl]) + _dot_nt(qhat[:, sl], st.astype(BF16))
            o_ref[:, sl] = o.astype(o_ref.dtype)
        st_ref[d, :, sl] = decay[:, sl] * st + _dot_tn(v[:, sl], khat[:, sl])


def _scan_kernel(n_ctx_steps, lmat_ref, lvl_ref, q_f, k_f, lf_f, v_f, q_b, k_b, lf_b, v_b,
                 kc_f, lfc_f, vc_f, kc_b, lfc_b, vc_b, o_f, o_b, st_ref, b_scr):
    s = pl.program_id(1)

    @pl.when(s == 0)
    def _():
        st_ref[...] = jnp.zeros_like(st_ref)

    @pl.when(s < n_ctx_steps)
    def _():
        _scan_step(0, lmat_ref, lvl_ref, kc_f, lfc_f, vc_f, st_ref, b_scr)
        _scan_step(1, lmat_ref, lvl_ref, kc_b, lfc_b, vc_b, st_ref, b_scr)

    @pl.when(s >= n_ctx_steps)
    def _():
        _scan_step(0, lmat_ref, lvl_ref, k_f, lf_f, v_f, st_ref, b_scr, q_f, o_f)
        _scan_step(1, lmat_ref, lvl_ref, k_b, lf_b, v_b, st_ref, b_scr, q_b, o_b)


def _hgrn_scan(q, k2, lf2, v, kc2, lfc2, vc, batch, seq, ctx_len):
    C = SCAN_CHUNK
    n_lat, n_ctx = seq // C, ctx_len // C
    lmat, lvl = _scan_tables()

    def lat_blk(d):
        def blk(b, s):
            j = jnp.maximum(s - n_ctx, 0)
            return b * n_lat + (j if d == 0 else n_lat - 1 - j)
        return blk

    def ctx_blk(d):
        def blk(b, s):
            i = jnp.minimum(s, n_ctx - 1)
            return b * n_ctx + (i if d == 0 else n_ctx - 1 - i)
        return blk

    def specs(blk_of, with_q):
        out = []
        for d in range(2):
            blk = blk_of(d)
            plain = pl.BlockSpec((C, HG_W), lambda b, s, blk=blk: (blk(b, s), 0))
            per_dir = pl.BlockSpec((None, C, HG_W), lambda b, s, blk=blk, d=d: (d, blk(b, s), 0))
            out += ([plain] if with_q else []) + [per_dir, per_dir, plain]
        return out

    full = lambda a: pl.BlockSpec(a.shape, lambda b, s: (0,) * a.ndim)
    o_specs = [pl.BlockSpec((C, HG_W), lambda b, s, blk=lat_blk(d): (blk(b, s), 0)) for d in range(2)]
    o_shape = jax.ShapeDtypeStruct((batch * seq, HG_W), BF16)
    return pl.pallas_call(
        functools.partial(_scan_kernel, n_ctx),
        grid=(batch, n_ctx + n_lat),
        in_specs=[full(lmat), full(lvl)] + specs(lat_blk, True) + specs(ctx_blk, False),
        out_specs=o_specs,
        out_shape=[o_shape, o_shape],
        scratch_shapes=[pltpu.VMEM((2, HG_DK, HG_W), F32), pltpu.VMEM((2, C, HG_W), F32)],
        compiler_params=_params(("parallel", "arbitrary")),
        name="hgrn_scan",
    )(lmat, lvl, q, k2, lf2, v, q, k2, lf2, v, kc2, lfc2, vc, kc2, lfc2, vc)


def _merge_kernel(of_ref, ob_ref, sg_ref, u_ref, vn_ref, sga_ref, sgb_ref, x_ref, gt1_ref, sh2_ref, sc2_ref,
                  gout_ref, ws_ref, bs_ref, wa_ref, wb_ref, wo_ref, gpost_ref, gffn_ref, wr_ref,
                  x1_ref, h2_ref, lg_ref):
    tm = x_ref.shape[0]
    o = of_ref[...].astype(F32) + ob_ref[...].astype(F32)
    sg = sg_ref[...].astype(F32)
    gout = gout_ref[...]
    a = jnp.concatenate(
        [_rms(o[:, h * HG_DK:(h + 1) * HG_DK], gout) * sg[:, h * HG_DK:(h + 1) * HG_DK] for h in range(HG_HEADS)],
        axis=1).astype(BF16)
    vn = vn_ref[...]
    gw = CM_W // CM_GROUPS
    z = jnp.concatenate(
        [jnp.concatenate([_dot(ws_ref[g], vn[c * CM_CHUNK:(c + 1) * CM_CHUNK, g * gw:(g + 1) * gw])
                          for g in range(CM_GROUPS)], axis=1) + bs_ref[...]
         for c in range(tm // CM_CHUNK)], axis=0)
    bm = (u_ref[...].astype(F32) * z).astype(BF16)
    y = sga_ref[...].astype(F32) * _dot(a, wa_ref[...]) + sgb_ref[...].astype(F32) * _dot(bm, wb_ref[...])
    yo = _dot(y.astype(BF16), wo_ref[...])
    x1 = x_ref[...] + gt1_ref[...] * _rms(yo, gpost_ref[...])
    x1_ref[...] = x1
    h2 = (_rms(x1, gffn_ref[...]) * (1.0 + sc2_ref[...]) + sh2_ref[...]).astype(BF16)
    h2_ref[...] = h2
    lg_ref[...] = _dot_nt(wr_ref[...], h2)


def _merge(o_f, o_b, sg, u, vn, sga, sgb, x2, mod3, g_out, ws_b, bs_full, wa_b, wb_b, wo_b, g_post, g_ffn, wr_b,
           seq, tm):
    n = x2.shape[0]
    tpb = seq // tm
    row = lambda w: pl.BlockSpec((tm, w), lambda i: (i, 0))
    full = lambda a: pl.BlockSpec(a.shape, lambda i: (0,) * a.ndim)
    return pl.pallas_call(
        _merge_kernel,
        grid=(n // tm,),
        in_specs=[row(HG_W), row(HG_W), row(HG_W), row(CM_W), row(CM_W),
                  row(D_MODEL), row(D_MODEL), row(D_MODEL), _mod_spec(tpb, 2), _mod_spec(tpb, 3),
                  _mod_spec(tpb, 4), full(g_out), full(ws_b), full(bs_full), full(wa_b), full(wb_b),
                  full(wo_b), full(g_post), full(g_ffn), full(wr_b)],
        out_specs=[row(D_MODEL), row(D_MODEL), pl.BlockSpec((wr_b.shape[0], tm), lambda i: (0, i))],
        out_shape=[jax.ShapeDtypeStruct((n, D_MODEL), F32), jax.ShapeDtypeStruct((n, D_MODEL), BF16),
                   jax.ShapeDtypeStruct((wr_b.shape[0], n), F32)],
        compiler_params=_params(("parallel",)),
        name="merge",
    )(o_f, o_b, sg, u, vn, sga, sgb, x2, mod3, mod3, mod3, g_out, ws_b, bs_full, wa_b, wb_b, wo_b, g_post, g_ffn,
      wr_b)


def _router_kernel(lg_ref, br_ref, w_ref):
    tm = lg_ref.shape[1]
    gsz = N_EXPERTS // N_GROUPS
    scores = jax.nn.sigmoid(lg_ref[:N_EXPERTS, :])
    sel = scores + jnp.concatenate([br_ref[...]] * (tm // br_ref.shape[1]), axis=1)
    neg = -jnp.inf

    def first_max(x, ids, sentinel, axis):
        m = jnp.max(x, axis=axis, keepdims=True)
        return m, jnp.min(jnp.where(x == m, ids, sentinel), axis=axis, keepdims=True)

    sel3 = sel.reshape(N_GROUPS, gsz, tm)
    j3 = lax.broadcasted_iota(jnp.int32, sel3.shape, 1)
    m1, i1 = first_max(sel3, j3, gsz, 1)
    gscore = m1 + jnp.max(jnp.where(j3 == i1, neg, sel3), axis=1, keepdims=True)
    g3 = lax.broadcasted_iota(jnp.int32, gscore.shape, 0)
    keep = jnp.zeros(gscore.shape, F32)
    for _ in range(TOPK_GROUPS):
        _, gi = first_max(gscore, g3, N_GROUPS, 0)
        keep = jnp.where(g3 == gi, 1.0, keep)
        gscore = jnp.where(g3 == gi, neg, gscore)
    x = jnp.where(keep > 0.0, sel3, neg).reshape(N_EXPERTS, tm)
    e_i = lax.broadcasted_iota(jnp.int32, x.shape, 0)
    w = jnp.zeros(x.shape, F32)
    for _ in range(TOP_K):
        _, ei = first_max(x, e_i, N_EXPERTS, 0)
        w = jnp.where(e_i == ei, scores, w)
        x = jnp.where(e_i == ei, neg, x)
    w = w / jnp.sum(w, axis=0, keepdims=True) * ROUTED_SCALE
    w_ref[...] = jnp.concatenate([w, jnp.zeros_like(w)], axis=0).T


def _router(logits_t, b_router_cols, tm):
    rows, n = logits_t.shape
    return pl.pallas_call(
        _router_kernel,
        grid=(n // tm,),
        in_specs=[pl.BlockSpec((rows, tm), lambda i: (0, i)),
                  pl.BlockSpec(b_router_cols.shape, lambda i: (0, 0))],
        out_specs=pl.BlockSpec((tm, rows), lambda i: (i, 0)),
        out_shape=jax.ShapeDtypeStruct((n, rows), F32),
        compiler_params=_params(("parallel",)),
        name="router",
    )(logits_t, b_router_cols)


MOE_TILE = 256
MOE_UNIT = 16
MOE_BLOCK = 512
MOE_MM_ROWS = 512
UNITS_PER_BLOCK = MOE_BLOCK // MOE_UNIT
TILE_ROWS = 3072
TILE_UNITS = TILE_ROWS // MOE_UNIT
ROW_CHUNK = 512
N_CHUNKS = TILE_ROWS // ROW_CHUNK
FULL_CHUNKS = MOE_TILE * TOP_K // ROW_CHUNK
CHUNK_UNITS = ROW_CHUNK // MOE_UNIT
KEY_W = 128
DIGIT_BITS = 6
DIGIT = 1 << DIGIT_BITS


def _swiglu_act(h, w_gu):
    gu = _dot(h, w_gu)
    de = gu.shape[1] // 2
    g = gu[:, :de]
    return g * jax.nn.sigmoid(g) * gu[:, de:]


def _token_keys(cw, starts_row):
    t = cw.shape[0]
    routed = cw > 0.0
    t_i = lax.broadcasted_iota(jnp.int32, (t, t), 0)
    s_i = lax.broadcasted_iota(jnp.int32, (t, t), 1)
    rank = _dot((s_i < t_i).astype(BF16), routed.astype(BF16))
    pos = (starts_row + rank).astype(jnp.int32)
    lane = lax.broadcasted_iota(jnp.int32, cw.shape, 1)
    hi = jnp.where(routed, jnp.right_shift(pos, DIGIT_BITS), -1)
    lo = jnp.where(routed, jnp.bitwise_and(pos, DIGIT - 1), -1)
    key_hi = jnp.where(lane < N_EXPERTS, hi, jnp.where(lane == N_EXPERTS, -1, 0))
    key_lo = jnp.where(lane < N_EXPERTS, lo, jnp.where(lane == N_EXPERTS + 1, -1, 0))
    return key_hi.astype(F32).astype(BF16), key_lo.astype(F32).astype(BF16)


def _segment_units(counts):
    return jnp.floor((counts + (MOE_UNIT - 1)) * (1.0 / MOE_UNIT))


def _dispatch_kernel(h_ref, cw_ref, digits_ref, xs_ref, cnt_ref):
    cw = cw_ref[...]
    t = cw.shape[0]
    routed = (cw > 0.0).astype(BF16)
    counts = _dot(jnp.ones((8, t), BF16), routed)
    cnt_ref[...] = counts.astype(jnp.int32)
    units = _segment_units(counts)
    e_i = lax.broadcasted_iota(jnp.int32, (KEY_W, KEY_W), 0)
    f_i = lax.broadcasted_iota(jnp.int32, (KEY_W, KEY_W), 1)
    starts = _dot(units.astype(BF16), (e_i < f_i).astype(BF16)) * MOE_UNIT
    ends = starts + units * MOE_UNIT
    key_hi, key_lo = _token_keys(cw, starts[:1])
    h = h_ref[...]
    lane = lax.broadcasted_iota(jnp.int32, (ROW_CHUNK, KEY_W), 1)
    used_rows = jnp.max(ends)

    def sort_chunk(c):
        rows = slice(c * ROW_CHUNK, (c + 1) * ROW_CHUNK)
        r = (lax.broadcasted_iota(jnp.int32, (ROW_CHUNK, KEY_W), 0) + c * ROW_CHUNK).astype(F32)
        in_seg = (r >= starts[:1]) & (r < ends[:1])
        rmap = jnp.where(lane < N_EXPERTS, in_seg.astype(F32), digits_ref[rows, :].astype(F32)).astype(BF16)
        hit = (_dot_nt(rmap, key_hi) == 0.0) & (_dot_nt(rmap, key_lo) == 0.0)
        xs_ref[rows, :] = _dot(hit.astype(BF16), h).astype(xs_ref.dtype)

    for c in range(N_CHUNKS):
        if c < FULL_CHUNKS:
            sort_chunk(c)
        else:
            pl.when(used_rows > c * ROW_CHUNK)(functools.partial(sort_chunk, c))

            @pl.when(used_rows <= c * ROW_CHUNK)
            def _(c=c):
                xs_ref[c * ROW_CHUNK:(c + 1) * ROW_CHUNK, :] = jnp.zeros((ROW_CHUNK, D_MODEL), xs_ref.dtype)


def _dispatch(h2, cw, digits):
    n = h2.shape[0]
    n_tiles = n // MOE_TILE
    return pl.pallas_call(
        _dispatch_kernel,
        grid=(n_tiles,),
        in_specs=[pl.BlockSpec((MOE_TILE, D_MODEL), lambda i: (i, 0)),
                  pl.BlockSpec((MOE_TILE, KEY_W), lambda i: (i, 0)),
                  pl.BlockSpec(digits.shape, lambda i: (0, 0))],
        out_specs=[pl.BlockSpec((TILE_ROWS, D_MODEL), lambda i: (i, 0)),
                   pl.BlockSpec((8, KEY_W), lambda i: (i, 0))],
        out_shape=[jax.ShapeDtypeStruct((n_tiles * TILE_ROWS, D_MODEL), BF16),
                   jax.ShapeDtypeStruct((n_tiles * 8, KEY_W), jnp.int32)],
        compiler_params=_params(("parallel",)),
        name="moe_dispatch",
    )(h2, cw, digits)


def _unit_copy(src_hbm, unit, dst, slot, pos, sem):
    return pltpu.make_async_copy(
        src_hbm.at[pl.ds(pl.multiple_of(unit * MOE_UNIT, MOE_UNIT), MOE_UNIT)],
        dst.at[slot, pl.ds(pos * MOE_UNIT, MOE_UNIT)], sem.at[slot])


def _experts_kernel(be_ref, src_ref, nb_ref, xs_hbm, wgu_ref, wdn_ref, ys_ref, xbuf, sem, wgu_b, wdn_b):
    j = pl.program_id(0)
    nb = nb_ref[0]

    def copies(blk, slot):
        return [_unit_copy(xs_hbm, src_ref[blk * UNITS_PER_BLOCK + u], xbuf, slot, u, sem)
                for u in range(UNITS_PER_BLOCK)]

    def fetch(blk, slot):
        for cp in copies(blk, slot):
            cp.start()

    @pl.when(j == 0)
    def _():
        fetch(0, 0)

    @pl.when((j == 0) | (be_ref[j] != be_ref[jnp.maximum(j - 1, 0)]))
    def _():
        wgu_b[...] = wgu_ref[...].astype(BF16)
        wdn_b[...] = wdn_ref[...].astype(BF16)

    @pl.when(j < nb)
    def _():
        slot = j % 2
        for cp in copies(j, slot):
            cp.wait()
        fetch(jnp.minimum(j + 1, nb - 1), 1 - slot)
        for g in range(MOE_BLOCK // MOE_MM_ROWS):
            rows = pl.ds(g * MOE_MM_ROWS, MOE_MM_ROWS)
            act = _swiglu_act(xbuf[slot, rows, :], wgu_b[...])
            ys_ref[rows, :] = _dot(act.astype(BF16), wdn_b[...]).astype(ys_ref.dtype)

    @pl.when(j == nb - 1)
    def _():
        for cp in copies(j, (j + 1) % 2):
            cp.wait()

    @pl.when(j >= nb)
    def _():
        ys_ref[...] = jnp.zeros_like(ys_ref)


def _experts(xs, block_expert, src_units, n_blocks_used, w_gu, w_dn):
    nb_max = block_expert.shape[0]
    grid_spec = pltpu.PrefetchScalarGridSpec(
        num_scalar_prefetch=3,
        grid=(nb_max,),
        in_specs=[pl.BlockSpec(memory_space=pl.ANY),
                  pl.BlockSpec((None, D_MODEL, 2 * D_EXPERT), lambda j, be, src, nb: (be[j], 0, 0)),
                  pl.BlockSpec((None, D_EXPERT, D_MODEL), lambda j, be, src, nb: (be[j], 0, 0))],
        out_specs=pl.BlockSpec((MOE_BLOCK, D_MODEL), lambda j, be, src, nb: (j, 0)),
        scratch_shapes=[pltpu.VMEM((2, MOE_BLOCK, D_MODEL), BF16), pltpu.SemaphoreType.DMA((2,)),
                        pltpu.VMEM((D_MODEL, 2 * D_EXPERT), BF16), pltpu.VMEM((D_EXPERT, D_MODEL), BF16)],
    )
    return pl.pallas_call(
        _experts_kernel,
        grid_spec=grid_spec,
        out_shape=jax.ShapeDtypeStruct((nb_max * MOE_BLOCK, D_MODEL), BF16),
        compiler_params=_params(("arbitrary",)),
        name="moe_experts",
    )(block_expert, src_units, n_blocks_used, xs, w_gu, w_dn)


def _combine_kernel(src_ref, used_ref, ys_hbm, cw_ref, h_ref, x1_ref, gt2_ref, gpost_ref, digits_t_ref,
                    wsgu_ref, wsdn_ref, o_ref, ybuf, sem, acc_ref):
    i = pl.program_id(0)

    def copies(tile, slot, c):
        return [_unit_copy(ys_hbm, src_ref[tile * TILE_UNITS + u], ybuf, slot, u, sem)
                for u in range(c * CHUNK_UNITS, (c + 1) * CHUNK_UNITS)]

    def chunk_used(tile, c):
        return used_ref[tile] > c * CHUNK_UNITS

    def for_used_chunks(tile, fn):
        for c in range(N_CHUNKS):
            if c < FULL_CHUNKS:
                fn(c)
            else:
                pl.when(chunk_used(tile, c))(functools.partial(fn, c))

    def fetch(tile, slot):
        def start(c):
            for cp in copies(tile, slot, c):
                cp.start()

        for_used_chunks(tile, start)

    def wait_all(tile, slot):
        def wait(c):
            for cp in copies(tile, slot, c):
                cp.wait()

        for_used_chunks(tile, wait)

    @pl.when(i == 0)
    def _():
        fetch(0, 0)

    @pl.when(i + 1 < pl.num_programs(0))
    def _():
        fetch(i + 1, (i + 1) % 2)

    cw = cw_ref[...]
    t = cw.shape[0]
    routed = (cw > 0.0).astype(BF16)
    e_i = lax.broadcasted_iota(jnp.int32, (KEY_W, KEY_W), 0)
    f_i = lax.broadcasted_iota(jnp.int32, (KEY_W, KEY_W), 1)
    units = _segment_units(_dot_tn(routed, jnp.ones((t, KEY_W), BF16)))
    starts = _dot((f_i < e_i).astype(BF16), units.astype(BF16)) * MOE_UNIT
    ends = starts + units * MOE_UNIT
    units_row = _segment_units(_dot(jnp.ones((8, t), BF16), routed))
    starts_row = _dot(units_row.astype(BF16), (e_i < f_i).astype(BF16)) * MOE_UNIT
    key_hi, key_lo = _token_keys(cw, starts_row[:1])
    wb = cw.astype(BF16)

    f = _dot(_swiglu_act(h_ref[...], wsgu_ref[...]).astype(BF16), wsdn_ref[...])
    slot = i % 2
    reps = ROW_CHUNK // KEY_W
    starts_c = jnp.concatenate([starts] * reps, axis=1)
    ends_c = jnp.concatenate([ends] * reps, axis=1)
    sub = lax.broadcasted_iota(jnp.int32, (KEY_W, ROW_CHUNK), 0)

    wait_all(i, slot)

    def chunk_sum(c):
        rows = slice(c * ROW_CHUNK, (c + 1) * ROW_CHUNK)
        r = (lax.broadcasted_iota(jnp.int32, (KEY_W, ROW_CHUNK), 1) + c * ROW_CHUNK).astype(F32)
        in_seg = (r >= starts_c) & (r < ends_c)
        rmap_t = jnp.where(sub < N_EXPERTS, in_seg.astype(F32), digits_t_ref[:, rows].astype(F32)).astype(BF16)
        hit = (_dot(key_hi, rmap_t) == 0.0) & (_dot(key_lo, rmap_t) == 0.0)
        w = _dot(wb, rmap_t)
        return _dot(jnp.where(hit, w, 0.0).astype(BF16), ybuf[slot, rows, :])

    for c in range(FULL_CHUNKS):
        f = f + chunk_sum(c)
    acc_ref[...] = f
    for c in range(FULL_CHUNKS, N_CHUNKS):
        @pl.when(chunk_used(i, c))
        def _(c=c):
            acc_ref[...] += chunk_sum(c)
    o_ref[...] = x1_ref[...] + gt2_ref[...] * _rms(acc_ref[...], gpost_ref[...])


def _combine(ys, src_units, used_units, cw, h2, x1, mod3, g_post, digits_t, wsgu_b, wsdn_b, seq):
    n = h2.shape[0]
    tpb = seq // MOE_TILE
    row = lambda w: pl.BlockSpec((MOE_TILE, w), lambda i, src, used: (i, 0))
    full = lambda a: pl.BlockSpec(a.shape, lambda i, src, used: (0,) * a.ndim)
    grid_spec = pltpu.PrefetchScalarGridSpec(
        num_scalar_prefetch=2,
        grid=(n // MOE_TILE,),
        in_specs=[pl.BlockSpec(memory_space=pl.ANY), row(KEY_W), row(D_MODEL), row(D_MODEL),
                  pl.BlockSpec((None, 1, D_MODEL), lambda i, src, used: (i // tpb, 0, 5)), full(g_post),
                  full(digits_t), full(wsgu_b), full(wsdn_b)],
        out_specs=row(D_MODEL),
        scratch_shapes=[pltpu.VMEM((2, TILE_ROWS, D_MODEL), BF16), pltpu.SemaphoreType.DMA((2,)),
                        pltpu.VMEM((MOE_TILE, D_MODEL), F32)],
    )
    return pl.pallas_call(
        _combine_kernel,
        grid_spec=grid_spec,
        out_shape=jax.ShapeDtypeStruct((n, D_MODEL), F32),
        compiler_params=_params(("arbitrary",)),
        name="moe_combine",
    )(src_units, used_units, ys, cw, h2, x1, mod3, g_post, digits_t, wsgu_b, wsdn_b)


def _row_digits():
    r = np.arange(TILE_ROWS)
    d = np.zeros((TILE_ROWS, KEY_W), np.float32)
    d[:, N_EXPERTS] = r // DIGIT
    d[:, N_EXPERTS + 1] = r % DIGIT
    return jnp.asarray(d, dtype=BF16)


def _moe_plan(counts, nb_max):
    n_tiles = counts.shape[0]
    s = (counts + (MOE_UNIT - 1)) // MOE_UNIT
    local = jnp.cumsum(s, axis=1) - s
    cs = jnp.cumsum(s, axis=0)
    per_expert = cs[-1]
    padded = (per_expert + UNITS_PER_BLOCK - 1) // UNITS_PER_BLOCK * UNITS_PER_BLOCK
    g_end = jnp.cumsum(padded)
    g_start = g_end - padded
    seg_start = g_start[None, :] + cs - s
    n_blocks_used = (g_end[-1] // UNITS_PER_BLOCK).astype(jnp.int32).reshape(1)
    jb = jnp.arange(nb_max, dtype=jnp.int32)
    one_e = ((jb[:, None] >= (g_start // UNITS_PER_BLOCK)[None, :])
             & (jb[:, None] < (g_end // UNITS_PER_BLOCK)[None, :])).astype(jnp.int32)
    pick_e = lambda table: jnp.sum(one_e[:, :, None] * table.T[None, :, :], axis=1)
    block_expert = jnp.where(jb < n_blocks_used[0], jnp.sum(one_e * jnp.arange(N_EXPERTS, dtype=jnp.int32), axis=1),
                             N_EXPERTS - 1).astype(jnp.int32)
    cs_b, s_b, local_b = pick_e(cs), pick_e(s), pick_e(local)
    q = (jb * UNITS_PER_BLOCK - jnp.sum(one_e * g_start[None, :], axis=1))[:, None] \
        + jnp.arange(UNITS_PER_BLOCK, dtype=jnp.int32)[None, :]
    tile = jnp.minimum(jnp.sum(cs_b[:, None, :] <= q[:, :, None], axis=2), n_tiles - 1)
    one_t = (tile[:, :, None] == jnp.arange(n_tiles, dtype=jnp.int32)).astype(jnp.int32)
    src = tile * TILE_UNITS + q + jnp.sum(one_t * (local_b - cs_b + s_b)[:, None, :], axis=2)
    valid = q < jnp.sum(one_e * per_expert[None, :], axis=1)[:, None]
    src_units = jnp.where(valid, src, 0).astype(jnp.int32).reshape(-1)
    u = jnp.arange(TILE_UNITS, dtype=jnp.int32)
    seg_end = local + s
    eu = jnp.minimum(jnp.sum(seg_end[:, None, :] <= u[None, :, None], axis=2), N_EXPERTS - 1)
    one_u = (eu[:, :, None] == jnp.arange(N_EXPERTS, dtype=jnp.int32)).astype(jnp.int32)
    back = u[None, :] + jnp.sum(one_u * (seg_start - local)[:, None, :], axis=2)
    back_units = jnp.where(u[None, :] < seg_end[:, -1:], back, 0).astype(jnp.int32).reshape(-1)
    return block_expert, src_units, n_blocks_used, back_units, seg_end[:, -1].astype(jnp.int32)


def _tile(n, pref):
    t = pref
    while n % t:
        t //= 2
    return t


def kernel(x, c, ctx, c_ctx, w_ada, b_ada, g_pre_mix, g_post_mix, g_pre_ffn, g_post_ffn, w_in, lb_logits, g_hgrn_out, cm_ln_g, cm_ln_b, w_spatial, b_spatial, w_branch_a, w_branch_b, w_out, w_router, b_router, w_expert_gu, w_expert_down, w_shared_gu, w_shared_down):
    B, T, D = x.shape
    L = ctx.shape[1]
    assert D == D_MODEL and w_ada.shape[0] == 1 and T % SCAN_CHUNK == 0 and L % SCAN_CHUNK == 0
    l = 0
    row = lambda a: a[l].reshape(1, -1)

    n_rows = -(-(B + 1) // 16) * 16
    cs = jnp.zeros((n_rows, D), F32).at[:B].set(c).at[B].set(c_ctx)
    mod3 = _ada_mod(cs, w_ada[l], row(b_ada)).reshape(n_rows, 1, 6 * D)

    w_in_b = w_in[l].astype(BF16)
    lbl = lb_logits[:, l:l + 2].reshape(4, HG_W)
    x2 = x.reshape(B * T, D)
    q, k2, lf2, v, sg, u, vn, sga, sgb = _proj_lat(
        x2, mod3, row(g_pre_mix), w_in_b, lbl, row(cm_ln_g), row(cm_ln_b), T, _tile(T, 256))
    kc2, lfc2, vc = _proj_ctx(ctx.reshape(B * L, D), mod3, B, row(g_pre_mix), w_in_b[:, HG_W:4 * HG_W], lbl,
                              _tile(B * L, 256))

    o_f, o_b = _hgrn_scan(q, k2, lf2, v, kc2, lfc2, vc, B, T, L)

    bs_full = jnp.repeat(b_spatial[l], CM_W // CM_GROUPS, axis=1)
    x1, h2, logits = _merge(
        o_f, o_b, sg, u, vn, sga, sgb, x2, mod3, row(g_hgrn_out), w_spatial[l].astype(BF16), bs_full,
        w_branch_a[l].astype(BF16), w_branch_b[l].astype(BF16), w_out[l].astype(BF16), row(g_post_mix),
        row(g_pre_ffn), jnp.pad(w_router[l].T, ((0, KEY_W - N_EXPERTS), (0, 0))).astype(BF16), T, _tile(T, 512))

    cw = _router(logits, jnp.broadcast_to(b_router[l][:, None], (N_EXPERTS, 128)), _tile(B * T, 512))

    n_tok = B * T
    n_tiles = n_tok // MOE_TILE
    digits = _row_digits()
    xs, cnt = _dispatch(h2, cw, digits)
    counts = cnt.reshape(n_tiles, 8, KEY_W)[:, 0, :N_EXPERTS]
    max_units = (n_tok * TOP_K + n_tiles * N_EXPERTS * (MOE_UNIT - 1)) // MOE_UNIT + N_EXPERTS * (UNITS_PER_BLOCK - 1)
    nb_max = -(-max_units // UNITS_PER_BLOCK)
    block_expert, src_units, n_blocks_used, back_units, tile_units = _moe_plan(counts, nb_max)
    ys = _experts(xs, block_expert, src_units, n_blocks_used, w_expert_gu[l], w_expert_down[l])
    out = _combine(ys, back_units, tile_units, cw, h2, x1, mod3, row(g_post_ffn), digits.T,
                   w_shared_gu[l].astype(BF16), w_shared_down[l].astype(BF16), T)
    return out.reshape(B, T, D)
```

```python
import functools

import numpy as np
import jax
import jax.numpy as jnp
from jax import lax
from jax.experimental import pallas as pl
from jax.experimental.pallas import tpu as pltpu

F32 = jnp.float32
BF16 = jnp.bfloat16

D_MODEL = 1024
EPS = 1e-6
HG_HEADS = 4
HG_DK = 128
HG_W = HG_HEADS * HG_DK
CM_W = 512
CM_CHUNK = 128
CM_GROUPS = 4
D_IN = 5 * HG_W + 2 * CM_W + 2 * D_MODEL
N_EXPERTS = 64
TOP_K = 8
N_GROUPS = 8
GROUP_BITS = 3
TOPK_GROUPS = 4
D_EXPERT = 256
ROUTED_SCALE = 2.5
SCAN_CHUNK = 128
SUB = 16
VMEM_LIMIT = 56 * 1024 * 1024


def _params(sem):
    return pltpu.CompilerParams(dimension_semantics=sem, vmem_limit_bytes=VMEM_LIMIT)


def _dot(a, b):
    return jnp.dot(a, b, preferred_element_type=F32)


def _dot_nt(a, b):
    return lax.dot_general(a, b, (((1,), (1,)), ((), ())), preferred_element_type=F32)


def _dot_tn(a, b):
    return lax.dot_general(a, b, (((0,), (0,)), ((), ())), preferred_element_type=F32)


def _rms(x, g):
    return x * lax.rsqrt(jnp.mean(x * x, axis=-1, keepdims=True) + EPS) * g


def _ada_kernel(c_ref, w_ref, b_ref, o_ref):
    c = c_ref[...]
    s = c * jax.nn.sigmoid(c)
    o_ref[...] = _dot(s.astype(BF16), w_ref[...].astype(BF16)) + b_ref[...]


def _ada_mod(cs, w_ada, b_ada):
    rows = cs.shape[0]
    n_out = w_ada.shape[1]
    return pl.pallas_call(
        _ada_kernel,
        grid=(n_out // D_MODEL,),
        in_specs=[
            pl.BlockSpec((rows, D_MODEL), lambda j: (0, 0)),
            pl.BlockSpec((D_MODEL, D_MODEL), lambda j: (0, j)),
            pl.BlockSpec((1, D_MODEL), lambda j: (0, j)),
        ],
        out_specs=pl.BlockSpec((rows, D_MODEL), lambda j: (0, j)),
        out_shape=jax.ShapeDtypeStruct((rows, n_out), F32),
        compiler_params=_params(("parallel",)),
        name="ada_mod",
    )(cs, w_ada, b_ada)


def _lower_bounds(lbl):
    out = []
    for d in range(2):
        l0, l1 = lbl[2 * d:2 * d + 1], lbl[2 * d + 1:2 * d + 2]
        m = jnp.maximum(l0, l1)
        e0, e1 = jnp.exp(l0 - m), jnp.exp(l1 - m)
        out.append(e0 / (e0 + e1))
    return out


def _prenorm(x_ref, sh_ref, sc_ref, g_ref):
    return (_rms(x_ref[...], g_ref[...]) * (1.0 + sc_ref[...]) + sh_ref[...]).astype(BF16)


def _gates(z, lb, k_ref, lf_ref, d):
    k_ref[d] = ((1.0 - lb) * jax.nn.sigmoid(-z)).astype(k_ref.dtype)
    lf_ref[d] = jnp.log2(lb + (1.0 - lb) * jax.nn.sigmoid(z))


def _proj_lat_kernel(x_ref, sh_ref, sc_ref, g_ref, w_ref, lbl_ref, lng_ref, lnb_ref,
                     q_ref, k_ref, lf_ref, v_ref, sg_ref, u_ref, vn_ref, sga_ref, sgb_ref):
    hb = _prenorm(x_ref, sh_ref, sc_ref, g_ref)
    lbs = _lower_bounds(lbl_ref[...])

    def mm(lo, width):
        return _dot(hb, w_ref[:, lo:lo + width])

    z = mm(0, HG_W)
    q_ref[...] = (z * jax.nn.sigmoid(z)).astype(q_ref.dtype)
    for d in range(2):
        _gates(mm((1 + d) * HG_W, HG_W), lbs[d], k_ref, lf_ref, d)
    v_ref[...] = mm(3 * HG_W, HG_W).astype(v_ref.dtype)
    z = mm(4 * HG_W, HG_W)
    sg_ref[...] = (z * jax.nn.sigmoid(z)).astype(sg_ref.dtype)
    u_ref[...] = jax.nn.gelu(mm(5 * HG_W, CM_W)).astype(u_ref.dtype)
    vv = jax.nn.gelu(mm(5 * HG_W + CM_W, CM_W))
    vc = vv - jnp.mean(vv, axis=-1, keepdims=True)
    vn = vc * lax.rsqrt(jnp.mean(vc * vc, axis=-1, keepdims=True) + EPS)
    vn_ref[...] = (vn * lng_ref[...] + lnb_ref[...]).astype(vn_ref.dtype)
    base = 5 * HG_W + 2 * CM_W
    sga_ref[...] = jax.nn.sigmoid(mm(base, D_MODEL)).astype(sga_ref.dtype)
    sgb_ref[...] = jax.nn.sigmoid(mm(base + D_MODEL, D_MODEL)).astype(sgb_ref.dtype)


def _mod_spec(rows_per_batch_tiles, col):
    return pl.BlockSpec((None, 1, D_MODEL), lambda i: (i // rows_per_batch_tiles, 0, col))


def _proj_lat(x2, mod3, g_pre, w_in_b, lbl, ln_g, ln_b, seq, tm):
    n = x2.shape[0]
    tpb = seq // tm
    row = lambda w: pl.BlockSpec((tm, w), lambda i: (i, 0))
    row2 = pl.BlockSpec((2, tm, HG_W), lambda i: (0, i, 0))
    full = lambda a: pl.BlockSpec(a.shape, lambda i: (0,) * a.ndim)
    outs = [
        (row(HG_W), jax.ShapeDtypeStruct((n, HG_W), BF16)),
        (row2, jax.ShapeDtypeStruct((2, n, HG_W), BF16)),
        (row2, jax.ShapeDtypeStruct((2, n, HG_W), F32)),
        (row(HG_W), jax.ShapeDtypeStruct((n, HG_W), BF16)),
        (row(HG_W), jax.ShapeDtypeStruct((n, HG_W), BF16)),
        (row(CM_W), jax.ShapeDtypeStruct((n, CM_W), BF16)),
        (row(CM_W), jax.ShapeDtypeStruct((n, CM_W), BF16)),
        (row(D_MODEL), jax.ShapeDtypeStruct((n, D_MODEL), BF16)),
        (row(D_MODEL), jax.ShapeDtypeStruct((n, D_MODEL), BF16)),
    ]
    return pl.pallas_call(
        _proj_lat_kernel,
        grid=(n // tm,),
        in_specs=[row(D_MODEL), _mod_spec(tpb, 0), _mod_spec(tpb, 1), full(g_pre), full(w_in_b),
                  full(lbl), full(ln_g), full(ln_b)],
        out_specs=[o[0] for o in outs],
        out_shape=[o[1] for o in outs],
        compiler_params=_params(("parallel",)),
        name="proj_lat",
    )(x2, mod3, mod3, g_pre, w_in_b, lbl, ln_g, ln_b)


def _proj_ctx_kernel(x_ref, sh_ref, sc_ref, g_ref, w_ref, lbl_ref, k_ref, lf_ref, v_ref):
    hb = _prenorm(x_ref, sh_ref, sc_ref, g_ref)
    lbs = _lower_bounds(lbl_ref[...])
    for d in range(2):
        _gates(_dot(hb, w_ref[:, d * HG_W:(d + 1) * HG_W]), lbs[d], k_ref, lf_ref, d)
    v_ref[...] = _dot(hb, w_ref[:, 2 * HG_W:3 * HG_W]).astype(v_ref.dtype)


def _proj_ctx(c2, mod3, ctx_row, g_pre, w_ctx_b, lbl, tm):
    n = c2.shape[0]
    row = lambda w: pl.BlockSpec((tm, w), lambda i: (i, 0))
    row2 = pl.BlockSpec((2, tm, HG_W), lambda i: (0, i, 0))
    full = lambda a: pl.BlockSpec(a.shape, lambda i: (0,) * a.ndim)
    mod = lambda col: pl.BlockSpec((None, 1, D_MODEL), lambda i: (ctx_row, 0, col))
    return pl.pallas_call(
        _proj_ctx_kernel,
        grid=(n // tm,),
        in_specs=[row(D_MODEL), mod(0), mod(1), full(g_pre), full(w_ctx_b), full(lbl)],
        out_specs=[row2, row2, row(HG_W)],
        out_shape=[jax.ShapeDtypeStruct((2, n, HG_W), BF16), jax.ShapeDtypeStruct((2, n, HG_W), F32),
                   jax.ShapeDtypeStruct((n, HG_W), BF16)],
        compiler_params=_params(("parallel",)),
        name="proj_ctx",
    )(c2, mod3, mod3, g_pre, w_ctx_b, lbl)


def _scan_tables():
    C = SCAN_CHUNK
    t = np.arange(C)
    lmats, lvls = [], []
    for d in range(2):
        p = t if d == 0 else C - 1 - t
        pt, ps = p[:, None], p[None, :]
        lmat = (ps <= pt).astype(np.float32)
        lmats.append(np.concatenate([lmat, lmat], axis=1))
        lvl = np.full((C, C), -1, np.int32)
        lvl[(pt // SUB == ps // SUB) & (ps <= pt)] = 0
        half, idx = SUB, 1
        while half < C:
            span = 2 * half
            lvl[(pt // span == ps // span) & ((pt // half) % 2 == 1) & ((ps // half) % 2 == 0)] = idx
            half, idx = span, idx + 1
        lvls.append(lvl)
    return jnp.asarray(np.stack(lmats), dtype=BF16), jnp.asarray(np.stack(lvls))


def _scan_step(d, lmat_ref, lvl_ref, k_ref, lf_ref, v_ref, st_ref, b_scr, q_ref=None, o_ref=None):
    C = SCAN_CHUNK
    lf = lf_ref[...]
    hi = lf.astype(BF16)
    lo = (lf - hi.astype(F32)).astype(BF16)
    b_scr[d] = _dot(lmat_ref[d], jnp.concatenate([hi, lo], axis=0))
    b = b_scr[d]

    def row(i):
        return b_scr[d, pl.ds(i, 1), :]

    b_last = row(C - 1 if d == 0 else 0)
    k = k_ref[...]
    v = v_ref[...]
    khat = k * jnp.exp2(b_last - b).astype(BF16)
    decay = jnp.exp2(b_last)

    if q_ref is not None:
        q = q_ref[...]
        qhat = q * jnp.exp2(b).astype(BF16)
        e0 = jnp.concatenate([b[m * SUB:(m + 1) * SUB] - row(m * SUB + SUB // 2 - 1 + d)
                              for m in range(C // SUB)], axis=0)
        factors = [(jnp.exp2(e0).astype(BF16), jnp.exp2(-e0).astype(BF16))]
        half = SUB
        while half < C:
            span = 2 * half
            e = jnp.concatenate([b[m * span:(m + 1) * span] - row(m * span + half - 1 + d)
                                 for m in range(C // span)], axis=0)
            w = jnp.exp2(-jnp.abs(e)).astype(BF16)
            factors.append((w, w))
            half = span
        lvl = lvl_ref[d]
        masks = [lvl == i for i in range(len(factors))]

    for h in range(HG_HEADS):
        sl = slice(h * HG_DK, (h + 1) * HG_DK)
        st = st_ref[d, :, sl]
        if q_ref is not None:
            a = jnp.zeros((C, C), F32)
            for (wq, wk), mask in zip(factors, masks):
                a = jnp.where(mask, _dot_nt(q[:, sl] * wq[:, sl], k[:, sl] * wk[:, sl]), a)
            o = _dot(a.astype(BF16), v[:, sl]) + _dot_nt(qhat[:, sl], st.astype(BF16))
            o_ref[:, sl] = o.astype(o_ref.dtype)
        st_ref[d, :, sl] = decay[:, sl] * st + _dot_tn(v[:, sl], khat[:, sl])


def _scan_kernel(n_ctx_steps, lmat_ref, lvl_ref, q_f, k_f, lf_f, v_f, q_b, k_b, lf_b, v_b,
                 kc_f, lfc_f, vc_f, kc_b, lfc_b, vc_b, o_f, o_b, st_ref, b_scr):
    s = pl.program_id(1)

    @pl.when(s == 0)
    def _():
        st_ref[...] = jnp.zeros_like(st_ref)

    @pl.when(s < n_ctx_steps)
    def _():
        _scan_step(0, lmat_ref, lvl_ref, kc_f, lfc_f, vc_f, st_ref, b_scr)
        _scan_step(1, lmat_ref, lvl_ref, kc_b, lfc_b, vc_b, st_ref, b_scr)

    @pl.when(s >= n_ctx_steps)
    def _():
        _scan_step(0, lmat_ref, lvl_ref, k_f, lf_f, v_f, st_ref, b_scr, q_f, o_f)
        _scan_step(1, lmat_ref, lvl_ref, k_b, lf_b, v_b, st_ref, b_scr, q_b, o_b)


def _hgrn_scan(q, k2, lf2, v, kc2, lfc2, vc, batch, seq, ctx_len):
    C = SCAN_CHUNK
    n_lat, n_ctx = seq // C, ctx_len // C
    lmat, lvl = _scan_tables()

    def lat_blk(d):
        def blk(b, s):
            j = jnp.maximum(s - n_ctx, 0)
            return b * n_lat + (j if d == 0 else n_lat - 1 - j)
        return blk

    def ctx_blk(d):
        def blk(b, s):
            i = jnp.minimum(s, n_ctx - 1)
            return b * n_ctx + (i if d == 0 else n_ctx - 1 - i)
        return blk

    def specs(blk_of, with_q):
        out = []
        for d in range(2):
            blk = blk_of(d)
            plain = pl.BlockSpec((C, HG_W), lambda b, s, blk=blk: (blk(b, s), 0))
            per_dir = pl.BlockSpec((None, C, HG_W), lambda b, s, blk=blk, d=d: (d, blk(b, s), 0))
            out += ([plain] if with_q else []) + [per_dir, per_dir, plain]
        return out

    full = lambda a: pl.BlockSpec(a.shape, lambda b, s: (0,) * a.ndim)
    o_specs = [pl.BlockSpec((C, HG_W), lambda b, s, blk=lat_blk(d): (blk(b, s), 0)) for d in range(2)]
    o_shape = jax.ShapeDtypeStruct((batch * seq, HG_W), BF16)
    return pl.pallas_call(
        functools.partial(_scan_kernel, n_ctx),
        grid=(batch, n_ctx + n_lat),
        in_specs=[full(lmat), full(lvl)] + specs(lat_blk, True) + specs(ctx_blk, False),
        out_specs=o_specs,
        out_shape=[o_shape, o_shape],
        scratch_shapes=[pltpu.VMEM((2, HG_DK, HG_W), F32), pltpu.VMEM((2, C, HG_W), F32)],
        compiler_params=_params(("parallel", "arbitrary")),
        name="hgrn_scan",
    )(lmat, lvl, q, k2, lf2, v, q, k2, lf2, v, kc2, lfc2, vc, kc2, lfc2, vc)


def _merge_kernel(of_ref, ob_ref, sg_ref, u_ref, vn_ref, sga_ref, sgb_ref, x_ref, gt1_ref, sh2_ref, sc2_ref,
                  gout_ref, ws_ref, bs_ref, wa_ref, wb_ref, wo_ref, gpost_ref, gffn_ref, wr_ref,
                  x1_ref, h2_ref, lg_ref):
    tm = x_ref.shape[0]
    o = of_ref[...].astype(F32) + ob_ref[...].astype(F32)
    sg = sg_ref[...].astype(F32)
    gout = gout_ref[...]
    a = jnp.concatenate(
        [_rms(o[:, h * HG_DK:(h + 1) * HG_DK], gout) * sg[:, h * HG_DK:(h + 1) * HG_DK] for h in range(HG_HEADS)],
        axis=1).astype(BF16)
    vn = vn_ref[...]
    gw = CM_W // CM_GROUPS
    z = jnp.concatenate(
        [jnp.concatenate([_dot(ws_ref[g], vn[c * CM_CHUNK:(c + 1) * CM_CHUNK, g * gw:(g + 1) * gw])
                          for g in range(CM_GROUPS)], axis=1) + bs_ref[...]
         for c in range(tm // CM_CHUNK)], axis=0)
    bm = (u_ref[...].astype(F32) * z).astype(BF16)
    y = sga_ref[...].astype(F32) * _dot(a, wa_ref[...]) + sgb_ref[...].astype(F32) * _dot(bm, wb_ref[...])
    yo = _dot(y.astype(BF16), wo_ref[...])
    x1 = x_ref[...] + gt1_ref[...] * _rms(yo, gpost_ref[...])
    x1_ref[...] = x1
    h2 = (_rms(x1, gffn_ref[...]) * (1.0 + sc2_ref[...]) + sh2_ref[...]).astype(BF16)
    h2_ref[...] = h2
    lg_ref[...] = _dot_nt(wr_ref[...], h2)


def _merge(o_f, o_b, sg, u, vn, sga, sgb, x2, mod3, g_out, ws_b, bs_full, wa_b, wb_b, wo_b, g_post, g_ffn, wr_b,
           seq, tm):
    n = x2.shape[0]
    tpb = seq // tm
    row = lambda w: pl.BlockSpec((tm, w), lambda i: (i, 0))
    full = lambda a: pl.BlockSpec(a.shape, lambda i: (0,) * a.ndim)
    return pl.pallas_call(
        _merge_kernel,
        grid=(n // tm,),
        in_specs=[row(HG_W), row(HG_W), row(HG_W), row(CM_W), row(CM_W),
                  row(D_MODEL), row(D_MODEL), row(D_MODEL), _mod_spec(tpb, 2), _mod_spec(tpb, 3),
                  _mod_spec(tpb, 4), full(g_out), full(ws_b), full(bs_full), full(wa_b), full(wb_b),
                  full(wo_b), full(g_post), full(g_ffn), full(wr_b)],
        out_specs=[row(D_MODEL), row(D_MODEL), pl.BlockSpec((wr_b.shape[0], tm), lambda i: (0, i))],
        out_shape=[jax.ShapeDtypeStruct((n, D_MODEL), F32), jax.ShapeDtypeStruct((n, D_MODEL), BF16),
                   jax.ShapeDtypeStruct((wr_b.shape[0], n), F32)],
        compiler_params=_params(("parallel",)),
        name="merge",
    )(o_f, o_b, sg, u, vn, sga, sgb, x2, mod3, mod3, mod3, g_out, ws_b, bs_full, wa_b, wb_b, wo_b, g_post, g_ffn,
      wr_b)


def _router_kernel(lg_ref, br_ref, w_ref):
    tm = lg_ref.shape[1]
    gsz = N_EXPERTS // N_GROUPS
    scores = jax.nn.sigmoid(lg_ref[:N_EXPERTS, :])
    sel = scores + jnp.concatenate([br_ref[...]] * (tm // br_ref.shape[1]), axis=1)
    neg = -jnp.inf

    def first_max(x, ids, sentinel, axis):
        m = jnp.max(x, axis=axis, keepdims=True)
        return m, jnp.min(jnp.where(x == m, ids, sentinel), axis=axis, keepdims=True)

    sel3 = sel.reshape(N_GROUPS, gsz, tm)
    j3 = lax.broadcasted_iota(jnp.int32, sel3.shape, 1)
    m1, i1 = first_max(sel3, j3, gsz, 1)
    gscore = m1 + jnp.max(jnp.where(j3 == i1, neg, sel3), axis=1, keepdims=True)
    g3 = lax.broadcasted_iota(jnp.int32, gscore.shape, 0)
    keep = jnp.zeros(gscore.shape, F32)
    for _ in range(TOPK_GROUPS):
        _, gi = first_max(gscore, g3, N_GROUPS, 0)
        keep = jnp.where(g3 == gi, 1.0, keep)
        gscore = jnp.where(g3 == gi, neg, gscore)
    x = jnp.where(keep > 0.0, sel3, neg).reshape(N_EXPERTS, tm)
    e_i = lax.broadcasted_iota(jnp.int32, x.shape, 0)
    w = jnp.zeros(x.shape, F32)
    for _ in range(TOP_K):
        _, ei = first_max(x, e_i, N_EXPERTS, 0)
        w = jnp.where(e_i == ei, scores, w)
        x = jnp.where(e_i == ei, neg, x)
    w = w / jnp.sum(w, axis=0, keepdims=True) * ROUTED_SCALE
    w_ref[...] = jnp.concatenate([w, jnp.zeros_like(w)], axis=0).T


def _router(logits_t, b_router_cols, tm):
    rows, n = logits_t.shape
    return pl.pallas_call(
        _router_kernel,
        grid=(n // tm,),
        in_specs=[pl.BlockSpec((rows, tm), lambda i: (0, i)),
                  pl.BlockSpec(b_router_cols.shape, lambda i: (0, 0))],
        out_specs=pl.BlockSpec((tm, rows), lambda i: (i, 0)),
        out_shape=jax.ShapeDtypeStruct((n, rows), F32),
        compiler_params=_params(("parallel",)),
        name="router",
    )(logits_t, b_router_cols)


MOE_TILE = 256
MOE_UNIT = 16
MOE_BLOCK = 512
MOE_MM_ROWS = 512
GATHER_SLOTS = 3
UNITS_PER_BLOCK = MOE_BLOCK // MOE_UNIT
TILE_ROWS = 3072
TILE_UNITS = TILE_ROWS // MOE_UNIT
ROW_CHUNK = 512
N_CHUNKS = TILE_ROWS // ROW_CHUNK
FULL_CHUNKS = MOE_TILE * TOP_K // ROW_CHUNK
CHUNK_UNITS = ROW_CHUNK // MOE_UNIT
KEY_W = 128
DIGIT_BITS = 6
DIGIT = 1 << DIGIT_BITS


def _swiglu_act(h, w_gu):
    gu = _dot(h, w_gu)
    de = gu.shape[1] // 2
    g = gu[:, :de]
    return g * jax.nn.sigmoid(g) * gu[:, de:]


def _token_keys(cw, starts_row):
    t = cw.shape[0]
    routed = cw > 0.0
    t_i = lax.broadcasted_iota(jnp.int32, (t, t), 0)
    s_i = lax.broadcasted_iota(jnp.int32, (t, t), 1)
    rank = _dot((s_i < t_i).astype(BF16), routed.astype(BF16))
    pos = (starts_row + rank).astype(jnp.int32)
    lane = lax.broadcasted_iota(jnp.int32, cw.shape, 1)
    hi = jnp.where(routed, jnp.right_shift(pos, DIGIT_BITS), -1)
    lo = jnp.where(routed, jnp.bitwise_and(pos, DIGIT - 1), -1)
    key_hi = jnp.where(lane < N_EXPERTS, hi, jnp.where(lane == N_EXPERTS, -1, 0))
    key_lo = jnp.where(lane < N_EXPERTS, lo, jnp.where(lane == N_EXPERTS + 1, -1, 0))
    return key_hi.astype(F32).astype(BF16), key_lo.astype(F32).astype(BF16)


def _segment_units(counts):
    return jnp.floor((counts + (MOE_UNIT - 1)) * (1.0 / MOE_UNIT))


def _dispatch_kernel(h_ref, cw_ref, digits_ref, xs_ref, cnt_ref):
    cw = cw_ref[...]
    t = cw.shape[0]
    routed = (cw > 0.0).astype(BF16)
    counts = _dot(jnp.ones((8, t), BF16), routed)
    cnt_ref[...] = counts.astype(jnp.int32)
    units = _segment_units(counts)
    e_i = lax.broadcasted_iota(jnp.int32, (KEY_W, KEY_W), 0)
    f_i = lax.broadcasted_iota(jnp.int32, (KEY_W, KEY_W), 1)
    starts = _dot(units.astype(BF16), (e_i < f_i).astype(BF16)) * MOE_UNIT
    ends = starts + units * MOE_UNIT
    key_hi, key_lo = _token_keys(cw, starts[:1])
    h = h_ref[...]
    lane = lax.broadcasted_iota(jnp.int32, (ROW_CHUNK, KEY_W), 1)
    used_rows = jnp.max(ends)

    def sort_chunk(c):
        rows = slice(c * ROW_CHUNK, (c + 1) * ROW_CHUNK)
        r = (lax.broadcasted_iota(jnp.int32, (ROW_CHUNK, KEY_W), 0) + c * ROW_CHUNK).astype(F32)
        in_seg = (r >= starts[:1]) & (r < ends[:1])
        rmap = jnp.where(lane < N_EXPERTS, in_seg.astype(F32), digits_ref[rows, :].astype(F32)).astype(BF16)
        hit = (_dot_nt(rmap, key_hi) == 0.0) & (_dot_nt(rmap, key_lo) == 0.0)
        xs_ref[rows, :] = _dot(hit.astype(BF16), h).astype(xs_ref.dtype)

    for c in range(N_CHUNKS):
        if c < FULL_CHUNKS:
            sort_chunk(c)
        else:
            pl.when(used_rows > c * ROW_CHUNK)(functools.partial(sort_chunk, c))

            @pl.when(used_rows <= c * ROW_CHUNK)
            def _(c=c):
                xs_ref[c * ROW_CHUNK:(c + 1) * ROW_CHUNK, :] = jnp.zeros((ROW_CHUNK, D_MODEL), xs_ref.dtype)


def _dispatch(h2, cw, digits):
    n = h2.shape[0]
    n_tiles = n // MOE_TILE
    return pl.pallas_call(
        _dispatch_kernel,
        grid=(n_tiles,),
        in_specs=[pl.BlockSpec((MOE_TILE, D_MODEL), lambda i: (i, 0)),
                  pl.BlockSpec((MOE_TILE, KEY_W), lambda i: (i, 0)),
                  pl.BlockSpec(digits.shape, lambda i: (0, 0))],
        out_specs=[pl.BlockSpec((TILE_ROWS, D_MODEL), lambda i: (i, 0)),
                   pl.BlockSpec((8, KEY_W), lambda i: (i, 0))],
        out_shape=[jax.ShapeDtypeStruct((n_tiles * TILE_ROWS, D_MODEL), BF16),
                   jax.ShapeDtypeStruct((n_tiles * 8, KEY_W), jnp.int32)],
        compiler_params=_params(("parallel",)),
        name="moe_dispatch",
    )(h2, cw, digits)


def _unit_copy(src_hbm, unit, dst, slot, pos, sem):
    return pltpu.make_async_copy(
        src_hbm.at[pl.ds(pl.multiple_of(unit * MOE_UNIT, MOE_UNIT), MOE_UNIT)],
        dst.at[slot, pl.ds(pos * MOE_UNIT, MOE_UNIT)], sem.at[slot])


def _experts_kernel(be_ref, src_ref, nb_ref, xs_hbm, wgu_ref, wdn_ref, ys_ref, xbuf, sem, wgu_b, wdn_b):
    j = pl.program_id(0)
    nb = nb_ref[0]

    def copies(blk, slot):
        return [_unit_copy(xs_hbm, src_ref[blk * UNITS_PER_BLOCK + u], xbuf, slot, u, sem)
                for u in range(UNITS_PER_BLOCK)]

    def fetch(blk, slot):
        for cp in copies(blk, slot):
            cp.start()

    ahead = GATHER_SLOTS - 1

    @pl.when(j == 0)
    def _():
        for a in range(ahead):
            fetch(jnp.minimum(a, nb - 1), a)

    @pl.when((j == 0) | (be_ref[j] != be_ref[jnp.maximum(j - 1, 0)]))
    def _():
        wgu_b[...] = wgu_ref[...].astype(BF16)
        wdn_b[...] = wdn_ref[...].astype(BF16)

    @pl.when(j < nb)
    def _():
        slot = j % GATHER_SLOTS
        for cp in copies(j, slot):
            cp.wait()
        fetch(jnp.minimum(j + ahead, nb - 1), (j + ahead) % GATHER_SLOTS)
        for g in range(MOE_BLOCK // MOE_MM_ROWS):
            rows = pl.ds(g * MOE_MM_ROWS, MOE_MM_ROWS)
            act = _swiglu_act(xbuf[slot, rows, :], wgu_b[...])
            ys_ref[rows, :] = _dot(act.astype(BF16), wdn_b[...]).astype(ys_ref.dtype)

    @pl.when(j == nb - 1)
    def _():
        for a in range(1, GATHER_SLOTS):
            for cp in copies(j, (j + a) % GATHER_SLOTS):
                cp.wait()

    @pl.when(j >= nb)
    def _():
        ys_ref[...] = jnp.zeros_like(ys_ref)


def _experts(xs, block_expert, src_units, n_blocks_used, w_gu, w_dn):
    nb_max = block_expert.shape[0]
    grid_spec = pltpu.PrefetchScalarGridSpec(
        num_scalar_prefetch=3,
        grid=(nb_max,),
        in_specs=[pl.BlockSpec(memory_space=pl.ANY),
                  pl.BlockSpec((None, D_MODEL, 2 * D_EXPERT), lambda j, be, src, nb: (be[j], 0, 0)),
                  pl.BlockSpec((None, D_EXPERT, D_MODEL), lambda j, be, src, nb: (be[j], 0, 0))],
        out_specs=pl.BlockSpec((MOE_BLOCK, D_MODEL), lambda j, be, src, nb: (j, 0)),
        scratch_shapes=[pltpu.VMEM((GATHER_SLOTS, MOE_BLOCK, D_MODEL), BF16),
                        pltpu.SemaphoreType.DMA((GATHER_SLOTS,)),
                        pltpu.VMEM((D_MODEL, 2 * D_EXPERT), BF16), pltpu.VMEM((D_EXPERT, D_MODEL), BF16)],
    )
    return pl.pallas_call(
        _experts_kernel,
        grid_spec=grid_spec,
        out_shape=jax.ShapeDtypeStruct((nb_max * MOE_BLOCK, D_MODEL), BF16),
        compiler_params=_params(("arbitrary",)),
        name="moe_experts",
    )(block_expert, src_units, n_blocks_used, xs, w_gu, w_dn)


def _combine_kernel(src_ref, used_ref, ys_hbm, cw_ref, h_ref, x1_ref, gt2_ref, gpost_ref, digits_t_ref,
                    wsgu_ref, wsdn_ref, o_ref, ybuf, sem, acc_ref):
    i = pl.program_id(0)

    def copies(tile, slot, c):
        return [_unit_copy(ys_hbm, src_ref[tile * TILE_UNITS + u], ybuf, slot, u, sem)
                for u in range(c * CHUNK_UNITS, (c + 1) * CHUNK_UNITS)]

    def chunk_used(tile, c):
        return used_ref[tile] > c * CHUNK_UNITS

    def for_used_chunks(tile, fn):
        for c in range(N_CHUNKS):
            if c < FULL_CHUNKS:
                fn(c)
            else:
                pl.when(chunk_used(tile, c))(functools.partial(fn, c))

    def fetch(tile, slot):
        def start(c):
            for cp in copies(tile, slot, c):
                cp.start()

        for_used_chunks(tile, start)

    def wait_all(tile, slot):
        def wait(c):
            for cp in copies(tile, slot, c):
                cp.wait()

        for_used_chunks(tile, wait)

    @pl.when(i == 0)
    def _():
        fetch(0, 0)

    @pl.when(i + 1 < pl.num_programs(0))
    def _():
        fetch(i + 1, (i + 1) % 2)

    cw = cw_ref[...]
    t = cw.shape[0]
    routed = (cw > 0.0).astype(BF16)
    e_i = lax.broadcasted_iota(jnp.int32, (KEY_W, KEY_W), 0)
    f_i = lax.broadcasted_iota(jnp.int32, (KEY_W, KEY_W), 1)
    units = _segment_units(_dot_tn(routed, jnp.ones((t, KEY_W), BF16)))
    starts = _dot((f_i < e_i).astype(BF16), units.astype(BF16)) * MOE_UNIT
    ends = starts + units * MOE_UNIT
    units_row = _segment_units(_dot(jnp.ones((8, t), BF16), routed))
    starts_row = _dot(units_row.astype(BF16), (e_i < f_i).astype(BF16)) * MOE_UNIT
    key_hi, key_lo = _token_keys(cw, starts_row[:1])
    wb = cw.astype(BF16)

    f = _dot(_swiglu_act(h_ref[...], wsgu_ref[...]).astype(BF16), wsdn_ref[...])
    slot = i % 2
    reps = ROW_CHUNK // KEY_W
    starts_c = jnp.concatenate([starts] * reps, axis=1)
    ends_c = jnp.concatenate([ends] * reps, axis=1)
    sub = lax.broadcasted_iota(jnp.int32, (KEY_W, ROW_CHUNK), 0)

    wait_all(i, slot)

    def chunk_sum(c):
        rows = slice(c * ROW_CHUNK, (c + 1) * ROW_CHUNK)
        r = (lax.broadcasted_iota(jnp.int32, (KEY_W, ROW_CHUNK), 1) + c * ROW_CHUNK).astype(F32)
        in_seg = (r >= starts_c) & (r < ends_c)
        rmap_t = jnp.where(sub < N_EXPERTS, in_seg.astype(F32), digits_t_ref[:, rows].astype(F32)).astype(BF16)
        hit = (_dot(key_hi, rmap_t) == 0.0) & (_dot(key_lo, rmap_t) == 0.0)
        w = _dot(wb, rmap_t)
        return _dot(jnp.where(hit, w, 0.0).astype(BF16), ybuf[slot, rows, :])

    for c in range(FULL_CHUNKS):
        f = f + chunk_sum(c)
    acc_ref[...] = f
    for c in range(FULL_CHUNKS, N_CHUNKS):
        @pl.when(chunk_used(i, c))
        def _(c=c):
            acc_ref[...] += chunk_sum(c)
    o_ref[...] = x1_ref[...] + gt2_ref[...] * _rms(acc_ref[...], gpost_ref[...])


def _combine(ys, src_units, used_units, cw, h2, x1, mod3, g_post, digits_t, wsgu_b, wsdn_b, seq):
    n = h2.shape[0]
    tpb = seq // MOE_TILE
    row = lambda w: pl.BlockSpec((MOE_TILE, w), lambda i, src, used: (i, 0))
    full = lambda a: pl.BlockSpec(a.shape, lambda i, src, used: (0,) * a.ndim)
    grid_spec = pltpu.PrefetchScalarGridSpec(
        num_scalar_prefetch=2,
        grid=(n // MOE_TILE,),
        in_specs=[pl.BlockSpec(memory_space=pl.ANY), row(KEY_W), row(D_MODEL), row(D_MODEL),
                  pl.BlockSpec((None, 1, D_MODEL), lambda i, src, used: (i // tpb, 0, 5)), full(g_post),
                  full(digits_t), full(wsgu_b), full(wsdn_b)],
        out_specs=row(D_MODEL),
        scratch_shapes=[pltpu.VMEM((2, TILE_ROWS, D_MODEL), BF16), pltpu.SemaphoreType.DMA((2,)),
                        pltpu.VMEM((MOE_TILE, D_MODEL), F32)],
    )
    return pl.pallas_call(
        _combine_kernel,
        grid_spec=grid_spec,
        out_shape=jax.ShapeDtypeStruct((n, D_MODEL), F32),
        compiler_params=_params(("arbitrary",)),
        name="moe_combine",
    )(src_units, used_units, ys, cw, h2, x1, mod3, g_post, digits_t, wsgu_b, wsdn_b)


def _row_digits():
    r = np.arange(TILE_ROWS)
    d = np.zeros((TILE_ROWS, KEY_W), np.float32)
    d[:, N_EXPERTS] = r // DIGIT
    d[:, N_EXPERTS + 1] = r % DIGIT
    return jnp.asarray(d, dtype=BF16)


def _moe_plan(counts, nb_max):
    n_tiles = counts.shape[0]
    s = (counts + (MOE_UNIT - 1)) // MOE_UNIT
    local = jnp.cumsum(s, axis=1) - s
    cs = jnp.cumsum(s, axis=0)
    per_expert = cs[-1]
    padded = (per_expert + UNITS_PER_BLOCK - 1) // UNITS_PER_BLOCK * UNITS_PER_BLOCK
    g_end = jnp.cumsum(padded)
    g_start = g_end - padded
    seg_start = g_start[None, :] + cs - s
    n_blocks_used = (g_end[-1] // UNITS_PER_BLOCK).astype(jnp.int32).reshape(1)
    jb = jnp.arange(nb_max, dtype=jnp.int32)
    one_e = ((jb[:, None] >= (g_start // UNITS_PER_BLOCK)[None, :])
             & (jb[:, None] < (g_end // UNITS_PER_BLOCK)[None, :])).astype(jnp.int32)
    pick_e = lambda table: jnp.sum(one_e[:, :, None] * table.T[None, :, :], axis=1)
    block_expert = jnp.where(jb < n_blocks_used[0], jnp.sum(one_e * jnp.arange(N_EXPERTS, dtype=jnp.int32), axis=1),
                             N_EXPERTS - 1).astype(jnp.int32)
    cs_b, s_b, local_b = pick_e(cs), pick_e(s), pick_e(local)
    q = (jb * UNITS_PER_BLOCK - jnp.sum(one_e * g_start[None, :], axis=1))[:, None] \
        + jnp.arange(UNITS_PER_BLOCK, dtype=jnp.int32)[None, :]
    tile = jnp.minimum(jnp.sum(cs_b[:, None, :] <= q[:, :, None], axis=2), n_tiles - 1)
    one_t = (tile[:, :, None] == jnp.arange(n_tiles, dtype=jnp.int32)).astype(jnp.int32)
    src = tile * TILE_UNITS + q + jnp.sum(one_t * (local_b - cs_b + s_b)[:, None, :], axis=2)
    valid = q < jnp.sum(one_e * per_expert[None, :], axis=1)[:, None]
    src_units = jnp.where(valid, src, 0).astype(jnp.int32).reshape(-1)
    u = jnp.arange(TILE_UNITS, dtype=jnp.int32)
    seg_end = local + s
    eu = jnp.minimum(jnp.sum(seg_end[:, None, :] <= u[None, :, None], axis=2), N_EXPERTS - 1)
    one_u = (eu[:, :, None] == jnp.arange(N_EXPERTS, dtype=jnp.int32)).astype(jnp.int32)
    back = u[None, :] + jnp.sum(one_u * (seg_start - local)[:, None, :], axis=2)
    back_units = jnp.where(u[None, :] < seg_end[:, -1:], back, 0).astype(jnp.int32).reshape(-1)
    return block_expert, src_units, n_blocks_used, back_units, seg_end[:, -1].astype(jnp.int32)


def _tile(n, pref):
    t = pref
    while n % t:
        t //= 2
    return t


def kernel(x, c, ctx, c_ctx, w_ada, b_ada, g_pre_mix, g_post_mix, g_pre_ffn, g_post_ffn, w_in, lb_logits, g_hgrn_out, cm_ln_g, cm_ln_b, w_spatial, b_spatial, w_branch_a, w_branch_b, w_out, w_router, b_router, w_expert_gu, w_expert_down, w_shared_gu, w_shared_down):
    B, T, D = x.shape
    L = ctx.shape[1]
    assert D == D_MODEL and w_ada.shape[0] == 1 and T % SCAN_CHUNK == 0 and L % SCAN_CHUNK == 0
    l = 0
    row = lambda a: a[l].reshape(1, -1)

    n_rows = -(-(B + 1) // 16) * 16
    cs = jnp.zeros((n_rows, D), F32).at[:B].set(c).at[B].set(c_ctx)
    mod3 = _ada_mod(cs, w_ada[l], row(b_ada)).reshape(n_rows, 1, 6 * D)

    w_in_b = w_in[l].astype(BF16)
    lbl = lb_logits[:, l:l + 2].reshape(4, HG_W)
    x2 = x.reshape(B * T, D)
    q, k2, lf2, v, sg, u, vn, sga, sgb = _proj_lat(
        x2, mod3, row(g_pre_mix), w_in_b, lbl, row(cm_ln_g), row(cm_ln_b), T, _tile(T, 256))
    kc2, lfc2, vc = _proj_ctx(ctx.reshape(B * L, D), mod3, B, row(g_pre_mix), w_in_b[:, HG_W:4 * HG_W], lbl,
                              _tile(B * L, 256))

    o_f, o_b = _hgrn_scan(q, k2, lf2, v, kc2, lfc2, vc, B, T, L)

    bs_full = jnp.repeat(b_spatial[l], CM_W // CM_GROUPS, axis=1)
    x1, h2, logits = _merge(
        o_f, o_b, sg, u, vn, sga, sgb, x2, mod3, row(g_hgrn_out), w_spatial[l].astype(BF16), bs_full,
        w_branch_a[l].astype(BF16), w_branch_b[l].astype(BF16), w_out[l].astype(BF16), row(g_post_mix),
        row(g_pre_ffn), jnp.pad(w_router[l].T, ((0, KEY_W - N_EXPERTS), (0, 0))).astype(BF16), T, _tile(T, 512))

    cw = _router(logits, jnp.broadcast_to(b_router[l][:, None], (N_EXPERTS, 128)), _tile(B * T, 512))

    n_tok = B * T
    n_tiles = n_tok // MOE_TILE
    digits = _row_digits()
    xs, cnt = _dispatch(h2, cw, digits)
    counts = cnt.reshape(n_tiles, 8, KEY_W)[:, 0, :N_EXPERTS]
    max_units = (n_tok * TOP_K + n_tiles * N_EXPERTS * (MOE_UNIT - 1)) // MOE_UNIT + N_EXPERTS * (UNITS_PER_BLOCK - 1)
    nb_max = -(-max_units // UNITS_PER_BLOCK)
    block_expert, src_units, n_blocks_used, back_units, tile_units = _moe_plan(counts, nb_max)
    ys = _experts(xs, block_expert, src_units, n_blocks_used, w_expert_gu[l], w_expert_down[l])
    out = _combine(ys, back_units, tile_units, cw, h2, x1, mod3, row(g_post_ffn), digits.T,
                   w_shared_gu[l].astype(BF16), w_shared_down[l].astype(BF16), T)
    return out.reshape(B, T, D)
```

```python
import functools

import numpy as np
import jax
import jax.numpy as jnp
from jax import lax
from jax.experimental import pallas as pl
from jax.experimental.pallas import tpu as pltpu

F32 = jnp.float32
BF16 = jnp.bfloat16

D_MODEL = 1024
EPS = 1e-6
HG_HEADS = 4
HG_DK = 128
HG_W = HG_HEADS * HG_DK
CM_W = 512
CM_CHUNK = 128
CM_GROUPS = 4
D_IN = 5 * HG_W + 2 * CM_W + 2 * D_MODEL
N_EXPERTS = 64
TOP_K = 8
N_GROUPS = 8
GROUP_BITS = 3
TOPK_GROUPS = 4
D_EXPERT = 256
ROUTED_SCALE = 2.5
SCAN_CHUNK = 128
SUB = 16
VMEM_LIMIT = 56 * 1024 * 1024


def _params(sem):
    return pltpu.CompilerParams(dimension_semantics=sem, vmem_limit_bytes=VMEM_LIMIT)


def _dot(a, b):
    return jnp.dot(a, b, preferred_element_type=F32)


def _dot_nt(a, b):
    return lax.dot_general(a, b, (((1,), (1,)), ((), ())), preferred_element_type=F32)


def _dot_tn(a, b):
    return lax.dot_general(a, b, (((0,), (0,)), ((), ())), preferred_element_type=F32)


def _rms(x, g):
    return x * lax.rsqrt(jnp.mean(x * x, axis=-1, keepdims=True) + EPS) * g


def _ada_kernel(c_ref, w_ref, b_ref, o_ref):
    c = c_ref[...]
    s = c * jax.nn.sigmoid(c)
    o_ref[...] = _dot(s.astype(BF16), w_ref[...].astype(BF16)) + b_ref[...]


def _ada_mod(cs, w_ada, b_ada):
    rows = cs.shape[0]
    n_out = w_ada.shape[1]
    return pl.pallas_call(
        _ada_kernel,
        grid=(n_out // D_MODEL,),
        in_specs=[
            pl.BlockSpec((rows, D_MODEL), lambda j: (0, 0)),
            pl.BlockSpec((D_MODEL, D_MODEL), lambda j: (0, j)),
            pl.BlockSpec((1, D_MODEL), lambda j: (0, j)),
        ],
        out_specs=pl.BlockSpec((rows, D_MODEL), lambda j: (0, j)),
        out_shape=jax.ShapeDtypeStruct((rows, n_out), F32),
        compiler_params=_params(("parallel",)),
        name="ada_mod",
    )(cs, w_ada, b_ada)


def _lower_bounds(lbl):
    out = []
    for d in range(2):
        l0, l1 = lbl[2 * d:2 * d + 1], lbl[2 * d + 1:2 * d + 2]
        m = jnp.maximum(l0, l1)
        e0, e1 = jnp.exp(l0 - m), jnp.exp(l1 - m)
        out.append(e0 / (e0 + e1))
    return out


def _prenorm(x_ref, sh_ref, sc_ref, g_ref):
    return (_rms(x_ref[...], g_ref[...]) * (1.0 + sc_ref[...]) + sh_ref[...]).astype(BF16)


def _gates(z, lb, k_ref, lf_ref, d):
    k_ref[d] = ((1.0 - lb) * jax.nn.sigmoid(-z)).astype(k_ref.dtype)
    lf_ref[d] = jnp.log2(lb + (1.0 - lb) * jax.nn.sigmoid(z))


def _proj_lat_kernel(x_ref, sh_ref, sc_ref, g_ref, w_ref, lbl_ref, lng_ref, lnb_ref,
                     q_ref, k_ref, lf_ref, v_ref, sg_ref, u_ref, vn_ref, sga_ref, sgb_ref):
    hb = _prenorm(x_ref, sh_ref, sc_ref, g_ref)
    lbs = _lower_bounds(lbl_ref[...])

    def mm(lo, width):
        return _dot(hb, w_ref[:, lo:lo + width])

    z = mm(0, HG_W)
    q_ref[...] = (z * jax.nn.sigmoid(z)).astype(q_ref.dtype)
    for d in range(2):
        _gates(mm((1 + d) * HG_W, HG_W), lbs[d], k_ref, lf_ref, d)
    v_ref[...] = mm(3 * HG_W, HG_W).astype(v_ref.dtype)
    z = mm(4 * HG_W, HG_W)
    sg_ref[...] = (z * jax.nn.sigmoid(z)).astype(sg_ref.dtype)
    u_ref[...] = jax.nn.gelu(mm(5 * HG_W, CM_W)).astype(u_ref.dtype)
    vv = jax.nn.gelu(mm(5 * HG_W + CM_W, CM_W))
    vc = vv - jnp.mean(vv, axis=-1, keepdims=True)
    vn = vc * lax.rsqrt(jnp.mean(vc * vc, axis=-1, keepdims=True) + EPS)
    vn_ref[...] = (vn * lng_ref[...] + lnb_ref[...]).astype(vn_ref.dtype)
    base = 5 * HG_W + 2 * CM_W
    sga_ref[...] = jax.nn.sigmoid(mm(base, D_MODEL)).astype(sga_ref.dtype)
    sgb_ref[...] = jax.nn.sigmoid(mm(base + D_MODEL, D_MODEL)).astype(sgb_ref.dtype)


def _mod_spec(rows_per_batch_tiles, col):
    return pl.BlockSpec((None, 1, D_MODEL), lambda i: (i // rows_per_batch_tiles, 0, col))


def _proj_lat(x2, mod3, g_pre, w_in_b, lbl, ln_g, ln_b, seq, tm):
    n = x2.shape[0]
    tpb = seq // tm
    row = lambda w: pl.BlockSpec((tm, w), lambda i: (i, 0))
    row2 = pl.BlockSpec((2, tm, HG_W), lambda i: (0, i, 0))
    full = lambda a: pl.BlockSpec(a.shape, lambda i: (0,) * a.ndim)
    outs = [
        (row(HG_W), jax.ShapeDtypeStruct((n, HG_W), BF16)),
        (row2, jax.ShapeDtypeStruct((2, n, HG_W), BF16)),
        (row2, jax.ShapeDtypeStruct((2, n, HG_W), F32)),
        (row(HG_W), jax.ShapeDtypeStruct((n, HG_W), BF16)),
        (row(HG_W), jax.ShapeDtypeStruct((n, HG_W), BF16)),
        (row(CM_W), jax.ShapeDtypeStruct((n, CM_W), BF16)),
        (row(CM_W), jax.ShapeDtypeStruct((n, CM_W), BF16)),
        (row(D_MODEL), jax.ShapeDtypeStruct((n, D_MODEL), BF16)),
        (row(D_MODEL), jax.ShapeDtypeStruct((n, D_MODEL), BF16)),
    ]
    return pl.pallas_call(
        _proj_lat_kernel,
        grid=(n // tm,),
        in_specs=[row(D_MODEL), _mod_spec(tpb, 0), _mod_spec(tpb, 1), full(g_pre), full(w_in_b),
                  full(lbl), full(ln_g), full(ln_b)],
        out_specs=[o[0] for o in outs],
        out_shape=[o[1] for o in outs],
        compiler_params=_params(("parallel",)),
        name="proj_lat",
    )(x2, mod3, mod3, g_pre, w_in_b, lbl, ln_g, ln_b)


def _proj_ctx_kernel(x_ref, sh_ref, sc_ref, g_ref, w_ref, lbl_ref, k_ref, lf_ref, v_ref):
    hb = _prenorm(x_ref, sh_ref, sc_ref, g_ref)
    lbs = _lower_bounds(lbl_ref[...])
    for d in range(2):
        _gates(_dot(hb, w_ref[:, d * HG_W:(d + 1) * HG_W]), lbs[d], k_ref, lf_ref, d)
    v_ref[...] = _dot(hb, w_ref[:, 2 * HG_W:3 * HG_W]).astype(v_ref.dtype)


def _proj_ctx(c2, mod3, ctx_row, g_pre, w_ctx_b, lbl, tm):
    n = c2.shape[0]
    row = lambda w: pl.BlockSpec((tm, w), lambda i: (i, 0))
    row2 = pl.BlockSpec((2, tm, HG_W), lambda i: (0, i, 0))
    full = lambda a: pl.BlockSpec(a.shape, lambda i: (0,) * a.ndim)
    mod = lambda col: pl.BlockSpec((None, 1, D_MODEL), lambda i: (ctx_row, 0, col))
    return pl.pallas_call(
        _proj_ctx_kernel,
        grid=(n // tm,),
        in_specs=[row(D_MODEL), mod(0), mod(1), full(g_pre), full(w_ctx_b), full(lbl)],
        out_specs=[row2, row2, row(HG_W)],
        out_shape=[jax.ShapeDtypeStruct((2, n, HG_W), BF16), jax.ShapeDtypeStruct((2, n, HG_W), F32),
                   jax.ShapeDtypeStruct((n, HG_W), BF16)],
        compiler_params=_params(("parallel",)),
        name="proj_ctx",
    )(c2, mod3, mod3, g_pre, w_ctx_b, lbl)


def _scan_tables():
    C = SCAN_CHUNK
    t = np.arange(C)
    lmats, lvls = [], []
    for d in range(2):
        p = t if d == 0 else C - 1 - t
        pt, ps = p[:, None], p[None, :]
        lmat = (ps <= pt).astype(np.float32)
        lmats.append(np.concatenate([lmat, lmat], axis=1))
        lvl = np.full((C, C), -1, np.int32)
        lvl[(pt // SUB == ps // SUB) & (ps <= pt)] = 0
        half, idx = SUB, 1
        while half < C:
            span = 2 * half
            lvl[(pt // span == ps // span) & ((pt // half) % 2 == 1) & ((ps // half) % 2 == 0)] = idx
            half, idx = span, idx + 1
        lvls.append(lvl)
    return jnp.asarray(np.stack(lmats), dtype=BF16), jnp.asarray(np.stack(lvls))


def _scan_step(d, slot, lmat_ref, lvl_ref, k_ref, lf_ref, v_ref, st_ref, b_scr, q_ref=None, o_ref=None):
    C = SCAN_CHUNK
    lf = lf_ref[...]
    hi = lf.astype(BF16)
    lo = (lf - hi.astype(F32)).astype(BF16)
    b_scr[slot] = _dot(lmat_ref[d], jnp.concatenate([hi, lo], axis=0))
    b = b_scr[slot]

    def row(i):
        return b_scr[slot, pl.ds(i, 1), :]

    b_last = row(C - 1 if d == 0 else 0)
    k = k_ref[...]
    v = v_ref[...]
    khat = k * jnp.exp2(b_last - b).astype(BF16)
    decay = jnp.exp2(b_last)

    if q_ref is not None:
        q = q_ref[...]
        qhat = q * jnp.exp2(b).astype(BF16)
        e0 = jnp.concatenate([b[m * SUB:(m + 1) * SUB] - row(m * SUB + SUB // 2 - 1 + d)
                              for m in range(C // SUB)], axis=0)
        factors = [(jnp.exp2(e0).astype(BF16), jnp.exp2(-e0).astype(BF16))]
        half = SUB
        while half < C:
            span = 2 * half
            e = jnp.concatenate([b[m * span:(m + 1) * span] - row(m * span + half - 1 + d)
                                 for m in range(C // span)], axis=0)
            w = jnp.exp2(-jnp.abs(e)).astype(BF16)
            factors.append((w, w))
            half = span
        lvl = lvl_ref[d]
        masks = [lvl == i for i in range(len(factors))]

    for h in range(HG_HEADS):
        sl = slice(h * HG_DK, (h + 1) * HG_DK)
        st = st_ref[slot, :, sl]
        if q_ref is not None:
            a = jnp.zeros((C, C), F32)
            for (wq, wk), mask in zip(factors, masks):
                a = jnp.where(mask, _dot_nt(q[:, sl] * wq[:, sl], k[:, sl] * wk[:, sl]), a)
            o = _dot(a.astype(BF16), v[:, sl]) + _dot_nt(qhat[:, sl], st.astype(BF16))
            o_ref[:, sl] = o.astype(o_ref.dtype)
        st_ref[slot, :, sl] = decay[:, sl] * st + _dot_tn(v[:, sl], khat[:, sl])


def _scan_kernel(n_ctx_steps, lmat_ref, lvl_ref, q_f, k_f, lf_f, v_f, q_b, k_b, lf_b, v_b,
                 kc_f, lfc_f, vc_f, kc_b, lfc_b, vc_b, o_f, o_b, st_ref, b_scr):
    s = pl.program_id(1)
    n_seq = q_f.shape[0]

    @pl.when(s == 0)
    def _():
        st_ref[...] = jnp.zeros_like(st_ref)

    @pl.when(s < n_ctx_steps)
    def _():
        for i in range(n_seq):
            _scan_step(0, 2 * i, lmat_ref, lvl_ref, kc_f.at[i], lfc_f.at[i], vc_f.at[i], st_ref, b_scr)
            _scan_step(1, 2 * i + 1, lmat_ref, lvl_ref, kc_b.at[i], lfc_b.at[i], vc_b.at[i], st_ref, b_scr)

    @pl.when(s >= n_ctx_steps)
    def _():
        for i in range(n_seq):
            _scan_step(0, 2 * i, lmat_ref, lvl_ref, k_f.at[i], lf_f.at[i], v_f.at[i], st_ref, b_scr,
                       q_f.at[i], o_f.at[i])
            _scan_step(1, 2 * i + 1, lmat_ref, lvl_ref, k_b.at[i], lf_b.at[i], v_b.at[i], st_ref, b_scr,
                       q_b.at[i], o_b.at[i])


def _hgrn_scan(q, k2, lf2, v, kc2, lfc2, vc, batch, seq, ctx_len):
    C = SCAN_CHUNK
    n_lat, n_ctx = seq // C, ctx_len // C
    n_seq = 2 if batch % 2 == 0 else 1
    lmat, lvl = _scan_tables()
    q, v = (a.reshape(batch, seq, HG_W) for a in (q, v))
    k2, lf2 = (a.reshape(2, batch, seq, HG_W) for a in (k2, lf2))
    vc = vc.reshape(batch, ctx_len, HG_W)
    kc2, lfc2 = (a.reshape(2, batch, ctx_len, HG_W) for a in (kc2, lfc2))

    def lat_blk(d):
        def blk(s):
            j = jnp.maximum(s - n_ctx, 0)
            return j if d == 0 else n_lat - 1 - j
        return blk

    def ctx_blk(d):
        def blk(s):
            i = jnp.minimum(s, n_ctx - 1)
            return i if d == 0 else n_ctx - 1 - i
        return blk

    def plain(blk):
        return pl.BlockSpec((n_seq, C, HG_W), lambda b, s: (b, blk(s), 0))

    def specs(blk_of, with_q):
        out = []
        for d in range(2):
            blk = blk_of(d)
            per_dir = pl.BlockSpec((None, n_seq, C, HG_W), lambda b, s, blk=blk, d=d: (d, b, blk(s), 0))
            out += ([plain(blk)] if with_q else []) + [per_dir, per_dir, plain(blk)]
        return out

    full = lambda a: pl.BlockSpec(a.shape, lambda b, s: (0,) * a.ndim)
    o_shape = jax.ShapeDtypeStruct((batch, seq, HG_W), BF16)
    n_chains = 2 * n_seq
    o_f, o_b = pl.pallas_call(
        functools.partial(_scan_kernel, n_ctx),
        grid=(batch // n_seq, n_ctx + n_lat),
        in_specs=[full(lmat), full(lvl)] + specs(lat_blk, True) + specs(ctx_blk, False),
        out_specs=[plain(lat_blk(d)) for d in range(2)],
        out_shape=[o_shape, o_shape],
        scratch_shapes=[pltpu.VMEM((n_chains, HG_DK, HG_W), F32), pltpu.VMEM((n_chains, C, HG_W), F32)],
        compiler_params=_params(("parallel", "arbitrary")),
        name="hgrn_scan",
    )(lmat, lvl, q, k2, lf2, v, q, k2, lf2, v, kc2, lfc2, vc, kc2, lfc2, vc)
    return o_f.reshape(batch * seq, HG_W), o_b.reshape(batch * seq, HG_W)


def _merge_kernel(of_ref, ob_ref, sg_ref, u_ref, vn_ref, sga_ref, sgb_ref, x_ref, gt1_ref, sh2_ref, sc2_ref,
                  gout_ref, ws_ref, bs_ref, wa_ref, wb_ref, wo_ref, gpost_ref, gffn_ref, wr_ref,
                  x1_ref, h2_ref, lg_ref):
    tm = x_ref.shape[0]
    o = of_ref[...].astype(F32) + ob_ref[...].astype(F32)
    sg = sg_ref[...].astype(F32)
    gout = gout_ref[...]
    a = jnp.concatenate(
        [_rms(o[:, h * HG_DK:(h + 1) * HG_DK], gout) * sg[:, h * HG_DK:(h + 1) * HG_DK] for h in range(HG_HEADS)],
        axis=1).astype(BF16)
    vn = vn_ref[...]
    gw = CM_W // CM_GROUPS
    z = jnp.concatenate(
        [jnp.concatenate([_dot(ws_ref[g], vn[c * CM_CHUNK:(c + 1) * CM_CHUNK, g * gw:(g + 1) * gw])
                          for g in range(CM_GROUPS)], axis=1) + bs_ref[...]
         for c in range(tm // CM_CHUNK)], axis=0)
    bm = (u_ref[...].astype(F32) * z).astype(BF16)
    y = sga_ref[...].astype(F32) * _dot(a, wa_ref[...]) + sgb_ref[...].astype(F32) * _dot(bm, wb_ref[...])
    yo = _dot(y.astype(BF16), wo_ref[...])
    x1 = x_ref[...] + gt1_ref[...] * _rms(yo, gpost_ref[...])
    x1_ref[...] = x1
    h2 = (_rms(x1, gffn_ref[...]) * (1.0 + sc2_ref[...]) + sh2_ref[...]).astype(BF16)
    h2_ref[...] = h2
    lg_ref[...] = _dot_nt(wr_ref[...], h2)


def _merge(o_f, o_b, sg, u, vn, sga, sgb, x2, mod3, g_out, ws_b, bs_full, wa_b, wb_b, wo_b, g_post, g_ffn, wr_b,
           seq, tm):
    n = x2.shape[0]
    tpb = seq // tm
    row = lambda w: pl.BlockSpec((tm, w), lambda i: (i, 0))
    full = lambda a: pl.BlockSpec(a.shape, lambda i: (0,) * a.ndim)
    return pl.pallas_call(
        _merge_kernel,
        grid=(n // tm,),
        in_specs=[row(HG_W), row(HG_W), row(HG_W), row(CM_W), row(CM_W),
                  row(D_MODEL), row(D_MODEL), row(D_MODEL), _mod_spec(tpb, 2), _mod_spec(tpb, 3),
                  _mod_spec(tpb, 4), full(g_out), full(ws_b), full(bs_full), full(wa_b), full(wb_b),
                  full(wo_b), full(g_post), full(g_ffn), full(wr_b)],
        out_specs=[row(D_MODEL), row(D_MODEL), pl.BlockSpec((wr_b.shape[0], tm), lambda i: (0, i))],
        out_shape=[jax.ShapeDtypeStruct((n, D_MODEL), F32), jax.ShapeDtypeStruct((n, D_MODEL), BF16),
                   jax.ShapeDtypeStruct((wr_b.shape[0], n), F32)],
        compiler_params=_params(("parallel",)),
        name="merge",
    )(o_f, o_b, sg, u, vn, sga, sgb, x2, mod3, mod3, mod3, g_out, ws_b, bs_full, wa_b, wb_b, wo_b, g_post, g_ffn,
      wr_b)


def _router_kernel(lg_ref, br_ref, w_ref):
    tm = lg_ref.shape[1]
    gsz = N_EXPERTS // N_GROUPS
    scores = jax.nn.sigmoid(lg_ref[:N_EXPERTS, :])
    sel = scores + jnp.concatenate([br_ref[...]] * (tm // br_ref.shape[1]), axis=1)
    neg = -jnp.inf

    def first_max(x, ids, sentinel, axis):
        m = jnp.max(x, axis=axis, keepdims=True)
        return m, jnp.min(jnp.where(x == m, ids, sentinel), axis=axis, keepdims=True)

    sel3 = sel.reshape(N_GROUPS, gsz, tm)
    j3 = lax.broadcasted_iota(jnp.int32, sel3.shape, 1)
    m1, i1 = first_max(sel3, j3, gsz, 1)
    gscore = m1 + jnp.max(jnp.where(j3 == i1, neg, sel3), axis=1, keepdims=True)
    g3 = lax.broadcasted_iota(jnp.int32, gscore.shape, 0)
    keep = jnp.zeros(gscore.shape, F32)
    for _ in range(TOPK_GROUPS):
        _, gi = first_max(gscore, g3, N_GROUPS, 0)
        keep = jnp.where(g3 == gi, 1.0, keep)
        gscore = jnp.where(g3 == gi, neg, gscore)
    x = jnp.where(keep > 0.0, sel3, neg).reshape(N_EXPERTS, tm)
    e_i = lax.broadcasted_iota(jnp.int32, x.shape, 0)
    w = jnp.zeros(x.shape, F32)
    for _ in range(TOP_K):
        _, ei = first_max(x, e_i, N_EXPERTS, 0)
        w = jnp.where(e_i == ei, scores, w)
        x = jnp.where(e_i == ei, neg, x)
    w = w / jnp.sum(w, axis=0, keepdims=True) * ROUTED_SCALE
    w_ref[...] = jnp.concatenate([w, jnp.zeros_like(w)], axis=0).T


def _router(logits_t, b_router_cols, tm):
    rows, n = logits_t.shape
    return pl.pallas_call(
        _router_kernel,
        grid=(n // tm,),
        in_specs=[pl.BlockSpec((rows, tm), lambda i: (0, i)),
                  pl.BlockSpec(b_router_cols.shape, lambda i: (0, 0))],
        out_specs=pl.BlockSpec((tm, rows), lambda i: (i, 0)),
        out_shape=jax.ShapeDtypeStruct((n, rows), F32),
        compiler_params=_params(("parallel",)),
        name="router",
    )(logits_t, b_router_cols)


MOE_TILE = 256
MOE_UNIT = 16
MOE_BLOCK = 512
MOE_MM_ROWS = 512
GATHER_SLOTS = 4
UNITS_PER_BLOCK = MOE_BLOCK // MOE_UNIT
TILE_ROWS = 3072
TILE_UNITS = TILE_ROWS // MOE_UNIT
ROW_CHUNK = 512
N_CHUNKS = TILE_ROWS // ROW_CHUNK
FULL_CHUNKS = MOE_TILE * TOP_K // ROW_CHUNK
CHUNK_UNITS = ROW_CHUNK // MOE_UNIT
KEY_W = 128
DIGIT_BITS = 6
DIGIT = 1 << DIGIT_BITS


def _swiglu_act(h, w_gu):
    gu = _dot(h, w_gu)
    de = gu.shape[1] // 2
    g = gu[:, :de]
    return g * jax.nn.sigmoid(g) * gu[:, de:]


def _token_keys(cw, starts_row):
    t = cw.shape[0]
    routed = cw > 0.0
    t_i = lax.broadcasted_iota(jnp.int32, (t, t), 0)
    s_i = lax.broadcasted_iota(jnp.int32, (t, t), 1)
    rank = _dot((s_i < t_i).astype(BF16), routed.astype(BF16))
    pos = (starts_row + rank).astype(jnp.int32)
    lane = lax.broadcasted_iota(jnp.int32, cw.shape, 1)
    hi = jnp.where(routed, jnp.right_shift(pos, DIGIT_BITS), -1)
    lo = jnp.where(routed, jnp.bitwise_and(pos, DIGIT - 1), -1)
    key_hi = jnp.where(lane < N_EXPERTS, hi, jnp.where(lane == N_EXPERTS, -1, 0))
    key_lo = jnp.where(lane < N_EXPERTS, lo, jnp.where(lane == N_EXPERTS + 1, -1, 0))
    return key_hi.astype(F32).astype(BF16), key_lo.astype(F32).astype(BF16)


def _segment_units(counts):
    return jnp.floor((counts + (MOE_UNIT - 1)) * (1.0 / MOE_UNIT))


def _dispatch_kernel(h_ref, cw_ref, digits_ref, xs_ref, cnt_ref):
    cw = cw_ref[...]
    t = cw.shape[0]
    routed = (cw > 0.0).astype(BF16)
    counts = _dot(jnp.ones((8, t), BF16), routed)
    cnt_ref[...] = counts.astype(jnp.int32)
    units = _segment_units(counts)
    e_i = lax.broadcasted_iota(jnp.int32, (KEY_W, KEY_W), 0)
    f_i = lax.broadcasted_iota(jnp.int32, (KEY_W, KEY_W), 1)
    starts = _dot(units.astype(BF16), (e_i < f_i).astype(BF16)) * MOE_UNIT
    ends = starts + units * MOE_UNIT
    key_hi, key_lo = _token_keys(cw, starts[:1])
    h = h_ref[...]
    lane = lax.broadcasted_iota(jnp.int32, (ROW_CHUNK, KEY_W), 1)
    used_rows = jnp.max(ends)

    def sort_chunk(c):
        rows = slice(c * ROW_CHUNK, (c + 1) * ROW_CHUNK)
        r = (lax.broadcasted_iota(jnp.int32, (ROW_CHUNK, KEY_W), 0) + c * ROW_CHUNK).astype(F32)
        in_seg = (r >= starts[:1]) & (r < ends[:1])
        rmap = jnp.where(lane < N_EXPERTS, in_seg.astype(F32), digits_ref[rows, :].astype(F32)).astype(BF16)
        hit = (_dot_nt(rmap, key_hi) == 0.0) & (_dot_nt(rmap, key_lo) == 0.0)
        xs_ref[rows, :] = _dot(hit.astype(BF16), h).astype(xs_ref.dtype)

    for c in range(N_CHUNKS):
        if c < FULL_CHUNKS:
            sort_chunk(c)
        else:
            pl.when(used_rows > c * ROW_CHUNK)(functools.partial(sort_chunk, c))

            @pl.when(used_rows <= c * ROW_CHUNK)
            def _(c=c):
                xs_ref[c * ROW_CHUNK:(c + 1) * ROW_CHUNK, :] = jnp.zeros((ROW_CHUNK, D_MODEL), xs_ref.dtype)


def _dispatch(h2, cw, digits):
    n = h2.shape[0]
    n_tiles = n // MOE_TILE
    return pl.pallas_call(
        _dispatch_kernel,
        grid=(n_tiles,),
        in_specs=[pl.BlockSpec((MOE_TILE, D_MODEL), lambda i: (i, 0)),
                  pl.BlockSpec((MOE_TILE, KEY_W), lambda i: (i, 0)),
                  pl.BlockSpec(digits.shape, lambda i: (0, 0))],
        out_specs=[pl.BlockSpec((TILE_ROWS, D_MODEL), lambda i: (i, 0)),
                   pl.BlockSpec((8, KEY_W), lambda i: (i, 0))],
        out_shape=[jax.ShapeDtypeStruct((n_tiles * TILE_ROWS, D_MODEL), BF16),
                   jax.ShapeDtypeStruct((n_tiles * 8, KEY_W), jnp.int32)],
        compiler_params=_params(("parallel",)),
        name="moe_dispatch",
    )(h2, cw, digits)


def _unit_copy(src_hbm, unit, dst, slot, pos, sem):
    return pltpu.make_async_copy(
        src_hbm.at[pl.ds(pl.multiple_of(unit * MOE_UNIT, MOE_UNIT), MOE_UNIT)],
        dst.at[slot, pl.ds(pos * MOE_UNIT, MOE_UNIT)], sem.at[slot])


def _experts_kernel(be_ref, src_ref, nb_ref, xs_hbm, wgu_ref, wdn_ref, ys_ref, xbuf, sem, wgu_b, wdn_b):
    j = pl.program_id(0)
    nb = nb_ref[0]

    def copies(blk, slot):
        return [_unit_copy(xs_hbm, src_ref[blk * UNITS_PER_BLOCK + u], xbuf, slot, u, sem)
                for u in range(UNITS_PER_BLOCK)]

    def fetch(blk, slot):
        for cp in copies(blk, slot):
            cp.start()

    ahead = GATHER_SLOTS - 1

    @pl.when(j == 0)
    def _():
        for a in range(ahead):
            fetch(jnp.minimum(a, nb - 1), a)

    @pl.when((j == 0) | (be_ref[j] != be_ref[jnp.maximum(j - 1, 0)]))
    def _():
        wgu_b[...] = wgu_ref[...].astype(BF16)
        wdn_b[...] = wdn_ref[...].astype(BF16)

    @pl.when(j < nb)
    def _():
        slot = j % GATHER_SLOTS
        for cp in copies(j, slot):
            cp.wait()
        fetch(jnp.minimum(j + ahead, nb - 1), (j + ahead) % GATHER_SLOTS)
        for g in range(MOE_BLOCK // MOE_MM_ROWS):
            rows = pl.ds(g * MOE_MM_ROWS, MOE_MM_ROWS)
            act = _swiglu_act(xbuf[slot, rows, :], wgu_b[...])
            ys_ref[rows, :] = _dot(act.astype(BF16), wdn_b[...]).astype(ys_ref.dtype)

    @pl.when(j == nb - 1)
    def _():
        for a in range(1, GATHER_SLOTS):
            for cp in copies(j, (j + a) % GATHER_SLOTS):
                cp.wait()

    @pl.when(j >= nb)
    def _():
        ys_ref[...] = jnp.zeros_like(ys_ref)


def _experts(xs, block_expert, src_units, n_blocks_used, w_gu, w_dn):
    nb_max = block_expert.shape[0]
    grid_spec = pltpu.PrefetchScalarGridSpec(
        num_scalar_prefetch=3,
        grid=(nb_max,),
        in_specs=[pl.BlockSpec(memory_space=pl.ANY),
                  pl.BlockSpec((None, D_MODEL, 2 * D_EXPERT), lambda j, be, src, nb: (be[j], 0, 0)),
                  pl.BlockSpec((None, D_EXPERT, D_MODEL), lambda j, be, src, nb: (be[j], 0, 0))],
        out_specs=pl.BlockSpec((MOE_BLOCK, D_MODEL), lambda j, be, src, nb: (j, 0)),
        scratch_shapes=[pltpu.VMEM((GATHER_SLOTS, MOE_BLOCK, D_MODEL), BF16),
                        pltpu.SemaphoreType.DMA((GATHER_SLOTS,)),
                        pltpu.VMEM((D_MODEL, 2 * D_EXPERT), BF16), pltpu.VMEM((D_EXPERT, D_MODEL), BF16)],
    )
    return pl.pallas_call(
        _experts_kernel,
        grid_spec=grid_spec,
        out_shape=jax.ShapeDtypeStruct((nb_max * MOE_BLOCK, D_MODEL), BF16),
        compiler_params=_params(("arbitrary",)),
        name="moe_experts",
    )(block_expert, src_units, n_blocks_used, xs, w_gu, w_dn)


def _combine_kernel(src_ref, used_ref, ys_hbm, cw_ref, h_ref, x1_ref, gt2_ref, gpost_ref, digits_t_ref,
                    wsgu_ref, wsdn_ref, o_ref, ybuf, sem, acc_ref):
    i = pl.program_id(0)

    def copies(tile, slot, c):
        return [_unit_copy(ys_hbm, src_ref[tile * TILE_UNITS + u], ybuf, slot, u, sem)
                for u in range(c * CHUNK_UNITS, (c + 1) * CHUNK_UNITS)]

    def chunk_used(tile, c):
        return used_ref[tile] > c * CHUNK_UNITS

    def for_used_chunks(tile, fn):
        for c in range(N_CHUNKS):
            if c < FULL_CHUNKS:
                fn(c)
            else:
                pl.when(chunk_used(tile, c))(functools.partial(fn, c))

    def fetch(tile, slot):
        def start(c):
            for cp in copies(tile, slot, c):
                cp.start()

        for_used_chunks(tile, start)

    def wait_all(tile, slot):
        def wait(c):
            for cp in copies(tile, slot, c):
                cp.wait()

        for_used_chunks(tile, wait)

    @pl.when(i == 0)
    def _():
        fetch(0, 0)

    @pl.when(i + 1 < pl.num_programs(0))
    def _():
        fetch(i + 1, (i + 1) % 2)

    cw = cw_ref[...]
    t = cw.shape[0]
    routed = (cw > 0.0).astype(BF16)
    e_i = lax.broadcasted_iota(jnp.int32, (KEY_W, KEY_W), 0)
    f_i = lax.broadcasted_iota(jnp.int32, (KEY_W, KEY_W), 1)
    units = _segment_units(_dot_tn(routed, jnp.ones((t, KEY_W), BF16)))
    starts = _dot((f_i < e_i).astype(BF16), units.astype(BF16)) * MOE_UNIT
    ends = starts + units * MOE_UNIT
    units_row = _segment_units(_dot(jnp.ones((8, t), BF16), routed))
    starts_row = _dot(units_row.astype(BF16), (e_i < f_i).astype(BF16)) * MOE_UNIT
    key_hi, key_lo = _token_keys(cw, starts_row[:1])
    wb = cw.astype(BF16)

    f = _dot(_swiglu_act(h_ref[...], wsgu_ref[...]).astype(BF16), wsdn_ref[...])
    slot = i % 2
    reps = ROW_CHUNK // KEY_W
    starts_c = jnp.concatenate([starts] * reps, axis=1)
    ends_c = jnp.concatenate([ends] * reps, axis=1)
    sub = lax.broadcasted_iota(jnp.int32, (KEY_W, ROW_CHUNK), 0)

    wait_all(i, slot)

    def chunk_sum(c):
        rows = slice(c * ROW_CHUNK, (c + 1) * ROW_CHUNK)
        r = (lax.broadcasted_iota(jnp.int32, (KEY_W, ROW_CHUNK), 1) + c * ROW_CHUNK).astype(F32)
        in_seg = (r >= starts_c) & (r < ends_c)
        rmap_t = jnp.where(sub < N_EXPERTS, in_seg.astype(F32), digits_t_ref[:, rows].astype(F32)).astype(BF16)
        hit = (_dot(key_hi, rmap_t) == 0.0) & (_dot(key_lo, rmap_t) == 0.0)
        w = _dot(wb, rmap_t)
        return _dot(jnp.where(hit, w, 0.0).astype(BF16), ybuf[slot, rows, :])

    for c in range(FULL_CHUNKS):
        f = f + chunk_sum(c)
    acc_ref[...] = f
    for c in range(FULL_CHUNKS, N_CHUNKS):
        @pl.when(chunk_used(i, c))
        def _(c=c):
            acc_ref[...] += chunk_sum(c)
    o_ref[...] = x1_ref[...] + gt2_ref[...] * _rms(acc_ref[...], gpost_ref[...])


def _combine(ys, src_units, used_units, cw, h2, x1, mod3, g_post, digits_t, wsgu_b, wsdn_b, seq):
    n = h2.shape[0]
    tpb = seq // MOE_TILE
    row = lambda w: pl.BlockSpec((MOE_TILE, w), lambda i, src, used: (i, 0))
    full = lambda a: pl.BlockSpec(a.shape, lambda i, src, used: (0,) * a.ndim)
    grid_spec = pltpu.PrefetchScalarGridSpec(
        num_scalar_prefetch=2,
        grid=(n // MOE_TILE,),
        in_specs=[pl.BlockSpec(memory_space=pl.ANY), row(KEY_W), row(D_MODEL), row(D_MODEL),
                  pl.BlockSpec((None, 1, D_MODEL), lambda i, src, used: (i // tpb, 0, 5)), full(g_post),
                  full(digits_t), full(wsgu_b), full(wsdn_b)],
        out_specs=row(D_MODEL),
        scratch_shapes=[pltpu.VMEM((2, TILE_ROWS, D_MODEL), BF16), pltpu.SemaphoreType.DMA((2,)),
                        pltpu.VMEM((MOE_TILE, D_MODEL), F32)],
    )
    return pl.pallas_call(
        _combine_kernel,
        grid_spec=grid_spec,
        out_shape=jax.ShapeDtypeStruct((n, D_MODEL), F32),
        compiler_params=_params(("arbitrary",)),
        name="moe_combine",
    )(src_units, used_units, ys, cw, h2, x1, mod3, g_post, digits_t, wsgu_b, wsdn_b)


def _row_digits():
    r = np.arange(TILE_ROWS)
    d = np.zeros((TILE_ROWS, KEY_W), np.float32)
    d[:, N_EXPERTS] = r // DIGIT
    d[:, N_EXPERTS + 1] = r % DIGIT
    return jnp.asarray(d, dtype=BF16)


def _moe_plan(counts, nb_max):
    n_tiles = counts.shape[0]
    s = (counts + (MOE_UNIT - 1)) // MOE_UNIT
    local = jnp.cumsum(s, axis=1) - s
    cs = jnp.cumsum(s, axis=0)
    per_expert = cs[-1]
    padded = (per_expert + UNITS_PER_BLOCK - 1) // UNITS_PER_BLOCK * UNITS_PER_BLOCK
    g_end = jnp.cumsum(padded)
    g_start = g_end - padded
    seg_start = g_start[None, :] + cs - s
    n_blocks_used = (g_end[-1] // UNITS_PER_BLOCK).astype(jnp.int32).reshape(1)
    jb = jnp.arange(nb_max, dtype=jnp.int32)
    one_e = ((jb[:, None] >= (g_start // UNITS_PER_BLOCK)[None, :])
             & (jb[:, None] < (g_end // UNITS_PER_BLOCK)[None, :])).astype(jnp.int32)
    pick_e = lambda table: jnp.sum(one_e[:, :, None] * table.T[None, :, :], axis=1)
    block_expert = jnp.where(jb < n_blocks_used[0], jnp.sum(one_e * jnp.arange(N_EXPERTS, dtype=jnp.int32), axis=1),
                             N_EXPERTS - 1).astype(jnp.int32)
    cs_b, s_b, local_b = pick_e(cs), pick_e(s), pick_e(local)
    q = (jb * UNITS_PER_BLOCK - jnp.sum(one_e * g_start[None, :], axis=1))[:, None] \
        + jnp.arange(UNITS_PER_BLOCK, dtype=jnp.int32)[None, :]
    tile = jnp.minimum(jnp.sum(cs_b[:, None, :] <= q[:, :, None], axis=2), n_tiles - 1)
    one_t = (tile[:, :, None] == jnp.arange(n_tiles, dtype=jnp.int32)).astype(jnp.int32)
    src = tile * TILE_UNITS + q + jnp.sum(one_t * (local_b - cs_b + s_b)[:, None, :], axis=2)
    valid = q < jnp.sum(one_e * per_expert[None, :], axis=1)[:, None]
    src_units = jnp.where(valid, src, 0).astype(jnp.int32).reshape(-1)
    u = jnp.arange(TILE_UNITS, dtype=jnp.int32)
    seg_end = local + s
    eu = jnp.minimum(jnp.sum(seg_end[:, None, :] <= u[None, :, None], axis=2), N_EXPERTS - 1)
    one_u = (eu[:, :, None] == jnp.arange(N_EXPERTS, dtype=jnp.int32)).astype(jnp.int32)
    back = u[None, :] + jnp.sum(one_u * (seg_start - local)[:, None, :], axis=2)
    back_units = jnp.where(u[None, :] < seg_end[:, -1:], back, 0).astype(jnp.int32).reshape(-1)
    return block_expert, src_units, n_blocks_used, back_units, seg_end[:, -1].astype(jnp.int32)


def _tile(n, pref):
    t = pref
    while n % t:
        t //= 2
    return t


def kernel(x, c, ctx, c_ctx, w_ada, b_ada, g_pre_mix, g_post_mix, g_pre_ffn, g_post_ffn, w_in, lb_logits, g_hgrn_out, cm_ln_g, cm_ln_b, w_spatial, b_spatial, w_branch_a, w_branch_b, w_out, w_router, b_router, w_expert_gu, w_expert_down, w_shared_gu, w_shared_down):
    B, T, D = x.shape
    L = ctx.shape[1]
    assert D == D_MODEL and w_ada.shape[0] == 1 and T % SCAN_CHUNK == 0 and L % SCAN_CHUNK == 0
    l = 0
    row = lambda a: a[l].reshape(1, -1)

    n_rows = -(-(B + 1) // 16) * 16
    cs = jnp.zeros((n_rows, D), F32).at[:B].set(c).at[B].set(c_ctx)
    mod3 = _ada_mod(cs, w_ada[l], row(b_ada)).reshape(n_rows, 1, 6 * D)

    w_in_b = w_in[l].astype(BF16)
    lbl = lb_logits[:, l:l + 2].reshape(4, HG_W)
    x2 = x.reshape(B * T, D)
    q, k2, lf2, v, sg, u, vn, sga, sgb = _proj_lat(
        x2, mod3, row(g_pre_mix), w_in_b, lbl, row(cm_ln_g), row(cm_ln_b), T, _tile(T, 256))
    kc2, lfc2, vc = _proj_ctx(ctx.reshape(B * L, D), mod3, B, row(g_pre_mix), w_in_b[:, HG_W:4 * HG_W], lbl,
                              _tile(B * L, 256))

    o_f, o_b = _hgrn_scan(q, k2, lf2, v, kc2, lfc2, vc, B, T, L)

    bs_full = jnp.repeat(b_spatial[l], CM_W // CM_GROUPS, axis=1)
    x1, h2, logits = _merge(
        o_f, o_b, sg, u, vn, sga, sgb, x2, mod3, row(g_hgrn_out), w_spatial[l].astype(BF16), bs_full,
        w_branch_a[l].astype(BF16), w_branch_b[l].astype(BF16), w_out[l].astype(BF16), row(g_post_mix),
        row(g_pre_ffn), jnp.pad(w_router[l].T, ((0, KEY_W - N_EXPERTS), (0, 0))).astype(BF16), T, _tile(T, 512))

    cw = _router(logits, jnp.broadcast_to(b_router[l][:, None], (N_EXPERTS, 128)), _tile(B * T, 512))

    n_tok = B * T
    n_tiles = n_tok // MOE_TILE
    digits = _row_digits()
    xs, cnt = _dispatch(h2, cw, digits)
    counts = cnt.reshape(n_tiles, 8, KEY_W)[:, 0, :N_EXPERTS]
    max_units = (n_tok * TOP_K + n_tiles * N_EXPERTS * (MOE_UNIT - 1)) // MOE_UNIT + N_EXPERTS * (UNITS_PER_BLOCK - 1)
    nb_max = -(-max_units // UNITS_PER_BLOCK)
    block_expert, src_units, n_blocks_used, back_units, tile_units = _moe_plan(counts, nb_max)
    ys = _experts(xs, block_expert, src_units, n_blocks_used, w_expert_gu[l], w_expert_down[l])
    out = _combine(ys, back_units, tile_units, cw, h2, x1, mod3, row(g_post_ffn), digits.T,
                   w_shared_gu[l].astype(BF16), w_shared_down[l].astype(BF16), T)
    return out.reshape(B, T, D)
```

```python
import functools

import numpy as np
import jax
import jax.numpy as jnp
from jax import lax
from jax.experimental import pallas as pl
from jax.experimental.pallas import tpu as pltpu

F32 = jnp.float32
BF16 = jnp.bfloat16

D_MODEL = 1024
EPS = 1e-6
HG_HEADS = 4
HG_DK = 128
HG_W = HG_HEADS * HG_DK
CM_W = 512
CM_CHUNK = 128
CM_GROUPS = 4
D_IN = 5 * HG_W + 2 * CM_W + 2 * D_MODEL
N_EXPERTS = 64
TOP_K = 8
N_GROUPS = 8
GROUP_BITS = 3
TOPK_GROUPS = 4
D_EXPERT = 256
ROUTED_SCALE = 2.5
SCAN_CHUNK = 128
SUB = 16
VMEM_LIMIT = 56 * 1024 * 1024


def _params(sem):
    return pltpu.CompilerParams(dimension_semantics=sem, vmem_limit_bytes=VMEM_LIMIT)


def _dot(a, b):
    return jnp.dot(a, b, preferred_element_type=F32)


def _dot_nt(a, b):
    return lax.dot_general(a, b, (((1,), (1,)), ((), ())), preferred_element_type=F32)


def _dot_tn(a, b):
    return lax.dot_general(a, b, (((0,), (0,)), ((), ())), preferred_element_type=F32)


def _sigmoid(x):
    return 0.5 * jnp.tanh(0.5 * x) + 0.5


def _rms(x, g):
    return x * lax.rsqrt(jnp.mean(x * x, axis=-1, keepdims=True) + EPS) * g


def _ada_kernel(c_ref, w_ref, b_ref, o_ref):
    c = c_ref[...]
    s = c * _sigmoid(c)
    o_ref[...] = _dot(s.astype(BF16), w_ref[...].astype(BF16)) + b_ref[...]


def _ada_mod(cs, w_ada, b_ada):
    rows = cs.shape[0]
    n_out = w_ada.shape[1]
    return pl.pallas_call(
        _ada_kernel,
        grid=(n_out // D_MODEL,),
        in_specs=[
            pl.BlockSpec((rows, D_MODEL), lambda j: (0, 0)),
            pl.BlockSpec((D_MODEL, D_MODEL), lambda j: (0, j)),
            pl.BlockSpec((1, D_MODEL), lambda j: (0, j)),
        ],
        out_specs=pl.BlockSpec((rows, D_MODEL), lambda j: (0, j)),
        out_shape=jax.ShapeDtypeStruct((rows, n_out), F32),
        compiler_params=_params(("parallel",)),
        name="ada_mod",
    )(cs, w_ada, b_ada)


def _lower_bounds(lbl):
    out = []
    for d in range(2):
        l0, l1 = lbl[2 * d:2 * d + 1], lbl[2 * d + 1:2 * d + 2]
        m = jnp.maximum(l0, l1)
        e0, e1 = jnp.exp(l0 - m), jnp.exp(l1 - m)
        out.append(e0 / (e0 + e1))
    return out


def _prenorm(x_ref, sh_ref, sc_ref, g_ref):
    return (_rms(x_ref[...], g_ref[...]) * (1.0 + sc_ref[...]) + sh_ref[...]).astype(BF16)


def _gates(z, lb, k_ref, lf_ref, d):
    half_t = 0.5 * jnp.tanh(0.5 * z)
    k_ref[d] = ((1.0 - lb) * (0.5 - half_t)).astype(k_ref.dtype)
    lf_ref[d] = jnp.log2(lb + (1.0 - lb) * (0.5 + half_t))


def _proj_lat_kernel(x_ref, sh_ref, sc_ref, g_ref, w_ref, lbl_ref, lng_ref, lnb_ref,
                     q_ref, k_ref, lf_ref, v_ref, sg_ref, u_ref, vn_ref, sga_ref, sgb_ref):
    hb = _prenorm(x_ref, sh_ref, sc_ref, g_ref)
    lbs = _lower_bounds(lbl_ref[...])

    def mm(lo, width):
        return _dot(hb, w_ref[:, lo:lo + width])

    z = mm(0, HG_W)
    q_ref[...] = (z * _sigmoid(z)).astype(q_ref.dtype)
    for d in range(2):
        _gates(mm((1 + d) * HG_W, HG_W), lbs[d], k_ref, lf_ref, d)
    z = mm(4 * HG_W, HG_W)
    sg_ref[...] = (z * _sigmoid(z)).astype(sg_ref.dtype)
    u_ref[...] = jax.nn.gelu(mm(5 * HG_W, CM_W)).astype(u_ref.dtype)
    vv = jax.nn.gelu(mm(5 * HG_W + CM_W, CM_W))
    vc = vv - jnp.mean(vv, axis=-1, keepdims=True)
    vn = vc * lax.rsqrt(jnp.mean(vc * vc, axis=-1, keepdims=True) + EPS)
    vn_ref[...] = (vn * lng_ref[...] + lnb_ref[...]).astype(vn_ref.dtype)
    base = 5 * HG_W + 2 * CM_W
    sga_ref[...] = _sigmoid(mm(base, D_MODEL)).astype(sga_ref.dtype)
    sgb_ref[...] = _sigmoid(mm(base + D_MODEL, D_MODEL)).astype(sgb_ref.dtype)
    v_ref[...] = mm(3 * HG_W, HG_W).astype(v_ref.dtype)


def _mod_spec(rows_per_batch_tiles, col):
    return pl.BlockSpec((None, 1, D_MODEL), lambda i: (i // rows_per_batch_tiles, 0, col))


def _proj_lat(x2, mod3, g_pre, w_in_b, lbl, ln_g, ln_b, seq, tm):
    n = x2.shape[0]
    tpb = seq // tm
    row = lambda w: pl.BlockSpec((tm, w), lambda i: (i, 0))
    row2 = pl.BlockSpec((2, tm, HG_W), lambda i: (0, i, 0))
    full = lambda a: pl.BlockSpec(a.shape, lambda i: (0,) * a.ndim)
    outs = [
        (row(HG_W), jax.ShapeDtypeStruct((n, HG_W), BF16)),
        (row2, jax.ShapeDtypeStruct((2, n, HG_W), BF16)),
        (row2, jax.ShapeDtypeStruct((2, n, HG_W), F32)),
        (row(HG_W), jax.ShapeDtypeStruct((n, HG_W), BF16)),
        (row(HG_W), jax.ShapeDtypeStruct((n, HG_W), BF16)),
        (row(CM_W), jax.ShapeDtypeStruct((n, CM_W), BF16)),
        (row(CM_W), jax.ShapeDtypeStruct((n, CM_W), BF16)),
        (row(D_MODEL), jax.ShapeDtypeStruct((n, D_MODEL), BF16)),
        (row(D_MODEL), jax.ShapeDtypeStruct((n, D_MODEL), BF16)),
    ]
    return pl.pallas_call(
        _proj_lat_kernel,
        grid=(n // tm,),
        in_specs=[row(D_MODEL), _mod_spec(tpb, 0), _mod_spec(tpb, 1), full(g_pre), full(w_in_b),
                  full(lbl), full(ln_g), full(ln_b)],
        out_specs=[o[0] for o in outs],
        out_shape=[o[1] for o in outs],
        compiler_params=_params(("parallel",)),
        name="proj_lat",
    )(x2, mod3, mod3, g_pre, w_in_b, lbl, ln_g, ln_b)


def _proj_ctx_kernel(x_ref, sh_ref, sc_ref, g_ref, w_ref, lbl_ref, k_ref, lf_ref, v_ref):
    hb = _prenorm(x_ref, sh_ref, sc_ref, g_ref)
    lbs = _lower_bounds(lbl_ref[...])
    for d in range(2):
        _gates(_dot(hb, w_ref[:, d * HG_W:(d + 1) * HG_W]), lbs[d], k_ref, lf_ref, d)
    v_ref[...] = _dot(hb, w_ref[:, 2 * HG_W:3 * HG_W]).astype(v_ref.dtype)


def _proj_ctx(c2, mod3, ctx_row, g_pre, w_ctx_b, lbl, tm):
    n = c2.shape[0]
    row = lambda w: pl.BlockSpec((tm, w), lambda i: (i, 0))
    row2 = pl.BlockSpec((2, tm, HG_W), lambda i: (0, i, 0))
    full = lambda a: pl.BlockSpec(a.shape, lambda i: (0,) * a.ndim)
    mod = lambda col: pl.BlockSpec((None, 1, D_MODEL), lambda i: (ctx_row, 0, col))
    return pl.pallas_call(
        _proj_ctx_kernel,
        grid=(n // tm,),
        in_specs=[row(D_MODEL), mod(0), mod(1), full(g_pre), full(w_ctx_b), full(lbl)],
        out_specs=[row2, row2, row(HG_W)],
        out_shape=[jax.ShapeDtypeStruct((2, n, HG_W), BF16), jax.ShapeDtypeStruct((2, n, HG_W), F32),
                   jax.ShapeDtypeStruct((n, HG_W), BF16)],
        compiler_params=_params(("parallel",)),
        name="proj_ctx",
    )(c2, mod3, mod3, g_pre, w_ctx_b, lbl)


def _scan_tables():
    C = SCAN_CHUNK
    t = np.arange(C)
    lmats, lvls = [], []
    for d in range(2):
        p = t if d == 0 else C - 1 - t
        pt, ps = p[:, None], p[None, :]
        lmat = (ps <= pt).astype(np.float32)
        lmats.append(np.concatenate([lmat, lmat], axis=1))
        lvl = np.full((C, C), -1, np.int32)
        lvl[(pt // SUB == ps // SUB) & (ps <= pt)] = 0
        half, idx = SUB, 1
        while half < C:
            span = 2 * half
            lvl[(pt // span == ps // span) & ((pt // half) % 2 == 1) & ((ps // half) % 2 == 0)] = idx
            half, idx = span, idx + 1
        lvls.append(lvl)
    return jnp.asarray(np.stack(lmats), dtype=BF16), jnp.asarray(np.stack(lvls))


def _scan_step(d, slot, lmat_ref, lvl_ref, k_ref, lf_ref, v_ref, st_ref, b_scr, q_ref=None, o_ref=None):
    C = SCAN_CHUNK
    lf = lf_ref[...]
    hi = lf.astype(BF16)
    lo = (lf - hi.astype(F32)).astype(BF16)
    b_scr[slot] = _dot(lmat_ref[d], jnp.concatenate([hi, lo], axis=0))
    b = b_scr[slot]

    def row(i):
        return b_scr[slot, pl.ds(i, 1), :]

    b_last = row(C - 1 if d == 0 else 0)
    k = k_ref[...]
    v = v_ref[...]
    khat = k * jnp.exp2(b_last - b).astype(BF16)
    decay = jnp.exp2(b_last)

    if q_ref is not None:
        q = q_ref[...]
        qhat = q * jnp.exp2(b).astype(BF16)
        e0 = jnp.concatenate([b[m * SUB:(m + 1) * SUB] - row(m * SUB + SUB // 2 - 1 + d)
                              for m in range(C // SUB)], axis=0)
        factors = [(jnp.exp2(e0).astype(BF16), jnp.exp2(-e0).astype(BF16))]
        half = SUB
        while half < C:
            span = 2 * half
            e = jnp.concatenate([b[m * span:(m + 1) * span] - row(m * span + half - 1 + d)
                                 for m in range(C // span)], axis=0)
            w = jnp.exp2(-jnp.abs(e)).astype(BF16)
            factors.append((w, w))
            half = span
        lvl = lvl_ref[d]
        masks = [lvl == i for i in range(len(factors))]

    for h in range(HG_HEADS):
        sl = slice(h * HG_DK, (h + 1) * HG_DK)
        st = st_ref[slot, :, sl]
        if q_ref is not None:
            a = jnp.zeros((C, C), F32)
            for (wq, wk), mask in zip(factors, masks):
                a = jnp.where(mask, _dot_nt(q[:, sl] * wq[:, sl], k[:, sl] * wk[:, sl]), a)
            o = _dot(a.astype(BF16), v[:, sl]) + _dot_nt(qhat[:, sl], st.astype(BF16))
            o_ref[:, sl] = o.astype(o_ref.dtype)
        st_ref[slot, :, sl] = decay[:, sl] * st + _dot_tn(v[:, sl], khat[:, sl])


def _scan_kernel(n_ctx_steps, lmat_ref, lvl_ref, q_f, k_f, lf_f, v_f, q_b, k_b, lf_b, v_b,
                 kc_f, lfc_f, vc_f, kc_b, lfc_b, vc_b, o_f, o_b, st_ref, b_scr):
    s = pl.program_id(1)
    n_seq = q_f.shape[0]

    @pl.when(s == 0)
    def _():
        st_ref[...] = jnp.zeros_like(st_ref)

    @pl.when(s < n_ctx_steps)
    def _():
        for i in range(n_seq):
            _scan_step(0, 2 * i, lmat_ref, lvl_ref, kc_f.at[i], lfc_f.at[i], vc_f.at[i], st_ref, b_scr)
            _scan_step(1, 2 * i + 1, lmat_ref, lvl_ref, kc_b.at[i], lfc_b.at[i], vc_b.at[i], st_ref, b_scr)

    @pl.when(s >= n_ctx_steps)
    def _():
        for i in range(n_seq):
            _scan_step(0, 2 * i, lmat_ref, lvl_ref, k_f.at[i], lf_f.at[i], v_f.at[i], st_ref, b_scr,
                       q_f.at[i], o_f.at[i])
            _scan_step(1, 2 * i + 1, lmat_ref, lvl_ref, k_b.at[i], lf_b.at[i], v_b.at[i], st_ref, b_scr,
                       q_b.at[i], o_b.at[i])


def _hgrn_scan(q, k2, lf2, v, kc2, lfc2, vc, batch, seq, ctx_len):
    C = SCAN_CHUNK
    n_lat, n_ctx = seq // C, ctx_len // C
    n_seq = 2 if batch % 2 == 0 else 1
    lmat, lvl = _scan_tables()
    q, v = (a.reshape(batch, seq, HG_W) for a in (q, v))
    k2, lf2 = (a.reshape(2, batch, seq, HG_W) for a in (k2, lf2))
    vc = vc.reshape(batch, ctx_len, HG_W)
    kc2, lfc2 = (a.reshape(2, batch, ctx_len, HG_W) for a in (kc2, lfc2))

    def lat_blk(d):
        def blk(s):
            j = jnp.maximum(s - n_ctx, 0)
            return j if d == 0 else n_lat - 1 - j
        return blk

    def ctx_blk(d):
        def blk(s):
            i = jnp.minimum(s, n_ctx - 1)
            return i if d == 0 else n_ctx - 1 - i
        return blk

    def plain(blk):
        return pl.BlockSpec((n_seq, C, HG_W), lambda b, s: (b, blk(s), 0))

    def specs(blk_of, with_q):
        out = []
        for d in range(2):
            blk = blk_of(d)
            per_dir = pl.BlockSpec((None, n_seq, C, HG_W), lambda b, s, blk=blk, d=d: (d, b, blk(s), 0))
            out += ([plain(blk)] if with_q else []) + [per_dir, per_dir, plain(blk)]
        return out

    full = lambda a: pl.BlockSpec(a.shape, lambda b, s: (0,) * a.ndim)
    o_shape = jax.ShapeDtypeStruct((batch, seq, HG_W), BF16)
    n_chains = 2 * n_seq
    o_f, o_b = pl.pallas_call(
        functools.partial(_scan_kernel, n_ctx),
        grid=(batch // n_seq, n_ctx + n_lat),
        in_specs=[full(lmat), full(lvl)] + specs(lat_blk, True) + specs(ctx_blk, False),
        out_specs=[plain(lat_blk(d)) for d in range(2)],
        out_shape=[o_shape, o_shape],
        scratch_shapes=[pltpu.VMEM((n_chains, HG_DK, HG_W), F32), pltpu.VMEM((n_chains, C, HG_W), F32)],
        compiler_params=_params(("parallel", "arbitrary")),
        name="hgrn_scan",
    )(lmat, lvl, q, k2, lf2, v, q, k2, lf2, v, kc2, lfc2, vc, kc2, lfc2, vc)
    return o_f.reshape(batch * seq, HG_W), o_b.reshape(batch * seq, HG_W)


def _merge_kernel(of_ref, ob_ref, sg_ref, u_ref, vn_ref, sga_ref, sgb_ref, x_ref, gt1_ref, sh2_ref, sc2_ref,
                  gout_ref, ws_ref, bs_ref, wa_ref, wb_ref, wo_ref, gpost_ref, gffn_ref, wr_ref,
                  x1_ref, h2_ref, lg_ref):
    tm = x_ref.shape[0]
    o = of_ref[...].astype(F32) + ob_ref[...].astype(F32)
    sg = sg_ref[...].astype(F32)
    gout = gout_ref[...]
    a = jnp.concatenate(
        [_rms(o[:, h * HG_DK:(h + 1) * HG_DK], gout) * sg[:, h * HG_DK:(h + 1) * HG_DK] for h in range(HG_HEADS)],
        axis=1).astype(BF16)
    vn = vn_ref[...]
    gw = CM_W // CM_GROUPS
    z = jnp.concatenate(
        [jnp.concatenate([_dot(ws_ref[g], vn[c * CM_CHUNK:(c + 1) * CM_CHUNK, g * gw:(g + 1) * gw])
                          for g in range(CM_GROUPS)], axis=1) + bs_ref[...]
         for c in range(tm // CM_CHUNK)], axis=0)
    bm = (u_ref[...].astype(F32) * z).astype(BF16)
    y = sga_ref[...].astype(F32) * _dot(a, wa_ref[...]) + sgb_ref[...].astype(F32) * _dot(bm, wb_ref[...])
    yo = _dot(y.astype(BF16), wo_ref[...])
    x1 = x_ref[...] + gt1_ref[...] * _rms(yo, gpost_ref[...])
    x1_ref[...] = x1
    h2 = (_rms(x1, gffn_ref[...]) * (1.0 + sc2_ref[...]) + sh2_ref[...]).astype(BF16)
    h2_ref[...] = h2
    lg_ref[...] = _dot_nt(wr_ref[...], h2)


def _merge(o_f, o_b, sg, u, vn, sga, sgb, x2, mod3, g_out, ws_b, bs_full, wa_b, wb_b, wo_b, g_post, g_ffn, wr_b,
           seq, tm):
    n = x2.shape[0]
    tpb = seq // tm
    row = lambda w: pl.BlockSpec((tm, w), lambda i: (i, 0))
    full = lambda a: pl.BlockSpec(a.shape, lambda i: (0,) * a.ndim)
    return pl.pallas_call(
        _merge_kernel,
        grid=(n // tm,),
        in_specs=[row(HG_W), row(HG_W), row(HG_W), row(CM_W), row(CM_W),
                  row(D_MODEL), row(D_MODEL), row(D_MODEL), _mod_spec(tpb, 2), _mod_spec(tpb, 3),
                  _mod_spec(tpb, 4), full(g_out), full(ws_b), full(bs_full), full(wa_b), full(wb_b),
                  full(wo_b), full(g_post), full(g_ffn), full(wr_b)],
        out_specs=[row(D_MODEL), row(D_MODEL), pl.BlockSpec((wr_b.shape[0], tm), lambda i: (0, i))],
        out_shape=[jax.ShapeDtypeStruct((n, D_MODEL), F32), jax.ShapeDtypeStruct((n, D_MODEL), BF16),
                   jax.ShapeDtypeStruct((wr_b.shape[0], n), F32)],
        compiler_params=_params(("parallel",)),
        name="merge",
    )(o_f, o_b, sg, u, vn, sga, sgb, x2, mod3, mod3, mod3, g_out, ws_b, bs_full, wa_b, wb_b, wo_b, g_post, g_ffn,
      wr_b)


def _router_kernel(lg_ref, br_ref, w_ref):
    tm = lg_ref.shape[1]
    gsz = N_EXPERTS // N_GROUPS
    scores = _sigmoid(lg_ref[:N_EXPERTS, :])
    sel = scores + jnp.concatenate([br_ref[...]] * (tm // br_ref.shape[1]), axis=1)
    neg = -jnp.inf

    def first_max(x, ids, sentinel, axis):
        m = jnp.max(x, axis=axis, keepdims=True)
        return m, jnp.min(jnp.where(x == m, ids, sentinel), axis=axis, keepdims=True)

    sel3 = sel.reshape(N_GROUPS, gsz, tm)
    j3 = lax.broadcasted_iota(jnp.int32, sel3.shape, 1)
    m1, i1 = first_max(sel3, j3, gsz, 1)
    gscore = m1 + jnp.max(jnp.where(j3 == i1, neg, sel3), axis=1, keepdims=True)
    g3 = lax.broadcasted_iota(jnp.int32, gscore.shape, 0)
    keep = jnp.zeros(gscore.shape, F32)
    for _ in range(TOPK_GROUPS):
        _, gi = first_max(gscore, g3, N_GROUPS, 0)
        keep = jnp.where(g3 == gi, 1.0, keep)
        gscore = jnp.where(g3 == gi, neg, gscore)
    x = jnp.where(keep > 0.0, sel3, neg).reshape(N_EXPERTS, tm)
    e_i = lax.broadcasted_iota(jnp.int32, x.shape, 0)
    w = jnp.zeros(x.shape, F32)
    for _ in range(TOP_K):
        _, ei = first_max(x, e_i, N_EXPERTS, 0)
        w = jnp.where(e_i == ei, scores, w)
        x = jnp.where(e_i == ei, neg, x)
    w = w / jnp.sum(w, axis=0, keepdims=True) * ROUTED_SCALE
    w_ref[...] = jnp.concatenate([w, jnp.zeros_like(w)], axis=0).T


def _router(logits_t, b_router_cols, tm):
    rows, n = logits_t.shape
    return pl.pallas_call(
        _router_kernel,
        grid=(n // tm,),
        in_specs=[pl.BlockSpec((rows, tm), lambda i: (0, i)),
                  pl.BlockSpec(b_router_cols.shape, lambda i: (0, 0))],
        out_specs=pl.BlockSpec((tm, rows), lambda i: (i, 0)),
        out_shape=jax.ShapeDtypeStruct((n, rows), F32),
        compiler_params=_params(("parallel",)),
        name="router",
    )(logits_t, b_router_cols)


MOE_TILE = 256
MOE_UNIT = 16
MOE_BLOCK = 512
MOE_MM_ROWS = 512
GATHER_SLOTS = 3
UNITS_PER_BLOCK = MOE_BLOCK // MOE_UNIT
TILE_ROWS = 3072
TILE_UNITS = TILE_ROWS // MOE_UNIT
ROW_CHUNK = 512
N_CHUNKS = TILE_ROWS // ROW_CHUNK
FULL_CHUNKS = MOE_TILE * TOP_K // ROW_CHUNK
CHUNK_UNITS = ROW_CHUNK // MOE_UNIT
KEY_W = 128
DIGIT_BITS = 6
DIGIT = 1 << DIGIT_BITS


def _swiglu_act(h, w_gu):
    gu = _dot(h, w_gu)
    de = gu.shape[1] // 2
    g = gu[:, :de]
    return g * _sigmoid(g) * gu[:, de:]


def _token_keys(cw, starts_row):
    t = cw.shape[0]
    routed = cw > 0.0
    t_i = lax.broadcasted_iota(jnp.int32, (t, t), 0)
    s_i = lax.broadcasted_iota(jnp.int32, (t, t), 1)
    rank = _dot((s_i < t_i).astype(BF16), routed.astype(BF16))
    pos = (starts_row + rank).astype(jnp.int32)
    lane = lax.broadcasted_iota(jnp.int32, cw.shape, 1)
    hi = jnp.where(routed, jnp.right_shift(pos, DIGIT_BITS), -1)
    lo = jnp.where(routed, jnp.bitwise_and(pos, DIGIT - 1), -1)
    key_hi = jnp.where(lane < N_EXPERTS, hi, jnp.where(lane == N_EXPERTS, -1, 0))
    key_lo = jnp.where(lane < N_EXPERTS, lo, jnp.where(lane == N_EXPERTS + 1, -1, 0))
    return key_hi.astype(F32).astype(BF16), key_lo.astype(F32).astype(BF16)


def _segment_units(counts):
    return jnp.floor((counts + (MOE_UNIT - 1)) * (1.0 / MOE_UNIT))


def _dispatch_kernel(h_ref, cw_ref, digits_ref, xs_ref, cnt_ref):
    cw = cw_ref[...]
    t = cw.shape[0]
    routed = (cw > 0.0).astype(BF16)
    counts = _dot(jnp.ones((8, t), BF16), routed)
    cnt_ref[...] = counts.astype(jnp.int32)
    units = _segment_units(counts)
    e_i = lax.broadcasted_iota(jnp.int32, (KEY_W, KEY_W), 0)
    f_i = lax.broadcasted_iota(jnp.int32, (KEY_W, KEY_W), 1)
    starts = _dot(units.astype(BF16), (e_i < f_i).astype(BF16)) * MOE_UNIT
    ends = starts + units * MOE_UNIT
    key_hi, key_lo = _token_keys(cw, starts[:1])
    h = h_ref[...]
    lane = lax.broadcasted_iota(jnp.int32, (ROW_CHUNK, KEY_W), 1)
    used_rows = jnp.max(ends)

    def sort_chunk(c):
        rows = slice(c * ROW_CHUNK, (c + 1) * ROW_CHUNK)
        r = (lax.broadcasted_iota(jnp.int32, (ROW_CHUNK, KEY_W), 0) + c * ROW_CHUNK).astype(F32)
        in_seg = (r >= starts[:1]) & (r < ends[:1])
        rmap = jnp.where(lane < N_EXPERTS, in_seg.astype(F32), digits_ref[rows, :].astype(F32)).astype(BF16)
        hit = (_dot_nt(rmap, key_hi) == 0.0) & (_dot_nt(rmap, key_lo) == 0.0)
        xs_ref[rows, :] = _dot(hit.astype(BF16), h).astype(xs_ref.dtype)

    for c in range(N_CHUNKS):
        if c < FULL_CHUNKS:
            sort_chunk(c)
        else:
            pl.when(used_rows > c * ROW_CHUNK)(functools.partial(sort_chunk, c))

            @pl.when(used_rows <= c * ROW_CHUNK)
            def _(c=c):
                xs_ref[c * ROW_CHUNK:(c + 1) * ROW_CHUNK, :] = jnp.zeros((ROW_CHUNK, D_MODEL), xs_ref.dtype)


def _dispatch(h2, cw, digits):
    n = h2.shape[0]
    n_tiles = n // MOE_TILE
    return pl.pallas_call(
        _dispatch_kernel,
        grid=(n_tiles,),
        in_specs=[pl.BlockSpec((MOE_TILE, D_MODEL), lambda i: (i, 0)),
                  pl.BlockSpec((MOE_TILE, KEY_W), lambda i: (i, 0)),
                  pl.BlockSpec(digits.shape, lambda i: (0, 0))],
        out_specs=[pl.BlockSpec((TILE_ROWS, D_MODEL), lambda i: (i, 0)),
                   pl.BlockSpec((8, KEY_W), lambda i: (i, 0))],
        out_shape=[jax.ShapeDtypeStruct((n_tiles * TILE_ROWS, D_MODEL), BF16),
                   jax.ShapeDtypeStruct((n_tiles * 8, KEY_W), jnp.int32)],
        compiler_params=_params(("parallel",)),
        name="moe_dispatch",
    )(h2, cw, digits)


def _unit_copy(src_hbm, unit, dst, slot, pos, sem):
    return pltpu.make_async_copy(
        src_hbm.at[pl.ds(pl.multiple_of(unit * MOE_UNIT, MOE_UNIT), MOE_UNIT)],
        dst.at[slot, pl.ds(pos * MOE_UNIT, MOE_UNIT)], sem.at[slot])


def _experts_kernel(be_ref, src_ref, nb_ref, xs_hbm, wgu_ref, wdn_ref, ys_ref, xbuf, sem, wgu_b, wdn_b):
    j = pl.program_id(0)
    nb = nb_ref[0]

    def copies(blk, slot):
        return [_unit_copy(xs_hbm, src_ref[blk * UNITS_PER_BLOCK + u], xbuf, slot, u, sem)
                for u in range(UNITS_PER_BLOCK)]

    def fetch(blk, slot):
        for cp in copies(blk, slot):
            cp.start()

    ahead = GATHER_SLOTS - 1

    @pl.when(j == 0)
    def _():
        for a in range(ahead):
            fetch(jnp.minimum(a, nb - 1), a)

    @pl.when((j == 0) | (be_ref[j] != be_ref[jnp.maximum(j - 1, 0)]))
    def _():
        wgu_b[...] = wgu_ref[...].astype(BF16)
        wdn_b[...] = wdn_ref[...].astype(BF16)

    @pl.when(j < nb)
    def _():
        slot = j % GATHER_SLOTS
        for cp in copies(j, slot):
            cp.wait()
        fetch(jnp.minimum(j + ahead, nb - 1), (j + ahead) % GATHER_SLOTS)
        for g in range(MOE_BLOCK // MOE_MM_ROWS):
            rows = pl.ds(g * MOE_MM_ROWS, MOE_MM_ROWS)
            act = _swiglu_act(xbuf[slot, rows, :], wgu_b[...])
            ys_ref[rows, :] = _dot(act.astype(BF16), wdn_b[...]).astype(ys_ref.dtype)

    @pl.when(j == nb - 1)
    def _():
        for a in range(1, GATHER_SLOTS):
            for cp in copies(j, (j + a) % GATHER_SLOTS):
                cp.wait()

    @pl.when(j >= nb)
    def _():
        ys_ref[...] = jnp.zeros_like(ys_ref)


def _experts(xs, block_expert, src_units, n_blocks_used, w_gu, w_dn):
    nb_max = block_expert.shape[0]
    grid_spec = pltpu.PrefetchScalarGridSpec(
        num_scalar_prefetch=3,
        grid=(nb_max,),
        in_specs=[pl.BlockSpec(memory_space=pl.ANY),
                  pl.BlockSpec((None, D_MODEL, 2 * D_EXPERT), lambda j, be, src, nb: (be[j], 0, 0)),
                  pl.BlockSpec((None, D_EXPERT, D_MODEL), lambda j, be, src, nb: (be[j], 0, 0))],
        out_specs=pl.BlockSpec((MOE_BLOCK, D_MODEL), lambda j, be, src, nb: (j, 0)),
        scratch_shapes=[pltpu.VMEM((GATHER_SLOTS, MOE_BLOCK, D_MODEL), BF16),
                        pltpu.SemaphoreType.DMA((GATHER_SLOTS,)),
                        pltpu.VMEM((D_MODEL, 2 * D_EXPERT), BF16), pltpu.VMEM((D_EXPERT, D_MODEL), BF16)],
    )
    return pl.pallas_call(
        _experts_kernel,
        grid_spec=grid_spec,
        out_shape=jax.ShapeDtypeStruct((nb_max * MOE_BLOCK, D_MODEL), BF16),
        compiler_params=_params(("arbitrary",)),
        name="moe_experts",
    )(block_expert, src_units, n_blocks_used, xs, w_gu, w_dn)


def _combine_kernel(src_ref, used_ref, ys_hbm, cw_ref, h_ref, x1_ref, gt2_ref, gpost_ref, digits_t_ref,
                    wsgu_ref, wsdn_ref, o_ref, ybuf, sem, acc_ref):
    i = pl.program_id(0)

    def copies(tile, slot, c):
        return [_unit_copy(ys_hbm, src_ref[tile * TILE_UNITS + u], ybuf, slot, u, sem)
                for u in range(c * CHUNK_UNITS, (c + 1) * CHUNK_UNITS)]

    def chunk_used(tile, c):
        return used_ref[tile] > c * CHUNK_UNITS

    def for_used_chunks(tile, fn):
        for c in range(N_CHUNKS):
            if c < FULL_CHUNKS:
                fn(c)
            else:
                pl.when(chunk_used(tile, c))(functools.partial(fn, c))

    def fetch(tile, slot):
        def start(c):
            for cp in copies(tile, slot, c):
                cp.start()

        for_used_chunks(tile, start)

    def wait_all(tile, slot):
        def wait(c):
            for cp in copies(tile, slot, c):
                cp.wait()

        for_used_chunks(tile, wait)

    @pl.when(i == 0)
    def _():
        fetch(0, 0)

    @pl.when(i + 1 < pl.num_programs(0))
    def _():
        fetch(i + 1, (i + 1) % 2)

    cw = cw_ref[...]
    t = cw.shape[0]
    routed = (cw > 0.0).astype(BF16)
    e_i = lax.broadcasted_iota(jnp.int32, (KEY_W, KEY_W), 0)
    f_i = lax.broadcasted_iota(jnp.int32, (KEY_W, KEY_W), 1)
    units = _segment_units(_dot_tn(routed, jnp.ones((t, KEY_W), BF16)))
    starts = _dot((f_i < e_i).astype(BF16), units.astype(BF16)) * MOE_UNIT
    ends = starts + units * MOE_UNIT
    units_row = _segment_units(_dot(jnp.ones((8, t), BF16), routed))
    starts_row = _dot(units_row.astype(BF16), (e_i < f_i).astype(BF16)) * MOE_UNIT
    key_hi, key_lo = _token_keys(cw, starts_row[:1])
    wb = cw.astype(BF16)

    f = _dot(_swiglu_act(h_ref[...], wsgu_ref[...]).astype(BF16), wsdn_ref[...])
    slot = i % 2
    reps = ROW_CHUNK // KEY_W
    starts_c = jnp.concatenate([starts] * reps, axis=1)
    ends_c = jnp.concatenate([ends] * reps, axis=1)
    sub = lax.broadcasted_iota(jnp.int32, (KEY_W, ROW_CHUNK), 0)

    wait_all(i, slot)

    def chunk_sum(c):
        rows = slice(c * ROW_CHUNK, (c + 1) * ROW_CHUNK)
        r = (lax.broadcasted_iota(jnp.int32, (KEY_W, ROW_CHUNK), 1) + c * ROW_CHUNK).astype(F32)
        in_seg = (r >= starts_c) & (r < ends_c)
        rmap_t = jnp.where(sub < N_EXPERTS, in_seg.astype(F32), digits_t_ref[:, rows].astype(F32)).astype(BF16)
        hit = (_dot(key_hi, rmap_t) == 0.0) & (_dot(key_lo, rmap_t) == 0.0)
        w = _dot(wb, rmap_t)
        return _dot(jnp.where(hit, w, 0.0).astype(BF16), ybuf[slot, rows, :])

    for c in range(FULL_CHUNKS):
        f = f + chunk_sum(c)
    acc_ref[...] = f
    for c in range(FULL_CHUNKS, N_CHUNKS):
        @pl.when(chunk_used(i, c))
        def _(c=c):
            acc_ref[...] += chunk_sum(c)
    o_ref[...] = x1_ref[...] + gt2_ref[...] * _rms(acc_ref[...], gpost_ref[...])


def _combine(ys, src_units, used_units, cw, h2, x1, mod3, g_post, digits_t, wsgu_b, wsdn_b, seq):
    n = h2.shape[0]
    tpb = seq // MOE_TILE
    row = lambda w: pl.BlockSpec((MOE_TILE, w), lambda i, src, used: (i, 0))
    full = lambda a: pl.BlockSpec(a.shape, lambda i, src, used: (0,) * a.ndim)
    grid_spec = pltpu.PrefetchScalarGridSpec(
        num_scalar_prefetch=2,
        grid=(n // MOE_TILE,),
        in_specs=[pl.BlockSpec(memory_space=pl.ANY), row(KEY_W), row(D_MODEL), row(D_MODEL),
                  pl.BlockSpec((None, 1, D_MODEL), lambda i, src, used: (i // tpb, 0, 5)), full(g_post),
                  full(digits_t), full(wsgu_b), full(wsdn_b)],
        out_specs=row(D_MODEL),
        scratch_shapes=[pltpu.VMEM((2, TILE_ROWS, D_MODEL), BF16), pltpu.SemaphoreType.DMA((2,)),
                        pltpu.VMEM((MOE_TILE, D_MODEL), F32)],
    )
    return pl.pallas_call(
        _combine_kernel,
        grid_spec=grid_spec,
        out_shape=jax.ShapeDtypeStruct((n, D_MODEL), F32),
        compiler_params=_params(("arbitrary",)),
        name="moe_combine",
    )(src_units, used_units, ys, cw, h2, x1, mod3, g_post, digits_t, wsgu_b, wsdn_b)


def _row_digits():
    r = np.arange(TILE_ROWS)
    d = np.zeros((TILE_ROWS, KEY_W), np.float32)
    d[:, N_EXPERTS] = r // DIGIT
    d[:, N_EXPERTS + 1] = r % DIGIT
    return jnp.asarray(d, dtype=BF16)


def _moe_plan(counts, nb_max):
    n_tiles = counts.shape[0]
    s = (counts + (MOE_UNIT - 1)) // MOE_UNIT
    local = jnp.cumsum(s, axis=1) - s
    cs = jnp.cumsum(s, axis=0)
    per_expert = cs[-1]
    padded = (per_expert + UNITS_PER_BLOCK - 1) // UNITS_PER_BLOCK * UNITS_PER_BLOCK
    g_end = jnp.cumsum(padded)
    g_start = g_end - padded
    seg_start = g_start[None, :] + cs - s
    n_blocks_used = (g_end[-1] // UNITS_PER_BLOCK).astype(jnp.int32).reshape(1)
    jb = jnp.arange(nb_max, dtype=jnp.int32)
    one_e = ((jb[:, None] >= (g_start // UNITS_PER_BLOCK)[None, :])
             & (jb[:, None] < (g_end // UNITS_PER_BLOCK)[None, :])).astype(jnp.int32)
    pick_e = lambda table: jnp.sum(one_e[:, :, None] * table.T[None, :, :], axis=1)
    block_expert = jnp.where(jb < n_blocks_used[0], jnp.sum(one_e * jnp.arange(N_EXPERTS, dtype=jnp.int32), axis=1),
                             N_EXPERTS - 1).astype(jnp.int32)
    cs_b, s_b, local_b = pick_e(cs), pick_e(s), pick_e(local)
    q = (jb * UNITS_PER_BLOCK - jnp.sum(one_e * g_start[None, :], axis=1))[:, None] \
        + jnp.arange(UNITS_PER_BLOCK, dtype=jnp.int32)[None, :]
    tile = jnp.minimum(jnp.sum(cs_b[:, None, :] <= q[:, :, None], axis=2), n_tiles - 1)
    one_t = (tile[:, :, None] == jnp.arange(n_tiles, dtype=jnp.int32)).astype(jnp.int32)
    src = tile * TILE_UNITS + q + jnp.sum(one_t * (local_b - cs_b + s_b)[:, None, :], axis=2)
    valid = q < jnp.sum(one_e * per_expert[None, :], axis=1)[:, None]
    src_units = jnp.where(valid, src, 0).astype(jnp.int32).reshape(-1)
    u = jnp.arange(TILE_UNITS, dtype=jnp.int32)
    seg_end = local + s
    eu = jnp.minimum(jnp.sum(seg_end[:, None, :] <= u[None, :, None], axis=2), N_EXPERTS - 1)
    one_u = (eu[:, :, None] == jnp.arange(N_EXPERTS, dtype=jnp.int32)).astype(jnp.int32)
    back = u[None, :] + jnp.sum(one_u * (seg_start - local)[:, None, :], axis=2)
    back_units = jnp.where(u[None, :] < seg_end[:, -1:], back, 0).astype(jnp.int32).reshape(-1)
    return block_expert, src_units, n_blocks_used, back_units, seg_end[:, -1].astype(jnp.int32)


def _tile(n, pref):
    t = pref
    while n % t:
        t //= 2
    return t


def kernel(x, c, ctx, c_ctx, w_ada, b_ada, g_pre_mix, g_post_mix, g_pre_ffn, g_post_ffn, w_in, lb_logits, g_hgrn_out, cm_ln_g, cm_ln_b, w_spatial, b_spatial, w_branch_a, w_branch_b, w_out, w_router, b_router, w_expert_gu, w_expert_down, w_shared_gu, w_shared_down):
    B, T, D = x.shape
    L = ctx.shape[1]
    assert D == D_MODEL and w_ada.shape[0] == 1 and T % SCAN_CHUNK == 0 and L % SCAN_CHUNK == 0
    l = 0
    row = lambda a: a[l].reshape(1, -1)

    n_rows = -(-(B + 1) // 16) * 16
    cs = jnp.zeros((n_rows, D), F32).at[:B].set(c).at[B].set(c_ctx)
    mod3 = _ada_mod(cs, w_ada[l], row(b_ada)).reshape(n_rows, 1, 6 * D)

    w_in_b = w_in[l].astype(BF16)
    lbl = lb_logits[:, l:l + 2].reshape(4, HG_W)
    x2 = x.reshape(B * T, D)
    q, k2, lf2, v, sg, u, vn, sga, sgb = _proj_lat(
        x2, mod3, row(g_pre_mix), w_in_b, lbl, row(cm_ln_g), row(cm_ln_b), T, _tile(T, 256))
    kc2, lfc2, vc = _proj_ctx(ctx.reshape(B * L, D), mod3, B, row(g_pre_mix), w_in_b[:, HG_W:4 * HG_W], lbl,
                              _tile(B * L, 256))

    o_f, o_b = _hgrn_scan(q, k2, lf2, v, kc2, lfc2, vc, B, T, L)

    bs_full = jnp.repeat(b_spatial[l], CM_W // CM_GROUPS, axis=1)
    x1, h2, logits = _merge(
        o_f, o_b, sg, u, vn, sga, sgb, x2, mod3, row(g_hgrn_out), w_spatial[l].astype(BF16), bs_full,
        w_branch_a[l].astype(BF16), w_branch_b[l].astype(BF16), w_out[l].astype(BF16), row(g_post_mix),
        row(g_pre_ffn), jnp.pad(w_router[l].T, ((0, KEY_W - N_EXPERTS), (0, 0))).astype(BF16), T, _tile(T, 512))

    cw = _router(logits, jnp.broadcast_to(b_router[l][:, None], (N_EXPERTS, 128)), _tile(B * T, 512))

    n_tok = B * T
    n_tiles = n_tok // MOE_TILE
    digits = _row_digits()
    xs, cnt = _dispatch(h2, cw, digits)
    counts = cnt.reshape(n_tiles, 8, KEY_W)[:, 0, :N_EXPERTS]
    max_units = (n_tok * TOP_K + n_tiles * N_EXPERTS * (MOE_UNIT - 1)) // MOE_UNIT + N_EXPERTS * (UNITS_PER_BLOCK - 1)
    nb_max = -(-max_units // UNITS_PER_BLOCK)
    block_expert, src_units, n_blocks_used, back_units, tile_units = _moe_plan(counts, nb_max)
    ys = _experts(xs, block_expert, src_units, n_blocks_used, w_expert_gu[l], w_expert_down[l])
    out = _combine(ys, back_units, tile_units, cw, h2, x1, mod3, row(g_post_ffn), digits.T,
                   w_shared_gu[l].astype(BF16), w_shared_down[l].astype(BF16), T)
    return out.reshape(B, T, D)
```

```python
import functools

import numpy as np
import jax
import jax.numpy as jnp
from jax import lax
from jax.experimental import pallas as pl
from jax.experimental.pallas import tpu as pltpu

F32 = jnp.float32
BF16 = jnp.bfloat16

D_MODEL = 1024
EPS = 1e-6
HG_HEADS = 4
HG_DK = 128
HG_W = HG_HEADS * HG_DK
CM_W = 512
CM_CHUNK = 128
CM_GROUPS = 4
D_IN = 5 * HG_W + 2 * CM_W + 2 * D_MODEL
N_EXPERTS = 64
TOP_K = 8
N_GROUPS = 8
GROUP_BITS = 3
TOPK_GROUPS = 4
D_EXPERT = 256
ROUTED_SCALE = 2.5
SCAN_CHUNK = 128
SUB = 16
N_LEVELS = 4
MERGE_ROWS = 512
VMEM_LIMIT = 56 * 1024 * 1024


def _params(sem):
    return pltpu.CompilerParams(dimension_semantics=sem, vmem_limit_bytes=VMEM_LIMIT)


def _dot(a, b):
    return jnp.dot(a, b, preferred_element_type=F32)


def _dot_nt(a, b):
    return lax.dot_general(a, b, (((1,), (1,)), ((), ())), preferred_element_type=F32)


def _dot_tn(a, b):
    return lax.dot_general(a, b, (((0,), (0,)), ((), ())), preferred_element_type=F32)


def _sigmoid(x):
    return 0.5 * jnp.tanh(0.5 * x) + 0.5


def _rms(x, g):
    return x * lax.rsqrt(jnp.mean(x * x, axis=-1, keepdims=True) + EPS) * g


def _ada_kernel(c_ref, w_ref, b_ref, o_ref):
    c = c_ref[...]
    s = c * _sigmoid(c)
    o_ref[...] = _dot(s.astype(BF16), w_ref[...].astype(BF16)) + b_ref[...]


def _ada_mod(cs, w_ada, b_ada):
    rows = cs.shape[0]
    n_out = w_ada.shape[1]
    return pl.pallas_call(
        _ada_kernel,
        grid=(n_out // D_MODEL,),
        in_specs=[
            pl.BlockSpec((rows, D_MODEL), lambda j: (0, 0)),
            pl.BlockSpec((D_MODEL, D_MODEL), lambda j: (0, j)),
            pl.BlockSpec((1, D_MODEL), lambda j: (0, j)),
        ],
        out_specs=pl.BlockSpec((rows, D_MODEL), lambda j: (0, j)),
        out_shape=jax.ShapeDtypeStruct((rows, n_out), F32),
        compiler_params=_params(("parallel",)),
        name="ada_mod",
    )(cs, w_ada, b_ada)


def _lower_bounds(lbl):
    out = []
    for d in range(2):
        l0, l1 = lbl[2 * d:2 * d + 1], lbl[2 * d + 1:2 * d + 2]
        m = jnp.maximum(l0, l1)
        e0, e1 = jnp.exp(l0 - m), jnp.exp(l1 - m)
        out.append(e0 / (e0 + e1))
    return out


def _prenorm(x_ref, sh_ref, sc_ref, g_ref):
    return (_rms(x_ref[...], g_ref[...]) * (1.0 + sc_ref[...]) + sh_ref[...]).astype(BF16)


def _gates(z, lb, k_ref, lf_ref, d):
    half_t = 0.5 * jnp.tanh(0.5 * z)
    k_ref[d] = ((1.0 - lb) * (0.5 - half_t)).astype(k_ref.dtype)
    lf_ref[d] = jnp.log2(lb + (1.0 - lb) * (0.5 + half_t))


def _proj_lat_kernel(x_ref, sh_ref, sc_ref, g_ref, w_ref, lbl_ref, lng_ref, lnb_ref,
                     q_ref, k_ref, lf_ref, v_ref, sg_ref, u_ref, vn_ref, sga_ref, sgb_ref):
    hb = _prenorm(x_ref, sh_ref, sc_ref, g_ref)
    lbs = _lower_bounds(lbl_ref[...])

    def mm(lo, width):
        return _dot(hb, w_ref[:, lo:lo + width])

    z = mm(0, HG_W)
    q_ref[...] = (z * _sigmoid(z)).astype(q_ref.dtype)
    for d in range(2):
        _gates(mm((1 + d) * HG_W, HG_W), lbs[d], k_ref, lf_ref, d)
    z = mm(4 * HG_W, HG_W)
    sg_ref[...] = (z * _sigmoid(z)).astype(sg_ref.dtype)
    u_ref[...] = jax.nn.gelu(mm(5 * HG_W, CM_W)).astype(u_ref.dtype)
    vv = jax.nn.gelu(mm(5 * HG_W + CM_W, CM_W))
    vc = vv - jnp.mean(vv, axis=-1, keepdims=True)
    vn = vc * lax.rsqrt(jnp.mean(vc * vc, axis=-1, keepdims=True) + EPS)
    vn_ref[...] = (vn * lng_ref[...] + lnb_ref[...]).astype(vn_ref.dtype)
    base = 5 * HG_W + 2 * CM_W
    sga_ref[...] = _sigmoid(mm(base, D_MODEL)).astype(sga_ref.dtype)
    sgb_ref[...] = _sigmoid(mm(base + D_MODEL, D_MODEL)).astype(sgb_ref.dtype)
    v_ref[...] = mm(3 * HG_W, HG_W).astype(v_ref.dtype)


def _mod_spec(rows_per_batch_tiles, col):
    return pl.BlockSpec((None, 1, D_MODEL), lambda i: (i // rows_per_batch_tiles, 0, col))


def _proj_lat(x2, mod3, g_pre, w_in_b, lbl, ln_g, ln_b, seq, tm):
    n = x2.shape[0]
    tpb = seq // tm
    row = lambda w: pl.BlockSpec((tm, w), lambda i: (i, 0))
    row2 = pl.BlockSpec((2, tm, HG_W), lambda i: (0, i, 0))
    full = lambda a: pl.BlockSpec(a.shape, lambda i: (0,) * a.ndim)
    outs = [
        (row(HG_W), jax.ShapeDtypeStruct((n, HG_W), BF16)),
        (row2, jax.ShapeDtypeStruct((2, n, HG_W), BF16)),
        (row2, jax.ShapeDtypeStruct((2, n, HG_W), F32)),
        (row(HG_W), jax.ShapeDtypeStruct((n, HG_W), BF16)),
        (row(HG_W), jax.ShapeDtypeStruct((n, HG_W), BF16)),
        (row(CM_W), jax.ShapeDtypeStruct((n, CM_W), BF16)),
        (row(CM_W), jax.ShapeDtypeStruct((n, CM_W), BF16)),
        (row(D_MODEL), jax.ShapeDtypeStruct((n, D_MODEL), BF16)),
        (row(D_MODEL), jax.ShapeDtypeStruct((n, D_MODEL), BF16)),
    ]
    return pl.pallas_call(
        _proj_lat_kernel,
        grid=(n // tm,),
        in_specs=[row(D_MODEL), _mod_spec(tpb, 0), _mod_spec(tpb, 1), full(g_pre), full(w_in_b),
                  full(lbl), full(ln_g), full(ln_b)],
        out_specs=[o[0] for o in outs],
        out_shape=[o[1] for o in outs],
        compiler_params=_params(("parallel",)),
        name="proj_lat",
    )(x2, mod3, mod3, g_pre, w_in_b, lbl, ln_g, ln_b)


def _proj_ctx_kernel(x_ref, sh_ref, sc_ref, g_ref, w_ref, lbl_ref, k_ref, lf_ref, v_ref):
    hb = _prenorm(x_ref, sh_ref, sc_ref, g_ref)
    lbs = _lower_bounds(lbl_ref[...])
    for d in range(2):
        _gates(_dot(hb, w_ref[:, d * HG_W:(d + 1) * HG_W]), lbs[d], k_ref, lf_ref, d)
    v_ref[...] = _dot(hb, w_ref[:, 2 * HG_W:3 * HG_W]).astype(v_ref.dtype)


def _proj_ctx(c2, mod3, ctx_row, g_pre, w_ctx_b, lbl, tm):
    n = c2.shape[0]
    row = lambda w: pl.BlockSpec((tm, w), lambda i: (i, 0))
    row2 = pl.BlockSpec((2, tm, HG_W), lambda i: (0, i, 0))
    full = lambda a: pl.BlockSpec(a.shape, lambda i: (0,) * a.ndim)
    mod = lambda col: pl.BlockSpec((None, 1, D_MODEL), lambda i: (ctx_row, 0, col))
    return pl.pallas_call(
        _proj_ctx_kernel,
        grid=(n // tm,),
        in_specs=[row(D_MODEL), mod(0), mod(1), full(g_pre), full(w_ctx_b), full(lbl)],
        out_specs=[row2, row2, row(HG_W)],
        out_shape=[jax.ShapeDtypeStruct((2, n, HG_W), BF16), jax.ShapeDtypeStruct((2, n, HG_W), F32),
                   jax.ShapeDtypeStruct((n, HG_W), BF16)],
        compiler_params=_params(("parallel",)),
        name="proj_ctx",
    )(c2, mod3, mod3, g_pre, w_ctx_b, lbl)


def _scan_tables():
    C = SCAN_CHUNK
    t = np.arange(C)
    lmats, lvls = [], []
    for d in range(2):
        p = t if d == 0 else C - 1 - t
        pt, ps = p[:, None], p[None, :]
        lmat = (ps <= pt).astype(np.float32)
        lmats.append(np.concatenate([lmat, lmat], axis=1))
        lvl = np.full((C, C), -1, np.int32)
        lvl[(pt // SUB == ps // SUB) & (ps <= pt)] = 0
        half, idx = SUB, 1
        while half < C:
            span = 2 * half
            lvl[(pt // span == ps // span) & ((pt // half) % 2 == 1) & ((ps // half) % 2 == 0)] = idx
            half, idx = span, idx + 1
        lvls.append(lvl)
    return jnp.asarray(np.stack(lmats), dtype=BF16), jnp.asarray(np.stack(lvls))


def _scan_step(d, slot, lmat_ref, lvl_ref, k_ref, lf_ref, v_ref, st_ref, b_scr, q_ref=None, o_ref=None):
    C = SCAN_CHUNK
    lf = lf_ref[...]
    hi = lf.astype(BF16)
    lo = (lf - hi.astype(F32)).astype(BF16)
    b_scr[slot] = _dot(lmat_ref[d], jnp.concatenate([hi, lo], axis=0))
    if q_ref is not None:
        lvl = lvl_ref[d]
        masks = [lvl == i for i in range(N_LEVELS)]

    for h in range(HG_HEADS):
        sl = slice(h * HG_DK, (h + 1) * HG_DK)
        b = b_scr[slot, :, sl]

        def row(i):
            return b_scr[slot, pl.ds(i, 1), sl]

        b_last = row(C - 1 if d == 0 else 0)
        k = k_ref[:, sl]
        v = v_ref[:, sl]
        st = st_ref[slot, :, sl]
        if q_ref is not None:
            q = q_ref[:, sl]
            e0 = jnp.concatenate([b[m * SUB:(m + 1) * SUB] - row(m * SUB + SUB // 2 - 1 + d)
                                  for m in range(C // SUB)], axis=0)
            factors = [(jnp.exp2(e0).astype(BF16), jnp.exp2(-e0).astype(BF16))]
            half = SUB
            while half < C:
                span = 2 * half
                e = jnp.concatenate([b[m * span:(m + 1) * span] - row(m * span + half - 1 + d)
                                     for m in range(C // span)], axis=0)
                w = jnp.exp2(-jnp.abs(e)).astype(BF16)
                factors.append((w, w))
                half = span
            a = jnp.zeros((C, C), F32)
            for (wq, wk), mask in zip(factors, masks):
                a = jnp.where(mask, _dot_nt(q * wq, k * wk), a)
            qhat = q * jnp.exp2(b).astype(BF16)
            o = _dot(a.astype(BF16), v) + _dot_nt(qhat, st.astype(BF16))
            o_ref[:, sl] = o.astype(o_ref.dtype)
        khat = k * jnp.exp2(b_last - b).astype(BF16)
        st_ref[slot, :, sl] = jnp.exp2(b_last) * st + _dot_tn(v, khat)


def _scan_kernel(n_ctx_steps, lmat_ref, lvl_ref, q_f, k_f, lf_f, v_f, q_b, k_b, lf_b, v_b,
                 kc_f, lfc_f, vc_f, kc_b, lfc_b, vc_b, o_f, o_b, st_ref, b_scr):
    s = pl.program_id(1)
    n_seq = q_f.shape[0]

    @pl.when(s == 0)
    def _():
        st_ref[...] = jnp.zeros_like(st_ref)

    @pl.when(s < n_ctx_steps)
    def _():
        for i in range(n_seq):
            _scan_step(0, 2 * i, lmat_ref, lvl_ref, kc_f.at[i], lfc_f.at[i], vc_f.at[i], st_ref, b_scr)
            _scan_step(1, 2 * i + 1, lmat_ref, lvl_ref, kc_b.at[i], lfc_b.at[i], vc_b.at[i], st_ref, b_scr)

    @pl.when(s >= n_ctx_steps)
    def _():
        for i in range(n_seq):
            _scan_step(0, 2 * i, lmat_ref, lvl_ref, k_f.at[i], lf_f.at[i], v_f.at[i], st_ref, b_scr,
                       q_f.at[i], o_f.at[i])
            _scan_step(1, 2 * i + 1, lmat_ref, lvl_ref, k_b.at[i], lf_b.at[i], v_b.at[i], st_ref, b_scr,
                       q_b.at[i], o_b.at[i])


def _hgrn_scan(q, k2, lf2, v, kc2, lfc2, vc, batch, seq, ctx_len):
    C = SCAN_CHUNK
    n_lat, n_ctx = seq // C, ctx_len // C
    n_seq = 2 if batch % 2 == 0 else 1
    lmat, lvl = _scan_tables()
    q, v = (a.reshape(batch, seq, HG_W) for a in (q, v))
    k2, lf2 = (a.reshape(2, batch, seq, HG_W) for a in (k2, lf2))
    vc = vc.reshape(batch, ctx_len, HG_W)
    kc2, lfc2 = (a.reshape(2, batch, ctx_len, HG_W) for a in (kc2, lfc2))

    def lat_blk(d):
        def blk(s):
            j = jnp.maximum(s - n_ctx, 0)
            return j if d == 0 else n_lat - 1 - j
        return blk

    def ctx_blk(d):
        def blk(s):
            i = jnp.minimum(s, n_ctx - 1)
            return i if d == 0 else n_ctx - 1 - i
        return blk

    def plain(blk):
        return pl.BlockSpec((n_seq, C, HG_W), lambda b, s: (b, blk(s), 0))

    def specs(blk_of, with_q):
        out = []
        for d in range(2):
            blk = blk_of(d)
            per_dir = pl.BlockSpec((None, n_seq, C, HG_W), lambda b, s, blk=blk, d=d: (d, b, blk(s), 0))
            out += ([plain(blk)] if with_q else []) + [per_dir, per_dir, plain(blk)]
        return out

    full = lambda a: pl.BlockSpec(a.shape, lambda b, s: (0,) * a.ndim)
    o_shape = jax.ShapeDtypeStruct((batch, seq, HG_W), BF16)
    n_chains = 2 * n_seq
    o_f, o_b = pl.pallas_call(
        functools.partial(_scan_kernel, n_ctx),
        grid=(batch // n_seq, n_ctx + n_lat),
        in_specs=[full(lmat), full(lvl)] + specs(lat_blk, True) + specs(ctx_blk, False),
        out_specs=[plain(lat_blk(d)) for d in range(2)],
        out_shape=[o_shape, o_shape],
        scratch_shapes=[pltpu.VMEM((n_chains, HG_DK, HG_W), F32), pltpu.VMEM((n_chains, C, HG_W), F32)],
        compiler_params=_params(("parallel", "arbitrary")),
        name="hgrn_scan",
    )(lmat, lvl, q, k2, lf2, v, q, k2, lf2, v, kc2, lfc2, vc, kc2, lfc2, vc)
    return o_f.reshape(batch * seq, HG_W), o_b.reshape(batch * seq, HG_W)


def _merge_kernel(of_ref, ob_ref, sg_ref, u_ref, vn_ref, sga_ref, sgb_ref, x_ref, gt1_ref, sh2_ref, sc2_ref,
                  gout_ref, ws_ref, bs_ref, wa_ref, wb_ref, wo_ref, gpost_ref, gffn_ref, wr_ref,
                  x1_ref, h2_ref, lg_ref):
    tm = x_ref.shape[0]
    gout = gout_ref[...]
    gw = CM_W // CM_GROUPS
    group = min(MERGE_ROWS, tm)
    for r0 in range(0, tm, group):
        rows = slice(r0, r0 + group)
        o = of_ref[rows, :].astype(F32) + ob_ref[rows, :].astype(F32)
        sg = sg_ref[rows, :].astype(F32)
        a = jnp.concatenate(
            [_rms(o[:, h * HG_DK:(h + 1) * HG_DK], gout) * sg[:, h * HG_DK:(h + 1) * HG_DK]
             for h in range(HG_HEADS)], axis=1).astype(BF16)
        vn = vn_ref[rows, :]
        z = jnp.concatenate(
            [jnp.concatenate([_dot(ws_ref[g], vn[c * CM_CHUNK:(c + 1) * CM_CHUNK, g * gw:(g + 1) * gw])
                              for g in range(CM_GROUPS)], axis=1) + bs_ref[...]
             for c in range(group // CM_CHUNK)], axis=0)
        bm = (u_ref[rows, :].astype(F32) * z).astype(BF16)
        y = (sga_ref[rows, :].astype(F32) * _dot(a, wa_ref[...])
             + sgb_ref[rows, :].astype(F32) * _dot(bm, wb_ref[...]))
        yo = _dot(y.astype(BF16), wo_ref[...])
        x1 = x_ref[rows, :] + gt1_ref[...] * _rms(yo, gpost_ref[...])
        x1_ref[rows, :] = x1
        h2 = (_rms(x1, gffn_ref[...]) * (1.0 + sc2_ref[...]) + sh2_ref[...]).astype(BF16)
        h2_ref[rows, :] = h2
        lg_ref[:, rows] = _dot_nt(wr_ref[...], h2)


def _merge(o_f, o_b, sg, u, vn, sga, sgb, x2, mod3, g_out, ws_b, bs_full, wa_b, wb_b, wo_b, g_post, g_ffn, wr_b,
           seq, tm):
    n = x2.shape[0]
    tpb = seq // tm
    row = lambda w: pl.BlockSpec((tm, w), lambda i: (i, 0))
    full = lambda a: pl.BlockSpec(a.shape, lambda i: (0,) * a.ndim)
    return pl.pallas_call(
        _merge_kernel,
        grid=(n // tm,),
        in_specs=[row(HG_W), row(HG_W), row(HG_W), row(CM_W), row(CM_W),
                  row(D_MODEL), row(D_MODEL), row(D_MODEL), _mod_spec(tpb, 2), _mod_spec(tpb, 3),
                  _mod_spec(tpb, 4), full(g_out), full(ws_b), full(bs_full), full(wa_b), full(wb_b),
                  full(wo_b), full(g_post), full(g_ffn), full(wr_b)],
        out_specs=[row(D_MODEL), row(D_MODEL), pl.BlockSpec((wr_b.shape[0], tm), lambda i: (0, i))],
        out_shape=[jax.ShapeDtypeStruct((n, D_MODEL), F32), jax.ShapeDtypeStruct((n, D_MODEL), BF16),
                   jax.ShapeDtypeStruct((wr_b.shape[0], n), F32)],
        compiler_params=_params(("parallel",)),
        name="merge",
    )(o_f, o_b, sg, u, vn, sga, sgb, x2, mod3, mod3, mod3, g_out, ws_b, bs_full, wa_b, wb_b, wo_b, g_post, g_ffn,
      wr_b)


def _router_kernel(lg_ref, br_ref, w_ref):
    tm = lg_ref.shape[1]
    gsz = N_EXPERTS // N_GROUPS
    scores = _sigmoid(lg_ref[:N_EXPERTS, :])
    sel = scores + jnp.concatenate([br_ref[...]] * (tm // br_ref.shape[1]), axis=1)
    neg = -jnp.inf

    def first_max(x, ids, sentinel, axis):
        m = jnp.max(x, axis=axis, keepdims=True)
        return m, jnp.min(jnp.where(x == m, ids, sentinel), axis=axis, keepdims=True)

    sel3 = sel.reshape(N_GROUPS, gsz, tm)
    j3 = lax.broadcasted_iota(jnp.int32, sel3.shape, 1)
    m1, i1 = first_max(sel3, j3, gsz, 1)
    gscore = m1 + jnp.max(jnp.where(j3 == i1, neg, sel3), axis=1, keepdims=True)
    g3 = lax.broadcasted_iota(jnp.int32, gscore.shape, 0)
    keep = jnp.zeros(gscore.shape, F32)
    for _ in range(TOPK_GROUPS):
        _, gi = first_max(gscore, g3, N_GROUPS, 0)
        keep = jnp.where(g3 == gi, 1.0, keep)
        gscore = jnp.where(g3 == gi, neg, gscore)
    x = jnp.where(keep > 0.0, sel3, neg).reshape(N_EXPERTS, tm)
    e_i = lax.broadcasted_iota(jnp.int32, x.shape, 0)
    w = jnp.zeros(x.shape, F32)
    for _ in range(TOP_K):
        _, ei = first_max(x, e_i, N_EXPERTS, 0)
        w = jnp.where(e_i == ei, scores, w)
        x = jnp.where(e_i == ei, neg, x)
    w = w / jnp.sum(w, axis=0, keepdims=True) * ROUTED_SCALE
    w_ref[...] = jnp.concatenate([w, jnp.zeros_like(w)], axis=0).T


def _router(logits_t, b_router_cols, tm):
    rows, n = logits_t.shape
    return pl.pallas_call(
        _router_kernel,
        grid=(n // tm,),
        in_specs=[pl.BlockSpec((rows, tm), lambda i: (0, i)),
                  pl.BlockSpec(b_router_cols.shape, lambda i: (0, 0))],
        out_specs=pl.BlockSpec((tm, rows), lambda i: (i, 0)),
        out_shape=jax.ShapeDtypeStruct((n, rows), F32),
        compiler_params=_params(("parallel",)),
        name="router",
    )(logits_t, b_router_cols)


MOE_TILE = 256
MOE_UNIT = 16
MOE_BLOCK = 512
MOE_MM_ROWS = 512
GATHER_SLOTS = 3
UNITS_PER_BLOCK = MOE_BLOCK // MOE_UNIT
TILE_ROWS = 3072
TILE_UNITS = TILE_ROWS // MOE_UNIT
ROW_CHUNK = 512
N_CHUNKS = TILE_ROWS // ROW_CHUNK
FULL_CHUNKS = MOE_TILE * TOP_K // ROW_CHUNK
CHUNK_UNITS = ROW_CHUNK // MOE_UNIT
KEY_W = 128
DIGIT_BITS = 6
DIGIT = 1 << DIGIT_BITS


def _swiglu_act(h, w_gu):
    gu = _dot(h, w_gu)
    de = gu.shape[1] // 2
    g = gu[:, :de]
    return g * _sigmoid(g) * gu[:, de:]


def _token_keys(cw, starts_row):
    t = cw.shape[0]
    routed = cw > 0.0
    t_i = lax.broadcasted_iota(jnp.int32, (t, t), 0)
    s_i = lax.broadcasted_iota(jnp.int32, (t, t), 1)
    rank = _dot((s_i < t_i).astype(BF16), routed.astype(BF16))
    pos = (starts_row + rank).astype(jnp.int32)
    lane = lax.broadcasted_iota(jnp.int32, cw.shape, 1)
    hi = jnp.where(routed, jnp.right_shift(pos, DIGIT_BITS), -1)
    lo = jnp.where(routed, jnp.bitwise_and(pos, DIGIT - 1), -1)
    key_hi = jnp.where(lane < N_EXPERTS, hi, jnp.where(lane == N_EXPERTS, -1, 0))
    key_lo = jnp.where(lane < N_EXPERTS, lo, jnp.where(lane == N_EXPERTS + 1, -1, 0))
    return key_hi.astype(F32).astype(BF16), key_lo.astype(F32).astype(BF16)


def _segment_units(counts):
    return jnp.floor((counts + (MOE_UNIT - 1)) * (1.0 / MOE_UNIT))


def _dispatch_kernel(h_ref, cw_ref, digits_ref, xs_ref, cnt_ref):
    cw = cw_ref[...]
    t = cw.shape[0]
    routed = (cw > 0.0).astype(BF16)
    counts = _dot(jnp.ones((8, t), BF16), routed)
    cnt_ref[...] = counts.astype(jnp.int32)
    units = _segment_units(counts)
    e_i = lax.broadcasted_iota(jnp.int32, (KEY_W, KEY_W), 0)
    f_i = lax.broadcasted_iota(jnp.int32, (KEY_W, KEY_W), 1)
    starts = _dot(units.astype(BF16), (e_i < f_i).astype(BF16)) * MOE_UNIT
    ends = starts + units * MOE_UNIT
    key_hi, key_lo = _token_keys(cw, starts[:1])
    h = h_ref[...]
    lane = lax.broadcasted_iota(jnp.int32, (ROW_CHUNK, KEY_W), 1)
    used_rows = jnp.max(ends)

    def sort_chunk(c):
        rows = slice(c * ROW_CHUNK, (c + 1) * ROW_CHUNK)
        r = (lax.broadcasted_iota(jnp.int32, (ROW_CHUNK, KEY_W), 0) + c * ROW_CHUNK).astype(F32)
        in_seg = (r >= starts[:1]) & (r < ends[:1])
        rmap = jnp.where(lane < N_EXPERTS, in_seg.astype(F32), digits_ref[rows, :].astype(F32)).astype(BF16)
        hit = (_dot_nt(rmap, key_hi) == 0.0) & (_dot_nt(rmap, key_lo) == 0.0)
        xs_ref[rows, :] = _dot(hit.astype(BF16), h).astype(xs_ref.dtype)

    for c in range(N_CHUNKS):
        if c < FULL_CHUNKS:
            sort_chunk(c)
        else:
            pl.when(used_rows > c * ROW_CHUNK)(functools.partial(sort_chunk, c))

            @pl.when(used_rows <= c * ROW_CHUNK)
            def _(c=c):
                xs_ref[c * ROW_CHUNK:(c + 1) * ROW_CHUNK, :] = jnp.zeros((ROW_CHUNK, D_MODEL), xs_ref.dtype)


def _dispatch(h2, cw, digits):
    n = h2.shape[0]
    n_tiles = n // MOE_TILE
    return pl.pallas_call(
        _dispatch_kernel,
        grid=(n_tiles,),
        in_specs=[pl.BlockSpec((MOE_TILE, D_MODEL), lambda i: (i, 0)),
                  pl.BlockSpec((MOE_TILE, KEY_W), lambda i: (i, 0)),
                  pl.BlockSpec(digits.shape, lambda i: (0, 0))],
        out_specs=[pl.BlockSpec((TILE_ROWS, D_MODEL), lambda i: (i, 0)),
                   pl.BlockSpec((8, KEY_W), lambda i: (i, 0))],
        out_shape=[jax.ShapeDtypeStruct((n_tiles * TILE_ROWS, D_MODEL), BF16),
                   jax.ShapeDtypeStruct((n_tiles * 8, KEY_W), jnp.int32)],
        compiler_params=_params(("parallel",)),
        name="moe_dispatch",
    )(h2, cw, digits)


def _unit_copy(src_hbm, unit, dst, slot, pos, sem):
    return pltpu.make_async_copy(
        src_hbm.at[pl.ds(pl.multiple_of(unit * MOE_UNIT, MOE_UNIT), MOE_UNIT)],
        dst.at[slot, pl.ds(pos * MOE_UNIT, MOE_UNIT)], sem.at[slot])


def _experts_kernel(be_ref, src_ref, nb_ref, xs_hbm, wgu_ref, wdn_ref, ys_ref, xbuf, sem, wgu_b, wdn_b):
    j = pl.program_id(0)
    nb = nb_ref[0]

    def copies(blk, slot):
        return [_unit_copy(xs_hbm, src_ref[blk * UNITS_PER_BLOCK + u], xbuf, slot, u, sem)
                for u in range(UNITS_PER_BLOCK)]

    def fetch(blk, slot):
        for cp in copies(blk, slot):
            cp.start()

    ahead = GATHER_SLOTS - 1

    @pl.when(j == 0)
    def _():
        for a in range(ahead):
            fetch(jnp.minimum(a, nb - 1), a)

    @pl.when((j == 0) | (be_ref[j] != be_ref[jnp.maximum(j - 1, 0)]))
    def _():
        wgu_b[...] = wgu_ref[...].astype(BF16)
        wdn_b[...] = wdn_ref[...].astype(BF16)

    @pl.when(j < nb)
    def _():
        slot = j % GATHER_SLOTS
        for cp in copies(j, slot):
            cp.wait()
        fetch(jnp.minimum(j + ahead, nb - 1), (j + ahead) % GATHER_SLOTS)
        for g in range(MOE_BLOCK // MOE_MM_ROWS):
            rows = pl.ds(g * MOE_MM_ROWS, MOE_MM_ROWS)
            act = _swiglu_act(xbuf[slot, rows, :], wgu_b[...])
            ys_ref[rows, :] = _dot(act.astype(BF16), wdn_b[...]).astype(ys_ref.dtype)

    @pl.when(j == nb - 1)
    def _():
        for a in range(1, GATHER_SLOTS):
            for cp in copies(j, (j + a) % GATHER_SLOTS):
                cp.wait()

    @pl.when(j >= nb)
    def _():
        ys_ref[...] = jnp.zeros_like(ys_ref)


def _experts(xs, block_expert, src_units, n_blocks_used, w_gu, w_dn):
    nb_max = block_expert.shape[0]
    grid_spec = pltpu.PrefetchScalarGridSpec(
        num_scalar_prefetch=3,
        grid=(nb_max,),
        in_specs=[pl.BlockSpec(memory_space=pl.ANY),
                  pl.BlockSpec((None, D_MODEL, 2 * D_EXPERT), lambda j, be, src, nb: (be[j], 0, 0)),
                  pl.BlockSpec((None, D_EXPERT, D_MODEL), lambda j, be, src, nb: (be[j], 0, 0))],
        out_specs=pl.BlockSpec((MOE_BLOCK, D_MODEL), lambda j, be, src, nb: (j, 0)),
        scratch_shapes=[pltpu.VMEM((GATHER_SLOTS, MOE_BLOCK, D_MODEL), BF16),
                        pltpu.SemaphoreType.DMA((GATHER_SLOTS,)),
                        pltpu.VMEM((D_MODEL, 2 * D_EXPERT), BF16), pltpu.VMEM((D_EXPERT, D_MODEL), BF16)],
    )
    return pl.pallas_call(
        _experts_kernel,
        grid_spec=grid_spec,
        out_shape=jax.ShapeDtypeStruct((nb_max * MOE_BLOCK, D_MODEL), BF16),
        compiler_params=_params(("arbitrary",)),
        name="moe_experts",
    )(block_expert, src_units, n_blocks_used, xs, w_gu, w_dn)


def _combine_kernel(src_ref, used_ref, ys_hbm, cw_ref, h_ref, x1_ref, gt2_ref, gpost_ref, digits_t_ref,
                    wsgu_ref, wsdn_ref, o_ref, ybuf, sem, acc_ref):
    i = pl.program_id(0)

    def copies(tile, slot, c):
        return [_unit_copy(ys_hbm, src_ref[tile * TILE_UNITS + u], ybuf, slot, u, sem)
                for u in range(c * CHUNK_UNITS, (c + 1) * CHUNK_UNITS)]

    def chunk_used(tile, c):
        return used_ref[tile] > c * CHUNK_UNITS

    def for_used_chunks(tile, fn):
        for c in range(N_CHUNKS):
            if c < FULL_CHUNKS:
                fn(c)
            else:
                pl.when(chunk_used(tile, c))(functools.partial(fn, c))

    def fetch(tile, slot):
        def start(c):
            for cp in copies(tile, slot, c):
                cp.start()

        for_used_chunks(tile, start)

    def wait_all(tile, slot):
        def wait(c):
            for cp in copies(tile, slot, c):
                cp.wait()

        for_used_chunks(tile, wait)

    @pl.when(i == 0)
    def _():
        fetch(0, 0)

    @pl.when(i + 1 < pl.num_programs(0))
    def _():
        fetch(i + 1, (i + 1) % 2)

    cw = cw_ref[...]
    t = cw.shape[0]
    routed = (cw > 0.0).astype(BF16)
    e_i = lax.broadcasted_iota(jnp.int32, (KEY_W, KEY_W), 0)
    f_i = lax.broadcasted_iota(jnp.int32, (KEY_W, KEY_W), 1)
    units = _segment_units(_dot_tn(routed, jnp.ones((t, KEY_W), BF16)))
    starts = _dot((f_i < e_i).astype(BF16), units.astype(BF16)) * MOE_UNIT
    ends = starts + units * MOE_UNIT
    units_row = _segment_units(_dot(jnp.ones((8, t), BF16), routed))
    starts_row = _dot(units_row.astype(BF16), (e_i < f_i).astype(BF16)) * MOE_UNIT
    key_hi, key_lo = _token_keys(cw, starts_row[:1])
    wb = cw.astype(BF16)

    f = _dot(_swiglu_act(h_ref[...], wsgu_ref[...]).astype(BF16), wsdn_ref[...])
    slot = i % 2
    reps = ROW_CHUNK // KEY_W
    starts_c = jnp.concatenate([starts] * reps, axis=1)
    ends_c = jnp.concatenate([ends] * reps, axis=1)
    sub = lax.broadcasted_iota(jnp.int32, (KEY_W, ROW_CHUNK), 0)

    wait_all(i, slot)

    def chunk_sum(c):
        rows = slice(c * ROW_CHUNK, (c + 1) * ROW_CHUNK)
        r = (lax.broadcasted_iota(jnp.int32, (KEY_W, ROW_CHUNK), 1) + c * ROW_CHUNK).astype(F32)
        in_seg = (r >= starts_c) & (r < ends_c)
        rmap_t = jnp.where(sub < N_EXPERTS, in_seg.astype(F32), digits_t_ref[:, rows].astype(F32)).astype(BF16)
        hit = (_dot(key_hi, rmap_t) == 0.0) & (_dot(key_lo, rmap_t) == 0.0)
        w = _dot(wb, rmap_t)
        return _dot(jnp.where(hit, w, 0.0).astype(BF16), ybuf[slot, rows, :])

    for c in range(FULL_CHUNKS):
        f = f + chunk_sum(c)
    acc_ref[...] = f
    for c in range(FULL_CHUNKS, N_CHUNKS):
        @pl.when(chunk_used(i, c))
        def _(c=c):
            acc_ref[...] += chunk_sum(c)
    o_ref[...] = x1_ref[...] + gt2_ref[...] * _rms(acc_ref[...], gpost_ref[...])


def _combine(ys, src_units, used_units, cw, h2, x1, mod3, g_post, digits_t, wsgu_b, wsdn_b, seq):
    n = h2.shape[0]
    tpb = seq // MOE_TILE
    row = lambda w: pl.BlockSpec((MOE_TILE, w), lambda i, src, used: (i, 0))
    full = lambda a: pl.BlockSpec(a.shape, lambda i, src, used: (0,) * a.ndim)
    grid_spec = pltpu.PrefetchScalarGridSpec(
        num_scalar_prefetch=2,
        grid=(n // MOE_TILE,),
        in_specs=[pl.BlockSpec(memory_space=pl.ANY), row(KEY_W), row(D_MODEL), row(D_MODEL),
                  pl.BlockSpec((None, 1, D_MODEL), lambda i, src, used: (i // tpb, 0, 5)), full(g_post),
                  full(digits_t), full(wsgu_b), full(wsdn_b)],
        out_specs=row(D_MODEL),
        scratch_shapes=[pltpu.VMEM((2, TILE_ROWS, D_MODEL), BF16), pltpu.SemaphoreType.DMA((2,)),
                        pltpu.VMEM((MOE_TILE, D_MODEL), F32)],
    )
    return pl.pallas_call(
        _combine_kernel,
        grid_spec=grid_spec,
        out_shape=jax.ShapeDtypeStruct((n, D_MODEL), F32),
        compiler_params=_params(("arbitrary",)),
        name="moe_combine",
    )(src_units, used_units, ys, cw, h2, x1, mod3, g_post, digits_t, wsgu_b, wsdn_b)


def _row_digits():
    r = np.arange(TILE_ROWS)
    d = np.zeros((TILE_ROWS, KEY_W), np.float32)
    d[:, N_EXPERTS] = r // DIGIT
    d[:, N_EXPERTS + 1] = r % DIGIT
    return jnp.asarray(d, dtype=BF16)


def _moe_plan(counts, nb_max):
    n_tiles = counts.shape[0]
    s = (counts + (MOE_UNIT - 1)) // MOE_UNIT
    local = jnp.cumsum(s, axis=1) - s
    cs = jnp.cumsum(s, axis=0)
    per_expert = cs[-1]
    padded = (per_expert + UNITS_PER_BLOCK - 1) // UNITS_PER_BLOCK * UNITS_PER_BLOCK
    g_end = jnp.cumsum(padded)
    g_start = g_end - padded
    seg_start = g_start[None, :] + cs - s
    n_blocks_used = (g_end[-1] // UNITS_PER_BLOCK).astype(jnp.int32).reshape(1)
    jb = jnp.arange(nb_max, dtype=jnp.int32)
    one_e = ((jb[:, None] >= (g_start // UNITS_PER_BLOCK)[None, :])
             & (jb[:, None] < (g_end // UNITS_PER_BLOCK)[None, :])).astype(jnp.int32)
    pick_e = lambda table: jnp.sum(one_e[:, :, None] * table.T[None, :, :], axis=1)
    block_expert = jnp.where(jb < n_blocks_used[0], jnp.sum(one_e * jnp.arange(N_EXPERTS, dtype=jnp.int32), axis=1),
                             N_EXPERTS - 1).astype(jnp.int32)
    cs_b, s_b, local_b = pick_e(cs), pick_e(s), pick_e(local)
    q = (jb * UNITS_PER_BLOCK - jnp.sum(one_e * g_start[None, :], axis=1))[:, None] \
        + jnp.arange(UNITS_PER_BLOCK, dtype=jnp.int32)[None, :]
    tile = jnp.minimum(jnp.sum(cs_b[:, None, :] <= q[:, :, None], axis=2), n_tiles - 1)
    one_t = (tile[:, :, None] == jnp.arange(n_tiles, dtype=jnp.int32)).astype(jnp.int32)
    src = tile * TILE_UNITS + q + jnp.sum(one_t * (local_b - cs_b + s_b)[:, None, :], axis=2)
    valid = q < jnp.sum(one_e * per_expert[None, :], axis=1)[:, None]
    src_units = jnp.where(valid, src, 0).astype(jnp.int32).reshape(-1)
    u = jnp.arange(TILE_UNITS, dtype=jnp.int32)
    seg_end = local + s
    eu = jnp.minimum(jnp.sum(seg_end[:, None, :] <= u[None, :, None], axis=2), N_EXPERTS - 1)
    one_u = (eu[:, :, None] == jnp.arange(N_EXPERTS, dtype=jnp.int32)).astype(jnp.int32)
    back = u[None, :] + jnp.sum(one_u * (seg_start - local)[:, None, :], axis=2)
    back_units = jnp.where(u[None, :] < seg_end[:, -1:], back, 0).astype(jnp.int32).reshape(-1)
    return block_expert, src_units, n_blocks_used, back_units, seg_end[:, -1].astype(jnp.int32)


def _tile(n, pref):
    t = pref
    while n % t:
        t //= 2
    return t


def kernel(x, c, ctx, c_ctx, w_ada, b_ada, g_pre_mix, g_post_mix, g_pre_ffn, g_post_ffn, w_in, lb_logits, g_hgrn_out, cm_ln_g, cm_ln_b, w_spatial, b_spatial, w_branch_a, w_branch_b, w_out, w_router, b_router, w_expert_gu, w_expert_down, w_shared_gu, w_shared_down):
    B, T, D = x.shape
    L = ctx.shape[1]
    assert D == D_MODEL and w_ada.shape[0] == 1 and T % SCAN_CHUNK == 0 and L % SCAN_CHUNK == 0
    l = 0
    row = lambda a: a[l].reshape(1, -1)

    n_rows = -(-(B + 1) // 16) * 16
    cs = jnp.zeros((n_rows, D), F32).at[:B].set(c).at[B].set(c_ctx)
    mod3 = _ada_mod(cs, w_ada[l], row(b_ada)).reshape(n_rows, 1, 6 * D)

    w_in_b = w_in[l].astype(BF16)
    lbl = lb_logits[:, l:l + 2].reshape(4, HG_W)
    x2 = x.reshape(B * T, D)
    q, k2, lf2, v, sg, u, vn, sga, sgb = _proj_lat(
        x2, mod3, row(g_pre_mix), w_in_b, lbl, row(cm_ln_g), row(cm_ln_b), T, _tile(T, 256))
    kc2, lfc2, vc = _proj_ctx(ctx.reshape(B * L, D), mod3, B, row(g_pre_mix), w_in_b[:, HG_W:4 * HG_W], lbl,
                              _tile(B * L, 256))

    o_f, o_b = _hgrn_scan(q, k2, lf2, v, kc2, lfc2, vc, B, T, L)

    bs_full = jnp.repeat(b_spatial[l], CM_W // CM_GROUPS, axis=1)
    x1, h2, logits = _merge(
        o_f, o_b, sg, u, vn, sga, sgb, x2, mod3, row(g_hgrn_out), w_spatial[l].astype(BF16), bs_full,
        w_branch_a[l].astype(BF16), w_branch_b[l].astype(BF16), w_out[l].astype(BF16), row(g_post_mix),
        row(g_pre_ffn), jnp.pad(w_router[l].T, ((0, KEY_W - N_EXPERTS), (0, 0))).astype(BF16), T, _tile(T, 512))

    cw = _router(logits, jnp.broadcast_to(b_router[l][:, None], (N_EXPERTS, 128)), _tile(B * T, 512))

    n_tok = B * T
    n_tiles = n_tok // MOE_TILE
    digits = _row_digits()
    xs, cnt = _dispatch(h2, cw, digits)
    counts = cnt.reshape(n_tiles, 8, KEY_W)[:, 0, :N_EXPERTS]
    max_units = (n_tok * TOP_K + n_tiles * N_EXPERTS * (MOE_UNIT - 1)) // MOE_UNIT + N_EXPERTS * (UNITS_PER_BLOCK - 1)
    nb_max = -(-max_units // UNITS_PER_BLOCK)
    block_expert, src_units, n_blocks_used, back_units, tile_units = _moe_plan(counts, nb_max)
    ys = _experts(xs, block_expert, src_units, n_blocks_used, w_expert_gu[l], w_expert_down[l])
    out = _combine(ys, back_units, tile_units, cw, h2, x1, mod3, row(g_post_ffn), digits.T,
                   w_shared_gu[l].astype(BF16), w_shared_down[l].astype(BF16), T)
    return out.reshape(B, T, D)
```

```python
import functools

import numpy as np
import jax
import jax.numpy as jnp
from jax import lax
from jax.experimental import pallas as pl
from jax.experimental.pallas import tpu as pltpu

F32 = jnp.float32
BF16 = jnp.bfloat16

D_MODEL = 1024
EPS = 1e-6
HG_HEADS = 4
HG_DK = 128
HG_W = HG_HEADS * HG_DK
CM_W = 512
CM_CHUNK = 128
CM_GROUPS = 4
D_IN = 5 * HG_W + 2 * CM_W + 2 * D_MODEL
N_EXPERTS = 64
TOP_K = 8
N_GROUPS = 8
GROUP_BITS = 3
TOPK_GROUPS = 4
D_EXPERT = 256
ROUTED_SCALE = 2.5
SCAN_CHUNK = 128
SUB = 16
N_LEVELS = 4
MERGE_ROWS = 512
VMEM_LIMIT = 56 * 1024 * 1024


def _params(sem):
    return pltpu.CompilerParams(dimension_semantics=sem, vmem_limit_bytes=VMEM_LIMIT)


def _dot(a, b):
    return jnp.dot(a, b, preferred_element_type=F32)


def _dot_nt(a, b):
    return lax.dot_general(a, b, (((1,), (1,)), ((), ())), preferred_element_type=F32)


def _dot_tn(a, b):
    return lax.dot_general(a, b, (((0,), (0,)), ((), ())), preferred_element_type=F32)


def _sigmoid(x):
    return 0.5 * jnp.tanh(0.5 * x) + 0.5


def _rms(x, g):
    return x * lax.rsqrt(jnp.mean(x * x, axis=-1, keepdims=True) + EPS) * g


def _ada_kernel(c_ref, w_ref, b_ref, o_ref):
    c = c_ref[...]
    s = c * _sigmoid(c)
    o_ref[...] = _dot(s.astype(BF16), w_ref[...].astype(BF16)) + b_ref[...]


def _ada_mod(cs, w_ada, b_ada):
    rows = cs.shape[0]
    n_out = w_ada.shape[1]
    return pl.pallas_call(
        _ada_kernel,
        grid=(n_out // D_MODEL,),
        in_specs=[
            pl.BlockSpec((rows, D_MODEL), lambda j: (0, 0)),
            pl.BlockSpec((D_MODEL, D_MODEL), lambda j: (0, j)),
            pl.BlockSpec((1, D_MODEL), lambda j: (0, j)),
        ],
        out_specs=pl.BlockSpec((rows, D_MODEL), lambda j: (0, j)),
        out_shape=jax.ShapeDtypeStruct((rows, n_out), F32),
        compiler_params=_params(("parallel",)),
        name="ada_mod",
    )(cs, w_ada, b_ada)


def _lower_bounds(lbl):
    out = []
    for d in range(2):
        l0, l1 = lbl[2 * d:2 * d + 1], lbl[2 * d + 1:2 * d + 2]
        m = jnp.maximum(l0, l1)
        e0, e1 = jnp.exp(l0 - m), jnp.exp(l1 - m)
        out.append(e0 / (e0 + e1))
    return out


def _prenorm(x_ref, sh_ref, sc_ref, g_ref):
    return (_rms(x_ref[...], g_ref[...]) * (1.0 + sc_ref[...]) + sh_ref[...]).astype(BF16)


def _gates(z, lb, k_ref, lf_ref, d):
    half_t = 0.5 * jnp.tanh(0.5 * z)
    k_ref[d] = ((1.0 - lb) * (0.5 - half_t)).astype(k_ref.dtype)
    lf_ref[d] = jnp.log2(lb + (1.0 - lb) * (0.5 + half_t))


def _proj_lat_kernel(x_ref, sh_ref, sc_ref, g_ref, w_ref, lbl_ref, lng_ref, lnb_ref,
                     q_ref, k_ref, lf_ref, v_ref, sg_ref, u_ref, vn_ref, sga_ref, sgb_ref):
    hb = _prenorm(x_ref, sh_ref, sc_ref, g_ref)
    lbs = _lower_bounds(lbl_ref[...])

    def mm(lo, width):
        return _dot(hb, w_ref[:, lo:lo + width])

    z = mm(0, HG_W)
    q_ref[...] = (z * _sigmoid(z)).astype(q_ref.dtype)
    for d in range(2):
        _gates(mm((1 + d) * HG_W, HG_W), lbs[d], k_ref, lf_ref, d)
    z = mm(4 * HG_W, HG_W)
    sg_ref[...] = (z * _sigmoid(z)).astype(sg_ref.dtype)
    u_ref[...] = jax.nn.gelu(mm(5 * HG_W, CM_W)).astype(u_ref.dtype)
    vv = jax.nn.gelu(mm(5 * HG_W + CM_W, CM_W))
    vc = vv - jnp.mean(vv, axis=-1, keepdims=True)
    vn = vc * lax.rsqrt(jnp.mean(vc * vc, axis=-1, keepdims=True) + EPS)
    vn_ref[...] = (vn * lng_ref[...] + lnb_ref[...]).astype(vn_ref.dtype)
    base = 5 * HG_W + 2 * CM_W
    sga_ref[...] = _sigmoid(mm(base, D_MODEL)).astype(sga_ref.dtype)
    sgb_ref[...] = _sigmoid(mm(base + D_MODEL, D_MODEL)).astype(sgb_ref.dtype)
    v_ref[...] = mm(3 * HG_W, HG_W).astype(v_ref.dtype)


def _mod_spec(rows_per_batch_tiles, col):
    return pl.BlockSpec((None, 1, D_MODEL), lambda i: (i // rows_per_batch_tiles, 0, col))


def _proj_lat(x2, mod3, g_pre, w_in_b, lbl, ln_g, ln_b, seq, tm):
    n = x2.shape[0]
    tpb = seq // tm
    row = lambda w: pl.BlockSpec((tm, w), lambda i: (i, 0))
    row2 = pl.BlockSpec((2, tm, HG_W), lambda i: (0, i, 0))
    full = lambda a: pl.BlockSpec(a.shape, lambda i: (0,) * a.ndim)
    outs = [
        (row(HG_W), jax.ShapeDtypeStruct((n, HG_W), BF16)),
        (row2, jax.ShapeDtypeStruct((2, n, HG_W), BF16)),
        (row2, jax.ShapeDtypeStruct((2, n, HG_W), F32)),
        (row(HG_W), jax.ShapeDtypeStruct((n, HG_W), BF16)),
        (row(HG_W), jax.ShapeDtypeStruct((n, HG_W), BF16)),
        (row(CM_W), jax.ShapeDtypeStruct((n, CM_W), BF16)),
        (row(CM_W), jax.ShapeDtypeStruct((n, CM_W), BF16)),
        (row(D_MODEL), jax.ShapeDtypeStruct((n, D_MODEL), BF16)),
        (row(D_MODEL), jax.ShapeDtypeStruct((n, D_MODEL), BF16)),
    ]
    return pl.pallas_call(
        _proj_lat_kernel,
        grid=(n // tm,),
        in_specs=[row(D_MODEL), _mod_spec(tpb, 0), _mod_spec(tpb, 1), full(g_pre), full(w_in_b),
                  full(lbl), full(ln_g), full(ln_b)],
        out_specs=[o[0] for o in outs],
        out_shape=[o[1] for o in outs],
        compiler_params=_params(("parallel",)),
        name="proj_lat",
    )(x2, mod3, mod3, g_pre, w_in_b, lbl, ln_g, ln_b)


def _proj_ctx_kernel(x_ref, sh_ref, sc_ref, g_ref, w_ref, lbl_ref, k_ref, lf_ref, v_ref):
    hb = _prenorm(x_ref, sh_ref, sc_ref, g_ref)
    lbs = _lower_bounds(lbl_ref[...])
    for d in range(2):
        _gates(_dot(hb, w_ref[:, d * HG_W:(d + 1) * HG_W]), lbs[d], k_ref, lf_ref, d)
    v_ref[...] = _dot(hb, w_ref[:, 2 * HG_W:3 * HG_W]).astype(v_ref.dtype)


def _proj_ctx(c2, mod3, ctx_row, g_pre, w_ctx_b, lbl, tm):
    n = c2.shape[0]
    row = lambda w: pl.BlockSpec((tm, w), lambda i: (i, 0))
    row2 = pl.BlockSpec((2, tm, HG_W), lambda i: (0, i, 0))
    full = lambda a: pl.BlockSpec(a.shape, lambda i: (0,) * a.ndim)
    mod = lambda col: pl.BlockSpec((None, 1, D_MODEL), lambda i: (ctx_row, 0, col))
    return pl.pallas_call(
        _proj_ctx_kernel,
        grid=(n // tm,),
        in_specs=[row(D_MODEL), mod(0), mod(1), full(g_pre), full(w_ctx_b), full(lbl)],
        out_specs=[row2, row2, row(HG_W)],
        out_shape=[jax.ShapeDtypeStruct((2, n, HG_W), BF16), jax.ShapeDtypeStruct((2, n, HG_W), F32),
                   jax.ShapeDtypeStruct((n, HG_W), BF16)],
        compiler_params=_params(("parallel",)),
        name="proj_ctx",
    )(c2, mod3, mod3, g_pre, w_ctx_b, lbl)


def _scan_tables():
    C = SCAN_CHUNK
    t = np.arange(C)
    lmats, lvls = [], []
    for d in range(2):
        p = t if d == 0 else C - 1 - t
        pt, ps = p[:, None], p[None, :]
        lmat = (ps <= pt).astype(np.float32)
        lmats.append(np.concatenate([lmat, lmat], axis=1))
        lvl = np.full((C, C), -1, np.int32)
        lvl[(pt // SUB == ps // SUB) & (ps <= pt)] = 0
        half, idx = SUB, 1
        while half < C:
            span = 2 * half
            lvl[(pt // span == ps // span) & ((pt // half) % 2 == 1) & ((ps // half) % 2 == 0)] = idx
            half, idx = span, idx + 1
        lvls.append(lvl)
    return jnp.asarray(np.stack(lmats), dtype=BF16), jnp.asarray(np.stack(lvls))


def _scan_step(d, slot, lmat_ref, lvl_ref, k_ref, lf_ref, v_ref, st_ref, b_scr, q_ref=None, o_ref=None):
    C = SCAN_CHUNK
    lf = lf_ref[...]
    hi = lf.astype(BF16)
    lo = (lf - hi.astype(F32)).astype(BF16)
    b_scr[slot] = _dot(lmat_ref[d], jnp.concatenate([hi, lo], axis=0))
    if q_ref is not None:
        lvl = lvl_ref[d]
        masks = [lvl == i for i in range(N_LEVELS)]

    for h in range(HG_HEADS):
        sl = slice(h * HG_DK, (h + 1) * HG_DK)
        b = b_scr[slot, :, sl]

        def row(i):
            return b_scr[slot, pl.ds(i, 1), sl]

        b_last = row(C - 1 if d == 0 else 0)
        k = k_ref[:, sl]
        v = v_ref[:, sl]
        st = st_ref[slot, :, sl]
        if q_ref is not None:
            q = q_ref[:, sl]
            e0 = jnp.concatenate([b[m * SUB:(m + 1) * SUB] - row(m * SUB + SUB // 2 - 1 + d)
                                  for m in range(C // SUB)], axis=0)
            factors = [(jnp.exp2(e0).astype(BF16), jnp.exp2(-e0).astype(BF16))]
            half = SUB
            while half < C:
                span = 2 * half
                e = jnp.concatenate([b[m * span:(m + 1) * span] - row(m * span + half - 1 + d)
                                     for m in range(C // span)], axis=0)
                w = jnp.exp2(-jnp.abs(e)).astype(BF16)
                factors.append((w, w))
                half = span
            a = jnp.zeros((C, C), F32)
            for (wq, wk), mask in zip(factors, masks):
                a = jnp.where(mask, _dot_nt(q * wq, k * wk), a)
            qhat = q * jnp.exp2(b).astype(BF16)
            o = _dot(a.astype(BF16), v) + _dot_nt(qhat, st.astype(BF16))
            o_ref[:, sl] = o.astype(o_ref.dtype)
        khat = k * jnp.exp2(b_last - b).astype(BF16)
        st_ref[slot, :, sl] = jnp.exp2(b_last) * st + _dot_tn(v, khat)


def _scan_kernel(n_ctx_steps, lmat_ref, lvl_ref, q_f, k_f, lf_f, v_f, q_b, k_b, lf_b, v_b,
                 kc_f, lfc_f, vc_f, kc_b, lfc_b, vc_b, o_f, o_b, st_ref, b_scr):
    s = pl.program_id(1)
    n_seq = q_f.shape[0]

    @pl.when(s == 0)
    def _():
        st_ref[...] = jnp.zeros_like(st_ref)

    @pl.when(s < n_ctx_steps)
    def _():
        for i in range(n_seq):
            _scan_step(0, 2 * i, lmat_ref, lvl_ref, kc_f.at[i], lfc_f.at[i], vc_f.at[i], st_ref, b_scr)
            _scan_step(1, 2 * i + 1, lmat_ref, lvl_ref, kc_b.at[i], lfc_b.at[i], vc_b.at[i], st_ref, b_scr)

    @pl.when(s >= n_ctx_steps)
    def _():
        for i in range(n_seq):
            _scan_step(0, 2 * i, lmat_ref, lvl_ref, k_f.at[i], lf_f.at[i], v_f.at[i], st_ref, b_scr,
                       q_f.at[i], o_f.at[i])
            _scan_step(1, 2 * i + 1, lmat_ref, lvl_ref, k_b.at[i], lf_b.at[i], v_b.at[i], st_ref, b_scr,
                       q_b.at[i], o_b.at[i])


def _hgrn_scan(q, k2, lf2, v, kc2, lfc2, vc, batch, seq, ctx_len):
    C = SCAN_CHUNK
    n_lat, n_ctx = seq // C, ctx_len // C
    n_seq = 2 if batch % 2 == 0 else 1
    lmat, lvl = _scan_tables()
    q, v = (a.reshape(batch, seq, HG_W) for a in (q, v))
    k2, lf2 = (a.reshape(2, batch, seq, HG_W) for a in (k2, lf2))
    vc = vc.reshape(batch, ctx_len, HG_W)
    kc2, lfc2 = (a.reshape(2, batch, ctx_len, HG_W) for a in (kc2, lfc2))

    def lat_blk(d):
        def blk(s):
            j = jnp.maximum(s - n_ctx, 0)
            return j if d == 0 else n_lat - 1 - j
        return blk

    def ctx_blk(d):
        def blk(s):
            i = jnp.minimum(s, n_ctx - 1)
            return i if d == 0 else n_ctx - 1 - i
        return blk

    def plain(blk):
        return pl.BlockSpec((n_seq, C, HG_W), lambda b, s: (b, blk(s), 0))

    def specs(blk_of, with_q):
        out = []
        for d in range(2):
            blk = blk_of(d)
            per_dir = pl.BlockSpec((None, n_seq, C, HG_W), lambda b, s, blk=blk, d=d: (d, b, blk(s), 0))
            out += ([plain(blk)] if with_q else []) + [per_dir, per_dir, plain(blk)]
        return out

    full = lambda a: pl.BlockSpec(a.shape, lambda b, s: (0,) * a.ndim)
    o_shape = jax.ShapeDtypeStruct((batch, seq, HG_W), BF16)
    n_chains = 2 * n_seq
    o_f, o_b = pl.pallas_call(
        functools.partial(_scan_kernel, n_ctx),
        grid=(batch // n_seq, n_ctx + n_lat),
        in_specs=[full(lmat), full(lvl)] + specs(lat_blk, True) + specs(ctx_blk, False),
        out_specs=[plain(lat_blk(d)) for d in range(2)],
        out_shape=[o_shape, o_shape],
        scratch_shapes=[pltpu.VMEM((n_chains, HG_DK, HG_W), F32), pltpu.VMEM((n_chains, C, HG_W), F32)],
        compiler_params=_params(("parallel", "arbitrary")),
        name="hgrn_scan",
    )(lmat, lvl, q, k2, lf2, v, q, k2, lf2, v, kc2, lfc2, vc, kc2, lfc2, vc)
    return o_f.reshape(batch * seq, HG_W), o_b.reshape(batch * seq, HG_W)


def _merge_kernel(of_ref, ob_ref, sg_ref, u_ref, vn_ref, sga_ref, sgb_ref, x_ref, gt1_ref, sh2_ref, sc2_ref,
                  gout_ref, ws_ref, bs_ref, wa_ref, wb_ref, wo_ref, gpost_ref, gffn_ref, wr_ref,
                  x1_ref, h2_ref, lg_ref):
    tm = x_ref.shape[0]
    gout = gout_ref[...]
    gw = CM_W // CM_GROUPS
    group = min(MERGE_ROWS, tm)
    for r0 in range(0, tm, group):
        rows = slice(r0, r0 + group)
        o = of_ref[rows, :].astype(F32) + ob_ref[rows, :].astype(F32)
        sg = sg_ref[rows, :].astype(F32)
        a = jnp.concatenate(
            [_rms(o[:, h * HG_DK:(h + 1) * HG_DK], gout) * sg[:, h * HG_DK:(h + 1) * HG_DK]
             for h in range(HG_HEADS)], axis=1).astype(BF16)
        vn = vn_ref[rows, :]
        z = jnp.concatenate(
            [jnp.concatenate([_dot(ws_ref[g], vn[c * CM_CHUNK:(c + 1) * CM_CHUNK, g * gw:(g + 1) * gw])
                              for g in range(CM_GROUPS)], axis=1) + bs_ref[...]
             for c in range(group // CM_CHUNK)], axis=0)
        bm = (u_ref[rows, :].astype(F32) * z).astype(BF16)
        y = (sga_ref[rows, :].astype(F32) * _dot(a, wa_ref[...])
             + sgb_ref[rows, :].astype(F32) * _dot(bm, wb_ref[...]))
        yo = _dot(y.astype(BF16), wo_ref[...])
        x1 = x_ref[rows, :] + gt1_ref[...] * _rms(yo, gpost_ref[...])
        x1_ref[rows, :] = x1
        h2 = (_rms(x1, gffn_ref[...]) * (1.0 + sc2_ref[...]) + sh2_ref[...]).astype(BF16)
        h2_ref[rows, :] = h2
        lg_ref[:, rows] = _dot_nt(wr_ref[...], h2)


def _merge(o_f, o_b, sg, u, vn, sga, sgb, x2, mod3, g_out, ws_b, bs_full, wa_b, wb_b, wo_b, g_post, g_ffn, wr_b,
           seq, tm):
    n = x2.shape[0]
    tpb = seq // tm
    row = lambda w: pl.BlockSpec((tm, w), lambda i: (i, 0))
    full = lambda a: pl.BlockSpec(a.shape, lambda i: (0,) * a.ndim)
    return pl.pallas_call(
        _merge_kernel,
        grid=(n // tm,),
        in_specs=[row(HG_W), row(HG_W), row(HG_W), row(CM_W), row(CM_W),
                  row(D_MODEL), row(D_MODEL), row(D_MODEL), _mod_spec(tpb, 2), _mod_spec(tpb, 3),
                  _mod_spec(tpb, 4), full(g_out), full(ws_b), full(bs_full), full(wa_b), full(wb_b),
                  full(wo_b), full(g_post), full(g_ffn), full(wr_b)],
        out_specs=[row(D_MODEL), row(D_MODEL), pl.BlockSpec((wr_b.shape[0], tm), lambda i: (0, i))],
        out_shape=[jax.ShapeDtypeStruct((n, D_MODEL), F32), jax.ShapeDtypeStruct((n, D_MODEL), BF16),
                   jax.ShapeDtypeStruct((wr_b.shape[0], n), F32)],
        compiler_params=_params(("parallel",)),
        name="merge",
    )(o_f, o_b, sg, u, vn, sga, sgb, x2, mod3, mod3, mod3, g_out, ws_b, bs_full, wa_b, wb_b, wo_b, g_post, g_ffn,
      wr_b)


def _router_kernel(lg_ref, br_ref, w_ref):
    tm = lg_ref.shape[1]
    gsz = N_EXPERTS // N_GROUPS
    scores = _sigmoid(lg_ref[:N_EXPERTS, :])
    sel = scores + jnp.concatenate([br_ref[...]] * (tm // br_ref.shape[1]), axis=1)
    neg = -jnp.inf

    def first_max(x, ids, sentinel, axis):
        m = jnp.max(x, axis=axis, keepdims=True)
        return m, jnp.min(jnp.where(x == m, ids, sentinel), axis=axis, keepdims=True)

    sel3 = sel.reshape(N_GROUPS, gsz, tm)
    j3 = lax.broadcasted_iota(jnp.int32, sel3.shape, 1)
    m1, i1 = first_max(sel3, j3, gsz, 1)
    gscore = m1 + jnp.max(jnp.where(j3 == i1, neg, sel3), axis=1, keepdims=True)
    g3 = lax.broadcasted_iota(jnp.int32, gscore.shape, 0)
    keep = jnp.zeros(gscore.shape, F32)
    for _ in range(TOPK_GROUPS):
        _, gi = first_max(gscore, g3, N_GROUPS, 0)
        keep = jnp.where(g3 == gi, 1.0, keep)
        gscore = jnp.where(g3 == gi, neg, gscore)
    x = jnp.where(keep > 0.0, sel3, neg).reshape(N_EXPERTS, tm)
    e_i = lax.broadcasted_iota(jnp.int32, x.shape, 0)
    w = jnp.zeros(x.shape, F32)
    for _ in range(TOP_K):
        _, ei = first_max(x, e_i, N_EXPERTS, 0)
        w = jnp.where(e_i == ei, scores, w)
        x = jnp.where(e_i == ei, neg, x)
    w = w / jnp.sum(w, axis=0, keepdims=True) * ROUTED_SCALE
    w_ref[...] = jnp.concatenate([w, jnp.zeros_like(w)], axis=0).T


def _router(logits_t, b_router_cols, tm):
    rows, n = logits_t.shape
    return pl.pallas_call(
        _router_kernel,
        grid=(n // tm,),
        in_specs=[pl.BlockSpec((rows, tm), lambda i: (0, i)),
                  pl.BlockSpec(b_router_cols.shape, lambda i: (0, 0))],
        out_specs=pl.BlockSpec((tm, rows), lambda i: (i, 0)),
        out_shape=jax.ShapeDtypeStruct((n, rows), F32),
        compiler_params=_params(("parallel",)),
        name="router",
    )(logits_t, b_router_cols)


MOE_TILE = 256
MOE_UNIT = 16
MOE_BLOCK = 512
MOE_MM_ROWS = 512
GATHER_SLOTS = 3
UNITS_PER_BLOCK = MOE_BLOCK // MOE_UNIT
TILE_ROWS = 3072
TILE_UNITS = TILE_ROWS // MOE_UNIT
ROW_CHUNK = 512
N_CHUNKS = TILE_ROWS // ROW_CHUNK
FULL_CHUNKS = MOE_TILE * TOP_K // ROW_CHUNK
CHUNK_UNITS = ROW_CHUNK // MOE_UNIT
SORT_ROWS = 1024
KEY_W = 128
DIGIT_BITS = 6
DIGIT = 1 << DIGIT_BITS


def _swiglu_act(h, w_gu):
    gu = _dot(h, w_gu)
    de = gu.shape[1] // 2
    g = gu[:, :de]
    return g * _sigmoid(g) * gu[:, de:]


def _token_keys(cw, starts_row):
    t = cw.shape[0]
    routed = cw > 0.0
    t_i = lax.broadcasted_iota(jnp.int32, (t, t), 0)
    s_i = lax.broadcasted_iota(jnp.int32, (t, t), 1)
    rank = _dot((s_i < t_i).astype(BF16), routed.astype(BF16))
    pos = (starts_row + rank).astype(jnp.int32)
    lane = lax.broadcasted_iota(jnp.int32, cw.shape, 1)
    hi = jnp.where(routed, jnp.right_shift(pos, DIGIT_BITS), -1)
    lo = jnp.where(routed, jnp.bitwise_and(pos, DIGIT - 1), -1)
    key_hi = jnp.where(lane < N_EXPERTS, hi * DIGIT,
                       jnp.where(lane == N_EXPERTS, -DIGIT, jnp.where(lane == N_EXPERTS + 1, -1, 0)))
    key_lo = jnp.where(lane < N_EXPERTS, lo, 0)
    return jnp.concatenate([key_hi, key_lo], axis=1).astype(F32).astype(BF16)


def _segment_units(counts):
    return jnp.floor((counts + (MOE_UNIT - 1)) * (1.0 / MOE_UNIT))


def _dispatch_kernel(h_ref, cw_ref, digits_ref, xs_ref, cnt_ref):
    cw = cw_ref[...]
    t = cw.shape[0]
    routed = (cw > 0.0).astype(BF16)
    counts = _dot(jnp.ones((8, t), BF16), routed)
    cnt_ref[...] = counts.astype(jnp.int32)
    units = _segment_units(counts)
    e_i = lax.broadcasted_iota(jnp.int32, (KEY_W, KEY_W), 0)
    f_i = lax.broadcasted_iota(jnp.int32, (KEY_W, KEY_W), 1)
    starts = _dot(units.astype(BF16), (e_i < f_i).astype(BF16)) * MOE_UNIT
    ends = starts + units * MOE_UNIT
    keys_t = _token_keys(cw, starts[:1]).astype(F32).T.astype(BF16)
    h = h_ref[...]
    used_rows = jnp.max(ends)

    def sort_rows(r0, n):
        rows = slice(r0, r0 + n)
        lane = lax.broadcasted_iota(jnp.int32, (n, KEY_W), 1)
        r = (lax.broadcasted_iota(jnp.int32, (n, KEY_W), 0) + r0).astype(F32)
        in_seg = (r >= starts[:1]) & (r < ends[:1])
        rmap = jnp.where(lane < N_EXPERTS, in_seg.astype(F32), digits_ref[rows, :].astype(F32)).astype(BF16)
        hit = _dot(jnp.concatenate([rmap, rmap], axis=1), keys_t) == 0.0
        xs_ref[rows, :] = _dot(hit.astype(BF16), h).astype(xs_ref.dtype)

    for r0 in range(0, FULL_CHUNKS * ROW_CHUNK, SORT_ROWS):
        sort_rows(r0, SORT_ROWS)
    for c in range(FULL_CHUNKS, N_CHUNKS):
        pl.when(used_rows > c * ROW_CHUNK)(functools.partial(sort_rows, c * ROW_CHUNK, ROW_CHUNK))

        @pl.when(used_rows <= c * ROW_CHUNK)
        def _(c=c):
            xs_ref[c * ROW_CHUNK:(c + 1) * ROW_CHUNK, :] = jnp.zeros((ROW_CHUNK, D_MODEL), xs_ref.dtype)


def _dispatch(h2, cw, digits):
    n = h2.shape[0]
    n_tiles = n // MOE_TILE
    return pl.pallas_call(
        _dispatch_kernel,
        grid=(n_tiles,),
        in_specs=[pl.BlockSpec((MOE_TILE, D_MODEL), lambda i: (i, 0)),
                  pl.BlockSpec((MOE_TILE, KEY_W), lambda i: (i, 0)),
                  pl.BlockSpec(digits.shape, lambda i: (0, 0))],
        out_specs=[pl.BlockSpec((TILE_ROWS, D_MODEL), lambda i: (i, 0)),
                   pl.BlockSpec((8, KEY_W), lambda i: (i, 0))],
        out_shape=[jax.ShapeDtypeStruct((n_tiles * TILE_ROWS, D_MODEL), BF16),
                   jax.ShapeDtypeStruct((n_tiles * 8, KEY_W), jnp.int32)],
        compiler_params=_params(("parallel",)),
        name="moe_dispatch",
    )(h2, cw, digits)


def _unit_copy(src_hbm, unit, dst, slot, pos, sem):
    return pltpu.make_async_copy(
        src_hbm.at[pl.ds(pl.multiple_of(unit * MOE_UNIT, MOE_UNIT), MOE_UNIT)],
        dst.at[slot, pl.ds(pos * MOE_UNIT, MOE_UNIT)], sem.at[slot])


def _experts_kernel(be_ref, src_ref, nb_ref, xs_hbm, wgu_ref, wdn_ref, ys_ref, xbuf, sem, wgu_b, wdn_b):
    j = pl.program_id(0)
    nb = nb_ref[0]

    def copies(blk, slot):
        return [_unit_copy(xs_hbm, src_ref[blk * UNITS_PER_BLOCK + u], xbuf, slot, u, sem)
                for u in range(UNITS_PER_BLOCK)]

    def fetch(blk, slot):
        for cp in copies(blk, slot):
            cp.start()

    ahead = GATHER_SLOTS - 1

    @pl.when(j == 0)
    def _():
        for a in range(ahead):
            fetch(jnp.minimum(a, nb - 1), a)

    @pl.when((j == 0) | (be_ref[j] != be_ref[jnp.maximum(j - 1, 0)]))
    def _():
        wgu_b[...] = wgu_ref[...].astype(BF16)
        wdn_b[...] = wdn_ref[...].astype(BF16)

    @pl.when(j < nb)
    def _():
        slot = j % GATHER_SLOTS
        for cp in copies(j, slot):
            cp.wait()
        fetch(jnp.minimum(j + ahead, nb - 1), (j + ahead) % GATHER_SLOTS)
        for g in range(MOE_BLOCK // MOE_MM_ROWS):
            rows = pl.ds(g * MOE_MM_ROWS, MOE_MM_ROWS)
            act = _swiglu_act(xbuf[slot, rows, :], wgu_b[...])
            ys_ref[rows, :] = _dot(act.astype(BF16), wdn_b[...]).astype(ys_ref.dtype)

    @pl.when(j == nb - 1)
    def _():
        for a in range(1, GATHER_SLOTS):
            for cp in copies(j, (j + a) % GATHER_SLOTS):
                cp.wait()

    @pl.when(j >= nb)
    def _():
        ys_ref[...] = jnp.zeros_like(ys_ref)


def _experts(xs, block_expert, src_units, n_blocks_used, w_gu, w_dn):
    nb_max = block_expert.shape[0]
    grid_spec = pltpu.PrefetchScalarGridSpec(
        num_scalar_prefetch=3,
        grid=(nb_max,),
        in_specs=[pl.BlockSpec(memory_space=pl.ANY),
                  pl.BlockSpec((None, D_MODEL, 2 * D_EXPERT), lambda j, be, src, nb: (be[j], 0, 0)),
                  pl.BlockSpec((None, D_EXPERT, D_MODEL), lambda j, be, src, nb: (be[j], 0, 0))],
        out_specs=pl.BlockSpec((MOE_BLOCK, D_MODEL), lambda j, be, src, nb: (j, 0)),
        scratch_shapes=[pltpu.VMEM((GATHER_SLOTS, MOE_BLOCK, D_MODEL), BF16),
                        pltpu.SemaphoreType.DMA((GATHER_SLOTS,)),
                        pltpu.VMEM((D_MODEL, 2 * D_EXPERT), BF16), pltpu.VMEM((D_EXPERT, D_MODEL), BF16)],
    )
    return pl.pallas_call(
        _experts_kernel,
        grid_spec=grid_spec,
        out_shape=jax.ShapeDtypeStruct((nb_max * MOE_BLOCK, D_MODEL), BF16),
        compiler_params=_params(("arbitrary",)),
        name="moe_experts",
    )(block_expert, src_units, n_blocks_used, xs, w_gu, w_dn)


def _combine_kernel(src_ref, used_ref, ys_hbm, cw_ref, h_ref, x1_ref, gt2_ref, gpost_ref, digits_t_ref,
                    wsgu_ref, wsdn_ref, o_ref, ybuf, sem, acc_ref):
    i = pl.program_id(0)

    def copies(tile, slot, c):
        return [_unit_copy(ys_hbm, src_ref[tile * TILE_UNITS + u], ybuf, slot, u, sem)
                for u in range(c * CHUNK_UNITS, (c + 1) * CHUNK_UNITS)]

    def chunk_used(tile, c):
        return used_ref[tile] > c * CHUNK_UNITS

    def for_used_chunks(tile, fn):
        for c in range(N_CHUNKS):
            if c < FULL_CHUNKS:
                fn(c)
            else:
                pl.when(chunk_used(tile, c))(functools.partial(fn, c))

    def fetch(tile, slot):
        def start(c):
            for cp in copies(tile, slot, c):
                cp.start()

        for_used_chunks(tile, start)

    def wait_all(tile, slot):
        def wait(c):
            for cp in copies(tile, slot, c):
                cp.wait()

        for_used_chunks(tile, wait)

    @pl.when(i == 0)
    def _():
        fetch(0, 0)

    @pl.when(i + 1 < pl.num_programs(0))
    def _():
        fetch(i + 1, (i + 1) % 2)

    cw = cw_ref[...]
    t = cw.shape[0]
    routed = (cw > 0.0).astype(BF16)
    e_i = lax.broadcasted_iota(jnp.int32, (KEY_W, KEY_W), 0)
    f_i = lax.broadcasted_iota(jnp.int32, (KEY_W, KEY_W), 1)
    units = _segment_units(_dot_tn(routed, jnp.ones((t, KEY_W), BF16)))
    starts = _dot((f_i < e_i).astype(BF16), units.astype(BF16)) * MOE_UNIT
    ends = starts + units * MOE_UNIT
    units_row = _segment_units(_dot(jnp.ones((8, t), BF16), routed))
    starts_row = _dot(units_row.astype(BF16), (e_i < f_i).astype(BF16)) * MOE_UNIT
    keys = _token_keys(cw, starts_row[:1])
    wb = cw.astype(BF16)

    f = _dot(_swiglu_act(h_ref[...], wsgu_ref[...]).astype(BF16), wsdn_ref[...])
    slot = i % 2
    reps = ROW_CHUNK // KEY_W
    starts_c = jnp.concatenate([starts] * reps, axis=1)
    ends_c = jnp.concatenate([ends] * reps, axis=1)
    sub = lax.broadcasted_iota(jnp.int32, (KEY_W, ROW_CHUNK), 0)

    wait_all(i, slot)

    def chunk_sum(c):
        rows = slice(c * ROW_CHUNK, (c + 1) * ROW_CHUNK)
        r = (lax.broadcasted_iota(jnp.int32, (KEY_W, ROW_CHUNK), 1) + c * ROW_CHUNK).astype(F32)
        in_seg = (r >= starts_c) & (r < ends_c)
        rmap_t = jnp.where(sub < N_EXPERTS, in_seg.astype(F32), digits_t_ref[:, rows].astype(F32)).astype(BF16)
        hit = _dot(keys, jnp.concatenate([rmap_t, rmap_t], axis=0)) == 0.0
        w = _dot(wb, rmap_t)
        return _dot(jnp.where(hit, w, 0.0).astype(BF16), ybuf[slot, rows, :])

    for c in range(FULL_CHUNKS):
        f = f + chunk_sum(c)
    acc_ref[...] = f
    for c in range(FULL_CHUNKS, N_CHUNKS):
        @pl.when(chunk_used(i, c))
        def _(c=c):
            acc_ref[...] += chunk_sum(c)
    o_ref[...] = x1_ref[...] + gt2_ref[...] * _rms(acc_ref[...], gpost_ref[...])


def _combine(ys, src_units, used_units, cw, h2, x1, mod3, g_post, digits_t, wsgu_b, wsdn_b, seq):
    n = h2.shape[0]
    tpb = seq // MOE_TILE
    row = lambda w: pl.BlockSpec((MOE_TILE, w), lambda i, src, used: (i, 0))
    full = lambda a: pl.BlockSpec(a.shape, lambda i, src, used: (0,) * a.ndim)
    grid_spec = pltpu.PrefetchScalarGridSpec(
        num_scalar_prefetch=2,
        grid=(n // MOE_TILE,),
        in_specs=[pl.BlockSpec(memory_space=pl.ANY), row(KEY_W), row(D_MODEL), row(D_MODEL),
                  pl.BlockSpec((None, 1, D_MODEL), lambda i, src, used: (i // tpb, 0, 5)), full(g_post),
                  full(digits_t), full(wsgu_b), full(wsdn_b)],
        out_specs=row(D_MODEL),
        scratch_shapes=[pltpu.VMEM((2, TILE_ROWS, D_MODEL), BF16), pltpu.SemaphoreType.DMA((2,)),
                        pltpu.VMEM((MOE_TILE, D_MODEL), F32)],
    )
    return pl.pallas_call(
        _combine_kernel,
        grid_spec=grid_spec,
        out_shape=jax.ShapeDtypeStruct((n, D_MODEL), F32),
        compiler_params=_params(("arbitrary",)),
        name="moe_combine",
    )(src_units, used_units, ys, cw, h2, x1, mod3, g_post, digits_t, wsgu_b, wsdn_b)


def _row_digits():
    r = np.arange(TILE_ROWS)
    d = np.zeros((TILE_ROWS, KEY_W), np.float32)
    d[:, N_EXPERTS] = r // DIGIT
    d[:, N_EXPERTS + 1] = r % DIGIT
    return jnp.asarray(d, dtype=BF16)


def _moe_plan(counts, nb_max):
    n_tiles = counts.shape[0]
    s = (counts + (MOE_UNIT - 1)) // MOE_UNIT
    local = jnp.cumsum(s, axis=1) - s
    cs = jnp.cumsum(s, axis=0)
    per_expert = cs[-1]
    padded = (per_expert + UNITS_PER_BLOCK - 1) // UNITS_PER_BLOCK * UNITS_PER_BLOCK
    g_end = jnp.cumsum(padded)
    g_start = g_end - padded
    seg_start = g_start[None, :] + cs - s
    n_blocks_used = (g_end[-1] // UNITS_PER_BLOCK).astype(jnp.int32).reshape(1)
    jb = jnp.arange(nb_max, dtype=jnp.int32)
    one_e = ((jb[:, None] >= (g_start // UNITS_PER_BLOCK)[None, :])
             & (jb[:, None] < (g_end // UNITS_PER_BLOCK)[None, :])).astype(jnp.int32)
    pick_e = lambda table: jnp.sum(one_e[:, :, None] * table.T[None, :, :], axis=1)
    block_expert = jnp.where(jb < n_blocks_used[0], jnp.sum(one_e * jnp.arange(N_EXPERTS, dtype=jnp.int32), axis=1),
                             N_EXPERTS - 1).astype(jnp.int32)
    cs_b, s_b, local_b = pick_e(cs), pick_e(s), pick_e(local)
    q = (jb * UNITS_PER_BLOCK - jnp.sum(one_e * g_start[None, :], axis=1))[:, None] \
        + jnp.arange(UNITS_PER_BLOCK, dtype=jnp.int32)[None, :]
    tile = jnp.minimum(jnp.sum(cs_b[:, None, :] <= q[:, :, None], axis=2), n_tiles - 1)
    one_t = (tile[:, :, None] == jnp.arange(n_tiles, dtype=jnp.int32)).astype(jnp.int32)
    src = tile * TILE_UNITS + q + jnp.sum(one_t * (local_b - cs_b + s_b)[:, None, :], axis=2)
    valid = q < jnp.sum(one_e * per_expert[None, :], axis=1)[:, None]
    src_units = jnp.where(valid, src, 0).astype(jnp.int32).reshape(-1)
    u = jnp.arange(TILE_UNITS, dtype=jnp.int32)
    seg_end = local + s
    eu = jnp.minimum(jnp.sum(seg_end[:, None, :] <= u[None, :, None], axis=2), N_EXPERTS - 1)
    one_u = (eu[:, :, None] == jnp.arange(N_EXPERTS, dtype=jnp.int32)).astype(jnp.int32)
    back = u[None, :] + jnp.sum(one_u * (seg_start - local)[:, None, :], axis=2)
    back_units = jnp.where(u[None, :] < seg_end[:, -1:], back, 0).astype(jnp.int32).reshape(-1)
    return block_expert, src_units, n_blocks_used, back_units, seg_end[:, -1].astype(jnp.int32)


def _tile(n, pref):
    t = pref
    while n % t:
        t //= 2
    return t


def kernel(x, c, ctx, c_ctx, w_ada, b_ada, g_pre_mix, g_post_mix, g_pre_ffn, g_post_ffn, w_in, lb_logits, g_hgrn_out, cm_ln_g, cm_ln_b, w_spatial, b_spatial, w_branch_a, w_branch_b, w_out, w_router, b_router, w_expert_gu, w_expert_down, w_shared_gu, w_shared_down):
    B, T, D = x.shape
    L = ctx.shape[1]
    assert D == D_MODEL and w_ada.shape[0] == 1 and T % SCAN_CHUNK == 0 and L % SCAN_CHUNK == 0
    l = 0
    row = lambda a: a[l].reshape(1, -1)

    n_rows = -(-(B + 1) // 16) * 16
    cs = jnp.zeros((n_rows, D), F32).at[:B].set(c).at[B].set(c_ctx)
    mod3 = _ada_mod(cs, w_ada[l], row(b_ada)).reshape(n_rows, 1, 6 * D)

    w_in_b = w_in[l].astype(BF16)
    lbl = lb_logits[:, l:l + 2].reshape(4, HG_W)
    x2 = x.reshape(B * T, D)
    q, k2, lf2, v, sg, u, vn, sga, sgb = _proj_lat(
        x2, mod3, row(g_pre_mix), w_in_b, lbl, row(cm_ln_g), row(cm_ln_b), T, _tile(T, 256))
    kc2, lfc2, vc = _proj_ctx(ctx.reshape(B * L, D), mod3, B, row(g_pre_mix), w_in_b[:, HG_W:4 * HG_W], lbl,
                              _tile(B * L, 256))

    o_f, o_b = _hgrn_scan(q, k2, lf2, v, kc2, lfc2, vc, B, T, L)

    bs_full = jnp.repeat(b_spatial[l], CM_W // CM_GROUPS, axis=1)
    x1, h2, logits = _merge(
        o_f, o_b, sg, u, vn, sga, sgb, x2, mod3, row(g_hgrn_out), w_spatial[l].astype(BF16), bs_full,
        w_branch_a[l].astype(BF16), w_branch_b[l].astype(BF16), w_out[l].astype(BF16), row(g_post_mix),
        row(g_pre_ffn), jnp.pad(w_router[l].T, ((0, KEY_W - N_EXPERTS), (0, 0))).astype(BF16), T, _tile(T, 1024))

    cw = _router(logits, jnp.broadcast_to(b_router[l][:, None], (N_EXPERTS, 128)), _tile(B * T, 512))

    n_tok = B * T
    n_tiles = n_tok // MOE_TILE
    digits = _row_digits()
    xs, cnt = _dispatch(h2, cw, digits)
    counts = cnt.reshape(n_tiles, 8, KEY_W)[:, 0, :N_EXPERTS]
    max_units = (n_tok * TOP_K + n_tiles * N_EXPERTS * (MOE_UNIT - 1)) // MOE_UNIT + N_EXPERTS * (UNITS_PER_BLOCK - 1)
    nb_max = -(-max_units // UNITS_PER_BLOCK)
    block_expert, src_units, n_blocks_used, back_units, tile_units = _moe_plan(counts, nb_max)
    ys = _experts(xs, block_expert, src_units, n_blocks_used, w_expert_gu[l], w_expert_down[l])
    out = _combine(ys, back_units, tile_units, cw, h2, x1, mod3, row(g_post_ffn), digits.T,
                   w_shared_gu[l].astype(BF16), w_shared_down[l].astype(BF16), T)
    return out.reshape(B, T, D)
```

```python
import functools

import numpy as np
import jax
import jax.numpy as jnp
from jax import lax
from jax.experimental import pallas as pl
from jax.experimental.pallas import tpu as pltpu

F32 = jnp.float32
BF16 = jnp.bfloat16

D_MODEL = 1024
EPS = 1e-6
HG_HEADS = 4
HG_DK = 128
HG_W = HG_HEADS * HG_DK
CM_W = 512
CM_CHUNK = 128
CM_GROUPS = 4
D_IN = 5 * HG_W + 2 * CM_W + 2 * D_MODEL
N_EXPERTS = 64
TOP_K = 8
N_GROUPS = 8
GROUP_BITS = 3
TOPK_GROUPS = 4
D_EXPERT = 256
ROUTED_SCALE = 2.5
SCAN_CHUNK = 128
SUB = 16
N_LEVELS = 4
MERGE_ROWS = 512
VMEM_LIMIT = 56 * 1024 * 1024


def _params(sem):
    return pltpu.CompilerParams(dimension_semantics=sem, vmem_limit_bytes=VMEM_LIMIT)


def _dot(a, b):
    return jnp.dot(a, b, preferred_element_type=F32)


def _dot_nt(a, b):
    return lax.dot_general(a, b, (((1,), (1,)), ((), ())), preferred_element_type=F32)


def _dot_tn(a, b):
    return lax.dot_general(a, b, (((0,), (0,)), ((), ())), preferred_element_type=F32)


def _sigmoid(x):
    return 0.5 * jnp.tanh(0.5 * x) + 0.5


def _rms(x, g):
    return x * lax.rsqrt(jnp.mean(x * x, axis=-1, keepdims=True) + EPS) * g


def _ada_kernel(c_ref, w_ref, b_ref, o_ref):
    c = c_ref[...]
    s = c * _sigmoid(c)
    o_ref[...] = _dot(s.astype(BF16), w_ref[...].astype(BF16)) + b_ref[...]


def _ada_mod(cs, w_ada, b_ada):
    rows = cs.shape[0]
    n_out = w_ada.shape[1]
    return pl.pallas_call(
        _ada_kernel,
        grid=(n_out // D_MODEL,),
        in_specs=[
            pl.BlockSpec((rows, D_MODEL), lambda j: (0, 0)),
            pl.BlockSpec((D_MODEL, D_MODEL), lambda j: (0, j)),
            pl.BlockSpec((1, D_MODEL), lambda j: (0, j)),
        ],
        out_specs=pl.BlockSpec((rows, D_MODEL), lambda j: (0, j)),
        out_shape=jax.ShapeDtypeStruct((rows, n_out), F32),
        compiler_params=_params(("parallel",)),
        name="ada_mod",
    )(cs, w_ada, b_ada)


def _lower_bounds(lbl):
    out = []
    for d in range(2):
        l0, l1 = lbl[2 * d:2 * d + 1], lbl[2 * d + 1:2 * d + 2]
        m = jnp.maximum(l0, l1)
        e0, e1 = jnp.exp(l0 - m), jnp.exp(l1 - m)
        out.append(e0 / (e0 + e1))
    return out


def _prenorm(x_ref, sh_ref, sc_ref, g_ref):
    return (_rms(x_ref[...], g_ref[...]) * (1.0 + sc_ref[...]) + sh_ref[...]).astype(BF16)


def _gates(z, lb, k_ref, lf_ref, d):
    half_t = 0.5 * jnp.tanh(0.5 * z)
    k_ref[d] = ((1.0 - lb) * (0.5 - half_t)).astype(k_ref.dtype)
    lf_ref[d] = jnp.log2(lb + (1.0 - lb) * (0.5 + half_t))


def _proj_lat_kernel(x_ref, sh_ref, sc_ref, g_ref, w_ref, lbl_ref, lng_ref, lnb_ref,
                     q_ref, k_ref, lf_ref, v_ref, sg_ref, u_ref, vn_ref, sga_ref, sgb_ref):
    hb = _prenorm(x_ref, sh_ref, sc_ref, g_ref)
    lbs = _lower_bounds(lbl_ref[...])

    def mm(lo, width):
        return _dot(hb, w_ref[:, lo:lo + width])

    z = mm(0, HG_W)
    q_ref[...] = (z * _sigmoid(z)).astype(q_ref.dtype)
    for d in range(2):
        _gates(mm((1 + d) * HG_W, HG_W), lbs[d], k_ref, lf_ref, d)
    z = mm(4 * HG_W, HG_W)
    sg_ref[...] = (z * _sigmoid(z)).astype(sg_ref.dtype)
    u_ref[...] = jax.nn.gelu(mm(5 * HG_W, CM_W)).astype(u_ref.dtype)
    vv = jax.nn.gelu(mm(5 * HG_W + CM_W, CM_W))
    vc = vv - jnp.mean(vv, axis=-1, keepdims=True)
    vn = vc * lax.rsqrt(jnp.mean(vc * vc, axis=-1, keepdims=True) + EPS)
    vn_ref[...] = (vn * lng_ref[...] + lnb_ref[...]).astype(vn_ref.dtype)
    base = 5 * HG_W + 2 * CM_W
    sga_ref[...] = _sigmoid(mm(base, D_MODEL)).astype(sga_ref.dtype)
    sgb_ref[...] = _sigmoid(mm(base + D_MODEL, D_MODEL)).astype(sgb_ref.dtype)
    v_ref[...] = mm(3 * HG_W, HG_W).astype(v_ref.dtype)


def _mod_spec(rows_per_batch_tiles, col):
    return pl.BlockSpec((None, 1, D_MODEL), lambda i: (i // rows_per_batch_tiles, 0, col))


def _proj_lat(x2, mod3, g_pre, w_in_b, lbl, ln_g, ln_b, seq, tm):
    n = x2.shape[0]
    tpb = seq // tm
    row = lambda w: pl.BlockSpec((tm, w), lambda i: (i, 0))
    row2 = pl.BlockSpec((2, tm, HG_W), lambda i: (0, i, 0))
    full = lambda a: pl.BlockSpec(a.shape, lambda i: (0,) * a.ndim)
    outs = [
        (row(HG_W), jax.ShapeDtypeStruct((n, HG_W), BF16)),
        (row2, jax.ShapeDtypeStruct((2, n, HG_W), BF16)),
        (row2, jax.ShapeDtypeStruct((2, n, HG_W), F32)),
        (row(HG_W), jax.ShapeDtypeStruct((n, HG_W), BF16)),
        (row(HG_W), jax.ShapeDtypeStruct((n, HG_W), BF16)),
        (row(CM_W), jax.ShapeDtypeStruct((n, CM_W), BF16)),
        (row(CM_W), jax.ShapeDtypeStruct((n, CM_W), BF16)),
        (row(D_MODEL), jax.ShapeDtypeStruct((n, D_MODEL), BF16)),
        (row(D_MODEL), jax.ShapeDtypeStruct((n, D_MODEL), BF16)),
    ]
    return pl.pallas_call(
        _proj_lat_kernel,
        grid=(n // tm,),
        in_specs=[row(D_MODEL), _mod_spec(tpb, 0), _mod_spec(tpb, 1), full(g_pre), full(w_in_b),
                  full(lbl), full(ln_g), full(ln_b)],
        out_specs=[o[0] for o in outs],
        out_shape=[o[1] for o in outs],
        compiler_params=_params(("parallel",)),
        name="proj_lat",
    )(x2, mod3, mod3, g_pre, w_in_b, lbl, ln_g, ln_b)


def _proj_ctx_kernel(x_ref, sh_ref, sc_ref, g_ref, w_ref, lbl_ref, k_ref, lf_ref, v_ref):
    hb = _prenorm(x_ref, sh_ref, sc_ref, g_ref)
    lbs = _lower_bounds(lbl_ref[...])
    for d in range(2):
        _gates(_dot(hb, w_ref[:, d * HG_W:(d + 1) * HG_W]), lbs[d], k_ref, lf_ref, d)
    v_ref[...] = _dot(hb, w_ref[:, 2 * HG_W:3 * HG_W]).astype(v_ref.dtype)


def _proj_ctx(c2, mod3, ctx_row, g_pre, w_ctx_b, lbl, tm):
    n = c2.shape[0]
    row = lambda w: pl.BlockSpec((tm, w), lambda i: (i, 0))
    row2 = pl.BlockSpec((2, tm, HG_W), lambda i: (0, i, 0))
    full = lambda a: pl.BlockSpec(a.shape, lambda i: (0,) * a.ndim)
    mod = lambda col: pl.BlockSpec((None, 1, D_MODEL), lambda i: (ctx_row, 0, col))
    return pl.pallas_call(
        _proj_ctx_kernel,
        grid=(n // tm,),
        in_specs=[row(D_MODEL), mod(0), mod(1), full(g_pre), full(w_ctx_b), full(lbl)],
        out_specs=[row2, row2, row(HG_W)],
        out_shape=[jax.ShapeDtypeStruct((2, n, HG_W), BF16), jax.ShapeDtypeStruct((2, n, HG_W), F32),
                   jax.ShapeDtypeStruct((n, HG_W), BF16)],
        compiler_params=_params(("parallel",)),
        name="proj_ctx",
    )(c2, mod3, mod3, g_pre, w_ctx_b, lbl)


def _scan_tables():
    C = SCAN_CHUNK
    t = np.arange(C)
    lmats, lvls = [], []
    for d in range(2):
        p = t if d == 0 else C - 1 - t
        pt, ps = p[:, None], p[None, :]
        lmat = (ps <= pt).astype(np.float32)
        lmats.append(np.concatenate([lmat, lmat], axis=1))
        lvl = np.full((C, C), -1, np.int32)
        lvl[(pt // SUB == ps // SUB) & (ps <= pt)] = 0
        half, idx = SUB, 1
        while half < C:
            span = 2 * half
            lvl[(pt // span == ps // span) & ((pt // half) % 2 == 1) & ((ps // half) % 2 == 0)] = idx
            half, idx = span, idx + 1
        lvls.append(lvl)
    return jnp.asarray(np.stack(lmats), dtype=BF16), jnp.asarray(np.stack(lvls))


def _scan_step(d, slot, lmat_ref, lvl_ref, k_ref, lf_ref, v_ref, st_ref, b_scr, q_ref=None, o_ref=None):
    C = SCAN_CHUNK
    lf = lf_ref[...]
    hi = lf.astype(BF16)
    lo = (lf - hi.astype(F32)).astype(BF16)
    b_scr[slot] = _dot(lmat_ref[d], jnp.concatenate([hi, lo], axis=0))
    if q_ref is not None:
        lvl = lvl_ref[d]
        masks = [lvl == i for i in range(N_LEVELS)]

    for h in range(HG_HEADS):
        sl = slice(h * HG_DK, (h + 1) * HG_DK)
        b = b_scr[slot, :, sl]

        def row(i):
            return b_scr[slot, pl.ds(i, 1), sl]

        b_last = row(C - 1 if d == 0 else 0)
        k = k_ref[:, sl]
        v = v_ref[:, sl]
        st = st_ref[slot, :, sl]
        if q_ref is not None:
            q = q_ref[:, sl]
            e0 = jnp.concatenate([b[m * SUB:(m + 1) * SUB] - row(m * SUB + SUB // 2 - 1 + d)
                                  for m in range(C // SUB)], axis=0)
            factors = [(jnp.exp2(e0).astype(BF16), jnp.exp2(-e0).astype(BF16))]
            half = SUB
            while half < C:
                span = 2 * half
                e = jnp.concatenate([b[m * span:(m + 1) * span] - row(m * span + half - 1 + d)
                                     for m in range(C // span)], axis=0)
                w = jnp.exp2(-jnp.abs(e)).astype(BF16)
                factors.append((w, w))
                half = span
            a = jnp.zeros((C, C), F32)
            for (wq, wk), mask in zip(factors, masks):
                a = jnp.where(mask, _dot_nt(q * wq, k * wk), a)
            qhat = q * jnp.exp2(b).astype(BF16)
            o = _dot(a.astype(BF16), v) + _dot_nt(qhat, st.astype(BF16))
            o_ref[:, sl] = o.astype(o_ref.dtype)
        khat = k * jnp.exp2(b_last - b).astype(BF16)
        st_ref[slot, :, sl] = jnp.exp2(b_last) * st + _dot_tn(v, khat)


def _scan_kernel(n_ctx_steps, lmat_ref, lvl_ref, q_f, k_f, lf_f, v_f, q_b, k_b, lf_b, v_b,
                 kc_f, lfc_f, vc_f, kc_b, lfc_b, vc_b, o_f, o_b, st_ref, b_scr):
    s = pl.program_id(1)
    n_seq = q_f.shape[0]

    @pl.when(s == 0)
    def _():
        st_ref[...] = jnp.zeros_like(st_ref)

    @pl.when(s < n_ctx_steps)
    def _():
        for i in range(n_seq):
            _scan_step(0, 2 * i, lmat_ref, lvl_ref, kc_f.at[i], lfc_f.at[i], vc_f.at[i], st_ref, b_scr)
            _scan_step(1, 2 * i + 1, lmat_ref, lvl_ref, kc_b.at[i], lfc_b.at[i], vc_b.at[i], st_ref, b_scr)

    @pl.when(s >= n_ctx_steps)
    def _():
        for i in range(n_seq):
            _scan_step(0, 2 * i, lmat_ref, lvl_ref, k_f.at[i], lf_f.at[i], v_f.at[i], st_ref, b_scr,
                       q_f.at[i], o_f.at[i])
            _scan_step(1, 2 * i + 1, lmat_ref, lvl_ref, k_b.at[i], lf_b.at[i], v_b.at[i], st_ref, b_scr,
                       q_b.at[i], o_b.at[i])


def _hgrn_scan(q, k2, lf2, v, kc2, lfc2, vc, batch, seq, ctx_len):
    C = SCAN_CHUNK
    n_lat, n_ctx = seq // C, ctx_len // C
    n_seq = 2 if batch % 2 == 0 else 1
    lmat, lvl = _scan_tables()
    q, v = (a.reshape(batch, seq, HG_W) for a in (q, v))
    k2, lf2 = (a.reshape(2, batch, seq, HG_W) for a in (k2, lf2))
    vc = vc.reshape(batch, ctx_len, HG_W)
    kc2, lfc2 = (a.reshape(2, batch, ctx_len, HG_W) for a in (kc2, lfc2))

    def lat_blk(d):
        def blk(s):
            j = jnp.maximum(s - n_ctx, 0)
            return j if d == 0 else n_lat - 1 - j
        return blk

    def ctx_blk(d):
        def blk(s):
            i = jnp.minimum(s, n_ctx - 1)
            return i if d == 0 else n_ctx - 1 - i
        return blk

    def plain(blk):
        return pl.BlockSpec((n_seq, C, HG_W), lambda b, s: (b, blk(s), 0))

    def specs(blk_of, with_q):
        out = []
        for d in range(2):
            blk = blk_of(d)
            per_dir = pl.BlockSpec((None, n_seq, C, HG_W), lambda b, s, blk=blk, d=d: (d, b, blk(s), 0))
            out += ([plain(blk)] if with_q else []) + [per_dir, per_dir, plain(blk)]
        return out

    full = lambda a: pl.BlockSpec(a.shape, lambda b, s: (0,) * a.ndim)
    o_shape = jax.ShapeDtypeStruct((batch, seq, HG_W), BF16)
    n_chains = 2 * n_seq
    o_f, o_b = pl.pallas_call(
        functools.partial(_scan_kernel, n_ctx),
        grid=(batch // n_seq, n_ctx + n_lat),
        in_specs=[full(lmat), full(lvl)] + specs(lat_blk, True) + specs(ctx_blk, False),
        out_specs=[plain(lat_blk(d)) for d in range(2)],
        out_shape=[o_shape, o_shape],
        scratch_shapes=[pltpu.VMEM((n_chains, HG_DK, HG_W), F32), pltpu.VMEM((n_chains, C, HG_W), F32)],
        compiler_params=_params(("parallel", "arbitrary")),
        name="hgrn_scan",
    )(lmat, lvl, q, k2, lf2, v, q, k2, lf2, v, kc2, lfc2, vc, kc2, lfc2, vc)
    return o_f.reshape(batch * seq, HG_W), o_b.reshape(batch * seq, HG_W)


def _merge_kernel(of_ref, ob_ref, sg_ref, u_ref, vn_ref, sga_ref, sgb_ref, x_ref, gt1_ref, sh2_ref, sc2_ref,
                  gout_ref, ws_ref, bs_ref, wa_ref, wb_ref, wo_ref, gpost_ref, gffn_ref, wr_ref,
                  x1_ref, h2_ref, lg_ref):
    tm = x_ref.shape[0]
    gout = gout_ref[...]
    gw = CM_W // CM_GROUPS
    group = min(MERGE_ROWS, tm)
    for r0 in range(0, tm, group):
        rows = slice(r0, r0 + group)
        o = of_ref[rows, :].astype(F32) + ob_ref[rows, :].astype(F32)
        sg = sg_ref[rows, :].astype(F32)
        a = jnp.concatenate(
            [_rms(o[:, h * HG_DK:(h + 1) * HG_DK], gout) * sg[:, h * HG_DK:(h + 1) * HG_DK]
             for h in range(HG_HEADS)], axis=1).astype(BF16)
        vn = vn_ref[rows, :]
        z = jnp.concatenate(
            [jnp.concatenate([_dot(ws_ref[g], vn[c * CM_CHUNK:(c + 1) * CM_CHUNK, g * gw:(g + 1) * gw])
                              for g in range(CM_GROUPS)], axis=1) + bs_ref[...]
             for c in range(group // CM_CHUNK)], axis=0)
        bm = (u_ref[rows, :].astype(F32) * z).astype(BF16)
        y = (sga_ref[rows, :].astype(F32) * _dot(a, wa_ref[...])
             + sgb_ref[rows, :].astype(F32) * _dot(bm, wb_ref[...]))
        yo = _dot(y.astype(BF16), wo_ref[...])
        x1 = x_ref[rows, :] + gt1_ref[...] * _rms(yo, gpost_ref[...])
        x1_ref[rows, :] = x1
        h2 = (_rms(x1, gffn_ref[...]) * (1.0 + sc2_ref[...]) + sh2_ref[...]).astype(BF16)
        h2_ref[rows, :] = h2
        lg_ref[:, rows] = _dot_nt(wr_ref[...], h2)


def _merge(o_f, o_b, sg, u, vn, sga, sgb, x2, mod3, g_out, ws_b, bs_full, wa_b, wb_b, wo_b, g_post, g_ffn, wr_b,
           seq, tm):
    n = x2.shape[0]
    tpb = seq // tm
    row = lambda w: pl.BlockSpec((tm, w), lambda i: (i, 0))
    full = lambda a: pl.BlockSpec(a.shape, lambda i: (0,) * a.ndim)
    return pl.pallas_call(
        _merge_kernel,
        grid=(n // tm,),
        in_specs=[row(HG_W), row(HG_W), row(HG_W), row(CM_W), row(CM_W),
                  row(D_MODEL), row(D_MODEL), row(D_MODEL), _mod_spec(tpb, 2), _mod_spec(tpb, 3),
                  _mod_spec(tpb, 4), full(g_out), full(ws_b), full(bs_full), full(wa_b), full(wb_b),
                  full(wo_b), full(g_post), full(g_ffn), full(wr_b)],
        out_specs=[row(D_MODEL), row(D_MODEL), pl.BlockSpec((wr_b.shape[0], tm), lambda i: (0, i))],
        out_shape=[jax.ShapeDtypeStruct((n, D_MODEL), F32), jax.ShapeDtypeStruct((n, D_MODEL), BF16),
                   jax.ShapeDtypeStruct((wr_b.shape[0], n), F32)],
        compiler_params=_params(("parallel",)),
        name="merge",
    )(o_f, o_b, sg, u, vn, sga, sgb, x2, mod3, mod3, mod3, g_out, ws_b, bs_full, wa_b, wb_b, wo_b, g_post, g_ffn,
      wr_b)


def _router_kernel(lg_ref, br_ref, w_ref):
    tm = lg_ref.shape[1]
    gsz = N_EXPERTS // N_GROUPS
    scores = _sigmoid(lg_ref[:N_EXPERTS, :])
    sel = scores + jnp.concatenate([br_ref[...]] * (tm // br_ref.shape[1]), axis=1)
    neg = -jnp.inf

    def first_max(x, ids, sentinel, axis):
        m = jnp.max(x, axis=axis, keepdims=True)
        return m, jnp.min(jnp.where(x == m, ids, sentinel), axis=axis, keepdims=True)

    sel3 = sel.reshape(N_GROUPS, gsz, tm)
    j3 = lax.broadcasted_iota(jnp.int32, sel3.shape, 1)
    m1, i1 = first_max(sel3, j3, gsz, 1)
    gscore = m1 + jnp.max(jnp.where(j3 == i1, neg, sel3), axis=1, keepdims=True)
    g3 = lax.broadcasted_iota(jnp.int32, gscore.shape, 0)
    keep = jnp.zeros(gscore.shape, F32)
    for _ in range(TOPK_GROUPS):
        _, gi = first_max(gscore, g3, N_GROUPS, 0)
        keep = jnp.where(g3 == gi, 1.0, keep)
        gscore = jnp.where(g3 == gi, neg, gscore)
    x = jnp.where(keep > 0.0, sel3, neg).reshape(N_EXPERTS, tm)
    e_i = lax.broadcasted_iota(jnp.int32, x.shape, 0)
    w = jnp.zeros(x.shape, F32)
    for _ in range(TOP_K):
        _, ei = first_max(x, e_i, N_EXPERTS, 0)
        w = jnp.where(e_i == ei, scores, w)
        x = jnp.where(e_i == ei, neg, x)
    w = w / jnp.sum(w, axis=0, keepdims=True) * ROUTED_SCALE
    w_ref[...] = jnp.concatenate([w, jnp.zeros_like(w)], axis=0).T


def _router(logits_t, b_router_cols, tm):
    rows, n = logits_t.shape
    return pl.pallas_call(
        _router_kernel,
        grid=(n // tm,),
        in_specs=[pl.BlockSpec((rows, tm), lambda i: (0, i)),
                  pl.BlockSpec(b_router_cols.shape, lambda i: (0, 0))],
        out_specs=pl.BlockSpec((tm, rows), lambda i: (i, 0)),
        out_shape=jax.ShapeDtypeStruct((n, rows), F32),
        compiler_params=_params(("parallel",)),
        name="router",
    )(logits_t, b_router_cols)


MOE_TILE = 256
MOE_UNIT = 16
MOE_BLOCK = 512
MOE_MM_ROWS = 512
GATHER_SLOTS = 3
EXPERT_GRID_STEP = 24
UNITS_PER_BLOCK = MOE_BLOCK // MOE_UNIT
TILE_ROWS = 3072
TILE_UNITS = TILE_ROWS // MOE_UNIT
ROW_CHUNK = 512
N_CHUNKS = TILE_ROWS // ROW_CHUNK
FULL_CHUNKS = MOE_TILE * TOP_K // ROW_CHUNK
CHUNK_UNITS = ROW_CHUNK // MOE_UNIT
SORT_ROWS = 1024
KEY_W = 128
DIGIT_BITS = 6
DIGIT = 1 << DIGIT_BITS


def _swiglu_act(h, w_gu):
    gu = _dot(h, w_gu)
    de = gu.shape[1] // 2
    g = gu[:, :de]
    return g * _sigmoid(g) * gu[:, de:]


def _token_keys(cw, starts_row):
    t = cw.shape[0]
    routed = cw > 0.0
    t_i = lax.broadcasted_iota(jnp.int32, (t, t), 0)
    s_i = lax.broadcasted_iota(jnp.int32, (t, t), 1)
    rank = _dot((s_i < t_i).astype(BF16), routed.astype(BF16))
    pos = (starts_row + rank).astype(jnp.int32)
    lane = lax.broadcasted_iota(jnp.int32, cw.shape, 1)
    hi = jnp.where(routed, jnp.right_shift(pos, DIGIT_BITS), -1)
    lo = jnp.where(routed, jnp.bitwise_and(pos, DIGIT - 1), -1)
    key_hi = jnp.where(lane < N_EXPERTS, hi * DIGIT,
                       jnp.where(lane == N_EXPERTS, -DIGIT, jnp.where(lane == N_EXPERTS + 1, -1, 0)))
    key_lo = jnp.where(lane < N_EXPERTS, lo, 0)
    return jnp.concatenate([key_hi, key_lo], axis=1).astype(F32).astype(BF16)


def _segment_units(counts):
    return jnp.floor((counts + (MOE_UNIT - 1)) * (1.0 / MOE_UNIT))


def _dispatch_kernel(h_ref, cw_ref, digits_ref, xs_ref, cnt_ref):
    cw = cw_ref[...]
    t = cw.shape[0]
    routed = (cw > 0.0).astype(BF16)
    counts = _dot(jnp.ones((8, t), BF16), routed)
    cnt_ref[...] = counts.astype(jnp.int32)
    units = _segment_units(counts)
    e_i = lax.broadcasted_iota(jnp.int32, (KEY_W, KEY_W), 0)
    f_i = lax.broadcasted_iota(jnp.int32, (KEY_W, KEY_W), 1)
    starts = _dot(units.astype(BF16), (e_i < f_i).astype(BF16)) * MOE_UNIT
    ends = starts + units * MOE_UNIT
    keys_t = _token_keys(cw, starts[:1]).astype(F32).T.astype(BF16)
    h = h_ref[...]
    used_rows = jnp.max(ends)

    def sort_rows(r0, n):
        rows = slice(r0, r0 + n)
        lane = lax.broadcasted_iota(jnp.int32, (n, KEY_W), 1)
        r = (lax.broadcasted_iota(jnp.int32, (n, KEY_W), 0) + r0).astype(F32)
        in_seg = (r >= starts[:1]) & (r < ends[:1])
        rmap = jnp.where(lane < N_EXPERTS, in_seg.astype(F32), digits_ref[rows, :].astype(F32)).astype(BF16)
        hit = _dot(jnp.concatenate([rmap, rmap], axis=1), keys_t) == 0.0
        xs_ref[rows, :] = _dot(hit.astype(BF16), h).astype(xs_ref.dtype)

    for r0 in range(0, FULL_CHUNKS * ROW_CHUNK, SORT_ROWS):
        sort_rows(r0, SORT_ROWS)
    for c in range(FULL_CHUNKS, N_CHUNKS):
        pl.when(used_rows > c * ROW_CHUNK)(functools.partial(sort_rows, c * ROW_CHUNK, ROW_CHUNK))

        @pl.when(used_rows <= c * ROW_CHUNK)
        def _(c=c):
            xs_ref[c * ROW_CHUNK:(c + 1) * ROW_CHUNK, :] = jnp.zeros((ROW_CHUNK, D_MODEL), xs_ref.dtype)


def _dispatch(h2, cw, digits):
    n = h2.shape[0]
    n_tiles = n // MOE_TILE
    return pl.pallas_call(
        _dispatch_kernel,
        grid=(n_tiles,),
        in_specs=[pl.BlockSpec((MOE_TILE, D_MODEL), lambda i: (i, 0)),
                  pl.BlockSpec((MOE_TILE, KEY_W), lambda i: (i, 0)),
                  pl.BlockSpec(digits.shape, lambda i: (0, 0))],
        out_specs=[pl.BlockSpec((TILE_ROWS, D_MODEL), lambda i: (i, 0)),
                   pl.BlockSpec((8, KEY_W), lambda i: (i, 0))],
        out_shape=[jax.ShapeDtypeStruct((n_tiles * TILE_ROWS, D_MODEL), BF16),
                   jax.ShapeDtypeStruct((n_tiles * 8, KEY_W), jnp.int32)],
        compiler_params=_params(("parallel",)),
        name="moe_dispatch",
    )(h2, cw, digits)


def _unit_copy(src_hbm, unit, dst, slot, pos, sem):
    return pltpu.make_async_copy(
        src_hbm.at[pl.ds(pl.multiple_of(unit * MOE_UNIT, MOE_UNIT), MOE_UNIT)],
        dst.at[slot, pl.ds(pos * MOE_UNIT, MOE_UNIT)], sem.at[slot])


def _experts_kernel(be_ref, src_ref, nb_ref, wplan_ref, xs_hbm, wgu_hbm, wdn_hbm, ys_ref, xbuf, sem,
                    wgu_f, wdn_f, wsem, wgu_b, wdn_b):
    j = pl.program_id(0)
    nb = nb_ref[0]
    n_steps = pl.num_programs(0)

    def copies(blk, slot):
        return [_unit_copy(xs_hbm, src_ref[blk * UNITS_PER_BLOCK + u], xbuf, slot, u, sem)
                for u in range(UNITS_PER_BLOCK)]

    def fetch(blk, slot):
        for cp in copies(blk, slot):
            cp.start()

    def weight_copies(expert, slot):
        return [pltpu.make_async_copy(wgu_hbm.at[expert], wgu_f.at[slot], wsem.at[slot]),
                pltpu.make_async_copy(wdn_hbm.at[expert], wdn_f.at[slot], wsem.at[slot])]

    ahead = GATHER_SLOTS - 1

    @pl.when(j == 0)
    def _():
        for a in range(ahead):
            fetch(jnp.minimum(a, nb - 1), a)
        for cp in weight_copies(be_ref[0], 0):
            cp.start()

    @pl.when((j < nb) & (wplan_ref[j] == 1))
    def _():
        slot = wplan_ref[n_steps + j]
        for cp in weight_copies(be_ref[j], slot):
            cp.wait()
        nxt = wplan_ref[2 * n_steps + j]

        @pl.when(nxt >= 0)
        def _():
            for cp in weight_copies(nxt, 1 - slot):
                cp.start()

        wgu_b[...] = wgu_f[slot].astype(BF16)
        wdn_b[...] = wdn_f[slot].astype(BF16)

    @pl.when(j < nb)
    def _():
        slot = j % GATHER_SLOTS
        for cp in copies(j, slot):
            cp.wait()
        fetch(jnp.minimum(j + ahead, nb - 1), (j + ahead) % GATHER_SLOTS)
        for g in range(MOE_BLOCK // MOE_MM_ROWS):
            rows = pl.ds(g * MOE_MM_ROWS, MOE_MM_ROWS)
            act = _swiglu_act(xbuf[slot, rows, :], wgu_b[...])
            ys_ref[rows, :] = _dot(act.astype(BF16), wdn_b[...]).astype(ys_ref.dtype)

    @pl.when(j == nb - 1)
    def _():
        for a in range(1, GATHER_SLOTS):
            for cp in copies(j, (j + a) % GATHER_SLOTS):
                cp.wait()

    @pl.when(j >= nb)
    def _():
        ys_ref[...] = jnp.zeros_like(ys_ref)


def _weight_plan(block_expert, n_blocks_used):
    nb = block_expert.shape[0]
    jb = jnp.arange(nb, dtype=jnp.int32)
    used = jb < n_blocks_used[0]
    first = used & ((jb == 0) | (block_expert != jnp.roll(block_expert, 1)))
    slot = (jnp.cumsum(first.astype(jnp.int32)) - 1) % 2
    first_at = jnp.where(first, jb, nb)
    nxt_first = jnp.min(jnp.where(first_at[None, :] > jb[:, None], first_at[None, :], nb), axis=1)
    nxt_expert = jnp.sum(jnp.where(jb[None, :] == nxt_first[:, None], block_expert[None, :], 0), axis=1)
    nxt = jnp.where(nxt_first < nb, nxt_expert, -1)
    return jnp.concatenate([first.astype(jnp.int32), slot.astype(jnp.int32), nxt.astype(jnp.int32)])


def _experts(xs, block_expert, src_units, n_blocks_used, w_gu, w_dn):
    nb_max = block_expert.shape[0]
    any_spec = pl.BlockSpec(memory_space=pl.ANY)
    grid_spec = pltpu.PrefetchScalarGridSpec(
        num_scalar_prefetch=4,
        grid=(nb_max,),
        in_specs=[any_spec, any_spec, any_spec],
        out_specs=pl.BlockSpec((MOE_BLOCK, D_MODEL), lambda j, be, src, nb, wplan: (j, 0)),
        scratch_shapes=[pltpu.VMEM((GATHER_SLOTS, MOE_BLOCK, D_MODEL), BF16),
                        pltpu.SemaphoreType.DMA((GATHER_SLOTS,)),
                        pltpu.VMEM((2, D_MODEL, 2 * D_EXPERT), F32), pltpu.VMEM((2, D_EXPERT, D_MODEL), F32),
                        pltpu.SemaphoreType.DMA((2,)),
                        pltpu.VMEM((D_MODEL, 2 * D_EXPERT), BF16), pltpu.VMEM((D_EXPERT, D_MODEL), BF16)],
    )
    return pl.pallas_call(
        _experts_kernel,
        grid_spec=grid_spec,
        out_shape=jax.ShapeDtypeStruct((nb_max * MOE_BLOCK, D_MODEL), BF16),
        compiler_params=_params(("arbitrary",)),
        name="moe_experts",
    )(block_expert, src_units, n_blocks_used, _weight_plan(block_expert, n_blocks_used), xs, w_gu, w_dn)


def _combine_kernel(src_ref, used_ref, ys_hbm, cw_ref, h_ref, x1_ref, gt2_ref, gpost_ref, digits_t_ref,
                    wsgu_ref, wsdn_ref, o_ref, ybuf, sem, acc_ref):
    i = pl.program_id(0)

    def copies(tile, slot, c):
        return [_unit_copy(ys_hbm, src_ref[tile * TILE_UNITS + u], ybuf, slot, u, sem)
                for u in range(c * CHUNK_UNITS, (c + 1) * CHUNK_UNITS)]

    def chunk_used(tile, c):
        return used_ref[tile] > c * CHUNK_UNITS

    def for_used_chunks(tile, fn):
        for c in range(N_CHUNKS):
            if c < FULL_CHUNKS:
                fn(c)
            else:
                pl.when(chunk_used(tile, c))(functools.partial(fn, c))

    def fetch(tile, slot):
        def start(c):
            for cp in copies(tile, slot, c):
                cp.start()

        for_used_chunks(tile, start)

    def wait_all(tile, slot):
        def wait(c):
            for cp in copies(tile, slot, c):
                cp.wait()

        for_used_chunks(tile, wait)

    @pl.when(i == 0)
    def _():
        fetch(0, 0)

    @pl.when(i + 1 < pl.num_programs(0))
    def _():
        fetch(i + 1, (i + 1) % 2)

    cw = cw_ref[...]
    t = cw.shape[0]
    routed = (cw > 0.0).astype(BF16)
    e_i = lax.broadcasted_iota(jnp.int32, (KEY_W, KEY_W), 0)
    f_i = lax.broadcasted_iota(jnp.int32, (KEY_W, KEY_W), 1)
    units = _segment_units(_dot_tn(routed, jnp.ones((t, KEY_W), BF16)))
    starts = _dot((f_i < e_i).astype(BF16), units.astype(BF16)) * MOE_UNIT
    ends = starts + units * MOE_UNIT
    units_row = _segment_units(_dot(jnp.ones((8, t), BF16), routed))
    starts_row = _dot(units_row.astype(BF16), (e_i < f_i).astype(BF16)) * MOE_UNIT
    keys = _token_keys(cw, starts_row[:1])
    wb = cw.astype(BF16)

    f = _dot(_swiglu_act(h_ref[...], wsgu_ref[...]).astype(BF16), wsdn_ref[...])
    slot = i % 2
    reps = ROW_CHUNK // KEY_W
    starts_c = jnp.concatenate([starts] * reps, axis=1)
    ends_c = jnp.concatenate([ends] * reps, axis=1)
    sub = lax.broadcasted_iota(jnp.int32, (KEY_W, ROW_CHUNK), 0)

    wait_all(i, slot)

    def chunk_sum(c):
        rows = slice(c * ROW_CHUNK, (c + 1) * ROW_CHUNK)
        r = (lax.broadcasted_iota(jnp.int32, (KEY_W, ROW_CHUNK), 1) + c * ROW_CHUNK).astype(F32)
        in_seg = (r >= starts_c) & (r < ends_c)
        rmap_t = jnp.where(sub < N_EXPERTS, in_seg.astype(F32), digits_t_ref[:, rows].astype(F32)).astype(BF16)
        hit = _dot(keys, jnp.concatenate([rmap_t, rmap_t], axis=0)) == 0.0
        w = _dot(wb, rmap_t)
        return _dot(jnp.where(hit, w, 0.0).astype(BF16), ybuf[slot, rows, :])

    for c in range(FULL_CHUNKS):
        f = f + chunk_sum(c)
    acc_ref[...] = f
    for c in range(FULL_CHUNKS, N_CHUNKS):
        @pl.when(chunk_used(i, c))
        def _(c=c):
            acc_ref[...] += chunk_sum(c)
    o_ref[...] = x1_ref[...] + gt2_ref[...] * _rms(acc_ref[...], gpost_ref[...])


def _combine(ys, src_units, used_units, cw, h2, x1, mod3, g_post, digits_t, wsgu_b, wsdn_b, seq):
    n = h2.shape[0]
    tpb = seq // MOE_TILE
    row = lambda w: pl.BlockSpec((MOE_TILE, w), lambda i, src, used: (i, 0))
    full = lambda a: pl.BlockSpec(a.shape, lambda i, src, used: (0,) * a.ndim)
    grid_spec = pltpu.PrefetchScalarGridSpec(
        num_scalar_prefetch=2,
        grid=(n // MOE_TILE,),
        in_specs=[pl.BlockSpec(memory_space=pl.ANY), row(KEY_W), row(D_MODEL), row(D_MODEL),
                  pl.BlockSpec((None, 1, D_MODEL), lambda i, src, used: (i // tpb, 0, 5)), full(g_post),
                  full(digits_t), full(wsgu_b), full(wsdn_b)],
        out_specs=row(D_MODEL),
        scratch_shapes=[pltpu.VMEM((2, TILE_ROWS, D_MODEL), BF16), pltpu.SemaphoreType.DMA((2,)),
                        pltpu.VMEM((MOE_TILE, D_MODEL), F32)],
    )
    return pl.pallas_call(
        _combine_kernel,
        grid_spec=grid_spec,
        out_shape=jax.ShapeDtypeStruct((n, D_MODEL), F32),
        compiler_params=_params(("arbitrary",)),
        name="moe_combine",
    )(src_units, used_units, ys, cw, h2, x1, mod3, g_post, digits_t, wsgu_b, wsdn_b)


def _row_digits():
    r = np.arange(TILE_ROWS)
    d = np.zeros((TILE_ROWS, KEY_W), np.float32)
    d[:, N_EXPERTS] = r // DIGIT
    d[:, N_EXPERTS + 1] = r % DIGIT
    return jnp.asarray(d, dtype=BF16)


def _moe_plan(counts, nb_max):
    n_tiles = counts.shape[0]
    s = (counts + (MOE_UNIT - 1)) // MOE_UNIT
    local = jnp.cumsum(s, axis=1) - s
    cs = jnp.cumsum(s, axis=0)
    per_expert = cs[-1]
    padded = (per_expert + UNITS_PER_BLOCK - 1) // UNITS_PER_BLOCK * UNITS_PER_BLOCK
    g_end = jnp.cumsum(padded)
    g_start = g_end - padded
    seg_start = g_start[None, :] + cs - s
    n_blocks_used = (g_end[-1] // UNITS_PER_BLOCK).astype(jnp.int32).reshape(1)
    jb = jnp.arange(nb_max, dtype=jnp.int32)
    one_e = ((jb[:, None] >= (g_start // UNITS_PER_BLOCK)[None, :])
             & (jb[:, None] < (g_end // UNITS_PER_BLOCK)[None, :])).astype(jnp.int32)
    pick_e = lambda table: jnp.sum(one_e[:, :, None] * table.T[None, :, :], axis=1)
    block_expert = jnp.where(jb < n_blocks_used[0], jnp.sum(one_e * jnp.arange(N_EXPERTS, dtype=jnp.int32), axis=1),
                             N_EXPERTS - 1).astype(jnp.int32)
    cs_b, s_b, local_b = pick_e(cs), pick_e(s), pick_e(local)
    q = (jb * UNITS_PER_BLOCK - jnp.sum(one_e * g_start[None, :], axis=1))[:, None] \
        + jnp.arange(UNITS_PER_BLOCK, dtype=jnp.int32)[None, :]
    tile = jnp.minimum(jnp.sum(cs_b[:, None, :] <= q[:, :, None], axis=2), n_tiles - 1)
    one_t = (tile[:, :, None] == jnp.arange(n_tiles, dtype=jnp.int32)).astype(jnp.int32)
    src = tile * TILE_UNITS + q + jnp.sum(one_t * (local_b - cs_b + s_b)[:, None, :], axis=2)
    valid = q < jnp.sum(one_e * per_expert[None, :], axis=1)[:, None]
    src_units = jnp.where(valid, src, 0).astype(jnp.int32).reshape(-1)
    u = jnp.arange(TILE_UNITS, dtype=jnp.int32)
    seg_end = local + s
    eu = jnp.minimum(jnp.sum(seg_end[:, None, :] <= u[None, :, None], axis=2), N_EXPERTS - 1)
    one_u = (eu[:, :, None] == jnp.arange(N_EXPERTS, dtype=jnp.int32)).astype(jnp.int32)
    back = u[None, :] + jnp.sum(one_u * (seg_start - local)[:, None, :], axis=2)
    back_units = jnp.where(u[None, :] < seg_end[:, -1:], back, 0).astype(jnp.int32).reshape(-1)
    return block_expert, src_units, n_blocks_used, back_units, seg_end[:, -1].astype(jnp.int32)


def _tile(n, pref):
    t = pref
    while n % t:
        t //= 2
    return t


def kernel(x, c, ctx, c_ctx, w_ada, b_ada, g_pre_mix, g_post_mix, g_pre_ffn, g_post_ffn, w_in, lb_logits, g_hgrn_out, cm_ln_g, cm_ln_b, w_spatial, b_spatial, w_branch_a, w_branch_b, w_out, w_router, b_router, w_expert_gu, w_expert_down, w_shared_gu, w_shared_down):
    B, T, D = x.shape
    L = ctx.shape[1]
    assert D == D_MODEL and w_ada.shape[0] == 1 and T % SCAN_CHUNK == 0 and L % SCAN_CHUNK == 0
    l = 0
    row = lambda a: a[l].reshape(1, -1)

    n_rows = -(-(B + 1) // 16) * 16
    cs = jnp.zeros((n_rows, D), F32).at[:B].set(c).at[B].set(c_ctx)
    mod3 = _ada_mod(cs, w_ada[l], row(b_ada)).reshape(n_rows, 1, 6 * D)

    w_in_b = w_in[l].astype(BF16)
    lbl = lb_logits[:, l:l + 2].reshape(4, HG_W)
    x2 = x.reshape(B * T, D)
    q, k2, lf2, v, sg, u, vn, sga, sgb = _proj_lat(
        x2, mod3, row(g_pre_mix), w_in_b, lbl, row(cm_ln_g), row(cm_ln_b), T, _tile(T, 256))
    kc2, lfc2, vc = _proj_ctx(ctx.reshape(B * L, D), mod3, B, row(g_pre_mix), w_in_b[:, HG_W:4 * HG_W], lbl,
                              _tile(B * L, 256))

    o_f, o_b = _hgrn_scan(q, k2, lf2, v, kc2, lfc2, vc, B, T, L)

    bs_full = jnp.repeat(b_spatial[l], CM_W // CM_GROUPS, axis=1)
    x1, h2, logits = _merge(
        o_f, o_b, sg, u, vn, sga, sgb, x2, mod3, row(g_hgrn_out), w_spatial[l].astype(BF16), bs_full,
        w_branch_a[l].astype(BF16), w_branch_b[l].astype(BF16), w_out[l].astype(BF16), row(g_post_mix),
        row(g_pre_ffn), jnp.pad(w_router[l].T, ((0, KEY_W - N_EXPERTS), (0, 0))).astype(BF16), T, _tile(T, 1024))

    cw = _router(logits, jnp.broadcast_to(b_router[l][:, None], (N_EXPERTS, 128)), _tile(B * T, 512))

    n_tok = B * T
    n_tiles = n_tok // MOE_TILE
    digits = _row_digits()
    xs, cnt = _dispatch(h2, cw, digits)
    counts = cnt.reshape(n_tiles, 8, KEY_W)[:, 0, :N_EXPERTS]
    max_units = (n_tok * TOP_K + n_tiles * N_EXPERTS * (MOE_UNIT - 1)) // MOE_UNIT + N_EXPERTS * (UNITS_PER_BLOCK - 1)
    nb_max = -(-max_units // UNITS_PER_BLOCK)
    block_expert, src_units, n_blocks_used, back_units, tile_units = _moe_plan(counts, nb_max)
    wsgu_b, wsdn_b = w_shared_gu[l].astype(BF16), w_shared_down[l].astype(BF16)

    nb_min = -(-(n_tok * TOP_K) // MOE_BLOCK)
    grids = sorted({min(nb_max, g) for g in range(nb_min + EXPERT_GRID_STEP, nb_max + EXPERT_GRID_STEP,
                                                  EXPERT_GRID_STEP)})

    def experts_and_combine(nb_grid):
        def run():
            ys = _experts(xs, block_expert[:nb_grid], src_units[:nb_grid * UNITS_PER_BLOCK], n_blocks_used,
                          w_expert_gu[l], w_expert_down[l])
            return _combine(ys, back_units, tile_units, cw, h2, x1, mod3, row(g_post_ffn), digits.T,
                            wsgu_b, wsdn_b, T)
        return run

    which = jnp.sum(n_blocks_used[0] > jnp.asarray(grids[:-1], jnp.int32))
    out = lax.switch(which, [experts_and_combine(g) for g in grids])
    return out.reshape(B, T, D)
```

```python
import functools

import numpy as np
import jax
import jax.numpy as jnp
from jax import lax
from jax.experimental import pallas as pl
from jax.experimental.pallas import tpu as pltpu

F32 = jnp.float32
BF16 = jnp.bfloat16

D_MODEL = 1024
EPS = 1e-6
HG_HEADS = 4
HG_DK = 128
HG_W = HG_HEADS * HG_DK
CM_W = 512
CM_CHUNK = 128
CM_GROUPS = 4
D_IN = 5 * HG_W + 2 * CM_W + 2 * D_MODEL
N_EXPERTS = 64
TOP_K = 8
N_GROUPS = 8
GROUP_BITS = 3
TOPK_GROUPS = 4
D_EXPERT = 256
ROUTED_SCALE = 2.5
SCAN_CHUNK = 128
SUB = 16
N_LEVELS = 4
MERGE_ROWS = 512
VMEM_LIMIT = 56 * 1024 * 1024


def _params(sem):
    return pltpu.CompilerParams(dimension_semantics=sem, vmem_limit_bytes=VMEM_LIMIT)


def _dot(a, b):
    return jnp.dot(a, b, preferred_element_type=F32)


def _dot_nt(a, b):
    return lax.dot_general(a, b, (((1,), (1,)), ((), ())), preferred_element_type=F32)


def _dot_tn(a, b):
    return lax.dot_general(a, b, (((0,), (0,)), ((), ())), preferred_element_type=F32)


def _sigmoid(x):
    return 0.5 * jnp.tanh(0.5 * x) + 0.5


def _rms(x, g):
    return x * lax.rsqrt(jnp.mean(x * x, axis=-1, keepdims=True) + EPS) * g


def _ada_kernel(c_ref, w_ref, b_ref, o_ref):
    c = c_ref[...]
    s = c * _sigmoid(c)
    o_ref[...] = _dot(s.astype(BF16), w_ref[...].astype(BF16)) + b_ref[...]


def _ada_mod(cs, w_ada, b_ada):
    rows = cs.shape[0]
    n_out = w_ada.shape[1]
    return pl.pallas_call(
        _ada_kernel,
        grid=(n_out // D_MODEL,),
        in_specs=[
            pl.BlockSpec((rows, D_MODEL), lambda j: (0, 0)),
            pl.BlockSpec((D_MODEL, D_MODEL), lambda j: (0, j)),
            pl.BlockSpec((1, D_MODEL), lambda j: (0, j)),
        ],
        out_specs=pl.BlockSpec((rows, D_MODEL), lambda j: (0, j)),
        out_shape=jax.ShapeDtypeStruct((rows, n_out), F32),
        compiler_params=_params(("parallel",)),
        name="ada_mod",
    )(cs, w_ada, b_ada)


def _lower_bounds(lbl):
    out = []
    for d in range(2):
        l0, l1 = lbl[2 * d:2 * d + 1], lbl[2 * d + 1:2 * d + 2]
        m = jnp.maximum(l0, l1)
        e0, e1 = jnp.exp(l0 - m), jnp.exp(l1 - m)
        out.append(e0 / (e0 + e1))
    return out


def _prenorm(x_ref, sh_ref, sc_ref, g_ref):
    return (_rms(x_ref[...], g_ref[...]) * (1.0 + sc_ref[...]) + sh_ref[...]).astype(BF16)


def _gates(z, lb, k_ref, lf_ref, d):
    half_t = 0.5 * jnp.tanh(0.5 * z)
    k_ref[d] = ((1.0 - lb) * (0.5 - half_t)).astype(k_ref.dtype)
    lf_ref[d] = jnp.log2(lb + (1.0 - lb) * (0.5 + half_t))


def _proj_lat_kernel(x_ref, sh_ref, sc_ref, g_ref, w_ref, lbl_ref, lng_ref, lnb_ref,
                     q_ref, k_ref, lf_ref, v_ref, sg_ref, u_ref, vn_ref, sga_ref, sgb_ref):
    hb = _prenorm(x_ref, sh_ref, sc_ref, g_ref)
    lbs = _lower_bounds(lbl_ref[...])

    def mm(lo, width):
        return _dot(hb, w_ref[:, lo:lo + width])

    z = mm(0, HG_W)
    q_ref[...] = (z * _sigmoid(z)).astype(q_ref.dtype)
    for d in range(2):
        _gates(mm((1 + d) * HG_W, HG_W), lbs[d], k_ref, lf_ref, d)
    z = mm(4 * HG_W, HG_W)
    sg_ref[...] = (z * _sigmoid(z)).astype(sg_ref.dtype)
    u_ref[...] = jax.nn.gelu(mm(5 * HG_W, CM_W)).astype(u_ref.dtype)
    vv = jax.nn.gelu(mm(5 * HG_W + CM_W, CM_W))
    vc = vv - jnp.mean(vv, axis=-1, keepdims=True)
    vn = vc * lax.rsqrt(jnp.mean(vc * vc, axis=-1, keepdims=True) + EPS)
    vn_ref[...] = (vn * lng_ref[...] + lnb_ref[...]).astype(vn_ref.dtype)
    base = 5 * HG_W + 2 * CM_W
    sga_ref[...] = _sigmoid(mm(base, D_MODEL)).astype(sga_ref.dtype)
    sgb_ref[...] = _sigmoid(mm(base + D_MODEL, D_MODEL)).astype(sgb_ref.dtype)
    v_ref[...] = mm(3 * HG_W, HG_W).astype(v_ref.dtype)


def _mod_spec(rows_per_batch_tiles, col):
    return pl.BlockSpec((None, 1, D_MODEL), lambda i: (i // rows_per_batch_tiles, 0, col))


def _proj_lat(x2, mod3, g_pre, w_in_b, lbl, ln_g, ln_b, seq, tm):
    n = x2.shape[0]
    tpb = seq // tm
    row = lambda w: pl.BlockSpec((tm, w), lambda i: (i, 0))
    row2 = pl.BlockSpec((2, tm, HG_W), lambda i: (0, i, 0))
    full = lambda a: pl.BlockSpec(a.shape, lambda i: (0,) * a.ndim)
    outs = [
        (row(HG_W), jax.ShapeDtypeStruct((n, HG_W), BF16)),
        (row2, jax.ShapeDtypeStruct((2, n, HG_W), BF16)),
        (row2, jax.ShapeDtypeStruct((2, n, HG_W), F32)),
        (row(HG_W), jax.ShapeDtypeStruct((n, HG_W), BF16)),
        (row(HG_W), jax.ShapeDtypeStruct((n, HG_W), BF16)),
        (row(CM_W), jax.ShapeDtypeStruct((n, CM_W), BF16)),
        (row(CM_W), jax.ShapeDtypeStruct((n, CM_W), BF16)),
        (row(D_MODEL), jax.ShapeDtypeStruct((n, D_MODEL), BF16)),
        (row(D_MODEL), jax.ShapeDtypeStruct((n, D_MODEL), BF16)),
    ]
    return pl.pallas_call(
        _proj_lat_kernel,
        grid=(n // tm,),
        in_specs=[row(D_MODEL), _mod_spec(tpb, 0), _mod_spec(tpb, 1), full(g_pre), full(w_in_b),
                  full(lbl), full(ln_g), full(ln_b)],
        out_specs=[o[0] for o in outs],
        out_shape=[o[1] for o in outs],
        compiler_params=_params(("parallel",)),
        name="proj_lat",
    )(x2, mod3, mod3, g_pre, w_in_b, lbl, ln_g, ln_b)


def _proj_ctx_kernel(x_ref, sh_ref, sc_ref, g_ref, w_ref, lbl_ref, k_ref, lf_ref, v_ref):
    hb = _prenorm(x_ref, sh_ref, sc_ref, g_ref)
    lbs = _lower_bounds(lbl_ref[...])
    for d in range(2):
        _gates(_dot(hb, w_ref[:, d * HG_W:(d + 1) * HG_W]), lbs[d], k_ref, lf_ref, d)
    v_ref[...] = _dot(hb, w_ref[:, 2 * HG_W:3 * HG_W]).astype(v_ref.dtype)


def _proj_ctx(c2, mod3, ctx_row, g_pre, w_ctx_b, lbl, tm):
    n = c2.shape[0]
    row = lambda w: pl.BlockSpec((tm, w), lambda i: (i, 0))
    row2 = pl.BlockSpec((2, tm, HG_W), lambda i: (0, i, 0))
    full = lambda a: pl.BlockSpec(a.shape, lambda i: (0,) * a.ndim)
    mod = lambda col: pl.BlockSpec((None, 1, D_MODEL), lambda i: (ctx_row, 0, col))
    return pl.pallas_call(
        _proj_ctx_kernel,
        grid=(n // tm,),
        in_specs=[row(D_MODEL), mod(0), mod(1), full(g_pre), full(w_ctx_b), full(lbl)],
        out_specs=[row2, row2, row(HG_W)],
        out_shape=[jax.ShapeDtypeStruct((2, n, HG_W), BF16), jax.ShapeDtypeStruct((2, n, HG_W), F32),
                   jax.ShapeDtypeStruct((n, HG_W), BF16)],
        compiler_params=_params(("parallel",)),
        name="proj_ctx",
    )(c2, mod3, mod3, g_pre, w_ctx_b, lbl)


def _scan_tables():
    C = SCAN_CHUNK
    t = np.arange(C)
    lmats, lvls = [], []
    for d in range(2):
        p = t if d == 0 else C - 1 - t
        pt, ps = p[:, None], p[None, :]
        lmat = (ps <= pt).astype(np.float32)
        lmats.append(np.concatenate([lmat, lmat], axis=1))
        lvl = np.full((C, C), -1, np.int32)
        lvl[(pt // SUB == ps // SUB) & (ps <= pt)] = 0
        half, idx = SUB, 1
        while half < C:
            span = 2 * half
            lvl[(pt // span == ps // span) & ((pt // half) % 2 == 1) & ((ps // half) % 2 == 0)] = idx
            half, idx = span, idx + 1
        lvls.append(lvl)
    return jnp.asarray(np.stack(lmats), dtype=BF16), jnp.asarray(np.stack(lvls))


def _scan_cumsum(d, slot, lmat_ref, lf_ref, b_scr):
    lf = lf_ref[...]
    hi = lf.astype(BF16)
    lo = (lf - hi.astype(F32)).astype(BF16)
    b_scr[slot] = _dot(lmat_ref[d], jnp.concatenate([hi, lo], axis=0))


def _scan_head(d, slot, h, masks, k_ref, v_ref, st_ref, b_scr, q_ref=None, o_ref=None):
    C = SCAN_CHUNK
    sl = slice(h * HG_DK, (h + 1) * HG_DK)
    b = b_scr[slot, :, sl]

    def row(i):
        return b_scr[slot, pl.ds(i, 1), sl]

    b_last = row(C - 1 if d == 0 else 0)
    k = k_ref[:, sl]
    v = v_ref[:, sl]
    st = st_ref[slot, :, sl]
    if q_ref is not None:
        q = q_ref[:, sl]
        e0 = jnp.concatenate([b[m * SUB:(m + 1) * SUB] - row(m * SUB + SUB // 2 - 1 + d)
                              for m in range(C // SUB)], axis=0)
        factors = [(jnp.exp2(e0).astype(BF16), jnp.exp2(-e0).astype(BF16))]
        half = SUB
        while half < C:
            span = 2 * half
            e = jnp.concatenate([b[m * span:(m + 1) * span] - row(m * span + half - 1 + d)
                                 for m in range(C // span)], axis=0)
            w = jnp.exp2(-jnp.abs(e)).astype(BF16)
            factors.append((w, w))
            half = span
        a = jnp.zeros((C, C), F32)
        for (wq, wk), mask in zip(factors, masks):
            a = jnp.where(mask, _dot_nt(q * wq, k * wk), a)
        qhat = q * jnp.exp2(b).astype(BF16)
        o = _dot(a.astype(BF16), v) + _dot_nt(qhat, st.astype(BF16))
        o_ref[:, sl] = o.astype(o_ref.dtype)
    khat = k * jnp.exp2(b_last - b).astype(BF16)
    st_ref[slot, :, sl] = jnp.exp2(b_last) * st + _dot_tn(v, khat)


def _scan_kernel(n_ctx_steps, lmat_ref, lvl_ref, q_f, k_f, lf_f, v_f, q_b, k_b, lf_b, v_b,
                 kc_f, lfc_f, vc_f, kc_b, lfc_b, vc_b, o_f, o_b, st_ref, b_scr):
    s = pl.program_id(1)
    n_seq = q_f.shape[0]

    @pl.when(s == 0)
    def _():
        st_ref[...] = jnp.zeros_like(st_ref)

    def step(refs, readout):
        chains = [(d, 2 * i + d, [r.at[i] if r is not None else None for r in refs[d]])
                  for i in range(n_seq) for d in range(2)]
        for d, slot, (k, lf, v, q, o) in chains:
            _scan_cumsum(d, slot, lmat_ref, lf, b_scr)
        masks = {d: [lvl_ref[d] == i for i in range(N_LEVELS)] for d in range(2)} if readout else {0: None, 1: None}
        for h in range(HG_HEADS):
            for d, slot, (k, lf, v, q, o) in chains:
                _scan_head(d, slot, h, masks[d], k, v, st_ref, b_scr, q, o)

    @pl.when(s < n_ctx_steps)
    def _():
        step([(kc_f, lfc_f, vc_f, None, None), (kc_b, lfc_b, vc_b, None, None)], False)

    @pl.when(s >= n_ctx_steps)
    def _():
        step([(k_f, lf_f, v_f, q_f, o_f), (k_b, lf_b, v_b, q_b, o_b)], True)


def _hgrn_scan(q, k2, lf2, v, kc2, lfc2, vc, batch, seq, ctx_len):
    C = SCAN_CHUNK
    n_lat, n_ctx = seq // C, ctx_len // C
    n_seq = 2 if batch % 2 == 0 else 1
    lmat, lvl = _scan_tables()
    q, v = (a.reshape(batch, seq, HG_W) for a in (q, v))
    k2, lf2 = (a.reshape(2, batch, seq, HG_W) for a in (k2, lf2))
    vc = vc.reshape(batch, ctx_len, HG_W)
    kc2, lfc2 = (a.reshape(2, batch, ctx_len, HG_W) for a in (kc2, lfc2))

    def lat_blk(d):
        def blk(s):
            j = jnp.maximum(s - n_ctx, 0)
            return j if d == 0 else n_lat - 1 - j
        return blk

    def ctx_blk(d):
        def blk(s):
            i = jnp.minimum(s, n_ctx - 1)
            return i if d == 0 else n_ctx - 1 - i
        return blk

    def plain(blk):
        return pl.BlockSpec((n_seq, C, HG_W), lambda b, s: (b, blk(s), 0))

    def specs(blk_of, with_q):
        out = []
        for d in range(2):
            blk = blk_of(d)
            per_dir = pl.BlockSpec((None, n_seq, C, HG_W), lambda b, s, blk=blk, d=d: (d, b, blk(s), 0))
            out += ([plain(blk)] if with_q else []) + [per_dir, per_dir, plain(blk)]
        return out

    full = lambda a: pl.BlockSpec(a.shape, lambda b, s: (0,) * a.ndim)
    o_shape = jax.ShapeDtypeStruct((batch, seq, HG_W), BF16)
    n_chains = 2 * n_seq
    o_f, o_b = pl.pallas_call(
        functools.partial(_scan_kernel, n_ctx),
        grid=(batch // n_seq, n_ctx + n_lat),
        in_specs=[full(lmat), full(lvl)] + specs(lat_blk, True) + specs(ctx_blk, False),
        out_specs=[plain(lat_blk(d)) for d in range(2)],
        out_shape=[o_shape, o_shape],
        scratch_shapes=[pltpu.VMEM((n_chains, HG_DK, HG_W), F32), pltpu.VMEM((n_chains, C, HG_W), F32)],
        compiler_params=_params(("parallel", "arbitrary")),
        name="hgrn_scan",
    )(lmat, lvl, q, k2, lf2, v, q, k2, lf2, v, kc2, lfc2, vc, kc2, lfc2, vc)
    return o_f.reshape(batch * seq, HG_W), o_b.reshape(batch * seq, HG_W)


def _merge_kernel(of_ref, ob_ref, sg_ref, u_ref, vn_ref, sga_ref, sgb_ref, x_ref, gt1_ref, sh2_ref, sc2_ref,
                  gout_ref, ws_ref, bs_ref, wa_ref, wb_ref, wo_ref, gpost_ref, gffn_ref, wr_ref,
                  x1_ref, h2_ref, lg_ref):
    tm = x_ref.shape[0]
    gout = gout_ref[...]
    gw = CM_W // CM_GROUPS
    group = min(MERGE_ROWS, tm)
    for r0 in range(0, tm, group):
        rows = slice(r0, r0 + group)
        o = of_ref[rows, :].astype(F32) + ob_ref[rows, :].astype(F32)
        sg = sg_ref[rows, :].astype(F32)
        a = jnp.concatenate(
            [_rms(o[:, h * HG_DK:(h + 1) * HG_DK], gout) * sg[:, h * HG_DK:(h + 1) * HG_DK]
             for h in range(HG_HEADS)], axis=1).astype(BF16)
        vn = vn_ref[rows, :]
        z = jnp.concatenate(
            [jnp.concatenate([_dot(ws_ref[g], vn[c * CM_CHUNK:(c + 1) * CM_CHUNK, g * gw:(g + 1) * gw])
                              for g in range(CM_GROUPS)], axis=1) + bs_ref[...]
             for c in range(group // CM_CHUNK)], axis=0)
        bm = (u_ref[rows, :].astype(F32) * z).astype(BF16)
        y = (sga_ref[rows, :].astype(F32) * _dot(a, wa_ref[...])
             + sgb_ref[rows, :].astype(F32) * _dot(bm, wb_ref[...]))
        yo = _dot(y.astype(BF16), wo_ref[...])
        x1 = x_ref[rows, :] + gt1_ref[...] * _rms(yo, gpost_ref[...])
        x1_ref[rows, :] = x1
        h2 = (_rms(x1, gffn_ref[...]) * (1.0 + sc2_ref[...]) + sh2_ref[...]).astype(BF16)
        h2_ref[rows, :] = h2
        lg_ref[:, rows] = _dot_nt(wr_ref[...], h2)


def _merge(o_f, o_b, sg, u, vn, sga, sgb, x2, mod3, g_out, ws_b, bs_full, wa_b, wb_b, wo_b, g_post, g_ffn, wr_b,
           seq, tm):
    n = x2.shape[0]
    tpb = seq // tm
    row = lambda w: pl.BlockSpec((tm, w), lambda i: (i, 0))
    full = lambda a: pl.BlockSpec(a.shape, lambda i: (0,) * a.ndim)
    return pl.pallas_call(
        _merge_kernel,
        grid=(n // tm,),
        in_specs=[row(HG_W), row(HG_W), row(HG_W), row(CM_W), row(CM_W),
                  row(D_MODEL), row(D_MODEL), row(D_MODEL), _mod_spec(tpb, 2), _mod_spec(tpb, 3),
                  _mod_spec(tpb, 4), full(g_out), full(ws_b), full(bs_full), full(wa_b), full(wb_b),
                  full(wo_b), full(g_post), full(g_ffn), full(wr_b)],
        out_specs=[row(D_MODEL), row(D_MODEL), pl.BlockSpec((wr_b.shape[0], tm), lambda i: (0, i))],
        out_shape=[jax.ShapeDtypeStruct((n, D_MODEL), F32), jax.ShapeDtypeStruct((n, D_MODEL), BF16),
                   jax.ShapeDtypeStruct((wr_b.shape[0], n), F32)],
        compiler_params=_params(("parallel",)),
        name="merge",
    )(o_f, o_b, sg, u, vn, sga, sgb, x2, mod3, mod3, mod3, g_out, ws_b, bs_full, wa_b, wb_b, wo_b, g_post, g_ffn,
      wr_b)


def _router_kernel(lg_ref, br_ref, w_ref):
    tm = lg_ref.shape[1]
    gsz = N_EXPERTS // N_GROUPS
    scores = _sigmoid(lg_ref[:N_EXPERTS, :])
    sel = scores + jnp.concatenate([br_ref[...]] * (tm // br_ref.shape[1]), axis=1)
    neg = -jnp.inf

    def first_max(x, ids, sentinel, axis):
        m = jnp.max(x, axis=axis, keepdims=True)
        return m, jnp.min(jnp.where(x == m, ids, sentinel), axis=axis, keepdims=True)

    sel3 = sel.reshape(N_GROUPS, gsz, tm)
    j3 = lax.broadcasted_iota(jnp.int32, sel3.shape, 1)
    m1, i1 = first_max(sel3, j3, gsz, 1)
    gscore = m1 + jnp.max(jnp.where(j3 == i1, neg, sel3), axis=1, keepdims=True)
    g3 = lax.broadcasted_iota(jnp.int32, gscore.shape, 0)
    keep = jnp.zeros(gscore.shape, F32)
    for _ in range(TOPK_GROUPS):
        _, gi = first_max(gscore, g3, N_GROUPS, 0)
        keep = jnp.where(g3 == gi, 1.0, keep)
        gscore = jnp.where(g3 == gi, neg, gscore)
    x = jnp.where(keep > 0.0, sel3, neg).reshape(N_EXPERTS, tm)
    e_i = lax.broadcasted_iota(jnp.int32, x.shape, 0)
    w = jnp.zeros(x.shape, F32)
    for _ in range(TOP_K):
        _, ei = first_max(x, e_i, N_EXPERTS, 0)
        w = jnp.where(e_i == ei, scores, w)
        x = jnp.where(e_i == ei, neg, x)
    w = w / jnp.sum(w, axis=0, keepdims=True) * ROUTED_SCALE
    w_ref[...] = jnp.concatenate([w, jnp.zeros_like(w)], axis=0).T


def _router(logits_t, b_router_cols, tm):
    rows, n = logits_t.shape
    return pl.pallas_call(
        _router_kernel,
        grid=(n // tm,),
        in_specs=[pl.BlockSpec((rows, tm), lambda i: (0, i)),
                  pl.BlockSpec(b_router_cols.shape, lambda i: (0, 0))],
        out_specs=pl.BlockSpec((tm, rows), lambda i: (i, 0)),
        out_shape=jax.ShapeDtypeStruct((n, rows), F32),
        compiler_params=_params(("parallel",)),
        name="router",
    )(logits_t, b_router_cols)


MOE_TILE = 256
MOE_UNIT = 16
MOE_BLOCK = 512
MOE_MM_ROWS = 512
GATHER_SLOTS = 3
EXPERT_GRID_STEP = 24
UNITS_PER_BLOCK = MOE_BLOCK // MOE_UNIT
TILE_ROWS = 3072
TILE_UNITS = TILE_ROWS // MOE_UNIT
ROW_CHUNK = 512
N_CHUNKS = TILE_ROWS // ROW_CHUNK
FULL_CHUNKS = MOE_TILE * TOP_K // ROW_CHUNK
CHUNK_UNITS = ROW_CHUNK // MOE_UNIT
SORT_ROWS = 1024
KEY_W = 128
DIGIT_BITS = 6
DIGIT = 1 << DIGIT_BITS


def _swiglu_act(h, w_gu):
    gu = _dot(h, w_gu)
    de = gu.shape[1] // 2
    g = gu[:, :de]
    return g * _sigmoid(g) * gu[:, de:]


def _token_keys(cw, starts_row):
    t = cw.shape[0]
    routed = cw > 0.0
    t_i = lax.broadcasted_iota(jnp.int32, (t, t), 0)
    s_i = lax.broadcasted_iota(jnp.int32, (t, t), 1)
    rank = _dot((s_i < t_i).astype(BF16), routed.astype(BF16))
    pos = (starts_row + rank).astype(jnp.int32)
    lane = lax.broadcasted_iota(jnp.int32, cw.shape, 1)
    hi = jnp.where(routed, jnp.right_shift(pos, DIGIT_BITS), -1)
    lo = jnp.where(routed, jnp.bitwise_and(pos, DIGIT - 1), -1)
    key_hi = jnp.where(lane < N_EXPERTS, hi * DIGIT,
                       jnp.where(lane == N_EXPERTS, -DIGIT, jnp.where(lane == N_EXPERTS + 1, -1, 0)))
    key_lo = jnp.where(lane < N_EXPERTS, lo, 0)
    return jnp.concatenate([key_hi, key_lo], axis=1).astype(F32).astype(BF16)


def _segment_units(counts):
    return jnp.floor((counts + (MOE_UNIT - 1)) * (1.0 / MOE_UNIT))


def _dispatch_kernel(h_ref, cw_ref, digits_ref, xs_ref, cnt_ref):
    cw = cw_ref[...]
    t = cw.shape[0]
    routed = (cw > 0.0).astype(BF16)
    counts = _dot(jnp.ones((8, t), BF16), routed)
    cnt_ref[...] = counts.astype(jnp.int32)
    units = _segment_units(counts)
    e_i = lax.broadcasted_iota(jnp.int32, (KEY_W, KEY_W), 0)
    f_i = lax.broadcasted_iota(jnp.int32, (KEY_W, KEY_W), 1)
    starts = _dot(units.astype(BF16), (e_i < f_i).astype(BF16)) * MOE_UNIT
    ends = starts + units * MOE_UNIT
    keys_t = _token_keys(cw, starts[:1]).astype(F32).T.astype(BF16)
    h = h_ref[...]
    used_rows = jnp.max(ends)

    def sort_rows(r0, n):
        rows = slice(r0, r0 + n)
        lane = lax.broadcasted_iota(jnp.int32, (n, KEY_W), 1)
        r = (lax.broadcasted_iota(jnp.int32, (n, KEY_W), 0) + r0).astype(F32)
        in_seg = (r >= starts[:1]) & (r < ends[:1])
        rmap = jnp.where(lane < N_EXPERTS, in_seg.astype(F32), digits_ref[rows, :].astype(F32)).astype(BF16)
        hit = _dot(jnp.concatenate([rmap, rmap], axis=1), keys_t) == 0.0
        xs_ref[rows, :] = _dot(hit.astype(BF16), h).astype(xs_ref.dtype)

    for r0 in range(0, FULL_CHUNKS * ROW_CHUNK, SORT_ROWS):
        sort_rows(r0, SORT_ROWS)
    for c in range(FULL_CHUNKS, N_CHUNKS):
        pl.when(used_rows > c * ROW_CHUNK)(functools.partial(sort_rows, c * ROW_CHUNK, ROW_CHUNK))

        @pl.when(used_rows <= c * ROW_CHUNK)
        def _(c=c):
            xs_ref[c * ROW_CHUNK:(c + 1) * ROW_CHUNK, :] = jnp.zeros((ROW_CHUNK, D_MODEL), xs_ref.dtype)


def _dispatch(h2, cw, digits):
    n = h2.shape[0]
    n_tiles = n // MOE_TILE
    return pl.pallas_call(
        _dispatch_kernel,
        grid=(n_tiles,),
        in_specs=[pl.BlockSpec((MOE_TILE, D_MODEL), lambda i: (i, 0)),
                  pl.BlockSpec((MOE_TILE, KEY_W), lambda i: (i, 0)),
                  pl.BlockSpec(digits.shape, lambda i: (0, 0))],
        out_specs=[pl.BlockSpec((TILE_ROWS, D_MODEL), lambda i: (i, 0)),
                   pl.BlockSpec((8, KEY_W), lambda i: (i, 0))],
        out_shape=[jax.ShapeDtypeStruct((n_tiles * TILE_ROWS, D_MODEL), BF16),
                   jax.ShapeDtypeStruct((n_tiles * 8, KEY_W), jnp.int32)],
        compiler_params=_params(("parallel",)),
        name="moe_dispatch",
    )(h2, cw, digits)


def _unit_copy(src_hbm, unit, dst, slot, pos, sem):
    return pltpu.make_async_copy(
        src_hbm.at[pl.ds(pl.multiple_of(unit * MOE_UNIT, MOE_UNIT), MOE_UNIT)],
        dst.at[slot, pl.ds(pos * MOE_UNIT, MOE_UNIT)], sem.at[slot])


def _experts_kernel(be_ref, src_ref, nb_ref, wplan_ref, xs_hbm, wgu_hbm, wdn_hbm, ys_ref, xbuf, sem,
                    wgu_f, wdn_f, wsem, wgu_b, wdn_b):
    j = pl.program_id(0)
    nb = nb_ref[0]
    n_steps = pl.num_programs(0)

    def copies(blk, slot):
        return [_unit_copy(xs_hbm, src_ref[blk * UNITS_PER_BLOCK + u], xbuf, slot, u, sem)
                for u in range(UNITS_PER_BLOCK)]

    def fetch(blk, slot):
        for cp in copies(blk, slot):
            cp.start()

    def weight_copies(expert, slot):
        return [pltpu.make_async_copy(wgu_hbm.at[expert], wgu_f.at[slot], wsem.at[slot]),
                pltpu.make_async_copy(wdn_hbm.at[expert], wdn_f.at[slot], wsem.at[slot])]

    ahead = GATHER_SLOTS - 1

    @pl.when(j == 0)
    def _():
        for a in range(ahead):
            fetch(jnp.minimum(a, nb - 1), a)
        for cp in weight_copies(be_ref[0], 0):
            cp.start()

    @pl.when((j < nb) & (wplan_ref[j] == 1))
    def _():
        slot = wplan_ref[n_steps + j]
        for cp in weight_copies(be_ref[j], slot):
            cp.wait()
        nxt = wplan_ref[2 * n_steps + j]

        @pl.when(nxt >= 0)
        def _():
            for cp in weight_copies(nxt, 1 - slot):
                cp.start()

        wgu_b[...] = wgu_f[slot].astype(BF16)
        wdn_b[...] = wdn_f[slot].astype(BF16)

    @pl.when(j < nb)
    def _():
        slot = j % GATHER_SLOTS
        for cp in copies(j, slot):
            cp.wait()
        fetch(jnp.minimum(j + ahead, nb - 1), (j + ahead) % GATHER_SLOTS)
        for g in range(MOE_BLOCK // MOE_MM_ROWS):
            rows = pl.ds(g * MOE_MM_ROWS, MOE_MM_ROWS)
            act = _swiglu_act(xbuf[slot, rows, :], wgu_b[...])
            ys_ref[rows, :] = _dot(act.astype(BF16), wdn_b[...]).astype(ys_ref.dtype)

    @pl.when(j == nb - 1)
    def _():
        for a in range(1, GATHER_SLOTS):
            for cp in copies(j, (j + a) % GATHER_SLOTS):
                cp.wait()

    @pl.when(j >= nb)
    def _():
        ys_ref[...] = jnp.zeros_like(ys_ref)


def _weight_plan(block_expert, n_blocks_used):
    nb = block_expert.shape[0]
    jb = jnp.arange(nb, dtype=jnp.int32)
    used = jb < n_blocks_used[0]
    first = used & ((jb == 0) | (block_expert != jnp.roll(block_expert, 1)))
    slot = (jnp.cumsum(first.astype(jnp.int32)) - 1) % 2
    first_at = jnp.where(first, jb, nb)
    nxt_first = jnp.min(jnp.where(first_at[None, :] > jb[:, None], first_at[None, :], nb), axis=1)
    nxt_expert = jnp.sum(jnp.where(jb[None, :] == nxt_first[:, None], block_expert[None, :], 0), axis=1)
    nxt = jnp.where(nxt_first < nb, nxt_expert, -1)
    return jnp.concatenate([first.astype(jnp.int32), slot.astype(jnp.int32), nxt.astype(jnp.int32)])


def _experts(xs, block_expert, src_units, n_blocks_used, w_gu, w_dn):
    nb_max = block_expert.shape[0]
    any_spec = pl.BlockSpec(memory_space=pl.ANY)
    grid_spec = pltpu.PrefetchScalarGridSpec(
        num_scalar_prefetch=4,
        grid=(nb_max,),
        in_specs=[any_spec, any_spec, any_spec],
        out_specs=pl.BlockSpec((MOE_BLOCK, D_MODEL), lambda j, be, src, nb, wplan: (j, 0)),
        scratch_shapes=[pltpu.VMEM((GATHER_SLOTS, MOE_BLOCK, D_MODEL), BF16),
                        pltpu.SemaphoreType.DMA((GATHER_SLOTS,)),
                        pltpu.VMEM((2, D_MODEL, 2 * D_EXPERT), F32), pltpu.VMEM((2, D_EXPERT, D_MODEL), F32),
                        pltpu.SemaphoreType.DMA((2,)),
                        pltpu.VMEM((D_MODEL, 2 * D_EXPERT), BF16), pltpu.VMEM((D_EXPERT, D_MODEL), BF16)],
    )
    return pl.pallas_call(
        _experts_kernel,
        grid_spec=grid_spec,
        out_shape=jax.ShapeDtypeStruct((nb_max * MOE_BLOCK, D_MODEL), BF16),
        compiler_params=_params(("arbitrary",)),
        name="moe_experts",
    )(block_expert, src_units, n_blocks_used, _weight_plan(block_expert, n_blocks_used), xs, w_gu, w_dn)


def _combine_kernel(src_ref, used_ref, ys_hbm, cw_ref, h_ref, x1_ref, gt2_ref, gpost_ref, digits_t_ref,
                    wsgu_ref, wsdn_ref, o_ref, ybuf, sem, acc_ref):
    i = pl.program_id(0)

    def copies(tile, slot, c):
        return [_unit_copy(ys_hbm, src_ref[tile * TILE_UNITS + u], ybuf, slot, u, sem)
                for u in range(c * CHUNK_UNITS, (c + 1) * CHUNK_UNITS)]

    def chunk_used(tile, c):
        return used_ref[tile] > c * CHUNK_UNITS

    def for_used_chunks(tile, fn):
        for c in range(N_CHUNKS):
            if c < FULL_CHUNKS:
                fn(c)
            else:
                pl.when(chunk_used(tile, c))(functools.partial(fn, c))

    def fetch(tile, slot):
        def start(c):
            for cp in copies(tile, slot, c):
                cp.start()

        for_used_chunks(tile, start)

    def wait_all(tile, slot):
        def wait(c):
            for cp in copies(tile, slot, c):
                cp.wait()

        for_used_chunks(tile, wait)

    @pl.when(i == 0)
    def _():
        fetch(0, 0)

    @pl.when(i + 1 < pl.num_programs(0))
    def _():
        fetch(i + 1, (i + 1) % 2)

    cw = cw_ref[...]
    t = cw.shape[0]
    routed = (cw > 0.0).astype(BF16)
    e_i = lax.broadcasted_iota(jnp.int32, (KEY_W, KEY_W), 0)
    f_i = lax.broadcasted_iota(jnp.int32, (KEY_W, KEY_W), 1)
    units = _segment_units(_dot_tn(routed, jnp.ones((t, KEY_W), BF16)))
    starts = _dot((f_i < e_i).astype(BF16), units.astype(BF16)) * MOE_UNIT
    ends = starts + units * MOE_UNIT
    units_row = _segment_units(_dot(jnp.ones((8, t), BF16), routed))
    starts_row = _dot(units_row.astype(BF16), (e_i < f_i).astype(BF16)) * MOE_UNIT
    keys = _token_keys(cw, starts_row[:1])
    wb = cw.astype(BF16)

    f = _dot(_swiglu_act(h_ref[...], wsgu_ref[...]).astype(BF16), wsdn_ref[...])
    slot = i % 2
    reps = ROW_CHUNK // KEY_W
    starts_c = jnp.concatenate([starts] * reps, axis=1)
    ends_c = jnp.concatenate([ends] * reps, axis=1)
    sub = lax.broadcasted_iota(jnp.int32, (KEY_W, ROW_CHUNK), 0)

    wait_all(i, slot)

    def chunk_sum(c):
        rows = slice(c * ROW_CHUNK, (c + 1) * ROW_CHUNK)
        r = (lax.broadcasted_iota(jnp.int32, (KEY_W, ROW_CHUNK), 1) + c * ROW_CHUNK).astype(F32)
        in_seg = (r >= starts_c) & (r < ends_c)
        rmap_t = jnp.where(sub < N_EXPERTS, in_seg.astype(F32), digits_t_ref[:, rows].astype(F32)).astype(BF16)
        hit = _dot(keys, jnp.concatenate([rmap_t, rmap_t], axis=0)) == 0.0
        w = _dot(wb, rmap_t)
        return _dot(jnp.where(hit, w, 0.0).astype(BF16), ybuf[slot, rows, :])

    for c in range(FULL_CHUNKS):
        f = f + chunk_sum(c)
    acc_ref[...] = f
    for c in range(FULL_CHUNKS, N_CHUNKS):
        @pl.when(chunk_used(i, c))
        def _(c=c):
            acc_ref[...] += chunk_sum(c)
    o_ref[...] = x1_ref[...] + gt2_ref[...] * _rms(acc_ref[...], gpost_ref[...])


def _combine(ys, src_units, used_units, cw, h2, x1, mod3, g_post, digits_t, wsgu_b, wsdn_b, seq):
    n = h2.shape[0]
    tpb = seq // MOE_TILE
    row = lambda w: pl.BlockSpec((MOE_TILE, w), lambda i, src, used: (i, 0))
    full = lambda a: pl.BlockSpec(a.shape, lambda i, src, used: (0,) * a.ndim)
    grid_spec = pltpu.PrefetchScalarGridSpec(
        num_scalar_prefetch=2,
        grid=(n // MOE_TILE,),
        in_specs=[pl.BlockSpec(memory_space=pl.ANY), row(KEY_W), row(D_MODEL), row(D_MODEL),
                  pl.BlockSpec((None, 1, D_MODEL), lambda i, src, used: (i // tpb, 0, 5)), full(g_post),
                  full(digits_t), full(wsgu_b), full(wsdn_b)],
        out_specs=row(D_MODEL),
        scratch_shapes=[pltpu.VMEM((2, TILE_ROWS, D_MODEL), BF16), pltpu.SemaphoreType.DMA((2,)),
                        pltpu.VMEM((MOE_TILE, D_MODEL), F32)],
    )
    return pl.pallas_call(
        _combine_kernel,
        grid_spec=grid_spec,
        out_shape=jax.ShapeDtypeStruct((n, D_MODEL), F32),
        compiler_params=_params(("arbitrary",)),
        name="moe_combine",
    )(src_units, used_units, ys, cw, h2, x1, mod3, g_post, digits_t, wsgu_b, wsdn_b)


def _row_digits():
    r = np.arange(TILE_ROWS)
    d = np.zeros((TILE_ROWS, KEY_W), np.float32)
    d[:, N_EXPERTS] = r // DIGIT
    d[:, N_EXPERTS + 1] = r % DIGIT
    return jnp.asarray(d, dtype=BF16)


def _moe_plan(counts, nb_max):
    n_tiles = counts.shape[0]
    s = (counts + (MOE_UNIT - 1)) // MOE_UNIT
    local = jnp.cumsum(s, axis=1) - s
    cs = jnp.cumsum(s, axis=0)
    per_expert = cs[-1]
    padded = (per_expert + UNITS_PER_BLOCK - 1) // UNITS_PER_BLOCK * UNITS_PER_BLOCK
    g_end = jnp.cumsum(padded)
    g_start = g_end - padded
    seg_start = g_start[None, :] + cs - s
    n_blocks_used = (g_end[-1] // UNITS_PER_BLOCK).astype(jnp.int32).reshape(1)
    jb = jnp.arange(nb_max, dtype=jnp.int32)
    one_e = ((jb[:, None] >= (g_start // UNITS_PER_BLOCK)[None, :])
             & (jb[:, None] < (g_end // UNITS_PER_BLOCK)[None, :])).astype(jnp.int32)
    pick_e = lambda table: jnp.sum(one_e[:, :, None] * table.T[None, :, :], axis=1)
    block_expert = jnp.where(jb < n_blocks_used[0], jnp.sum(one_e * jnp.arange(N_EXPERTS, dtype=jnp.int32), axis=1),
                             N_EXPERTS - 1).astype(jnp.int32)
    cs_b, s_b, local_b = pick_e(cs), pick_e(s), pick_e(local)
    q = (jb * UNITS_PER_BLOCK - jnp.sum(one_e * g_start[None, :], axis=1))[:, None] \
        + jnp.arange(UNITS_PER_BLOCK, dtype=jnp.int32)[None, :]
    tile = jnp.minimum(jnp.sum(cs_b[:, None, :] <= q[:, :, None], axis=2), n_tiles - 1)
    one_t = (tile[:, :, None] == jnp.arange(n_tiles, dtype=jnp.int32)).astype(jnp.int32)
    src = tile * TILE_UNITS + q + jnp.sum(one_t * (local_b - cs_b + s_b)[:, None, :], axis=2)
    valid = q < jnp.sum(one_e * per_expert[None, :], axis=1)[:, None]
    src_units = jnp.where(valid, src, 0).astype(jnp.int32).reshape(-1)
    u = jnp.arange(TILE_UNITS, dtype=jnp.int32)
    seg_end = local + s
    eu = jnp.minimum(jnp.sum(seg_end[:, None, :] <= u[None, :, None], axis=2), N_EXPERTS - 1)
    one_u = (eu[:, :, None] == jnp.arange(N_EXPERTS, dtype=jnp.int32)).astype(jnp.int32)
    back = u[None, :] + jnp.sum(one_u * (seg_start - local)[:, None, :], axis=2)
    back_units = jnp.where(u[None, :] < seg_end[:, -1:], back, 0).astype(jnp.int32).reshape(-1)
    return block_expert, src_units, n_blocks_used, back_units, seg_end[:, -1].astype(jnp.int32)


def _tile(n, pref):
    t = pref
    while n % t:
        t //= 2
    return t


def kernel(x, c, ctx, c_ctx, w_ada, b_ada, g_pre_mix, g_post_mix, g_pre_ffn, g_post_ffn, w_in, lb_logits, g_hgrn_out, cm_ln_g, cm_ln_b, w_spatial, b_spatial, w_branch_a, w_branch_b, w_out, w_router, b_router, w_expert_gu, w_expert_down, w_shared_gu, w_shared_down):
    B, T, D = x.shape
    L = ctx.shape[1]
    assert D == D_MODEL and w_ada.shape[0] == 1 and T % SCAN_CHUNK == 0 and L % SCAN_CHUNK == 0
    l = 0
    row = lambda a: a[l].reshape(1, -1)

    n_rows = -(-(B + 1) // 16) * 16
    cs = jnp.zeros((n_rows, D), F32).at[:B].set(c).at[B].set(c_ctx)
    mod3 = _ada_mod(cs, w_ada[l], row(b_ada)).reshape(n_rows, 1, 6 * D)

    w_in_b = w_in[l].astype(BF16)
    lbl = lb_logits[:, l:l + 2].reshape(4, HG_W)
    x2 = x.reshape(B * T, D)
    q, k2, lf2, v, sg, u, vn, sga, sgb = _proj_lat(
        x2, mod3, row(g_pre_mix), w_in_b, lbl, row(cm_ln_g), row(cm_ln_b), T, _tile(T, 256))
    kc2, lfc2, vc = _proj_ctx(ctx.reshape(B * L, D), mod3, B, row(g_pre_mix), w_in_b[:, HG_W:4 * HG_W], lbl,
                              _tile(B * L, 256))

    o_f, o_b = _hgrn_scan(q, k2, lf2, v, kc2, lfc2, vc, B, T, L)

    bs_full = jnp.repeat(b_spatial[l], CM_W // CM_GROUPS, axis=1)
    x1, h2, logits = _merge(
        o_f, o_b, sg, u, vn, sga, sgb, x2, mod3, row(g_hgrn_out), w_spatial[l].astype(BF16), bs_full,
        w_branch_a[l].astype(BF16), w_branch_b[l].astype(BF16), w_out[l].astype(BF16), row(g_post_mix),
        row(g_pre_ffn), jnp.pad(w_router[l].T, ((0, KEY_W - N_EXPERTS), (0, 0))).astype(BF16), T, _tile(T, 1024))

    cw = _router(logits, jnp.broadcast_to(b_router[l][:, None], (N_EXPERTS, 128)), _tile(B * T, 512))

    n_tok = B * T
    n_tiles = n_tok // MOE_TILE
    digits = _row_digits()
    xs, cnt = _dispatch(h2, cw, digits)
    counts = cnt.reshape(n_tiles, 8, KEY_W)[:, 0, :N_EXPERTS]
    max_units = (n_tok * TOP_K + n_tiles * N_EXPERTS * (MOE_UNIT - 1)) // MOE_UNIT + N_EXPERTS * (UNITS_PER_BLOCK - 1)
    nb_max = -(-max_units // UNITS_PER_BLOCK)
    block_expert, src_units, n_blocks_used, back_units, tile_units = _moe_plan(counts, nb_max)
    wsgu_b, wsdn_b = w_shared_gu[l].astype(BF16), w_shared_down[l].astype(BF16)

    nb_min = -(-(n_tok * TOP_K) // MOE_BLOCK)
    grids = sorted({min(nb_max, g) for g in range(nb_min + EXPERT_GRID_STEP, nb_max + EXPERT_GRID_STEP,
                                                  EXPERT_GRID_STEP)})

    def experts_and_combine(nb_grid):
        def run():
            ys = _experts(xs, block_expert[:nb_grid], src_units[:nb_grid * UNITS_PER_BLOCK], n_blocks_used,
                          w_expert_gu[l], w_expert_down[l])
            return _combine(ys, back_units, tile_units, cw, h2, x1, mod3, row(g_post_ffn), digits.T,
                            wsgu_b, wsdn_b, T)
        return run

    which = jnp.sum(n_blocks_used[0] > jnp.asarray(grids[:-1], jnp.int32))
    out = lax.switch(which, [experts_and_combine(g) for g in grids])
    return out.reshape(B, T, D)
```

```python
import functools

import numpy as np
import jax
import jax.numpy as jnp
from jax import lax
from jax.experimental import pallas as pl
from jax.experimental.pallas import tpu as pltpu

F32 = jnp.float32
BF16 = jnp.bfloat16

D_MODEL = 1024
EPS = 1e-6
HG_HEADS = 4
HG_DK = 128
HG_W = HG_HEADS * HG_DK
CM_W = 512
CM_CHUNK = 128
CM_GROUPS = 4
D_IN = 5 * HG_W + 2 * CM_W + 2 * D_MODEL
N_EXPERTS = 64
TOP_K = 8
N_GROUPS = 8
GROUP_BITS = 3
TOPK_GROUPS = 4
D_EXPERT = 256
ROUTED_SCALE = 2.5
SCAN_CHUNK = 128
SUB = 16
N_LEVELS = 4
MERGE_ROWS = 512
VMEM_LIMIT = 56 * 1024 * 1024


def _params(sem):
    return pltpu.CompilerParams(dimension_semantics=sem, vmem_limit_bytes=VMEM_LIMIT)


def _dot(a, b):
    return jnp.dot(a, b, preferred_element_type=F32)


def _dot_nt(a, b):
    return lax.dot_general(a, b, (((1,), (1,)), ((), ())), preferred_element_type=F32)


def _dot_tn(a, b):
    return lax.dot_general(a, b, (((0,), (0,)), ((), ())), preferred_element_type=F32)


def _sigmoid(x):
    return 0.5 * jnp.tanh(0.5 * x) + 0.5


def _rms(x, g):
    return x * lax.rsqrt(jnp.mean(x * x, axis=-1, keepdims=True) + EPS) * g


def _ada_kernel(c_ref, w_ref, b_ref, o_ref):
    c = c_ref[...]
    s = c * _sigmoid(c)
    o_ref[...] = _dot(s.astype(BF16), w_ref[...].astype(BF16)) + b_ref[...]


def _ada_mod(cs, w_ada, b_ada):
    rows = cs.shape[0]
    n_out = w_ada.shape[1]
    return pl.pallas_call(
        _ada_kernel,
        grid=(n_out // D_MODEL,),
        in_specs=[
            pl.BlockSpec((rows, D_MODEL), lambda j: (0, 0)),
            pl.BlockSpec((D_MODEL, D_MODEL), lambda j: (0, j)),
            pl.BlockSpec((1, D_MODEL), lambda j: (0, j)),
        ],
        out_specs=pl.BlockSpec((rows, D_MODEL), lambda j: (0, j)),
        out_shape=jax.ShapeDtypeStruct((rows, n_out), F32),
        compiler_params=_params(("parallel",)),
        name="ada_mod",
    )(cs, w_ada, b_ada)


def _lower_bounds(lbl):
    out = []
    for d in range(2):
        l0, l1 = lbl[2 * d:2 * d + 1], lbl[2 * d + 1:2 * d + 2]
        m = jnp.maximum(l0, l1)
        e0, e1 = jnp.exp(l0 - m), jnp.exp(l1 - m)
        out.append(e0 / (e0 + e1))
    return out


def _prenorm(x_ref, sh_ref, sc_ref, g_ref):
    return (_rms(x_ref[...], g_ref[...]) * (1.0 + sc_ref[...]) + sh_ref[...]).astype(BF16)


def _gates(z, lb, k_ref, lf_ref, d):
    half_t = 0.5 * jnp.tanh(0.5 * z)
    k_ref[d] = ((1.0 - lb) * (0.5 - half_t)).astype(k_ref.dtype)
    lf_ref[d] = jnp.log2(lb + (1.0 - lb) * (0.5 + half_t))


def _proj_lat_kernel(x_ref, sh_ref, sc_ref, g_ref, w_ref, lbl_ref, lng_ref, lnb_ref,
                     q_ref, k_ref, lf_ref, v_ref, sg_ref, u_ref, vn_ref, sga_ref, sgb_ref):
    hb = _prenorm(x_ref, sh_ref, sc_ref, g_ref)
    lbs = _lower_bounds(lbl_ref[...])

    def mm(lo, width):
        return _dot(hb, w_ref[:, lo:lo + width])

    z = mm(0, HG_W)
    q_ref[...] = (z * _sigmoid(z)).astype(q_ref.dtype)
    for d in range(2):
        _gates(mm((1 + d) * HG_W, HG_W), lbs[d], k_ref, lf_ref, d)
    z = mm(4 * HG_W, HG_W)
    sg_ref[...] = (z * _sigmoid(z)).astype(sg_ref.dtype)
    u_ref[...] = jax.nn.gelu(mm(5 * HG_W, CM_W)).astype(u_ref.dtype)
    vv = jax.nn.gelu(mm(5 * HG_W + CM_W, CM_W))
    vc = vv - jnp.mean(vv, axis=-1, keepdims=True)
    vn = vc * lax.rsqrt(jnp.mean(vc * vc, axis=-1, keepdims=True) + EPS)
    vn_ref[...] = (vn * lng_ref[...] + lnb_ref[...]).astype(vn_ref.dtype)
    base = 5 * HG_W + 2 * CM_W
    sga_ref[...] = _sigmoid(mm(base, D_MODEL)).astype(sga_ref.dtype)
    sgb_ref[...] = _sigmoid(mm(base + D_MODEL, D_MODEL)).astype(sgb_ref.dtype)
    v_ref[...] = mm(3 * HG_W, HG_W).astype(v_ref.dtype)


def _mod_spec(rows_per_batch_tiles, col):
    return pl.BlockSpec((None, 1, D_MODEL), lambda i: (i // rows_per_batch_tiles, 0, col))


def _proj_lat(x2, mod3, g_pre, w_in_b, lbl, ln_g, ln_b, seq, tm):
    n = x2.shape[0]
    tpb = seq // tm
    row = lambda w: pl.BlockSpec((tm, w), lambda i: (i, 0))
    row2 = pl.BlockSpec((2, tm, HG_W), lambda i: (0, i, 0))
    full = lambda a: pl.BlockSpec(a.shape, lambda i: (0,) * a.ndim)
    outs = [
        (row(HG_W), jax.ShapeDtypeStruct((n, HG_W), BF16)),
        (row2, jax.ShapeDtypeStruct((2, n, HG_W), BF16)),
        (row2, jax.ShapeDtypeStruct((2, n, HG_W), F32)),
        (row(HG_W), jax.ShapeDtypeStruct((n, HG_W), BF16)),
        (row(HG_W), jax.ShapeDtypeStruct((n, HG_W), BF16)),
        (row(CM_W), jax.ShapeDtypeStruct((n, CM_W), BF16)),
        (row(CM_W), jax.ShapeDtypeStruct((n, CM_W), BF16)),
        (row(D_MODEL), jax.ShapeDtypeStruct((n, D_MODEL), BF16)),
        (row(D_MODEL), jax.ShapeDtypeStruct((n, D_MODEL), BF16)),
    ]
    return pl.pallas_call(
        _proj_lat_kernel,
        grid=(n // tm,),
        in_specs=[row(D_MODEL), _mod_spec(tpb, 0), _mod_spec(tpb, 1), full(g_pre), full(w_in_b),
                  full(lbl), full(ln_g), full(ln_b)],
        out_specs=[o[0] for o in outs],
        out_shape=[o[1] for o in outs],
        compiler_params=_params(("parallel",)),
        name="proj_lat",
    )(x2, mod3, mod3, g_pre, w_in_b, lbl, ln_g, ln_b)


def _proj_ctx_kernel(x_ref, sh_ref, sc_ref, g_ref, w_ref, lbl_ref, k_ref, lf_ref, v_ref):
    hb = _prenorm(x_ref, sh_ref, sc_ref, g_ref)
    lbs = _lower_bounds(lbl_ref[...])
    for d in range(2):
        _gates(_dot(hb, w_ref[:, d * HG_W:(d + 1) * HG_W]), lbs[d], k_ref, lf_ref, d)
    v_ref[...] = _dot(hb, w_ref[:, 2 * HG_W:3 * HG_W]).astype(v_ref.dtype)


def _proj_ctx(c2, mod3, ctx_row, g_pre, w_ctx_b, lbl, tm):
    n = c2.shape[0]
    row = lambda w: pl.BlockSpec((tm, w), lambda i: (i, 0))
    row2 = pl.BlockSpec((2, tm, HG_W), lambda i: (0, i, 0))
    full = lambda a: pl.BlockSpec(a.shape, lambda i: (0,) * a.ndim)
    mod = lambda col: pl.BlockSpec((None, 1, D_MODEL), lambda i: (ctx_row, 0, col))
    return pl.pallas_call(
        _proj_ctx_kernel,
        grid=(n // tm,),
        in_specs=[row(D_MODEL), mod(0), mod(1), full(g_pre), full(w_ctx_b), full(lbl)],
        out_specs=[row2, row2, row(HG_W)],
        out_shape=[jax.ShapeDtypeStruct((2, n, HG_W), BF16), jax.ShapeDtypeStruct((2, n, HG_W), F32),
                   jax.ShapeDtypeStruct((n, HG_W), BF16)],
        compiler_params=_params(("parallel",)),
        name="proj_ctx",
    )(c2, mod3, mod3, g_pre, w_ctx_b, lbl)


def _scan_tables():
    C = SCAN_CHUNK
    t = np.arange(C)
    lmats, lvls = [], []
    for d in range(2):
        p = t if d == 0 else C - 1 - t
        pt, ps = p[:, None], p[None, :]
        lmat = (ps <= pt).astype(np.float32)
        lmats.append(np.concatenate([lmat, lmat], axis=1))
        lvl = np.full((C, C), -1, np.int32)
        lvl[(pt // SUB == ps // SUB) & (ps <= pt)] = 0
        half, idx = SUB, 1
        while half < C:
            span = 2 * half
            lvl[(pt // span == ps // span) & ((pt // half) % 2 == 1) & ((ps // half) % 2 == 0)] = idx
            half, idx = span, idx + 1
        lvls.append(lvl)
    return jnp.asarray(np.stack(lmats), dtype=BF16), jnp.asarray(np.stack(lvls))


def _scan_cumsum(d, slot, lmat_ref, lf_ref, b_scr):
    lf = lf_ref[...]
    hi = lf.astype(BF16)
    lo = (lf - hi.astype(F32)).astype(BF16)
    b_scr[slot] = _dot(lmat_ref[d], jnp.concatenate([hi, lo], axis=0))


def _scan_head(d, slot, h, masks, k_ref, v_ref, st_ref, b_scr, q_ref=None, o_ref=None):
    C = SCAN_CHUNK
    sl = slice(h * HG_DK, (h + 1) * HG_DK)
    b = b_scr[slot, :, sl]

    def row(i):
        return b_scr[slot, pl.ds(i, 1), sl]

    b_last = row(C - 1 if d == 0 else 0)
    k = k_ref[:, sl]
    v = v_ref[:, sl]
    st = st_ref[slot, :, sl]
    if q_ref is not None:
        q = q_ref[:, sl]
        e0 = jnp.concatenate([b[m * SUB:(m + 1) * SUB] - row(m * SUB + SUB // 2 - 1 + d)
                              for m in range(C // SUB)], axis=0)
        factors = [(jnp.exp2(e0).astype(BF16), jnp.exp2(-e0).astype(BF16))]
        half = SUB
        while half < C:
            span = 2 * half
            e = jnp.concatenate([b[m * span:(m + 1) * span] - row(m * span + half - 1 + d)
                                 for m in range(C // span)], axis=0)
            w = jnp.exp2(-jnp.abs(e)).astype(BF16)
            factors.append((w, w))
            half = span
        a = jnp.zeros((C, C), F32)
        for (wq, wk), mask in zip(factors, masks):
            a = jnp.where(mask, _dot_nt(q * wq, k * wk), a)
        qhat = q * jnp.exp2(b).astype(BF16)
        o = _dot(a.astype(BF16), v) + _dot_nt(qhat, st.astype(BF16))
        o_ref[:, sl] = o.astype(o_ref.dtype)
    khat = k * jnp.exp2(b_last - b).astype(BF16)
    st_ref[slot, :, sl] = jnp.exp2(b_last) * st + _dot_tn(v, khat)


def _scan_kernel(n_ctx_steps, lmat_ref, lvl_ref, q_f, k_f, lf_f, v_f, q_b, k_b, lf_b, v_b,
                 kc_f, lfc_f, vc_f, kc_b, lfc_b, vc_b, o_f, o_b, st_ref, b_scr):
    s = pl.program_id(1)
    n_seq = q_f.shape[0]

    @pl.when(s == 0)
    def _():
        st_ref[...] = jnp.zeros_like(st_ref)

    def step(refs, readout):
        chains = [(d, 2 * i + d, [r.at[i] if r is not None else None for r in refs[d]])
                  for i in range(n_seq) for d in range(2)]
        for d, slot, (k, lf, v, q, o) in chains:
            _scan_cumsum(d, slot, lmat_ref, lf, b_scr)
        masks = {d: [lvl_ref[d] == i for i in range(N_LEVELS)] for d in range(2)} if readout else {0: None, 1: None}
        for h in range(HG_HEADS):
            for d, slot, (k, lf, v, q, o) in chains:
                _scan_head(d, slot, h, masks[d], k, v, st_ref, b_scr, q, o)

    @pl.when(s < n_ctx_steps)
    def _():
        step([(kc_f, lfc_f, vc_f, None, None), (kc_b, lfc_b, vc_b, None, None)], False)

    @pl.when(s >= n_ctx_steps)
    def _():
        step([(k_f, lf_f, v_f, q_f, o_f), (k_b, lf_b, v_b, q_b, o_b)], True)


def _hgrn_scan(q, k2, lf2, v, kc2, lfc2, vc, batch, seq, ctx_len):
    C = SCAN_CHUNK
    n_lat, n_ctx = seq // C, ctx_len // C
    n_seq = 2 if batch % 2 == 0 else 1
    lmat, lvl = _scan_tables()
    q, v = (a.reshape(batch, seq, HG_W) for a in (q, v))
    k2, lf2 = (a.reshape(2, batch, seq, HG_W) for a in (k2, lf2))
    vc = vc.reshape(batch, ctx_len, HG_W)
    kc2, lfc2 = (a.reshape(2, batch, ctx_len, HG_W) for a in (kc2, lfc2))

    def lat_blk(d):
        def blk(s):
            j = jnp.maximum(s - n_ctx, 0)
            return j if d == 0 else n_lat - 1 - j
        return blk

    def ctx_blk(d):
        def blk(s):
            i = jnp.minimum(s, n_ctx - 1)
            return i if d == 0 else n_ctx - 1 - i
        return blk

    def plain(blk):
        return pl.BlockSpec((n_seq, C, HG_W), lambda b, s: (b, blk(s), 0))

    def specs(blk_of, with_q):
        out = []
        for d in range(2):
            blk = blk_of(d)
            per_dir = pl.BlockSpec((None, n_seq, C, HG_W), lambda b, s, blk=blk, d=d: (d, b, blk(s), 0))
            out += ([plain(blk)] if with_q else []) + [per_dir, per_dir, plain(blk)]
        return out

    full = lambda a: pl.BlockSpec(a.shape, lambda b, s: (0,) * a.ndim)
    o_shape = jax.ShapeDtypeStruct((batch, seq, HG_W), BF16)
    n_chains = 2 * n_seq
    o_f, o_b = pl.pallas_call(
        functools.partial(_scan_kernel, n_ctx),
        grid=(batch // n_seq, n_ctx + n_lat),
        in_specs=[full(lmat), full(lvl)] + specs(lat_blk, True) + specs(ctx_blk, False),
        out_specs=[plain(lat_blk(d)) for d in range(2)],
        out_shape=[o_shape, o_shape],
        scratch_shapes=[pltpu.VMEM((n_chains, HG_DK, HG_W), F32), pltpu.VMEM((n_chains, C, HG_W), F32)],
        compiler_params=_params(("parallel", "arbitrary")),
        name="hgrn_scan",
    )(lmat, lvl, q, k2, lf2, v, q, k2, lf2, v, kc2, lfc2, vc, kc2, lfc2, vc)
    return o_f.reshape(batch * seq, HG_W), o_b.reshape(batch * seq, HG_W)


def _merge_kernel(of_ref, ob_ref, sg_ref, u_ref, vn_ref, sga_ref, sgb_ref, x_ref, gt1_ref, sh2_ref, sc2_ref,
                  gout_ref, ws_ref, bs_ref, wa_ref, wb_ref, wo_ref, gpost_ref, gffn_ref, wr_ref,
                  x1_ref, h2_ref, lg_ref):
    tm = x_ref.shape[0]
    gout = gout_ref[...]
    gw = CM_W // CM_GROUPS
    group = min(MERGE_ROWS, tm)
    groups = [slice(r0, r0 + group) for r0 in range(0, tm, group)]

    def branch_inputs(rows):
        o = of_ref[rows, :].astype(F32) + ob_ref[rows, :].astype(F32)
        sg = sg_ref[rows, :].astype(F32)
        a = jnp.concatenate(
            [_rms(o[:, h * HG_DK:(h + 1) * HG_DK], gout) * sg[:, h * HG_DK:(h + 1) * HG_DK]
             for h in range(HG_HEADS)], axis=1).astype(BF16)
        vn = vn_ref[rows, :]
        z = jnp.concatenate(
            [jnp.concatenate([_dot(ws_ref[g], vn[c * CM_CHUNK:(c + 1) * CM_CHUNK, g * gw:(g + 1) * gw])
                              for g in range(CM_GROUPS)], axis=1) + bs_ref[...]
             for c in range(group // CM_CHUNK)], axis=0)
        return a, (u_ref[rows, :].astype(F32) * z).astype(BF16)

    ab = [branch_inputs(rows) for rows in groups]
    ys = [(sga_ref[rows, :].astype(F32) * _dot(a, wa_ref[...])
           + sgb_ref[rows, :].astype(F32) * _dot(bm, wb_ref[...])).astype(BF16)
          for rows, (a, bm) in zip(groups, ab)]
    yos = [_dot(y, wo_ref[...]) for y in ys]
    for rows, yo in zip(groups, yos):
        x1 = x_ref[rows, :] + gt1_ref[...] * _rms(yo, gpost_ref[...])
        x1_ref[rows, :] = x1
        h2 = (_rms(x1, gffn_ref[...]) * (1.0 + sc2_ref[...]) + sh2_ref[...]).astype(BF16)
        h2_ref[rows, :] = h2
        lg_ref[:, rows] = _dot_nt(wr_ref[...], h2)


def _merge(o_f, o_b, sg, u, vn, sga, sgb, x2, mod3, g_out, ws_b, bs_full, wa_b, wb_b, wo_b, g_post, g_ffn, wr_b,
           seq, tm):
    n = x2.shape[0]
    tpb = seq // tm
    row = lambda w: pl.BlockSpec((tm, w), lambda i: (i, 0))
    full = lambda a: pl.BlockSpec(a.shape, lambda i: (0,) * a.ndim)
    return pl.pallas_call(
        _merge_kernel,
        grid=(n // tm,),
        in_specs=[row(HG_W), row(HG_W), row(HG_W), row(CM_W), row(CM_W),
                  row(D_MODEL), row(D_MODEL), row(D_MODEL), _mod_spec(tpb, 2), _mod_spec(tpb, 3),
                  _mod_spec(tpb, 4), full(g_out), full(ws_b), full(bs_full), full(wa_b), full(wb_b),
                  full(wo_b), full(g_post), full(g_ffn), full(wr_b)],
        out_specs=[row(D_MODEL), row(D_MODEL), pl.BlockSpec((wr_b.shape[0], tm), lambda i: (0, i))],
        out_shape=[jax.ShapeDtypeStruct((n, D_MODEL), F32), jax.ShapeDtypeStruct((n, D_MODEL), BF16),
                   jax.ShapeDtypeStruct((wr_b.shape[0], n), F32)],
        compiler_params=_params(("parallel",)),
        name="merge",
    )(o_f, o_b, sg, u, vn, sga, sgb, x2, mod3, mod3, mod3, g_out, ws_b, bs_full, wa_b, wb_b, wo_b, g_post, g_ffn,
      wr_b)


def _router_kernel(lg_ref, br_ref, w_ref):
    tm = lg_ref.shape[1]
    gsz = N_EXPERTS // N_GROUPS
    scores = _sigmoid(lg_ref[:N_EXPERTS, :])
    sel = scores + jnp.concatenate([br_ref[...]] * (tm // br_ref.shape[1]), axis=1)
    neg = -jnp.inf

    def first_max(x, ids, sentinel, axis):
        m = jnp.max(x, axis=axis, keepdims=True)
        return m, jnp.min(jnp.where(x == m, ids, sentinel), axis=axis, keepdims=True)

    sel3 = sel.reshape(N_GROUPS, gsz, tm)
    j3 = lax.broadcasted_iota(jnp.int32, sel3.shape, 1)
    m1, i1 = first_max(sel3, j3, gsz, 1)
    gscore = m1 + jnp.max(jnp.where(j3 == i1, neg, sel3), axis=1, keepdims=True)
    g3 = lax.broadcasted_iota(jnp.int32, gscore.shape, 0)
    keep = jnp.zeros(gscore.shape, F32)
    for _ in range(TOPK_GROUPS):
        _, gi = first_max(gscore, g3, N_GROUPS, 0)
        keep = jnp.where(g3 == gi, 1.0, keep)
        gscore = jnp.where(g3 == gi, neg, gscore)
    x = jnp.where(keep > 0.0, sel3, neg).reshape(N_EXPERTS, tm)
    e_i = lax.broadcasted_iota(jnp.int32, x.shape, 0)
    w = jnp.zeros(x.shape, F32)
    for _ in range(TOP_K):
        _, ei = first_max(x, e_i, N_EXPERTS, 0)
        w = jnp.where(e_i == ei, scores, w)
        x = jnp.where(e_i == ei, neg, x)
    w = w / jnp.sum(w, axis=0, keepdims=True) * ROUTED_SCALE
    w_ref[...] = jnp.concatenate([w, jnp.zeros_like(w)], axis=0).T


def _router(logits_t, b_router_cols, tm):
    rows, n = logits_t.shape
    return pl.pallas_call(
        _router_kernel,
        grid=(n // tm,),
        in_specs=[pl.BlockSpec((rows, tm), lambda i: (0, i)),
                  pl.BlockSpec(b_router_cols.shape, lambda i: (0, 0))],
        out_specs=pl.BlockSpec((tm, rows), lambda i: (i, 0)),
        out_shape=jax.ShapeDtypeStruct((n, rows), F32),
        compiler_params=_params(("parallel",)),
        name="router",
    )(logits_t, b_router_cols)


MOE_TILE = 256
MOE_UNIT = 16
MOE_BLOCK = 512
MOE_MM_ROWS = 512
GATHER_SLOTS = 3
EXPERT_GRID_STEP = 24
UNITS_PER_BLOCK = MOE_BLOCK // MOE_UNIT
TILE_ROWS = 3072
TILE_UNITS = TILE_ROWS // MOE_UNIT
ROW_CHUNK = 512
N_CHUNKS = TILE_ROWS // ROW_CHUNK
FULL_CHUNKS = MOE_TILE * TOP_K // ROW_CHUNK
CHUNK_UNITS = ROW_CHUNK // MOE_UNIT
SORT_ROWS = 1024
KEY_W = 128
DIGIT_BITS = 6
DIGIT = 1 << DIGIT_BITS


def _swiglu_act(h, w_gu):
    gu = _dot(h, w_gu)
    de = gu.shape[1] // 2
    g = gu[:, :de]
    return g * _sigmoid(g) * gu[:, de:]


def _token_keys(cw, starts_row):
    t = cw.shape[0]
    routed = cw > 0.0
    t_i = lax.broadcasted_iota(jnp.int32, (t, t), 0)
    s_i = lax.broadcasted_iota(jnp.int32, (t, t), 1)
    rank = _dot((s_i < t_i).astype(BF16), routed.astype(BF16))
    pos = (starts_row + rank).astype(jnp.int32)
    lane = lax.broadcasted_iota(jnp.int32, cw.shape, 1)
    hi = jnp.where(routed, jnp.right_shift(pos, DIGIT_BITS), -1)
    lo = jnp.where(routed, jnp.bitwise_and(pos, DIGIT - 1), -1)
    key_hi = jnp.where(lane < N_EXPERTS, hi * DIGIT,
                       jnp.where(lane == N_EXPERTS, -DIGIT, jnp.where(lane == N_EXPERTS + 1, -1, 0)))
    key_lo = jnp.where(lane < N_EXPERTS, lo, 0)
    return jnp.concatenate([key_hi, key_lo], axis=1).astype(F32).astype(BF16)


def _segment_units(counts):
    return jnp.floor((counts + (MOE_UNIT - 1)) * (1.0 / MOE_UNIT))


def _dispatch_kernel(h_ref, cw_ref, digits_ref, xs_ref, cnt_ref):
    cw = cw_ref[...]
    t = cw.shape[0]
    routed = (cw > 0.0).astype(BF16)
    counts = _dot(jnp.ones((8, t), BF16), routed)
    cnt_ref[...] = counts.astype(jnp.int32)
    units = _segment_units(counts)
    e_i = lax.broadcasted_iota(jnp.int32, (KEY_W, KEY_W), 0)
    f_i = lax.broadcasted_iota(jnp.int32, (KEY_W, KEY_W), 1)
    starts = _dot(units.astype(BF16), (e_i < f_i).astype(BF16)) * MOE_UNIT
    ends = starts + units * MOE_UNIT
    keys_t = _token_keys(cw, starts[:1]).astype(F32).T.astype(BF16)
    h = h_ref[...]
    used_rows = jnp.max(ends)

    def sort_rows(r0, n):
        rows = slice(r0, r0 + n)
        lane = lax.broadcasted_iota(jnp.int32, (n, KEY_W), 1)
        r = (lax.broadcasted_iota(jnp.int32, (n, KEY_W), 0) + r0).astype(F32)
        in_seg = (r >= starts[:1]) & (r < ends[:1])
        rmap = jnp.where(lane < N_EXPERTS, in_seg.astype(F32), digits_ref[rows, :].astype(F32)).astype(BF16)
        hit = _dot(jnp.concatenate([rmap, rmap], axis=1), keys_t) == 0.0
        xs_ref[rows, :] = _dot(hit.astype(BF16), h).astype(xs_ref.dtype)

    for r0 in range(0, FULL_CHUNKS * ROW_CHUNK, SORT_ROWS):
        sort_rows(r0, SORT_ROWS)
    for c in range(FULL_CHUNKS, N_CHUNKS):
        pl.when(used_rows > c * ROW_CHUNK)(functools.partial(sort_rows, c * ROW_CHUNK, ROW_CHUNK))

        @pl.when(used_rows <= c * ROW_CHUNK)
        def _(c=c):
            xs_ref[c * ROW_CHUNK:(c + 1) * ROW_CHUNK, :] = jnp.zeros((ROW_CHUNK, D_MODEL), xs_ref.dtype)


def _dispatch(h2, cw, digits):
    n = h2.shape[0]
    n_tiles = n // MOE_TILE
    return pl.pallas_call(
        _dispatch_kernel,
        grid=(n_tiles,),
        in_specs=[pl.BlockSpec((MOE_TILE, D_MODEL), lambda i: (i, 0)),
                  pl.BlockSpec((MOE_TILE, KEY_W), lambda i: (i, 0)),
                  pl.BlockSpec(digits.shape, lambda i: (0, 0))],
        out_specs=[pl.BlockSpec((TILE_ROWS, D_MODEL), lambda i: (i, 0)),
                   pl.BlockSpec((8, KEY_W), lambda i: (i, 0))],
        out_shape=[jax.ShapeDtypeStruct((n_tiles * TILE_ROWS, D_MODEL), BF16),
                   jax.ShapeDtypeStruct((n_tiles * 8, KEY_W), jnp.int32)],
        compiler_params=_params(("parallel",)),
        name="moe_dispatch",
    )(h2, cw, digits)


def _unit_copy(src_hbm, unit, dst, slot, pos, sem):
    return pltpu.make_async_copy(
        src_hbm.at[pl.ds(pl.multiple_of(unit * MOE_UNIT, MOE_UNIT), MOE_UNIT)],
        dst.at[slot, pl.ds(pos * MOE_UNIT, MOE_UNIT)], sem.at[slot])


def _experts_kernel(be_ref, src_ref, nb_ref, wplan_ref, xs_hbm, wgu_hbm, wdn_hbm, ys_ref, xbuf, sem,
                    wgu_f, wdn_f, wsem, wgu_b, wdn_b):
    j = pl.program_id(0)
    nb = nb_ref[0]
    n_steps = pl.num_programs(0)

    def copies(blk, slot):
        return [_unit_copy(xs_hbm, src_ref[blk * UNITS_PER_BLOCK + u], xbuf, slot, u, sem)
                for u in range(UNITS_PER_BLOCK)]

    def fetch(blk, slot):
        for cp in copies(blk, slot):
            cp.start()

    def weight_copies(expert, slot):
        return [pltpu.make_async_copy(wgu_hbm.at[expert], wgu_f.at[slot], wsem.at[slot]),
                pltpu.make_async_copy(wdn_hbm.at[expert], wdn_f.at[slot], wsem.at[slot])]

    ahead = GATHER_SLOTS - 1

    @pl.when(j == 0)
    def _():
        for a in range(ahead):
            fetch(jnp.minimum(a, nb - 1), a)
        for cp in weight_copies(be_ref[0], 0):
            cp.start()

    @pl.when((j < nb) & (wplan_ref[j] == 1))
    def _():
        slot = wplan_ref[n_steps + j]
        for cp in weight_copies(be_ref[j], slot):
            cp.wait()
        nxt = wplan_ref[2 * n_steps + j]

        @pl.when(nxt >= 0)
        def _():
            for cp in weight_copies(nxt, 1 - slot):
                cp.start()

        wgu_b[...] = wgu_f[slot].astype(BF16)
        wdn_b[...] = wdn_f[slot].astype(BF16)

    @pl.when(j < nb)
    def _():
        slot = j % GATHER_SLOTS
        for cp in copies(j, slot):
            cp.wait()
        fetch(jnp.minimum(j + ahead, nb - 1), (j + ahead) % GATHER_SLOTS)
        for g in range(MOE_BLOCK // MOE_MM_ROWS):
            rows = pl.ds(g * MOE_MM_ROWS, MOE_MM_ROWS)
            act = _swiglu_act(xbuf[slot, rows, :], wgu_b[...])
            ys_ref[rows, :] = _dot(act.astype(BF16), wdn_b[...]).astype(ys_ref.dtype)

    @pl.when(j == nb - 1)
    def _():
        for a in range(1, GATHER_SLOTS):
            for cp in copies(j, (j + a) % GATHER_SLOTS):
                cp.wait()

    @pl.when(j >= nb)
    def _():
        ys_ref[...] = jnp.zeros_like(ys_ref)


def _weight_plan(block_expert, n_blocks_used):
    nb = block_expert.shape[0]
    jb = jnp.arange(nb, dtype=jnp.int32)
    used = jb < n_blocks_used[0]
    first = used & ((jb == 0) | (block_expert != jnp.roll(block_expert, 1)))
    slot = (jnp.cumsum(first.astype(jnp.int32)) - 1) % 2
    first_at = jnp.where(first, jb, nb)
    nxt_first = jnp.min(jnp.where(first_at[None, :] > jb[:, None], first_at[None, :], nb), axis=1)
    nxt_expert = jnp.sum(jnp.where(jb[None, :] == nxt_first[:, None], block_expert[None, :], 0), axis=1)
    nxt = jnp.where(nxt_first < nb, nxt_expert, -1)
    return jnp.concatenate([first.astype(jnp.int32), slot.astype(jnp.int32), nxt.astype(jnp.int32)])


def _experts(xs, block_expert, src_units, n_blocks_used, w_gu, w_dn):
    nb_max = block_expert.shape[0]
    any_spec = pl.BlockSpec(memory_space=pl.ANY)
    grid_spec = pltpu.PrefetchScalarGridSpec(
        num_scalar_prefetch=4,
        grid=(nb_max,),
        in_specs=[any_spec, any_spec, any_spec],
        out_specs=pl.BlockSpec((MOE_BLOCK, D_MODEL), lambda j, be, src, nb, wplan: (j, 0)),
        scratch_shapes=[pltpu.VMEM((GATHER_SLOTS, MOE_BLOCK, D_MODEL), BF16),
                        pltpu.SemaphoreType.DMA((GATHER_SLOTS,)),
                        pltpu.VMEM((2, D_MODEL, 2 * D_EXPERT), F32), pltpu.VMEM((2, D_EXPERT, D_MODEL), F32),
                        pltpu.SemaphoreType.DMA((2,)),
                        pltpu.VMEM((D_MODEL, 2 * D_EXPERT), BF16), pltpu.VMEM((D_EXPERT, D_MODEL), BF16)],
    )
    return pl.pallas_call(
        _experts_kernel,
        grid_spec=grid_spec,
        out_shape=jax.ShapeDtypeStruct((nb_max * MOE_BLOCK, D_MODEL), BF16),
        compiler_params=_params(("arbitrary",)),
        name="moe_experts",
    )(block_expert, src_units, n_blocks_used, _weight_plan(block_expert, n_blocks_used), xs, w_gu, w_dn)


def _combine_kernel(src_ref, used_ref, ys_hbm, cw_ref, h_ref, x1_ref, gt2_ref, gpost_ref, digits_t_ref,
                    wsgu_ref, wsdn_ref, o_ref, ybuf, sem, acc_ref):
    i = pl.program_id(0)

    def copies(tile, slot, c):
        return [_unit_copy(ys_hbm, src_ref[tile * TILE_UNITS + u], ybuf, slot, u, sem)
                for u in range(c * CHUNK_UNITS, (c + 1) * CHUNK_UNITS)]

    def chunk_used(tile, c):
        return used_ref[tile] > c * CHUNK_UNITS

    def for_used_chunks(tile, fn):
        for c in range(N_CHUNKS):
            if c < FULL_CHUNKS:
                fn(c)
            else:
                pl.when(chunk_used(tile, c))(functools.partial(fn, c))

    def fetch(tile, slot):
        def start(c):
            for cp in copies(tile, slot, c):
                cp.start()

        for_used_chunks(tile, start)

    def wait_all(tile, slot):
        def wait(c):
            for cp in copies(tile, slot, c):
                cp.wait()

        for_used_chunks(tile, wait)

    @pl.when(i == 0)
    def _():
        fetch(0, 0)

    @pl.when(i + 1 < pl.num_programs(0))
    def _():
        fetch(i + 1, (i + 1) % 2)

    cw = cw_ref[...]
    t = cw.shape[0]
    routed = (cw > 0.0).astype(BF16)
    e_i = lax.broadcasted_iota(jnp.int32, (KEY_W, KEY_W), 0)
    f_i = lax.broadcasted_iota(jnp.int32, (KEY_W, KEY_W), 1)
    units = _segment_units(_dot_tn(routed, jnp.ones((t, KEY_W), BF16)))
    starts = _dot((f_i < e_i).astype(BF16), units.astype(BF16)) * MOE_UNIT
    ends = starts + units * MOE_UNIT
    units_row = _segment_units(_dot(jnp.ones((8, t), BF16), routed))
    starts_row = _dot(units_row.astype(BF16), (e_i < f_i).astype(BF16)) * MOE_UNIT
    keys = _token_keys(cw, starts_row[:1])
    wb = cw.astype(BF16)

    f = _dot(_swiglu_act(h_ref[...], wsgu_ref[...]).astype(BF16), wsdn_ref[...])
    slot = i % 2
    reps = ROW_CHUNK // KEY_W
    starts_c = jnp.concatenate([starts] * reps, axis=1)
    ends_c = jnp.concatenate([ends] * reps, axis=1)
    sub = lax.broadcasted_iota(jnp.int32, (KEY_W, ROW_CHUNK), 0)

    wait_all(i, slot)

    def chunk_weights(c):
        rows = slice(c * ROW_CHUNK, (c + 1) * ROW_CHUNK)
        r = (lax.broadcasted_iota(jnp.int32, (KEY_W, ROW_CHUNK), 1) + c * ROW_CHUNK).astype(F32)
        in_seg = (r >= starts_c) & (r < ends_c)
        rmap_t = jnp.where(sub < N_EXPERTS, in_seg.astype(F32), digits_t_ref[:, rows].astype(F32)).astype(BF16)
        hit = _dot(keys, jnp.concatenate([rmap_t, rmap_t], axis=0)) == 0.0
        return jnp.where(hit, _dot(wb, rmap_t), 0.0).astype(BF16)

    def chunk_sum(c):
        return _dot(chunk_weights(c), ybuf[slot, c * ROW_CHUNK:(c + 1) * ROW_CHUNK, :])

    pw = jnp.concatenate([chunk_weights(c) for c in range(FULL_CHUNKS)], axis=1)
    acc_ref[...] = f + _dot(pw, ybuf[slot, :FULL_CHUNKS * ROW_CHUNK, :])
    for c in range(FULL_CHUNKS, N_CHUNKS):
        @pl.when(chunk_used(i, c))
        def _(c=c):
            acc_ref[...] += chunk_sum(c)
    o_ref[...] = x1_ref[...] + gt2_ref[...] * _rms(acc_ref[...], gpost_ref[...])


def _combine(ys, src_units, used_units, cw, h2, x1, mod3, g_post, digits_t, wsgu_b, wsdn_b, seq):
    n = h2.shape[0]
    tpb = seq // MOE_TILE
    row = lambda w: pl.BlockSpec((MOE_TILE, w), lambda i, src, used: (i, 0))
    full = lambda a: pl.BlockSpec(a.shape, lambda i, src, used: (0,) * a.ndim)
    grid_spec = pltpu.PrefetchScalarGridSpec(
        num_scalar_prefetch=2,
        grid=(n // MOE_TILE,),
        in_specs=[pl.BlockSpec(memory_space=pl.ANY), row(KEY_W), row(D_MODEL), row(D_MODEL),
                  pl.BlockSpec((None, 1, D_MODEL), lambda i, src, used: (i // tpb, 0, 5)), full(g_post),
                  full(digits_t), full(wsgu_b), full(wsdn_b)],
        out_specs=row(D_MODEL),
        scratch_shapes=[pltpu.VMEM((2, TILE_ROWS, D_MODEL), BF16), pltpu.SemaphoreType.DMA((2,)),
                        pltpu.VMEM((MOE_TILE, D_MODEL), F32)],
    )
    return pl.pallas_call(
        _combine_kernel,
        grid_spec=grid_spec,
        out_shape=jax.ShapeDtypeStruct((n, D_MODEL), F32),
        compiler_params=_params(("arbitrary",)),
        name="moe_combine",
    )(src_units, used_units, ys, cw, h2, x1, mod3, g_post, digits_t, wsgu_b, wsdn_b)


def _row_digits():
    r = np.arange(TILE_ROWS)
    d = np.zeros((TILE_ROWS, KEY_W), np.float32)
    d[:, N_EXPERTS] = r // DIGIT
    d[:, N_EXPERTS + 1] = r % DIGIT
    return jnp.asarray(d, dtype=BF16)


def _moe_plan(counts, nb_max):
    n_tiles = counts.shape[0]
    s = (counts + (MOE_UNIT - 1)) // MOE_UNIT
    local = jnp.cumsum(s, axis=1) - s
    cs = jnp.cumsum(s, axis=0)
    per_expert = cs[-1]
    padded = (per_expert + UNITS_PER_BLOCK - 1) // UNITS_PER_BLOCK * UNITS_PER_BLOCK
    g_end = jnp.cumsum(padded)
    g_start = g_end - padded
    seg_start = g_start[None, :] + cs - s
    n_blocks_used = (g_end[-1] // UNITS_PER_BLOCK).astype(jnp.int32).reshape(1)
    jb = jnp.arange(nb_max, dtype=jnp.int32)
    one_e = ((jb[:, None] >= (g_start // UNITS_PER_BLOCK)[None, :])
             & (jb[:, None] < (g_end // UNITS_PER_BLOCK)[None, :])).astype(jnp.int32)
    pick_e = lambda table: jnp.sum(one_e[:, :, None] * table.T[None, :, :], axis=1)
    block_expert = jnp.where(jb < n_blocks_used[0], jnp.sum(one_e * jnp.arange(N_EXPERTS, dtype=jnp.int32), axis=1),
                             N_EXPERTS - 1).astype(jnp.int32)
    cs_b, s_b, local_b = pick_e(cs), pick_e(s), pick_e(local)
    q = (jb * UNITS_PER_BLOCK - jnp.sum(one_e * g_start[None, :], axis=1))[:, None] \
        + jnp.arange(UNITS_PER_BLOCK, dtype=jnp.int32)[None, :]
    tile = jnp.minimum(jnp.sum(cs_b[:, None, :] <= q[:, :, None], axis=2), n_tiles - 1)
    one_t = (tile[:, :, None] == jnp.arange(n_tiles, dtype=jnp.int32)).astype(jnp.int32)
    src = tile * TILE_UNITS + q + jnp.sum(one_t * (local_b - cs_b + s_b)[:, None, :], axis=2)
    valid = q < jnp.sum(one_e * per_expert[None, :], axis=1)[:, None]
    src_units = jnp.where(valid, src, 0).astype(jnp.int32).reshape(-1)
    u = jnp.arange(TILE_UNITS, dtype=jnp.int32)
    seg_end = local + s
    eu = jnp.minimum(jnp.sum(seg_end[:, None, :] <= u[None, :, None], axis=2), N_EXPERTS - 1)
    one_u = (eu[:, :, None] == jnp.arange(N_EXPERTS, dtype=jnp.int32)).astype(jnp.int32)
    back = u[None, :] + jnp.sum(one_u * (seg_start - local)[:, None, :], axis=2)
    back_units = jnp.where(u[None, :] < seg_end[:, -1:], back, 0).astype(jnp.int32).reshape(-1)
    return block_expert, src_units, n_blocks_used, back_units, seg_end[:, -1].astype(jnp.int32)


def _tile(n, pref):
    t = pref
    while n % t:
        t //= 2
    return t


def kernel(x, c, ctx, c_ctx, w_ada, b_ada, g_pre_mix, g_post_mix, g_pre_ffn, g_post_ffn, w_in, lb_logits, g_hgrn_out, cm_ln_g, cm_ln_b, w_spatial, b_spatial, w_branch_a, w_branch_b, w_out, w_router, b_router, w_expert_gu, w_expert_down, w_shared_gu, w_shared_down):
    B, T, D = x.shape
    L = ctx.shape[1]
    assert D == D_MODEL and w_ada.shape[0] == 1 and T % SCAN_CHUNK == 0 and L % SCAN_CHUNK == 0
    l = 0
    row = lambda a: a[l].reshape(1, -1)

    n_rows = -(-(B + 1) // 16) * 16
    cs = jnp.zeros((n_rows, D), F32).at[:B].set(c).at[B].set(c_ctx)
    mod3 = _ada_mod(cs, w_ada[l], row(b_ada)).reshape(n_rows, 1, 6 * D)

    w_in_b = w_in[l].astype(BF16)
    lbl = lb_logits[:, l:l + 2].reshape(4, HG_W)
    x2 = x.reshape(B * T, D)
    q, k2, lf2, v, sg, u, vn, sga, sgb = _proj_lat(
        x2, mod3, row(g_pre_mix), w_in_b, lbl, row(cm_ln_g), row(cm_ln_b), T, _tile(T, 256))
    kc2, lfc2, vc = _proj_ctx(ctx.reshape(B * L, D), mod3, B, row(g_pre_mix), w_in_b[:, HG_W:4 * HG_W], lbl,
                              _tile(B * L, 256))

    o_f, o_b = _hgrn_scan(q, k2, lf2, v, kc2, lfc2, vc, B, T, L)

    bs_full = jnp.repeat(b_spatial[l], CM_W // CM_GROUPS, axis=1)
    x1, h2, logits = _merge(
        o_f, o_b, sg, u, vn, sga, sgb, x2, mod3, row(g_hgrn_out), w_spatial[l].astype(BF16), bs_full,
        w_branch_a[l].astype(BF16), w_branch_b[l].astype(BF16), w_out[l].astype(BF16), row(g_post_mix),
        row(g_pre_ffn), jnp.pad(w_router[l].T, ((0, KEY_W - N_EXPERTS), (0, 0))).astype(BF16), T, _tile(T, 1024))

    cw = _router(logits, jnp.broadcast_to(b_router[l][:, None], (N_EXPERTS, 128)), _tile(B * T, 512))

    n_tok = B * T
    n_tiles = n_tok // MOE_TILE
    digits = _row_digits()
    xs, cnt = _dispatch(h2, cw, digits)
    counts = cnt.reshape(n_tiles, 8, KEY_W)[:, 0, :N_EXPERTS]
    max_units = (n_tok * TOP_K + n_tiles * N_EXPERTS * (MOE_UNIT - 1)) // MOE_UNIT + N_EXPERTS * (UNITS_PER_BLOCK - 1)
    nb_max = -(-max_units // UNITS_PER_BLOCK)
    block_expert, src_units, n_blocks_used, back_units, tile_units = _moe_plan(counts, nb_max)
    wsgu_b, wsdn_b = w_shared_gu[l].astype(BF16), w_shared_down[l].astype(BF16)

    nb_min = -(-(n_tok * TOP_K) // MOE_BLOCK)
    grids = sorted({min(nb_max, g) for g in range(nb_min + EXPERT_GRID_STEP, nb_max + EXPERT_GRID_STEP,
                                                  EXPERT_GRID_STEP)})

    def experts_and_combine(nb_grid):
        def run():
            ys = _experts(xs, block_expert[:nb_grid], src_units[:nb_grid * UNITS_PER_BLOCK], n_blocks_used,
                          w_expert_gu[l], w_expert_down[l])
            return _combine(ys, back_units, tile_units, cw, h2, x1, mod3, row(g_post_ffn), digits.T,
                            wsgu_b, wsdn_b, T)
        return run

    which = jnp.sum(n_blocks_used[0] > jnp.asarray(grids[:-1], jnp.int32))
    out = lax.switch(which, [experts_and_combine(g) for g in grids])
    return out.reshape(B, T, D)
```

```python
import functools

import numpy as np
import jax
import jax.numpy as jnp
from jax import lax
from jax.experimental import pallas as pl
from jax.experimental.pallas import tpu as pltpu

F32 = jnp.float32
BF16 = jnp.bfloat16

D_MODEL = 1024
EPS = 1e-6
HG_HEADS = 4
HG_DK = 128
HG_W = HG_HEADS * HG_DK
CM_W = 512
CM_CHUNK = 128
CM_GROUPS = 4
N_EXPERTS = 64
TOP_K = 8
N_GROUPS = 8
TOPK_GROUPS = 4
D_EXPERT = 256
ROUTED_SCALE = 2.5
LANES = 128
SUBLANES = 8
BF16_SUBLANES = 16
SCAN_CHUNK = 128
SUB = 16
N_LEVELS = (SCAN_CHUNK // SUB).bit_length()
PROJ_TILE, PROJ_ROWS = 512, 256
CTX_TILE = 256
MERGE_TILE, MERGE_ROWS = 1024, 512
ROUTER_TILE = 512
V7X_VMEM_BYTES = 64 * 1024 * 1024
VMEM_LIMIT = V7X_VMEM_BYTES - 8 * 1024 * 1024


def _params(sem):
    return pltpu.CompilerParams(dimension_semantics=sem, vmem_limit_bytes=VMEM_LIMIT)


def _dot(a, b):
    return jnp.dot(a, b, preferred_element_type=F32)


def _dot_nt(a, b):
    return lax.dot_general(a, b, (((1,), (1,)), ((), ())), preferred_element_type=F32)


def _dot_tn(a, b):
    return lax.dot_general(a, b, (((0,), (0,)), ((), ())), preferred_element_type=F32)


def _sigmoid(x):
    return 0.5 * jnp.tanh(0.5 * x) + 0.5


def _rms(x, g):
    return x * lax.rsqrt(jnp.mean(x * x, axis=-1, keepdims=True) + EPS) * g


def _ada_kernel(c_ref, w_ref, b_ref, o_ref):
    c = c_ref[...]
    s = c * _sigmoid(c)
    o_ref[...] = _dot(s.astype(BF16), w_ref[...].astype(BF16)) + b_ref[...]


def _ada_mod(cs, w_ada, b_ada):
    rows = cs.shape[0]
    n_out = w_ada.shape[1]
    return pl.pallas_call(
        _ada_kernel,
        grid=(n_out // D_MODEL,),
        in_specs=[
            pl.BlockSpec((rows, D_MODEL), lambda j: (0, 0)),
            pl.BlockSpec((D_MODEL, D_MODEL), lambda j: (0, j)),
            pl.BlockSpec((1, D_MODEL), lambda j: (0, j)),
        ],
        out_specs=pl.BlockSpec((rows, D_MODEL), lambda j: (0, j)),
        out_shape=jax.ShapeDtypeStruct((rows, n_out), F32),
        compiler_params=_params(("parallel",)),
        name="ada_mod",
    )(cs, w_ada, b_ada)


def _lower_bounds(lbl):
    out = []
    for d in range(2):
        l0, l1 = lbl[2 * d:2 * d + 1], lbl[2 * d + 1:2 * d + 2]
        m = jnp.maximum(l0, l1)
        e0, e1 = jnp.exp(l0 - m), jnp.exp(l1 - m)
        out.append(e0 / (e0 + e1))
    return out


def _prenorm(x_ref, sh_ref, sc_ref, g_ref, rows=slice(None)):
    return (_rms(x_ref[rows, :], g_ref[...]) * (1.0 + sc_ref[...]) + sh_ref[...]).astype(BF16)


def _gates(z, lb, k_ref, lf_ref, d, rows=slice(None)):
    half_t = 0.5 * jnp.tanh(0.5 * z)
    k_ref[d, rows, :] = ((1.0 - lb) * (0.5 - half_t)).astype(k_ref.dtype)
    lf_ref[d, rows, :] = jnp.log2(lb + (1.0 - lb) * (0.5 + half_t))


def _proj_lat_kernel(x_ref, sh_ref, sc_ref, g_ref, w_ref, lbl_ref, lng_ref, lnb_ref,
                     q_ref, k_ref, lf_ref, v_ref, sg_ref, u_ref, vn_ref, sga_ref, sgb_ref):
    tm = x_ref.shape[0]
    group = min(PROJ_ROWS, tm)
    groups = [slice(r0, r0 + group) for r0 in range(0, tm, group)]
    hbs = [_prenorm(x_ref, sh_ref, sc_ref, g_ref, rows) for rows in groups]
    lbs = _lower_bounds(lbl_ref[...])

    def columns(lo, width, epilogue):
        zs = [_dot(hb, w_ref[:, lo:lo + width]) for hb in hbs]
        for rows, z in zip(groups, zs):
            epilogue(rows, z)

    def silu_to(ref):
        def epilogue(rows, z):
            ref[rows, :] = (z * _sigmoid(z)).astype(ref.dtype)
        return epilogue

    def sigmoid_to(ref):
        def epilogue(rows, z):
            ref[rows, :] = _sigmoid(z).astype(ref.dtype)
        return epilogue

    def gelu_to_u(rows, z):
        u_ref[rows, :] = jax.nn.gelu(z).astype(u_ref.dtype)

    def gelu_layernorm_to_vn(rows, z):
        vv = jax.nn.gelu(z)
        vc = vv - jnp.mean(vv, axis=-1, keepdims=True)
        vn = vc * lax.rsqrt(jnp.mean(vc * vc, axis=-1, keepdims=True) + EPS)
        vn_ref[rows, :] = (vn * lng_ref[...] + lnb_ref[...]).astype(vn_ref.dtype)

    def cast_to_v(rows, z):
        v_ref[rows, :] = z.astype(v_ref.dtype)

    columns(0, HG_W, silu_to(q_ref))
    for d in range(2):
        columns((1 + d) * HG_W, HG_W, lambda rows, z, d=d: _gates(z, lbs[d], k_ref, lf_ref, d, rows))
    columns(4 * HG_W, HG_W, silu_to(sg_ref))
    columns(5 * HG_W, CM_W, gelu_to_u)
    columns(5 * HG_W + CM_W, CM_W, gelu_layernorm_to_vn)
    base = 5 * HG_W + 2 * CM_W
    columns(base, D_MODEL, sigmoid_to(sga_ref))
    columns(base + D_MODEL, D_MODEL, sigmoid_to(sgb_ref))
    columns(3 * HG_W, HG_W, cast_to_v)


def _mod_spec(rows_per_batch_tiles, col):
    return pl.BlockSpec((None, 1, D_MODEL), lambda i: (i // rows_per_batch_tiles, 0, col))


def _proj_lat(x2, mod3, g_pre, w_in_b, lbl, ln_g, ln_b, seq, tm):
    n = x2.shape[0]
    tpb = seq // tm
    row = lambda w: pl.BlockSpec((tm, w), lambda i: (i, 0))
    row2 = pl.BlockSpec((2, tm, HG_W), lambda i: (0, i, 0))
    full = lambda a: pl.BlockSpec(a.shape, lambda i: (0,) * a.ndim)
    outs = [
        (row(HG_W), jax.ShapeDtypeStruct((n, HG_W), BF16)),
        (row2, jax.ShapeDtypeStruct((2, n, HG_W), BF16)),
        (row2, jax.ShapeDtypeStruct((2, n, HG_W), F32)),
        (row(HG_W), jax.ShapeDtypeStruct((n, HG_W), BF16)),
        (row(HG_W), jax.ShapeDtypeStruct((n, HG_W), BF16)),
        (row(CM_W), jax.ShapeDtypeStruct((n, CM_W), BF16)),
        (row(CM_W), jax.ShapeDtypeStruct((n, CM_W), BF16)),
        (row(D_MODEL), jax.ShapeDtypeStruct((n, D_MODEL), BF16)),
        (row(D_MODEL), jax.ShapeDtypeStruct((n, D_MODEL), BF16)),
    ]
    return pl.pallas_call(
        _proj_lat_kernel,
        grid=(n // tm,),
        in_specs=[row(D_MODEL), _mod_spec(tpb, 0), _mod_spec(tpb, 1), full(g_pre), full(w_in_b),
                  full(lbl), full(ln_g), full(ln_b)],
        out_specs=[o[0] for o in outs],
        out_shape=[o[1] for o in outs],
        compiler_params=_params(("parallel",)),
        name="proj_lat",
    )(x2, mod3, mod3, g_pre, w_in_b, lbl, ln_g, ln_b)


def _proj_ctx_kernel(x_ref, sh_ref, sc_ref, g_ref, w_ref, lbl_ref, k_ref, lf_ref, v_ref):
    hb = _prenorm(x_ref, sh_ref, sc_ref, g_ref)
    lbs = _lower_bounds(lbl_ref[...])
    for d in range(2):
        _gates(_dot(hb, w_ref[:, d * HG_W:(d + 1) * HG_W]), lbs[d], k_ref, lf_ref, d)
    v_ref[...] = _dot(hb, w_ref[:, 2 * HG_W:3 * HG_W]).astype(v_ref.dtype)


def _proj_ctx(c2, mod3, ctx_row, g_pre, w_ctx_b, lbl, tm):
    n = c2.shape[0]
    row = lambda w: pl.BlockSpec((tm, w), lambda i: (i, 0))
    row2 = pl.BlockSpec((2, tm, HG_W), lambda i: (0, i, 0))
    full = lambda a: pl.BlockSpec(a.shape, lambda i: (0,) * a.ndim)
    mod = lambda col: pl.BlockSpec((None, 1, D_MODEL), lambda i: (ctx_row, 0, col))
    return pl.pallas_call(
        _proj_ctx_kernel,
        grid=(n // tm,),
        in_specs=[row(D_MODEL), mod(0), mod(1), full(g_pre), full(w_ctx_b), full(lbl)],
        out_specs=[row2, row2, row(HG_W)],
        out_shape=[jax.ShapeDtypeStruct((2, n, HG_W), BF16), jax.ShapeDtypeStruct((2, n, HG_W), F32),
                   jax.ShapeDtypeStruct((n, HG_W), BF16)],
        compiler_params=_params(("parallel",)),
        name="proj_ctx",
    )(c2, mod3, mod3, g_pre, w_ctx_b, lbl)


def _scan_tables():
    C = SCAN_CHUNK
    t = np.arange(C)
    lmats, lvls = [], []
    for d in range(2):
        p = t if d == 0 else C - 1 - t
        pt, ps = p[:, None], p[None, :]
        lmat = (ps <= pt).astype(np.float32)
        lmats.append(np.concatenate([lmat, lmat], axis=1))
        lvl = np.full((C, C), -1, np.int32)
        lvl[(pt // SUB == ps // SUB) & (ps <= pt)] = 0
        half, idx = SUB, 1
        while half < C:
            span = 2 * half
            lvl[(pt // span == ps // span) & ((pt // half) % 2 == 1) & ((ps // half) % 2 == 0)] = idx
            half, idx = span, idx + 1
        lvls.append(lvl)
    return jnp.asarray(np.stack(lmats), dtype=BF16), jnp.asarray(np.stack(lvls))


def _scan_cumsum(d, slot, lmat_ref, lf_ref, b_scr):
    lf = lf_ref[...]
    hi = lf.astype(BF16)
    lo = (lf - hi.astype(F32)).astype(BF16)
    b_scr[slot] = _dot(lmat_ref[d], jnp.concatenate([hi, lo], axis=0))


def _scan_head(d, slot, h, masks, k_ref, v_ref, st_ref, b_scr, q_ref=None, o_ref=None):
    C = SCAN_CHUNK
    sl = slice(h * HG_DK, (h + 1) * HG_DK)
    b = b_scr[slot, :, sl]

    def row(i):
        return b_scr[slot, pl.ds(i, 1), sl]

    b_last = row(C - 1 if d == 0 else 0)
    k = k_ref[:, sl]
    v = v_ref[:, sl]
    st = st_ref[slot, :, sl]
    if q_ref is not None:
        q = q_ref[:, sl]
        e0 = jnp.concatenate([b[m * SUB:(m + 1) * SUB] - row(m * SUB + SUB // 2 - 1 + d)
                              for m in range(C // SUB)], axis=0)
        factors = [(jnp.exp2(e0).astype(BF16), jnp.exp2(-e0).astype(BF16))]
        half = SUB
        while half < C:
            span = 2 * half
            pieces = []
            for m in range(C // span):
                ref = row(m * span + half - 1 + d)
                lo_half, hi_half = b[m * span:m * span + half], b[m * span + half:(m + 1) * span]
                pieces += [ref - lo_half, hi_half - ref] if d == 0 else [lo_half - ref, ref - hi_half]
            w = jnp.exp2(jnp.concatenate(pieces, axis=0)).astype(BF16)
            factors.append((w, w))
            half = span
        a = jnp.zeros((C, C), F32)
        for (wq, wk), mask in zip(factors, masks):
            a = jnp.where(mask, _dot_nt(q * wq, k * wk), a)
        qhat = q * jnp.exp2(b).astype(BF16)
        o = _dot(a.astype(BF16), v) + _dot_nt(qhat, st.astype(BF16))
        o_ref[:, sl] = o.astype(o_ref.dtype)
    khat = k * jnp.exp2(b_last - b).astype(BF16)
    st_ref[slot, :, sl] = jnp.exp2(b_last) * st + _dot_tn(v, khat)


def _scan_kernel(n_ctx_steps, lmat_ref, lvl_ref, q_f, k_f, lf_f, v_f, q_b, k_b, lf_b, v_b,
                 kc_f, lfc_f, vc_f, kc_b, lfc_b, vc_b, o_f, o_b, st_ref, b_scr):
    s = pl.program_id(1)
    n_seq = q_f.shape[0]

    @pl.when(s == 0)
    def _():
        st_ref[...] = jnp.zeros_like(st_ref)

    def step(refs, readout):
        chains = [(d, 2 * i + d, [r.at[i] if r is not None else None for r in refs[d]])
                  for i in range(n_seq) for d in range(2)]
        for d, slot, (k, lf, v, q, o) in chains:
            _scan_cumsum(d, slot, lmat_ref, lf, b_scr)
        masks = {d: [lvl_ref[d] == i for i in range(N_LEVELS)] for d in range(2)} if readout else {0: None, 1: None}
        for h in range(HG_HEADS):
            for d, slot, (k, lf, v, q, o) in chains:
                _scan_head(d, slot, h, masks[d], k, v, st_ref, b_scr, q, o)

    @pl.when(s < n_ctx_steps)
    def _():
        step([(kc_f, lfc_f, vc_f, None, None), (kc_b, lfc_b, vc_b, None, None)], False)

    @pl.when(s >= n_ctx_steps)
    def _():
        step([(k_f, lf_f, v_f, q_f, o_f), (k_b, lf_b, v_b, q_b, o_b)], True)


def _hgrn_scan(q, k2, lf2, v, kc2, lfc2, vc, batch, seq, ctx_len):
    C = SCAN_CHUNK
    n_lat, n_ctx = seq // C, ctx_len // C
    n_seq = 2 if batch % 2 == 0 else 1
    lmat, lvl = _scan_tables()
    q, v = (a.reshape(batch, seq, HG_W) for a in (q, v))
    k2, lf2 = (a.reshape(2, batch, seq, HG_W) for a in (k2, lf2))
    vc = vc.reshape(batch, ctx_len, HG_W)
    kc2, lfc2 = (a.reshape(2, batch, ctx_len, HG_W) for a in (kc2, lfc2))

    def lat_blk(d):
        def blk(s):
            j = jnp.maximum(s - n_ctx, 0)
            return j if d == 0 else n_lat - 1 - j
        return blk

    def ctx_blk(d):
        def blk(s):
            i = jnp.minimum(s, n_ctx - 1)
            return i if d == 0 else n_ctx - 1 - i
        return blk

    def plain(blk):
        return pl.BlockSpec((n_seq, C, HG_W), lambda b, s: (b, blk(s), 0))

    def specs(blk_of, with_q):
        out = []
        for d in range(2):
            blk = blk_of(d)
            per_dir = pl.BlockSpec((None, n_seq, C, HG_W), lambda b, s, blk=blk, d=d: (d, b, blk(s), 0))
            out += ([plain(blk)] if with_q else []) + [per_dir, per_dir, plain(blk)]
        return out

    full = lambda a: pl.BlockSpec(a.shape, lambda b, s: (0,) * a.ndim)
    o_shape = jax.ShapeDtypeStruct((batch, seq, HG_W), BF16)
    n_chains = 2 * n_seq
    o_f, o_b = pl.pallas_call(
        functools.partial(_scan_kernel, n_ctx),
        grid=(batch // n_seq, n_ctx + n_lat),
        in_specs=[full(lmat), full(lvl)] + specs(lat_blk, True) + specs(ctx_blk, False),
        out_specs=[plain(lat_blk(d)) for d in range(2)],
        out_shape=[o_shape, o_shape],
        scratch_shapes=[pltpu.VMEM((n_chains, HG_DK, HG_W), F32), pltpu.VMEM((n_chains, C, HG_W), F32)],
        compiler_params=_params(("parallel", "arbitrary")),
        name="hgrn_scan",
    )(lmat, lvl, q, k2, lf2, v, q, k2, lf2, v, kc2, lfc2, vc, kc2, lfc2, vc)
    return o_f.reshape(batch * seq, HG_W), o_b.reshape(batch * seq, HG_W)


def _merge_kernel(of_ref, ob_ref, sg_ref, u_ref, vn_ref, sga_ref, sgb_ref, x_ref, gt1_ref, sh2_ref, sc2_ref,
                  gout_ref, ws_ref, bs_ref, wa_ref, wb_ref, wo_ref, gpost_ref, gffn_ref, wr_ref,
                  x1_ref, h2_ref, lg_ref):
    tm = x_ref.shape[0]
    gout = gout_ref[...]
    gw = CM_W // CM_GROUPS
    group = min(MERGE_ROWS, tm)
    groups = [slice(r0, r0 + group) for r0 in range(0, tm, group)]

    def branch_inputs(rows):
        o = of_ref[rows, :].astype(F32) + ob_ref[rows, :].astype(F32)
        sg = sg_ref[rows, :].astype(F32)
        a = jnp.concatenate(
            [_rms(o[:, h * HG_DK:(h + 1) * HG_DK], gout) * sg[:, h * HG_DK:(h + 1) * HG_DK]
             for h in range(HG_HEADS)], axis=1).astype(BF16)
        vn = vn_ref[rows, :]
        z = jnp.concatenate(
            [jnp.concatenate([_dot(ws_ref[g], vn[c * CM_CHUNK:(c + 1) * CM_CHUNK, g * gw:(g + 1) * gw])
                              for g in range(CM_GROUPS)], axis=1) + bs_ref[...]
             for c in range(group // CM_CHUNK)], axis=0)
        return a, (u_ref[rows, :].astype(F32) * z).astype(BF16)

    ab = [branch_inputs(rows) for rows in groups]
    ys = [(sga_ref[rows, :].astype(F32) * _dot(a, wa_ref[...])
           + sgb_ref[rows, :].astype(F32) * _dot(bm, wb_ref[...])).astype(BF16)
          for rows, (a, bm) in zip(groups, ab)]
    yos = [_dot(y, wo_ref[...]) for y in ys]
    for rows, yo in zip(groups, yos):
        x1 = x_ref[rows, :] + gt1_ref[...] * _rms(yo, gpost_ref[...])
        x1_ref[rows, :] = x1
        h2 = (_rms(x1, gffn_ref[...]) * (1.0 + sc2_ref[...]) + sh2_ref[...]).astype(BF16)
        h2_ref[rows, :] = h2
        lg_ref[:, rows] = _dot_nt(wr_ref[...], h2)


def _merge(o_f, o_b, sg, u, vn, sga, sgb, x2, mod3, g_out, ws_b, bs_full, wa_b, wb_b, wo_b, g_post, g_ffn, wr_b,
           seq, tm):
    n = x2.shape[0]
    tpb = seq // tm
    row = lambda w: pl.BlockSpec((tm, w), lambda i: (i, 0))
    full = lambda a: pl.BlockSpec(a.shape, lambda i: (0,) * a.ndim)
    return pl.pallas_call(
        _merge_kernel,
        grid=(n // tm,),
        in_specs=[row(HG_W), row(HG_W), row(HG_W), row(CM_W), row(CM_W),
                  row(D_MODEL), row(D_MODEL), row(D_MODEL), _mod_spec(tpb, 2), _mod_spec(tpb, 3),
                  _mod_spec(tpb, 4), full(g_out), full(ws_b), full(bs_full), full(wa_b), full(wb_b),
                  full(wo_b), full(g_post), full(g_ffn), full(wr_b)],
        out_specs=[row(D_MODEL), row(D_MODEL), pl.BlockSpec((wr_b.shape[0], tm), lambda i: (0, i))],
        out_shape=[jax.ShapeDtypeStruct((n, D_MODEL), F32), jax.ShapeDtypeStruct((n, D_MODEL), BF16),
                   jax.ShapeDtypeStruct((wr_b.shape[0], n), F32)],
        compiler_params=_params(("parallel",)),
        name="merge",
    )(o_f, o_b, sg, u, vn, sga, sgb, x2, mod3, mod3, mod3, g_out, ws_b, bs_full, wa_b, wb_b, wo_b, g_post, g_ffn,
      wr_b)


def _router_kernel(lg_ref, br_ref, w_ref):
    tm = lg_ref.shape[1]
    gsz = N_EXPERTS // N_GROUPS
    scores = _sigmoid(lg_ref[:N_EXPERTS, :])
    sel = scores + jnp.concatenate([br_ref[...]] * (tm // br_ref.shape[1]), axis=1)
    neg = -jnp.inf

    def first_max(x, ids, sentinel, axis):
        m = jnp.max(x, axis=axis, keepdims=True)
        return m, jnp.min(jnp.where(x == m, ids, sentinel), axis=axis, keepdims=True)

    sel3 = sel.reshape(N_GROUPS, gsz, tm)
    j3 = lax.broadcasted_iota(jnp.int32, sel3.shape, 1)
    m1, i1 = first_max(sel3, j3, gsz, 1)
    gscore = m1 + jnp.max(jnp.where(j3 == i1, neg, sel3), axis=1, keepdims=True)
    g3 = lax.broadcasted_iota(jnp.int32, gscore.shape, 0)
    keep = jnp.zeros(gscore.shape, F32)
    for _ in range(TOPK_GROUPS):
        _, gi = first_max(gscore, g3, N_GROUPS, 0)
        keep = jnp.where(g3 == gi, 1.0, keep)
        gscore = jnp.where(g3 == gi, neg, gscore)
    x = jnp.where(keep > 0.0, sel3, neg).reshape(N_EXPERTS, tm)
    e_i = lax.broadcasted_iota(jnp.int32, x.shape, 0)
    w = jnp.zeros(x.shape, F32)
    for _ in range(TOP_K):
        _, ei = first_max(x, e_i, N_EXPERTS, 0)
        w = jnp.where(e_i == ei, scores, w)
        x = jnp.where(e_i == ei, neg, x)
    w = w / jnp.sum(w, axis=0, keepdims=True) * ROUTED_SCALE
    w_ref[...] = jnp.concatenate([w, jnp.zeros_like(w)], axis=0).T


def _router(logits_t, b_router_cols, tm):
    rows, n = logits_t.shape
    return pl.pallas_call(
        _router_kernel,
        grid=(n // tm,),
        in_specs=[pl.BlockSpec((rows, tm), lambda i: (0, i)),
                  pl.BlockSpec(b_router_cols.shape, lambda i: (0, 0))],
        out_specs=pl.BlockSpec((tm, rows), lambda i: (i, 0)),
        out_shape=jax.ShapeDtypeStruct((n, rows), F32),
        compiler_params=_params(("parallel",)),
        name="router",
    )(logits_t, b_router_cols)


MOE_TILE = 256
MOE_UNIT = BF16_SUBLANES
MOE_BLOCK = 512
GATHER_SLOTS = 3
EXPERT_GRID_STEP = 24
UNITS_PER_BLOCK = MOE_BLOCK // MOE_UNIT
ROW_CHUNK = 512
TILE_ROWS = -(-(MOE_TILE * TOP_K + N_EXPERTS * (MOE_UNIT - 1)) // ROW_CHUNK) * ROW_CHUNK
TILE_UNITS = TILE_ROWS // MOE_UNIT
N_CHUNKS = TILE_ROWS // ROW_CHUNK
FULL_CHUNKS = MOE_TILE * TOP_K // ROW_CHUNK
CHUNK_UNITS = ROW_CHUNK // MOE_UNIT
SORT_ROWS = 1024
KEY_W = 128
DIGIT_BITS = 6
DIGIT = 1 << DIGIT_BITS


def _swiglu_act(h, w_gu):
    gu = _dot(h, w_gu)
    de = gu.shape[1] // 2
    g = gu[:, :de]
    return g * _sigmoid(g) * gu[:, de:]


def _token_keys(cw, starts_row):
    t = cw.shape[0]
    routed = cw > 0.0
    t_i = lax.broadcasted_iota(jnp.int32, (t, t), 0)
    s_i = lax.broadcasted_iota(jnp.int32, (t, t), 1)
    rank = _dot((s_i < t_i).astype(BF16), routed.astype(BF16))
    pos = (starts_row + rank).astype(jnp.int32)
    lane = lax.broadcasted_iota(jnp.int32, cw.shape, 1)
    hi = jnp.where(routed, jnp.right_shift(pos, DIGIT_BITS), -1)
    lo = jnp.where(routed, jnp.bitwise_and(pos, DIGIT - 1), -1)
    key_hi = jnp.where(lane < N_EXPERTS, hi * DIGIT,
                       jnp.where(lane == N_EXPERTS, -DIGIT, jnp.where(lane == N_EXPERTS + 1, -1, 0)))
    key_lo = jnp.where(lane < N_EXPERTS, lo, 0)
    return jnp.concatenate([key_hi, key_lo], axis=1).astype(F32).astype(BF16)


def _segment_units(counts):
    return jnp.floor((counts + (MOE_UNIT - 1)) * (1.0 / MOE_UNIT))


def _dispatch_kernel(h_ref, cw_ref, digits_ref, xs_ref, cnt_ref):
    cw = cw_ref[...]
    t = cw.shape[0]
    routed = (cw > 0.0).astype(BF16)
    counts = _dot(jnp.ones((SUBLANES, t), BF16), routed)
    cnt_ref[...] = counts.astype(jnp.int32)
    units = _segment_units(counts)
    e_i = lax.broadcasted_iota(jnp.int32, (KEY_W, KEY_W), 0)
    f_i = lax.broadcasted_iota(jnp.int32, (KEY_W, KEY_W), 1)
    starts = _dot(units.astype(BF16), (e_i < f_i).astype(BF16)) * MOE_UNIT
    ends = starts + units * MOE_UNIT
    keys_t = _token_keys(cw, starts[:1]).astype(F32).T.astype(BF16)
    h = h_ref[...]
    used_rows = jnp.max(ends)

    def row_hits(r0, n):
        rows = slice(r0, r0 + n)
        lane = lax.broadcasted_iota(jnp.int32, (n, KEY_W), 1)
        r = (lax.broadcasted_iota(jnp.int32, (n, KEY_W), 0) + r0).astype(F32)
        in_seg = (r >= starts[:1]) & (r < ends[:1])
        rmap = jnp.where(lane < N_EXPERTS, in_seg.astype(F32), digits_ref[rows, :].astype(F32)).astype(BF16)
        return (_dot(jnp.concatenate([rmap, rmap], axis=1), keys_t) == 0.0).astype(BF16)

    def sort_rows(r0, n):
        xs_ref[r0:r0 + n, :] = _dot(row_hits(r0, n), h).astype(xs_ref.dtype)

    pieces = range(0, FULL_CHUNKS * ROW_CHUNK, SORT_ROWS)
    hits = [row_hits(r0, SORT_ROWS) for r0 in pieces]
    for r0, hit in zip(pieces, hits):
        xs_ref[r0:r0 + SORT_ROWS, :] = _dot(hit, h).astype(xs_ref.dtype)
    for c in range(FULL_CHUNKS, N_CHUNKS):
        pl.when(used_rows > c * ROW_CHUNK)(functools.partial(sort_rows, c * ROW_CHUNK, ROW_CHUNK))

        @pl.when(used_rows <= c * ROW_CHUNK)
        def _(c=c):
            xs_ref[c * ROW_CHUNK:(c + 1) * ROW_CHUNK, :] = jnp.zeros((ROW_CHUNK, D_MODEL), xs_ref.dtype)


def _dispatch(h2, cw, digits):
    n = h2.shape[0]
    n_tiles = n // MOE_TILE
    return pl.pallas_call(
        _dispatch_kernel,
        grid=(n_tiles,),
        in_specs=[pl.BlockSpec((MOE_TILE, D_MODEL), lambda i: (i, 0)),
                  pl.BlockSpec((MOE_TILE, KEY_W), lambda i: (i, 0)),
                  pl.BlockSpec(digits.shape, lambda i: (0, 0))],
        out_specs=[pl.BlockSpec((TILE_ROWS, D_MODEL), lambda i: (i, 0)),
                   pl.BlockSpec((SUBLANES, KEY_W), lambda i: (i, 0))],
        out_shape=[jax.ShapeDtypeStruct((n_tiles * TILE_ROWS, D_MODEL), BF16),
                   jax.ShapeDtypeStruct((n_tiles * SUBLANES, KEY_W), jnp.int32)],
        compiler_params=_params(("parallel",)),
        name="moe_dispatch",
    )(h2, cw, digits)


def _unit_copy(src_hbm, unit, dst, slot, pos, sem):
    return pltpu.make_async_copy(
        src_hbm.at[pl.ds(pl.multiple_of(unit * MOE_UNIT, MOE_UNIT), MOE_UNIT)],
        dst.at[slot, pl.ds(pos * MOE_UNIT, MOE_UNIT)], sem.at[slot])


def _experts_kernel(be_ref, src_ref, nb_ref, wplan_ref, xs_hbm, wgu_hbm, wdn_hbm, ys_ref, xbuf, sem,
                    wgu_f, wdn_f, wsem, wgu_b, wdn_b):
    j = pl.program_id(0)
    nb = nb_ref[0]
    n_steps = pl.num_programs(0)

    def copies(blk, slot):
        return [_unit_copy(xs_hbm, src_ref[blk * UNITS_PER_BLOCK + u], xbuf, slot, u, sem)
                for u in range(UNITS_PER_BLOCK)]

    def fetch(blk, slot):
        for cp in copies(blk, slot):
            cp.start()

    def weight_copies(expert, slot):
        return [pltpu.make_async_copy(wgu_hbm.at[expert], wgu_f.at[slot], wsem.at[slot]),
                pltpu.make_async_copy(wdn_hbm.at[expert], wdn_f.at[slot], wsem.at[slot])]

    ahead = GATHER_SLOTS - 1

    @pl.when(j == 0)
    def _():
        for a in range(ahead):
            fetch(jnp.minimum(a, nb - 1), a)
        for cp in weight_copies(be_ref[0], 0):
            cp.start()

    @pl.when((j < nb) & (wplan_ref[j] == 1))
    def _():
        slot = wplan_ref[n_steps + j]
        for cp in weight_copies(be_ref[j], slot):
            cp.wait()
        nxt = wplan_ref[2 * n_steps + j]

        @pl.when(nxt >= 0)
        def _():
            for cp in weight_copies(nxt, 1 - slot):
                cp.start()

        wgu_b[...] = wgu_f[slot].astype(BF16)
        wdn_b[...] = wdn_f[slot].astype(BF16)

    @pl.when(j < nb)
    def _():
        slot = j % GATHER_SLOTS
        for cp in copies(j, slot):
            cp.wait()
        fetch(jnp.minimum(j + ahead, nb - 1), (j + ahead) % GATHER_SLOTS)

    def ffn(slot, n_rows):
        act = _swiglu_act(xbuf[slot, :n_rows, :], wgu_b[...])
        ys_ref[:n_rows, :] = _dot(act.astype(BF16), wdn_b[...]).astype(ys_ref.dtype)

    half_full = wplan_ref[3 * n_steps + j] <= UNITS_PER_BLOCK // 2

    @pl.when((j < nb) & jnp.logical_not(half_full))
    def _():
        ffn(j % GATHER_SLOTS, MOE_BLOCK)

    @pl.when((j < nb) & half_full)
    def _():
        ffn(j % GATHER_SLOTS, MOE_BLOCK // 2)
        ys_ref[MOE_BLOCK // 2:, :] = jnp.zeros((MOE_BLOCK // 2, D_MODEL), ys_ref.dtype)

    @pl.when(j == nb - 1)
    def _():
        for a in range(1, GATHER_SLOTS):
            for cp in copies(j, (j + a) % GATHER_SLOTS):
                cp.wait()

    @pl.when(j >= nb)
    def _():
        ys_ref[...] = jnp.zeros_like(ys_ref)


def _weight_plan(block_expert, block_units, n_blocks_used):
    nb = block_expert.shape[0]
    jb = jnp.arange(nb, dtype=jnp.int32)
    used = jb < n_blocks_used[0]
    first = used & ((jb == 0) | (block_expert != jnp.roll(block_expert, 1)))
    slot = (jnp.cumsum(first.astype(jnp.int32)) - 1) % 2
    first_at = jnp.where(first, jb, nb)
    nxt_first = jnp.min(jnp.where(first_at[None, :] > jb[:, None], first_at[None, :], nb), axis=1)
    nxt_expert = jnp.sum(jnp.where(jb[None, :] == nxt_first[:, None], block_expert[None, :], 0), axis=1)
    nxt = jnp.where(nxt_first < nb, nxt_expert, -1)
    return jnp.concatenate([first.astype(jnp.int32), slot.astype(jnp.int32), nxt.astype(jnp.int32),
                            block_units.astype(jnp.int32)])


def _experts(xs, block_expert, src_units, block_units, n_blocks_used, w_gu, w_dn):
    nb_max = block_expert.shape[0]
    any_spec = pl.BlockSpec(memory_space=pl.ANY)
    grid_spec = pltpu.PrefetchScalarGridSpec(
        num_scalar_prefetch=4,
        grid=(nb_max,),
        in_specs=[any_spec, any_spec, any_spec],
        out_specs=pl.BlockSpec((MOE_BLOCK, D_MODEL), lambda j, be, src, nb, wplan: (j, 0)),
        scratch_shapes=[pltpu.VMEM((GATHER_SLOTS, MOE_BLOCK, D_MODEL), BF16),
                        pltpu.SemaphoreType.DMA((GATHER_SLOTS,)),
                        pltpu.VMEM((2, D_MODEL, 2 * D_EXPERT), F32), pltpu.VMEM((2, D_EXPERT, D_MODEL), F32),
                        pltpu.SemaphoreType.DMA((2,)),
                        pltpu.VMEM((D_MODEL, 2 * D_EXPERT), BF16), pltpu.VMEM((D_EXPERT, D_MODEL), BF16)],
    )
    return pl.pallas_call(
        _experts_kernel,
        grid_spec=grid_spec,
        out_shape=jax.ShapeDtypeStruct((nb_max * MOE_BLOCK, D_MODEL), BF16),
        compiler_params=_params(("arbitrary",)),
        name="moe_experts",
    )(block_expert, src_units, n_blocks_used, _weight_plan(block_expert, block_units, n_blocks_used), xs, w_gu,
      w_dn)


def _combine_kernel(src_ref, used_ref, ys_hbm, cw_ref, h_ref, x1_ref, gt2_ref, gpost_ref, digits_t_ref,
                    wsgu_ref, wsdn_ref, o_ref, ybuf, sem, acc_ref):
    i = pl.program_id(0)

    def copies(tile, slot, c):
        return [_unit_copy(ys_hbm, src_ref[tile * TILE_UNITS + u], ybuf, slot, u, sem)
                for u in range(c * CHUNK_UNITS, (c + 1) * CHUNK_UNITS)]

    def chunk_used(tile, c):
        return used_ref[tile] > c * CHUNK_UNITS

    def for_used_chunks(tile, fn):
        for c in range(N_CHUNKS):
            if c < FULL_CHUNKS:
                fn(c)
            else:
                pl.when(chunk_used(tile, c))(functools.partial(fn, c))

    def fetch(tile, slot):
        def start(c):
            for cp in copies(tile, slot, c):
                cp.start()

        for_used_chunks(tile, start)

    def wait_all(tile, slot):
        def wait(c):
            for cp in copies(tile, slot, c):
                cp.wait()

        for_used_chunks(tile, wait)

    @pl.when(i == 0)
    def _():
        fetch(0, 0)

    @pl.when(i + 1 < pl.num_programs(0))
    def _():
        fetch(i + 1, (i + 1) % 2)

    cw = cw_ref[...]
    t = cw.shape[0]
    routed = (cw > 0.0).astype(BF16)
    e_i = lax.broadcasted_iota(jnp.int32, (KEY_W, KEY_W), 0)
    f_i = lax.broadcasted_iota(jnp.int32, (KEY_W, KEY_W), 1)
    units = _segment_units(_dot_tn(routed, jnp.ones((t, KEY_W), BF16)))
    starts = _dot((f_i < e_i).astype(BF16), units.astype(BF16)) * MOE_UNIT
    ends = starts + units * MOE_UNIT
    units_row = _segment_units(_dot(jnp.ones((SUBLANES, t), BF16), routed))
    starts_row = _dot(units_row.astype(BF16), (e_i < f_i).astype(BF16)) * MOE_UNIT
    keys = _token_keys(cw, starts_row[:1])
    wb = cw.astype(BF16)

    f = _dot(_swiglu_act(h_ref[...], wsgu_ref[...]).astype(BF16), wsdn_ref[...])
    slot = i % 2
    reps = ROW_CHUNK // KEY_W
    starts_c = jnp.concatenate([starts] * reps, axis=1)
    ends_c = jnp.concatenate([ends] * reps, axis=1)
    sub = lax.broadcasted_iota(jnp.int32, (KEY_W, ROW_CHUNK), 0)

    wait_all(i, slot)

    def chunk_weights(c):
        rows = slice(c * ROW_CHUNK, (c + 1) * ROW_CHUNK)
        r = (lax.broadcasted_iota(jnp.int32, (KEY_W, ROW_CHUNK), 1) + c * ROW_CHUNK).astype(F32)
        in_seg = (r >= starts_c) & (r < ends_c)
        rmap_t = jnp.where(sub < N_EXPERTS, in_seg.astype(F32), digits_t_ref[:, rows].astype(F32)).astype(BF16)
        hit = _dot(keys, jnp.concatenate([rmap_t, rmap_t], axis=0)) == 0.0
        return jnp.where(hit, _dot(wb, rmap_t), 0.0).astype(BF16)

    def chunk_sum(c):
        return _dot(chunk_weights(c), ybuf[slot, c * ROW_CHUNK:(c + 1) * ROW_CHUNK, :])

    pw = jnp.concatenate([chunk_weights(c) for c in range(FULL_CHUNKS)], axis=1)
    acc_ref[...] = f + _dot(pw, ybuf[slot, :FULL_CHUNKS * ROW_CHUNK, :])
    for c in range(FULL_CHUNKS, N_CHUNKS):
        @pl.when(chunk_used(i, c))
        def _(c=c):
            acc_ref[...] += chunk_sum(c)
    o_ref[...] = x1_ref[...] + gt2_ref[...] * _rms(acc_ref[...], gpost_ref[...])


def _combine(ys, src_units, used_units, cw, h2, x1, mod3, g_post, digits_t, wsgu_b, wsdn_b, seq):
    n = h2.shape[0]
    tpb = seq // MOE_TILE
    row = lambda w: pl.BlockSpec((MOE_TILE, w), lambda i, src, used: (i, 0))
    full = lambda a: pl.BlockSpec(a.shape, lambda i, src, used: (0,) * a.ndim)
    grid_spec = pltpu.PrefetchScalarGridSpec(
        num_scalar_prefetch=2,
        grid=(n // MOE_TILE,),
        in_specs=[pl.BlockSpec(memory_space=pl.ANY), row(KEY_W), row(D_MODEL), row(D_MODEL),
                  pl.BlockSpec((None, 1, D_MODEL), lambda i, src, used: (i // tpb, 0, 5)), full(g_post),
                  full(digits_t), full(wsgu_b), full(wsdn_b)],
        out_specs=row(D_MODEL),
        scratch_shapes=[pltpu.VMEM((2, TILE_ROWS, D_MODEL), BF16), pltpu.SemaphoreType.DMA((2,)),
                        pltpu.VMEM((MOE_TILE, D_MODEL), F32)],
    )
    return pl.pallas_call(
        _combine_kernel,
        grid_spec=grid_spec,
        out_shape=jax.ShapeDtypeStruct((n, D_MODEL), F32),
        compiler_params=_params(("arbitrary",)),
        name="moe_combine",
    )(src_units, used_units, ys, cw, h2, x1, mod3, g_post, digits_t, wsgu_b, wsdn_b)


def _row_digits():
    r = np.arange(TILE_ROWS)
    d = np.zeros((TILE_ROWS, KEY_W), np.float32)
    d[:, N_EXPERTS] = r // DIGIT
    d[:, N_EXPERTS + 1] = r % DIGIT
    return jnp.asarray(d, dtype=BF16)


def _moe_plan(counts, nb_max):
    n_tiles = counts.shape[0]
    s = (counts + (MOE_UNIT - 1)) // MOE_UNIT
    local = jnp.cumsum(s, axis=1) - s
    cs = jnp.cumsum(s, axis=0)
    per_expert = cs[-1]
    padded = (per_expert + UNITS_PER_BLOCK - 1) // UNITS_PER_BLOCK * UNITS_PER_BLOCK
    g_end = jnp.cumsum(padded)
    g_start = g_end - padded
    seg_start = g_start[None, :] + cs - s
    n_blocks_used = (g_end[-1] // UNITS_PER_BLOCK).astype(jnp.int32).reshape(1)
    jb = jnp.arange(nb_max, dtype=jnp.int32)
    one_e = ((jb[:, None] >= (g_start // UNITS_PER_BLOCK)[None, :])
             & (jb[:, None] < (g_end // UNITS_PER_BLOCK)[None, :])).astype(jnp.int32)
    pick_e = lambda table: jnp.sum(one_e[:, :, None] * table.T[None, :, :], axis=1)
    block_expert = jnp.where(jb < n_blocks_used[0], jnp.sum(one_e * jnp.arange(N_EXPERTS, dtype=jnp.int32), axis=1),
                             N_EXPERTS - 1).astype(jnp.int32)
    cs_b, s_b, local_b = pick_e(cs), pick_e(s), pick_e(local)
    q = (jb * UNITS_PER_BLOCK - jnp.sum(one_e * g_start[None, :], axis=1))[:, None] \
        + jnp.arange(UNITS_PER_BLOCK, dtype=jnp.int32)[None, :]
    tile = jnp.minimum(jnp.sum(cs_b[:, None, :] <= q[:, :, None], axis=2), n_tiles - 1)
    one_t = (tile[:, :, None] == jnp.arange(n_tiles, dtype=jnp.int32)).astype(jnp.int32)
    src = tile * TILE_UNITS + q + jnp.sum(one_t * (local_b - cs_b + s_b)[:, None, :], axis=2)
    valid = q < jnp.sum(one_e * per_expert[None, :], axis=1)[:, None]
    src_units = jnp.where(valid, src, 0).astype(jnp.int32).reshape(-1)
    block_units = jnp.sum(valid, axis=1)
    u = jnp.arange(TILE_UNITS, dtype=jnp.int32)
    seg_end = local + s
    eu = jnp.minimum(jnp.sum(seg_end[:, None, :] <= u[None, :, None], axis=2), N_EXPERTS - 1)
    one_u = (eu[:, :, None] == jnp.arange(N_EXPERTS, dtype=jnp.int32)).astype(jnp.int32)
    back = u[None, :] + jnp.sum(one_u * (seg_start - local)[:, None, :], axis=2)
    back_units = jnp.where(u[None, :] < seg_end[:, -1:], back, 0).astype(jnp.int32).reshape(-1)
    return block_expert, src_units, block_units, n_blocks_used, back_units, seg_end[:, -1].astype(jnp.int32)


def _tile(n, pref):
    t = pref
    while n % t:
        t //= 2
    return t


def kernel(x, c, ctx, c_ctx, w_ada, b_ada, g_pre_mix, g_post_mix, g_pre_ffn, g_post_ffn, w_in, lb_logits, g_hgrn_out, cm_ln_g, cm_ln_b, w_spatial, b_spatial, w_branch_a, w_branch_b, w_out, w_router, b_router, w_expert_gu, w_expert_down, w_shared_gu, w_shared_down):
    B, T, D = x.shape
    L = ctx.shape[1]
    assert D == D_MODEL and w_ada.shape[0] == 1 and T % SCAN_CHUNK == 0 and L % SCAN_CHUNK == 0
    assert T % MOE_TILE == 0 and MOE_TILE % CM_CHUNK == 0
    l = 0
    row = lambda a: a[l].reshape(1, -1)

    n_rows = -(-(B + 1) // BF16_SUBLANES) * BF16_SUBLANES
    cs = jnp.zeros((n_rows, D), F32).at[:B].set(c).at[B].set(c_ctx)
    mod3 = _ada_mod(cs, w_ada[l], row(b_ada)).reshape(n_rows, 1, 6 * D)

    w_in_b = w_in[l].astype(BF16)
    lbl = lb_logits[:, l:l + 2].reshape(4, HG_W)
    x2 = x.reshape(B * T, D)
    q, k2, lf2, v, sg, u, vn, sga, sgb = _proj_lat(
        x2, mod3, row(g_pre_mix), w_in_b, lbl, row(cm_ln_g), row(cm_ln_b), T, _tile(T, PROJ_TILE))
    kc2, lfc2, vc = _proj_ctx(ctx.reshape(B * L, D), mod3, B, row(g_pre_mix), w_in_b[:, HG_W:4 * HG_W], lbl,
                              _tile(B * L, CTX_TILE))

    o_f, o_b = _hgrn_scan(q, k2, lf2, v, kc2, lfc2, vc, B, T, L)

    bs_full = jnp.repeat(b_spatial[l], CM_W // CM_GROUPS, axis=1)
    x1, h2, logits = _merge(
        o_f, o_b, sg, u, vn, sga, sgb, x2, mod3, row(g_hgrn_out), w_spatial[l].astype(BF16), bs_full,
        w_branch_a[l].astype(BF16), w_branch_b[l].astype(BF16), w_out[l].astype(BF16), row(g_post_mix),
        row(g_pre_ffn), jnp.pad(w_router[l].T, ((0, KEY_W - N_EXPERTS), (0, 0))).astype(BF16), T, _tile(T, MERGE_TILE))

    cw = _router(logits, jnp.broadcast_to(b_router[l][:, None], (N_EXPERTS, LANES)), _tile(B * T, ROUTER_TILE))

    n_tok = B * T
    n_tiles = n_tok // MOE_TILE
    digits = _row_digits()
    xs, cnt = _dispatch(h2, cw, digits)
    counts = cnt.reshape(n_tiles, SUBLANES, KEY_W)[:, 0, :N_EXPERTS]
    max_units = (n_tok * TOP_K + n_tiles * N_EXPERTS * (MOE_UNIT - 1)) // MOE_UNIT + N_EXPERTS * (UNITS_PER_BLOCK - 1)
    nb_max = -(-max_units // UNITS_PER_BLOCK)
    block_expert, src_units, block_units, n_blocks_used, back_units, tile_units = _moe_plan(counts, nb_max)
    wsgu_b, wsdn_b = w_shared_gu[l].astype(BF16), w_shared_down[l].astype(BF16)

    nb_min = -(-(n_tok * TOP_K) // MOE_BLOCK)
    grids = sorted({min(nb_max, g) for g in range(nb_min + EXPERT_GRID_STEP, nb_max + EXPERT_GRID_STEP,
                                                  EXPERT_GRID_STEP)})

    def experts_and_combine(nb_grid):
        def run():
            ys = _experts(xs, block_expert[:nb_grid], src_units[:nb_grid * UNITS_PER_BLOCK], block_units[:nb_grid],
                          n_blocks_used, w_expert_gu[l], w_expert_down[l])
            return _combine(ys, back_units, tile_units, cw, h2, x1, mod3, row(g_post_ffn), digits.T,
                            wsgu_b, wsdn_b, T)
        return run

    which = jnp.sum(n_blocks_used[0] > jnp.asarray(grids[:-1], jnp.int32))
    out = lax.switch(which, [experts_and_combine(g) for g in grids])
    return out.reshape(B, T, D)
```

```python
import functools

import numpy as np
import jax
import jax.numpy as jnp
from jax import lax
from jax.experimental import pallas as pl
from jax.experimental.pallas import tpu as pltpu

F32 = jnp.float32
BF16 = jnp.bfloat16

D_MODEL = 1024
EPS = 1e-6
HG_HEADS = 4
HG_DK = 128
HG_W = HG_HEADS * HG_DK
CM_W = 512
CM_CHUNK = 128
CM_GROUPS = 4
N_EXPERTS = 64
TOP_K = 8
N_GROUPS = 8
TOPK_GROUPS = 4
D_EXPERT = 256
ROUTED_SCALE = 2.5
LANES = 128
SUBLANES = 8
BF16_SUBLANES = 16
SCAN_CHUNK = 128
SUB = 16
N_LEVELS = (SCAN_CHUNK // SUB).bit_length()
PROJ_TILE, PROJ_ROWS = 512, 256
CTX_TILE = 256
MERGE_TILE, MERGE_ROWS = 1024, 512
ROUTER_TILE = 512
V7X_VMEM_BYTES = 64 * 1024 * 1024
VMEM_LIMIT = V7X_VMEM_BYTES - 8 * 1024 * 1024


def _params(sem):
    return pltpu.CompilerParams(dimension_semantics=sem, vmem_limit_bytes=VMEM_LIMIT)


def _dot(a, b):
    return jnp.dot(a, b, preferred_element_type=F32)


def _dot_nt(a, b):
    return lax.dot_general(a, b, (((1,), (1,)), ((), ())), preferred_element_type=F32)


def _dot_tn(a, b):
    return lax.dot_general(a, b, (((0,), (0,)), ((), ())), preferred_element_type=F32)


def _sigmoid(x):
    return 0.5 * jnp.tanh(0.5 * x) + 0.5


def _rms(x, g):
    return x * lax.rsqrt(jnp.mean(x * x, axis=-1, keepdims=True) + EPS) * g


def _ada_kernel(c_ref, w_ref, b_ref, o_ref):
    c = c_ref[...]
    s = c * _sigmoid(c)
    o_ref[...] = _dot(s.astype(BF16), w_ref[...].astype(BF16)) + b_ref[...]


def _ada_mod(cs, w_ada, b_ada):
    rows = cs.shape[0]
    n_out = w_ada.shape[1]
    return pl.pallas_call(
        _ada_kernel,
        grid=(n_out // D_MODEL,),
        in_specs=[
            pl.BlockSpec((rows, D_MODEL), lambda j: (0, 0)),
            pl.BlockSpec((D_MODEL, D_MODEL), lambda j: (0, j)),
            pl.BlockSpec((1, D_MODEL), lambda j: (0, j)),
        ],
        out_specs=pl.BlockSpec((rows, D_MODEL), lambda j: (0, j)),
        out_shape=jax.ShapeDtypeStruct((rows, n_out), F32),
        compiler_params=_params(("parallel",)),
        name="ada_mod",
    )(cs, w_ada, b_ada)


def _lower_bounds(lbl):
    out = []
    for d in range(2):
        l0, l1 = lbl[2 * d:2 * d + 1], lbl[2 * d + 1:2 * d + 2]
        m = jnp.maximum(l0, l1)
        e0, e1 = jnp.exp(l0 - m), jnp.exp(l1 - m)
        out.append(e0 / (e0 + e1))
    return out


def _prenorm(x_ref, sh_ref, sc_ref, g_ref, rows=slice(None)):
    return (_rms(x_ref[rows, :], g_ref[...]) * (1.0 + sc_ref[...]) + sh_ref[...]).astype(BF16)


def _gates(z, lb, k_ref, lf_ref, d, rows=slice(None)):
    half_t = 0.5 * jnp.tanh(0.5 * z)
    k_ref[d, rows, :] = ((1.0 - lb) * (0.5 - half_t)).astype(k_ref.dtype)
    lf_ref[d, rows, :] = jnp.log2(lb + (1.0 - lb) * (0.5 + half_t))


def _proj_lat_kernel(x_ref, sh_ref, sc_ref, g_ref, w_ref, lbl_ref, lng_ref, lnb_ref,
                     q_ref, k_ref, lf_ref, v_ref, sg_ref, u_ref, vn_ref, sga_ref, sgb_ref):
    tm = x_ref.shape[0]
    group = min(PROJ_ROWS, tm)
    groups = [slice(r0, r0 + group) for r0 in range(0, tm, group)]
    hbs = [_prenorm(x_ref, sh_ref, sc_ref, g_ref, rows) for rows in groups]
    lbs = _lower_bounds(lbl_ref[...])

    def columns(lo, width, epilogue):
        zs = [_dot(hb, w_ref[:, lo:lo + width]) for hb in hbs]
        for rows, z in zip(groups, zs):
            epilogue(rows, z)

    def silu_to(ref):
        def epilogue(rows, z):
            ref[rows, :] = (z * _sigmoid(z)).astype(ref.dtype)
        return epilogue

    def sigmoid_to(ref):
        def epilogue(rows, z):
            ref[rows, :] = _sigmoid(z).astype(ref.dtype)
        return epilogue

    def gelu_to_u(rows, z):
        u_ref[rows, :] = jax.nn.gelu(z).astype(u_ref.dtype)

    def gelu_layernorm_to_vn(rows, z):
        vv = jax.nn.gelu(z)
        vc = vv - jnp.mean(vv, axis=-1, keepdims=True)
        vn = vc * lax.rsqrt(jnp.mean(vc * vc, axis=-1, keepdims=True) + EPS)
        vn_ref[rows, :] = (vn * lng_ref[...] + lnb_ref[...]).astype(vn_ref.dtype)

    def cast_to_v(rows, z):
        v_ref[rows, :] = z.astype(v_ref.dtype)

    columns(0, HG_W, silu_to(q_ref))
    for d in range(2):
        columns((1 + d) * HG_W, HG_W, lambda rows, z, d=d: _gates(z, lbs[d], k_ref, lf_ref, d, rows))
    columns(4 * HG_W, HG_W, silu_to(sg_ref))
    columns(5 * HG_W, CM_W, gelu_to_u)
    columns(5 * HG_W + CM_W, CM_W, gelu_layernorm_to_vn)
    base = 5 * HG_W + 2 * CM_W
    columns(base, D_MODEL, sigmoid_to(sga_ref))
    columns(base + D_MODEL, D_MODEL, sigmoid_to(sgb_ref))
    columns(3 * HG_W, HG_W, cast_to_v)


def _mod_spec(rows_per_batch_tiles, col):
    return pl.BlockSpec((None, 1, D_MODEL), lambda i: (i // rows_per_batch_tiles, 0, col))


def _proj_lat(x2, mod3, g_pre, w_in_b, lbl, ln_g, ln_b, seq, tm):
    n = x2.shape[0]
    tpb = seq // tm
    row = lambda w: pl.BlockSpec((tm, w), lambda i: (i, 0))
    row2 = pl.BlockSpec((2, tm, HG_W), lambda i: (0, i, 0))
    full = lambda a: pl.BlockSpec(a.shape, lambda i: (0,) * a.ndim)
    outs = [
        (row(HG_W), jax.ShapeDtypeStruct((n, HG_W), BF16)),
        (row2, jax.ShapeDtypeStruct((2, n, HG_W), BF16)),
        (row2, jax.ShapeDtypeStruct((2, n, HG_W), F32)),
        (row(HG_W), jax.ShapeDtypeStruct((n, HG_W), BF16)),
        (row(HG_W), jax.ShapeDtypeStruct((n, HG_W), BF16)),
        (row(CM_W), jax.ShapeDtypeStruct((n, CM_W), BF16)),
        (row(CM_W), jax.ShapeDtypeStruct((n, CM_W), BF16)),
        (row(D_MODEL), jax.ShapeDtypeStruct((n, D_MODEL), BF16)),
        (row(D_MODEL), jax.ShapeDtypeStruct((n, D_MODEL), BF16)),
    ]
    return pl.pallas_call(
        _proj_lat_kernel,
        grid=(n // tm,),
        in_specs=[row(D_MODEL), _mod_spec(tpb, 0), _mod_spec(tpb, 1), full(g_pre), full(w_in_b),
                  full(lbl), full(ln_g), full(ln_b)],
        out_specs=[o[0] for o in outs],
        out_shape=[o[1] for o in outs],
        compiler_params=_params(("parallel",)),
        name="proj_lat",
    )(x2, mod3, mod3, g_pre, w_in_b, lbl, ln_g, ln_b)


def _proj_ctx_kernel(x_ref, sh_ref, sc_ref, g_ref, w_ref, lbl_ref, k_ref, lf_ref, v_ref):
    hb = _prenorm(x_ref, sh_ref, sc_ref, g_ref)
    lbs = _lower_bounds(lbl_ref[...])
    for d in range(2):
        _gates(_dot(hb, w_ref[:, d * HG_W:(d + 1) * HG_W]), lbs[d], k_ref, lf_ref, d)
    v_ref[...] = _dot(hb, w_ref[:, 2 * HG_W:3 * HG_W]).astype(v_ref.dtype)


def _proj_ctx(c2, mod3, ctx_row, g_pre, w_ctx_b, lbl, tm):
    n = c2.shape[0]
    row = lambda w: pl.BlockSpec((tm, w), lambda i: (i, 0))
    row2 = pl.BlockSpec((2, tm, HG_W), lambda i: (0, i, 0))
    full = lambda a: pl.BlockSpec(a.shape, lambda i: (0,) * a.ndim)
    mod = lambda col: pl.BlockSpec((None, 1, D_MODEL), lambda i: (ctx_row, 0, col))
    return pl.pallas_call(
        _proj_ctx_kernel,
        grid=(n // tm,),
        in_specs=[row(D_MODEL), mod(0), mod(1), full(g_pre), full(w_ctx_b), full(lbl)],
        out_specs=[row2, row2, row(HG_W)],
        out_shape=[jax.ShapeDtypeStruct((2, n, HG_W), BF16), jax.ShapeDtypeStruct((2, n, HG_W), F32),
                   jax.ShapeDtypeStruct((n, HG_W), BF16)],
        compiler_params=_params(("parallel",)),
        name="proj_ctx",
    )(c2, mod3, mod3, g_pre, w_ctx_b, lbl)


def _scan_tables():
    C = SCAN_CHUNK
    t = np.arange(C)
    lmats, lvls = [], []
    for d in range(2):
        p = t if d == 0 else C - 1 - t
        pt, ps = p[:, None], p[None, :]
        lmat = (ps <= pt).astype(np.float32)
        lmats.append(np.concatenate([lmat, lmat], axis=1))
        lvl = np.full((C, C), -1, np.int32)
        lvl[(pt // SUB == ps // SUB) & (ps <= pt)] = 0
        half, idx = SUB, 1
        while half < C:
            span = 2 * half
            lvl[(pt // span == ps // span) & ((pt // half) % 2 == 1) & ((ps // half) % 2 == 0)] = idx
            half, idx = span, idx + 1
        lvls.append(lvl)
    return jnp.asarray(np.stack(lmats), dtype=BF16), jnp.asarray(np.stack(lvls))


def _scan_cumsum(d, slot, lmat_ref, lf_ref, b_scr):
    lf = lf_ref[...]
    hi = lf.astype(BF16)
    lo = (lf - hi.astype(F32)).astype(BF16)
    b_scr[slot] = _dot(lmat_ref[d], jnp.concatenate([hi, lo], axis=0))


def _scan_head(d, slot, h, masks, k_ref, v_ref, st_ref, b_scr, q_ref=None, o_ref=None):
    C = SCAN_CHUNK
    sl = slice(h * HG_DK, (h + 1) * HG_DK)
    b = b_scr[slot, :, sl]

    def row(i):
        return b_scr[slot, pl.ds(i, 1), sl]

    b_last = row(C - 1 if d == 0 else 0)
    k = k_ref[:, sl]
    v = v_ref[:, sl]
    st = st_ref[slot, :, sl]
    if q_ref is not None:
        q = q_ref[:, sl]
        e0 = jnp.concatenate([b[m * SUB:(m + 1) * SUB] - row(m * SUB + SUB // 2 - 1 + d)
                              for m in range(C // SUB)], axis=0)
        factors = [(jnp.exp2(e0).astype(BF16), jnp.exp2(-e0).astype(BF16))]
        half = SUB
        while half < C:
            span = 2 * half
            pieces = []
            for m in range(C // span):
                ref = row(m * span + half - 1 + d)
                lo_half, hi_half = b[m * span:m * span + half], b[m * span + half:(m + 1) * span]
                pieces += [ref - lo_half, hi_half - ref] if d == 0 else [lo_half - ref, ref - hi_half]
            w = jnp.exp2(jnp.concatenate(pieces, axis=0)).astype(BF16)
            factors.append((w, w))
            half = span
        a = jnp.zeros((C, C), F32)
        for (wq, wk), mask in zip(factors, masks):
            a = jnp.where(mask, _dot_nt(q * wq, k * wk), a)
        qhat = q * jnp.exp2(b).astype(BF16)
        o = _dot(a.astype(BF16), v) + _dot_nt(qhat, st.astype(BF16))
        o_ref[:, sl] = o.astype(o_ref.dtype)
    khat = k * jnp.exp2(b_last - b).astype(BF16)
    st_ref[slot, :, sl] = jnp.exp2(b_last) * st + _dot_tn(v, khat)


def _scan_kernel(n_ctx_steps, lmat_ref, lvl_ref, q_f, k_f, lf_f, v_f, q_b, k_b, lf_b, v_b,
                 kc_f, lfc_f, vc_f, kc_b, lfc_b, vc_b, o_f, o_b, st_ref, b_scr):
    s = pl.program_id(1)
    n_seq = q_f.shape[0]

    @pl.when(s == 0)
    def _():
        st_ref[...] = jnp.zeros_like(st_ref)

    def step(refs, readout):
        chains = [(d, 2 * i + d, [r.at[i] if r is not None else None for r in refs[d]])
                  for i in range(n_seq) for d in range(2)]
        for d, slot, (k, lf, v, q, o) in chains:
            _scan_cumsum(d, slot, lmat_ref, lf, b_scr)
        masks = {d: [lvl_ref[d] == i for i in range(N_LEVELS)] for d in range(2)} if readout else {0: None, 1: None}
        for h in range(HG_HEADS):
            for d, slot, (k, lf, v, q, o) in chains:
                _scan_head(d, slot, h, masks[d], k, v, st_ref, b_scr, q, o)

    @pl.when(s < n_ctx_steps)
    def _():
        step([(kc_f, lfc_f, vc_f, None, None), (kc_b, lfc_b, vc_b, None, None)], False)

    @pl.when(s >= n_ctx_steps)
    def _():
        step([(k_f, lf_f, v_f, q_f, o_f), (k_b, lf_b, v_b, q_b, o_b)], True)


def _hgrn_scan(q, k2, lf2, v, kc2, lfc2, vc, batch, seq, ctx_len):
    C = SCAN_CHUNK
    n_lat, n_ctx = seq // C, ctx_len // C
    n_seq = 4 if batch % 4 == 0 else 2 if batch % 2 == 0 else 1
    lmat, lvl = _scan_tables()
    q, v = (a.reshape(batch, seq, HG_W) for a in (q, v))
    k2, lf2 = (a.reshape(2, batch, seq, HG_W) for a in (k2, lf2))
    vc = vc.reshape(batch, ctx_len, HG_W)
    kc2, lfc2 = (a.reshape(2, batch, ctx_len, HG_W) for a in (kc2, lfc2))

    def lat_blk(d):
        def blk(s):
            j = jnp.maximum(s - n_ctx, 0)
            return j if d == 0 else n_lat - 1 - j
        return blk

    def ctx_blk(d):
        def blk(s):
            i = jnp.minimum(s, n_ctx - 1)
            return i if d == 0 else n_ctx - 1 - i
        return blk

    def plain(blk):
        return pl.BlockSpec((n_seq, C, HG_W), lambda b, s: (b, blk(s), 0))

    def specs(blk_of, with_q):
        out = []
        for d in range(2):
            blk = blk_of(d)
            per_dir = pl.BlockSpec((None, n_seq, C, HG_W), lambda b, s, blk=blk, d=d: (d, b, blk(s), 0))
            out += ([plain(blk)] if with_q else []) + [per_dir, per_dir, plain(blk)]
        return out

    full = lambda a: pl.BlockSpec(a.shape, lambda b, s: (0,) * a.ndim)
    o_shape = jax.ShapeDtypeStruct((batch, seq, HG_W), BF16)
    n_chains = 2 * n_seq
    o_f, o_b = pl.pallas_call(
        functools.partial(_scan_kernel, n_ctx),
        grid=(batch // n_seq, n_ctx + n_lat),
        in_specs=[full(lmat), full(lvl)] + specs(lat_blk, True) + specs(ctx_blk, False),
        out_specs=[plain(lat_blk(d)) for d in range(2)],
        out_shape=[o_shape, o_shape],
        scratch_shapes=[pltpu.VMEM((n_chains, HG_DK, HG_W), F32), pltpu.VMEM((n_chains, C, HG_W), F32)],
        compiler_params=_params(("parallel", "arbitrary")),
        name="hgrn_scan",
    )(lmat, lvl, q, k2, lf2, v, q, k2, lf2, v, kc2, lfc2, vc, kc2, lfc2, vc)
    return o_f.reshape(batch * seq, HG_W), o_b.reshape(batch * seq, HG_W)


def _merge_kernel(of_ref, ob_ref, sg_ref, u_ref, vn_ref, sga_ref, sgb_ref, x_ref, gt1_ref, sh2_ref, sc2_ref,
                  gout_ref, ws_ref, bs_ref, wa_ref, wb_ref, wo_ref, gpost_ref, gffn_ref, wr_ref,
                  x1_ref, h2_ref, lg_ref):
    tm = x_ref.shape[0]
    gout = gout_ref[...]
    gw = CM_W // CM_GROUPS
    group = min(MERGE_ROWS, tm)
    groups = [slice(r0, r0 + group) for r0 in range(0, tm, group)]

    def branch_inputs(rows):
        o = of_ref[rows, :].astype(F32) + ob_ref[rows, :].astype(F32)
        sg = sg_ref[rows, :].astype(F32)
        a = jnp.concatenate(
            [_rms(o[:, h * HG_DK:(h + 1) * HG_DK], gout) * sg[:, h * HG_DK:(h + 1) * HG_DK]
             for h in range(HG_HEADS)], axis=1).astype(BF16)
        vn = vn_ref[rows, :]
        z = jnp.concatenate(
            [jnp.concatenate([_dot(ws_ref[g], vn[c * CM_CHUNK:(c + 1) * CM_CHUNK, g * gw:(g + 1) * gw])
                              for g in range(CM_GROUPS)], axis=1) + bs_ref[...]
             for c in range(group // CM_CHUNK)], axis=0)
        return a, (u_ref[rows, :].astype(F32) * z).astype(BF16)

    ab = [branch_inputs(rows) for rows in groups]
    ys = [(sga_ref[rows, :].astype(F32) * _dot(a, wa_ref[...])
           + sgb_ref[rows, :].astype(F32) * _dot(bm, wb_ref[...])).astype(BF16)
          for rows, (a, bm) in zip(groups, ab)]
    yos = [_dot(y, wo_ref[...]) for y in ys]
    for rows, yo in zip(groups, yos):
        x1 = x_ref[rows, :] + gt1_ref[...] * _rms(yo, gpost_ref[...])
        x1_ref[rows, :] = x1
        h2 = (_rms(x1, gffn_ref[...]) * (1.0 + sc2_ref[...]) + sh2_ref[...]).astype(BF16)
        h2_ref[rows, :] = h2
        lg_ref[:, rows] = _dot_nt(wr_ref[...], h2)


def _merge(o_f, o_b, sg, u, vn, sga, sgb, x2, mod3, g_out, ws_b, bs_full, wa_b, wb_b, wo_b, g_post, g_ffn, wr_b,
           seq, tm):
    n = x2.shape[0]
    tpb = seq // tm
    row = lambda w: pl.BlockSpec((tm, w), lambda i: (i, 0))
    full = lambda a: pl.BlockSpec(a.shape, lambda i: (0,) * a.ndim)
    return pl.pallas_call(
        _merge_kernel,
        grid=(n // tm,),
        in_specs=[row(HG_W), row(HG_W), row(HG_W), row(CM_W), row(CM_W),
                  row(D_MODEL), row(D_MODEL), row(D_MODEL), _mod_spec(tpb, 2), _mod_spec(tpb, 3),
                  _mod_spec(tpb, 4), full(g_out), full(ws_b), full(bs_full), full(wa_b), full(wb_b),
                  full(wo_b), full(g_post), full(g_ffn), full(wr_b)],
        out_specs=[row(D_MODEL), row(D_MODEL), pl.BlockSpec((wr_b.shape[0], tm), lambda i: (0, i))],
        out_shape=[jax.ShapeDtypeStruct((n, D_MODEL), F32), jax.ShapeDtypeStruct((n, D_MODEL), BF16),
                   jax.ShapeDtypeStruct((wr_b.shape[0], n), F32)],
        compiler_params=_params(("parallel",)),
        name="merge",
    )(o_f, o_b, sg, u, vn, sga, sgb, x2, mod3, mod3, mod3, g_out, ws_b, bs_full, wa_b, wb_b, wo_b, g_post, g_ffn,
      wr_b)


def _router_kernel(lg_ref, br_ref, w_ref):
    tm = lg_ref.shape[1]
    gsz = N_EXPERTS // N_GROUPS
    scores = _sigmoid(lg_ref[:N_EXPERTS, :])
    sel = scores + jnp.concatenate([br_ref[...]] * (tm // br_ref.shape[1]), axis=1)
    neg = -jnp.inf

    def first_max(x, ids, sentinel, axis):
        m = jnp.max(x, axis=axis, keepdims=True)
        return m, jnp.min(jnp.where(x == m, ids, sentinel), axis=axis, keepdims=True)

    sel3 = sel.reshape(N_GROUPS, gsz, tm)
    j3 = lax.broadcasted_iota(jnp.int32, sel3.shape, 1)
    m1, i1 = first_max(sel3, j3, gsz, 1)
    gscore = m1 + jnp.max(jnp.where(j3 == i1, neg, sel3), axis=1, keepdims=True)
    g3 = lax.broadcasted_iota(jnp.int32, gscore.shape, 0)
    keep = jnp.zeros(gscore.shape, F32)
    for _ in range(TOPK_GROUPS):
        _, gi = first_max(gscore, g3, N_GROUPS, 0)
        keep = jnp.where(g3 == gi, 1.0, keep)
        gscore = jnp.where(g3 == gi, neg, gscore)
    x = jnp.where(keep > 0.0, sel3, neg).reshape(N_EXPERTS, tm)
    e_i = lax.broadcasted_iota(jnp.int32, x.shape, 0)
    w = jnp.zeros(x.shape, F32)
    for _ in range(TOP_K):
        _, ei = first_max(x, e_i, N_EXPERTS, 0)
        w = jnp.where(e_i == ei, scores, w)
        x = jnp.where(e_i == ei, neg, x)
    w = w / jnp.sum(w, axis=0, keepdims=True) * ROUTED_SCALE
    w_ref[...] = jnp.concatenate([w, jnp.zeros_like(w)], axis=0).T


def _router(logits_t, b_router_cols, tm):
    rows, n = logits_t.shape
    return pl.pallas_call(
        _router_kernel,
        grid=(n // tm,),
        in_specs=[pl.BlockSpec((rows, tm), lambda i: (0, i)),
                  pl.BlockSpec(b_router_cols.shape, lambda i: (0, 0))],
        out_specs=pl.BlockSpec((tm, rows), lambda i: (i, 0)),
        out_shape=jax.ShapeDtypeStruct((n, rows), F32),
        compiler_params=_params(("parallel",)),
        name="router",
    )(logits_t, b_router_cols)


MOE_TILE = 256
MOE_UNIT = BF16_SUBLANES
MOE_BLOCK = 512
GATHER_SLOTS = 3
EXPERT_GRID_STEP = 24
UNITS_PER_BLOCK = MOE_BLOCK // MOE_UNIT
ROW_CHUNK = 512
TILE_ROWS = -(-(MOE_TILE * TOP_K + N_EXPERTS * (MOE_UNIT - 1)) // ROW_CHUNK) * ROW_CHUNK
TILE_UNITS = TILE_ROWS // MOE_UNIT
N_CHUNKS = TILE_ROWS // ROW_CHUNK
FULL_CHUNKS = MOE_TILE * TOP_K // ROW_CHUNK
EAGER_CHUNKS = min(FULL_CHUNKS + 1, N_CHUNKS)
CHUNK_UNITS = ROW_CHUNK // MOE_UNIT
SORT_ROWS = 1024
KEY_W = 128
DIGIT_BITS = 6
DIGIT = 1 << DIGIT_BITS


def _swiglu_act(h, w_gu):
    gu = _dot(h, w_gu)
    de = gu.shape[1] // 2
    g = gu[:, :de]
    return g * _sigmoid(g) * gu[:, de:]


def _token_keys(cw, starts_row):
    t = cw.shape[0]
    routed = cw > 0.0
    t_i = lax.broadcasted_iota(jnp.int32, (t, t), 0)
    s_i = lax.broadcasted_iota(jnp.int32, (t, t), 1)
    rank = _dot((s_i < t_i).astype(BF16), routed.astype(BF16))
    pos = (starts_row + rank).astype(jnp.int32)
    lane = lax.broadcasted_iota(jnp.int32, cw.shape, 1)
    hi = jnp.where(routed, jnp.right_shift(pos, DIGIT_BITS), -1)
    lo = jnp.where(routed, jnp.bitwise_and(pos, DIGIT - 1), -1)
    key_hi = jnp.where(lane < N_EXPERTS, hi * DIGIT,
                       jnp.where(lane == N_EXPERTS, -DIGIT, jnp.where(lane == N_EXPERTS + 1, -1, 0)))
    key_lo = jnp.where(lane < N_EXPERTS, lo, 0)
    return jnp.concatenate([key_hi, key_lo], axis=1).astype(F32).astype(BF16)


def _segment_units(counts):
    return jnp.floor((counts + (MOE_UNIT - 1)) * (1.0 / MOE_UNIT))


def _dispatch_kernel(h_ref, cw_ref, digits_ref, xs_ref, cnt_ref):
    cw = cw_ref[...]
    t = cw.shape[0]
    routed = (cw > 0.0).astype(BF16)
    counts = _dot(jnp.ones((SUBLANES, t), BF16), routed)
    cnt_ref[...] = counts.astype(jnp.int32)
    units = _segment_units(counts)
    e_i = lax.broadcasted_iota(jnp.int32, (KEY_W, KEY_W), 0)
    f_i = lax.broadcasted_iota(jnp.int32, (KEY_W, KEY_W), 1)
    starts = _dot(units.astype(BF16), (e_i < f_i).astype(BF16)) * MOE_UNIT
    ends = starts + units * MOE_UNIT
    keys_t = _token_keys(cw, starts[:1]).astype(F32).T.astype(BF16)
    h = h_ref[...]
    used_rows = jnp.max(ends)

    def row_hits(r0, n):
        rows = slice(r0, r0 + n)
        lane = lax.broadcasted_iota(jnp.int32, (n, KEY_W), 1)
        r = (lax.broadcasted_iota(jnp.int32, (n, KEY_W), 0) + r0).astype(F32)
        in_seg = (r >= starts[:1]) & (r < ends[:1])
        rmap = jnp.where(lane < N_EXPERTS, in_seg.astype(F32), digits_ref[rows, :].astype(F32)).astype(BF16)
        return (_dot(jnp.concatenate([rmap, rmap], axis=1), keys_t) == 0.0).astype(BF16)

    def sort_rows(r0, n):
        xs_ref[r0:r0 + n, :] = _dot(row_hits(r0, n), h).astype(xs_ref.dtype)

    eager_rows = EAGER_CHUNKS * ROW_CHUNK
    pieces = [(r0, min(SORT_ROWS, eager_rows - r0)) for r0 in range(0, eager_rows, SORT_ROWS)]
    hits = [row_hits(r0, n) for r0, n in pieces]
    for (r0, n), hit in zip(pieces, hits):
        xs_ref[r0:r0 + n, :] = _dot(hit, h).astype(xs_ref.dtype)
    for c in range(EAGER_CHUNKS, N_CHUNKS):
        pl.when(used_rows > c * ROW_CHUNK)(functools.partial(sort_rows, c * ROW_CHUNK, ROW_CHUNK))

        @pl.when(used_rows <= c * ROW_CHUNK)
        def _(c=c):
            xs_ref[c * ROW_CHUNK:(c + 1) * ROW_CHUNK, :] = jnp.zeros((ROW_CHUNK, D_MODEL), xs_ref.dtype)


def _dispatch(h2, cw, digits):
    n = h2.shape[0]
    n_tiles = n // MOE_TILE
    return pl.pallas_call(
        _dispatch_kernel,
        grid=(n_tiles,),
        in_specs=[pl.BlockSpec((MOE_TILE, D_MODEL), lambda i: (i, 0)),
                  pl.BlockSpec((MOE_TILE, KEY_W), lambda i: (i, 0)),
                  pl.BlockSpec(digits.shape, lambda i: (0, 0))],
        out_specs=[pl.BlockSpec((TILE_ROWS, D_MODEL), lambda i: (i, 0)),
                   pl.BlockSpec((SUBLANES, KEY_W), lambda i: (i, 0))],
        out_shape=[jax.ShapeDtypeStruct((n_tiles * TILE_ROWS, D_MODEL), BF16),
                   jax.ShapeDtypeStruct((n_tiles * SUBLANES, KEY_W), jnp.int32)],
        compiler_params=_params(("parallel",)),
        name="moe_dispatch",
    )(h2, cw, digits)


def _unit_copy(src_hbm, unit, dst, slot, pos, sem):
    return pltpu.make_async_copy(
        src_hbm.at[pl.ds(pl.multiple_of(unit * MOE_UNIT, MOE_UNIT), MOE_UNIT)],
        dst.at[slot, pl.ds(pos * MOE_UNIT, MOE_UNIT)], sem.at[slot])


def _experts_kernel(be_ref, src_ref, nb_ref, wplan_ref, xs_hbm, wgu_hbm, wdn_hbm, ys_ref, xbuf, sem,
                    wgu_f, wdn_f, wsem, wgu_b, wdn_b):
    j = pl.program_id(0)
    nb = nb_ref[0]
    n_steps = pl.num_programs(0)

    def copies(blk, slot):
        return [_unit_copy(xs_hbm, src_ref[blk * UNITS_PER_BLOCK + u], xbuf, slot, u, sem)
                for u in range(UNITS_PER_BLOCK)]

    def fetch(blk, slot):
        for cp in copies(blk, slot):
            cp.start()

    def weight_copies(expert, slot):
        return [pltpu.make_async_copy(wgu_hbm.at[expert], wgu_f.at[slot], wsem.at[slot]),
                pltpu.make_async_copy(wdn_hbm.at[expert], wdn_f.at[slot], wsem.at[slot])]

    ahead = GATHER_SLOTS - 1

    @pl.when(j == 0)
    def _():
        for a in range(ahead):
            fetch(jnp.minimum(a, nb - 1), a)
        for cp in weight_copies(be_ref[0], 0):
            cp.start()

    @pl.when((j < nb) & (wplan_ref[j] == 1))
    def _():
        slot = wplan_ref[n_steps + j]
        for cp in weight_copies(be_ref[j], slot):
            cp.wait()
        nxt = wplan_ref[2 * n_steps + j]

        @pl.when(nxt >= 0)
        def _():
            for cp in weight_copies(nxt, 1 - slot):
                cp.start()

        wgu_b[...] = wgu_f[slot].astype(BF16)
        wdn_b[...] = wdn_f[slot].astype(BF16)

    @pl.when(j < nb)
    def _():
        slot = j % GATHER_SLOTS
        for cp in copies(j, slot):
            cp.wait()
        fetch(jnp.minimum(j + ahead, nb - 1), (j + ahead) % GATHER_SLOTS)

    def ffn(slot, n_rows):
        act = _swiglu_act(xbuf[slot, :n_rows, :], wgu_b[...])
        ys_ref[:n_rows, :] = _dot(act.astype(BF16), wdn_b[...]).astype(ys_ref.dtype)

    half_full = wplan_ref[3 * n_steps + j] <= UNITS_PER_BLOCK // 2

    @pl.when((j < nb) & jnp.logical_not(half_full))
    def _():
        ffn(j % GATHER_SLOTS, MOE_BLOCK)

    @pl.when((j < nb) & half_full)
    def _():
        ffn(j % GATHER_SLOTS, MOE_BLOCK // 2)
        ys_ref[MOE_BLOCK // 2:, :] = jnp.zeros((MOE_BLOCK // 2, D_MODEL), ys_ref.dtype)

    @pl.when(j == nb - 1)
    def _():
        for a in range(1, GATHER_SLOTS):
            for cp in copies(j, (j + a) % GATHER_SLOTS):
                cp.wait()

    @pl.when(j >= nb)
    def _():
        ys_ref[...] = jnp.zeros_like(ys_ref)


def _weight_plan(block_expert, block_units, n_blocks_used):
    nb = block_expert.shape[0]
    jb = jnp.arange(nb, dtype=jnp.int32)
    used = jb < n_blocks_used[0]
    first = used & ((jb == 0) | (block_expert != jnp.roll(block_expert, 1)))
    slot = (jnp.cumsum(first.astype(jnp.int32)) - 1) % 2
    first_at = jnp.where(first, jb, nb)
    nxt_first = jnp.min(jnp.where(first_at[None, :] > jb[:, None], first_at[None, :], nb), axis=1)
    nxt_expert = jnp.sum(jnp.where(jb[None, :] == nxt_first[:, None], block_expert[None, :], 0), axis=1)
    nxt = jnp.where(nxt_first < nb, nxt_expert, -1)
    return jnp.concatenate([first.astype(jnp.int32), slot.astype(jnp.int32), nxt.astype(jnp.int32),
                            block_units.astype(jnp.int32)])


def _experts(xs, block_expert, src_units, block_units, n_blocks_used, w_gu, w_dn):
    nb_max = block_expert.shape[0]
    any_spec = pl.BlockSpec(memory_space=pl.ANY)
    grid_spec = pltpu.PrefetchScalarGridSpec(
        num_scalar_prefetch=4,
        grid=(nb_max,),
        in_specs=[any_spec, any_spec, any_spec],
        out_specs=pl.BlockSpec((MOE_BLOCK, D_MODEL), lambda j, be, src, nb, wplan: (j, 0)),
        scratch_shapes=[pltpu.VMEM((GATHER_SLOTS, MOE_BLOCK, D_MODEL), BF16),
                        pltpu.SemaphoreType.DMA((GATHER_SLOTS,)),
                        pltpu.VMEM((2, D_MODEL, 2 * D_EXPERT), F32), pltpu.VMEM((2, D_EXPERT, D_MODEL), F32),
                        pltpu.SemaphoreType.DMA((2,)),
                        pltpu.VMEM((D_MODEL, 2 * D_EXPERT), BF16), pltpu.VMEM((D_EXPERT, D_MODEL), BF16)],
    )
    return pl.pallas_call(
        _experts_kernel,
        grid_spec=grid_spec,
        out_shape=jax.ShapeDtypeStruct((nb_max * MOE_BLOCK, D_MODEL), BF16),
        compiler_params=_params(("arbitrary",)),
        name="moe_experts",
    )(block_expert, src_units, n_blocks_used, _weight_plan(block_expert, block_units, n_blocks_used), xs, w_gu,
      w_dn)


def _combine_kernel(src_ref, used_ref, ys_hbm, cw_ref, h_ref, x1_ref, gt2_ref, gpost_ref, digits_t_ref,
                    wsgu_ref, wsdn_ref, o_ref, ybuf, sem, acc_ref):
    i = pl.program_id(0)

    def copies(tile, slot, c):
        return [_unit_copy(ys_hbm, src_ref[tile * TILE_UNITS + u], ybuf, slot, u, sem)
                for u in range(c * CHUNK_UNITS, (c + 1) * CHUNK_UNITS)]

    def chunk_used(tile, c):
        return used_ref[tile] > c * CHUNK_UNITS

    def for_used_chunks(tile, fn):
        for c in range(N_CHUNKS):
            if c < EAGER_CHUNKS:
                fn(c)
            else:
                pl.when(chunk_used(tile, c))(functools.partial(fn, c))

    def fetch(tile, slot):
        def start(c):
            for cp in copies(tile, slot, c):
                cp.start()

        for_used_chunks(tile, start)

    def wait_all(tile, slot):
        def wait(c):
            for cp in copies(tile, slot, c):
                cp.wait()

        for_used_chunks(tile, wait)

    @pl.when(i == 0)
    def _():
        fetch(0, 0)

    @pl.when(i + 1 < pl.num_programs(0))
    def _():
        fetch(i + 1, (i + 1) % 2)

    cw = cw_ref[...]
    t = cw.shape[0]
    routed = (cw > 0.0).astype(BF16)
    e_i = lax.broadcasted_iota(jnp.int32, (KEY_W, KEY_W), 0)
    f_i = lax.broadcasted_iota(jnp.int32, (KEY_W, KEY_W), 1)
    units = _segment_units(_dot_tn(routed, jnp.ones((t, KEY_W), BF16)))
    starts = _dot((f_i < e_i).astype(BF16), units.astype(BF16)) * MOE_UNIT
    ends = starts + units * MOE_UNIT
    units_row = _segment_units(_dot(jnp.ones((SUBLANES, t), BF16), routed))
    starts_row = _dot(units_row.astype(BF16), (e_i < f_i).astype(BF16)) * MOE_UNIT
    keys = _token_keys(cw, starts_row[:1])
    wb = cw.astype(BF16)

    f = _dot(_swiglu_act(h_ref[...], wsgu_ref[...]).astype(BF16), wsdn_ref[...])
    slot = i % 2
    reps = ROW_CHUNK // KEY_W
    starts_c = jnp.concatenate([starts] * reps, axis=1)
    ends_c = jnp.concatenate([ends] * reps, axis=1)
    sub = lax.broadcasted_iota(jnp.int32, (KEY_W, ROW_CHUNK), 0)

    wait_all(i, slot)

    def chunk_weights(c):
        rows = slice(c * ROW_CHUNK, (c + 1) * ROW_CHUNK)
        r = (lax.broadcasted_iota(jnp.int32, (KEY_W, ROW_CHUNK), 1) + c * ROW_CHUNK).astype(F32)
        in_seg = (r >= starts_c) & (r < ends_c)
        rmap_t = jnp.where(sub < N_EXPERTS, in_seg.astype(F32), digits_t_ref[:, rows].astype(F32)).astype(BF16)
        hit = _dot(keys, jnp.concatenate([rmap_t, rmap_t], axis=0)) == 0.0
        return jnp.where(hit, _dot(wb, rmap_t), 0.0).astype(BF16)

    def chunk_sum(c):
        return _dot(chunk_weights(c), ybuf[slot, c * ROW_CHUNK:(c + 1) * ROW_CHUNK, :])

    pw = jnp.concatenate([chunk_weights(c) for c in range(EAGER_CHUNKS)], axis=1)
    acc_ref[...] = f + _dot(pw, ybuf[slot, :EAGER_CHUNKS * ROW_CHUNK, :])
    for c in range(EAGER_CHUNKS, N_CHUNKS):
        @pl.when(chunk_used(i, c))
        def _(c=c):
            acc_ref[...] += chunk_sum(c)
    o_ref[...] = x1_ref[...] + gt2_ref[...] * _rms(acc_ref[...], gpost_ref[...])


def _combine(ys, src_units, used_units, cw, h2, x1, mod3, g_post, digits_t, wsgu_b, wsdn_b, seq):
    n = h2.shape[0]
    tpb = seq // MOE_TILE
    row = lambda w: pl.BlockSpec((MOE_TILE, w), lambda i, src, used: (i, 0))
    full = lambda a: pl.BlockSpec(a.shape, lambda i, src, used: (0,) * a.ndim)
    grid_spec = pltpu.PrefetchScalarGridSpec(
        num_scalar_prefetch=2,
        grid=(n // MOE_TILE,),
        in_specs=[pl.BlockSpec(memory_space=pl.ANY), row(KEY_W), row(D_MODEL), row(D_MODEL),
                  pl.BlockSpec((None, 1, D_MODEL), lambda i, src, used: (i // tpb, 0, 5)), full(g_post),
                  full(digits_t), full(wsgu_b), full(wsdn_b)],
        out_specs=row(D_MODEL),
        scratch_shapes=[pltpu.VMEM((2, TILE_ROWS, D_MODEL), BF16), pltpu.SemaphoreType.DMA((2,)),
                        pltpu.VMEM((MOE_TILE, D_MODEL), F32)],
    )
    return pl.pallas_call(
        _combine_kernel,
        grid_spec=grid_spec,
        out_shape=jax.ShapeDtypeStruct((n, D_MODEL), F32),
        compiler_params=_params(("arbitrary",)),
        name="moe_combine",
    )(src_units, used_units, ys, cw, h2, x1, mod3, g_post, digits_t, wsgu_b, wsdn_b)


def _row_digits():
    r = np.arange(TILE_ROWS)
    d = np.zeros((TILE_ROWS, KEY_W), np.float32)
    d[:, N_EXPERTS] = r // DIGIT
    d[:, N_EXPERTS + 1] = r % DIGIT
    return jnp.asarray(d, dtype=BF16)


def _moe_plan(counts, nb_max):
    n_tiles = counts.shape[0]
    s = (counts + (MOE_UNIT - 1)) // MOE_UNIT
    local = jnp.cumsum(s, axis=1) - s
    cs = jnp.cumsum(s, axis=0)
    per_expert = cs[-1]
    padded = (per_expert + UNITS_PER_BLOCK - 1) // UNITS_PER_BLOCK * UNITS_PER_BLOCK
    g_end = jnp.cumsum(padded)
    g_start = g_end - padded
    seg_start = g_start[None, :] + cs - s
    n_blocks_used = (g_end[-1] // UNITS_PER_BLOCK).astype(jnp.int32).reshape(1)
    jb = jnp.arange(nb_max, dtype=jnp.int32)
    one_e = ((jb[:, None] >= (g_start // UNITS_PER_BLOCK)[None, :])
             & (jb[:, None] < (g_end // UNITS_PER_BLOCK)[None, :])).astype(jnp.int32)
    pick_e = lambda table: jnp.sum(one_e[:, :, None] * table.T[None, :, :], axis=1)
    block_expert = jnp.where(jb < n_blocks_used[0], jnp.sum(one_e * jnp.arange(N_EXPERTS, dtype=jnp.int32), axis=1),
                             N_EXPERTS - 1).astype(jnp.int32)
    cs_b, s_b, local_b = pick_e(cs), pick_e(s), pick_e(local)
    q = (jb * UNITS_PER_BLOCK - jnp.sum(one_e * g_start[None, :], axis=1))[:, None] \
        + jnp.arange(UNITS_PER_BLOCK, dtype=jnp.int32)[None, :]
    tile = jnp.minimum(jnp.sum(cs_b[:, None, :] <= q[:, :, None], axis=2), n_tiles - 1)
    one_t = (tile[:, :, None] == jnp.arange(n_tiles, dtype=jnp.int32)).astype(jnp.int32)
    src = tile * TILE_UNITS + q + jnp.sum(one_t * (local_b - cs_b + s_b)[:, None, :], axis=2)
    valid = q < jnp.sum(one_e * per_expert[None, :], axis=1)[:, None]
    src_units = jnp.where(valid, src, 0).astype(jnp.int32).reshape(-1)
    block_units = jnp.sum(valid, axis=1)
    u = jnp.arange(TILE_UNITS, dtype=jnp.int32)
    seg_end = local + s
    eu = jnp.minimum(jnp.sum(seg_end[:, None, :] <= u[None, :, None], axis=2), N_EXPERTS - 1)
    one_u = (eu[:, :, None] == jnp.arange(N_EXPERTS, dtype=jnp.int32)).astype(jnp.int32)
    back = u[None, :] + jnp.sum(one_u * (seg_start - local)[:, None, :], axis=2)
    back_units = jnp.where(u[None, :] < seg_end[:, -1:], back, 0).astype(jnp.int32).reshape(-1)
    return block_expert, src_units, block_units, n_blocks_used, back_units, seg_end[:, -1].astype(jnp.int32)


def _tile(n, pref):
    t = pref
    while n % t:
        t //= 2
    return t


def kernel(x, c, ctx, c_ctx, w_ada, b_ada, g_pre_mix, g_post_mix, g_pre_ffn, g_post_ffn, w_in, lb_logits, g_hgrn_out, cm_ln_g, cm_ln_b, w_spatial, b_spatial, w_branch_a, w_branch_b, w_out, w_router, b_router, w_expert_gu, w_expert_down, w_shared_gu, w_shared_down):
    B, T, D = x.shape
    L = ctx.shape[1]
    assert D == D_MODEL and w_ada.shape[0] == 1 and T % SCAN_CHUNK == 0 and L % SCAN_CHUNK == 0
    assert T % MOE_TILE == 0 and MOE_TILE % CM_CHUNK == 0
    l = 0
    row = lambda a: a[l].reshape(1, -1)

    n_rows = -(-(B + 1) // BF16_SUBLANES) * BF16_SUBLANES
    cs = jnp.zeros((n_rows, D), F32).at[:B].set(c).at[B].set(c_ctx)
    mod3 = _ada_mod(cs, w_ada[l], row(b_ada)).reshape(n_rows, 1, 6 * D)

    w_in_b = w_in[l].astype(BF16)
    lbl = lb_logits[:, l:l + 2].reshape(4, HG_W)
    x2 = x.reshape(B * T, D)
    q, k2, lf2, v, sg, u, vn, sga, sgb = _proj_lat(
        x2, mod3, row(g_pre_mix), w_in_b, lbl, row(cm_ln_g), row(cm_ln_b), T, _tile(T, PROJ_TILE))
    kc2, lfc2, vc = _proj_ctx(ctx.reshape(B * L, D), mod3, B, row(g_pre_mix), w_in_b[:, HG_W:4 * HG_W], lbl,
                              _tile(B * L, CTX_TILE))

    o_f, o_b = _hgrn_scan(q, k2, lf2, v, kc2, lfc2, vc, B, T, L)

    bs_full = jnp.repeat(b_spatial[l], CM_W // CM_GROUPS, axis=1)
    x1, h2, logits = _merge(
        o_f, o_b, sg, u, vn, sga, sgb, x2, mod3, row(g_hgrn_out), w_spatial[l].astype(BF16), bs_full,
        w_branch_a[l].astype(BF16), w_branch_b[l].astype(BF16), w_out[l].astype(BF16), row(g_post_mix),
        row(g_pre_ffn), jnp.pad(w_router[l].T, ((0, KEY_W - N_EXPERTS), (0, 0))).astype(BF16), T, _tile(T, MERGE_TILE))

    cw = _router(logits, jnp.broadcast_to(b_router[l][:, None], (N_EXPERTS, LANES)), _tile(B * T, ROUTER_TILE))

    n_tok = B * T
    n_tiles = n_tok // MOE_TILE
    digits = _row_digits()
    xs, cnt = _dispatch(h2, cw, digits)
    counts = cnt.reshape(n_tiles, SUBLANES, KEY_W)[:, 0, :N_EXPERTS]
    max_units = (n_tok * TOP_K + n_tiles * N_EXPERTS * (MOE_UNIT - 1)) // MOE_UNIT + N_EXPERTS * (UNITS_PER_BLOCK - 1)
    nb_max = -(-max_units // UNITS_PER_BLOCK)
    block_expert, src_units, block_units, n_blocks_used, back_units, tile_units = _moe_plan(counts, nb_max)
    wsgu_b, wsdn_b = w_shared_gu[l].astype(BF16), w_shared_down[l].astype(BF16)

    nb_min = -(-(n_tok * TOP_K) // MOE_BLOCK)
    grids = sorted({min(nb_max, g) for g in range(nb_min + EXPERT_GRID_STEP, nb_max + EXPERT_GRID_STEP,
                                                  EXPERT_GRID_STEP)})

    def experts_and_combine(nb_grid):
        def run():
            ys = _experts(xs, block_expert[:nb_grid], src_units[:nb_grid * UNITS_PER_BLOCK], block_units[:nb_grid],
                          n_blocks_used, w_expert_gu[l], w_expert_down[l])
            return _combine(ys, back_units, tile_units, cw, h2, x1, mod3, row(g_post_ffn), digits.T,
                            wsgu_b, wsdn_b, T)
        return run

    which = jnp.sum(n_blocks_used[0] > jnp.asarray(grids[:-1], jnp.int32))
    out = lax.switch(which, [experts_and_combine(g) for g in grids])
    return out.reshape(B, T, D)
```

```python
import functools

import numpy as np
import jax
import jax.numpy as jnp
from jax import lax
from jax.experimental import pallas as pl
from jax.experimental.pallas import tpu as pltpu

F32 = jnp.float32
BF16 = jnp.bfloat16

D_MODEL = 1024
EPS = 1e-6
HG_HEADS = 4
HG_DK = 128
HG_W = HG_HEADS * HG_DK
CM_W = 512
CM_CHUNK = 128
CM_GROUPS = 4
N_EXPERTS = 64
TOP_K = 8
N_GROUPS = 8
TOPK_GROUPS = 4
D_EXPERT = 256
ROUTED_SCALE = 2.5
LANES = 128
SUBLANES = 8
BF16_SUBLANES = 16
SCAN_CHUNK = 128
SUB = 16
N_LEVELS = (SCAN_CHUNK // SUB).bit_length()
PROJ_TILE, PROJ_ROWS = 512, 256
CTX_TILE = 256
MERGE_TILE, MERGE_ROWS = 1024, 512
V7X_VMEM_BYTES = 64 * 1024 * 1024
VMEM_LIMIT = V7X_VMEM_BYTES - 8 * 1024 * 1024


def _params(sem):
    return pltpu.CompilerParams(dimension_semantics=sem, vmem_limit_bytes=VMEM_LIMIT)


def _dot(a, b):
    return jnp.dot(a, b, preferred_element_type=F32)


def _dot_nt(a, b):
    return lax.dot_general(a, b, (((1,), (1,)), ((), ())), preferred_element_type=F32)


def _dot_tn(a, b):
    return lax.dot_general(a, b, (((0,), (0,)), ((), ())), preferred_element_type=F32)


def _sigmoid(x):
    return 0.5 * jnp.tanh(0.5 * x) + 0.5


def _rms(x, g):
    return x * lax.rsqrt(jnp.mean(x * x, axis=-1, keepdims=True) + EPS) * g


def _ada_kernel(c_ref, w_ref, b_ref, o_ref):
    c = c_ref[...]
    s = c * _sigmoid(c)
    o_ref[...] = _dot(s.astype(BF16), w_ref[...].astype(BF16)) + b_ref[...]


def _ada_mod(cs, w_ada, b_ada):
    rows = cs.shape[0]
    n_out = w_ada.shape[1]
    return pl.pallas_call(
        _ada_kernel,
        grid=(n_out // D_MODEL,),
        in_specs=[
            pl.BlockSpec((rows, D_MODEL), lambda j: (0, 0)),
            pl.BlockSpec((D_MODEL, D_MODEL), lambda j: (0, j)),
            pl.BlockSpec((1, D_MODEL), lambda j: (0, j)),
        ],
        out_specs=pl.BlockSpec((rows, D_MODEL), lambda j: (0, j)),
        out_shape=jax.ShapeDtypeStruct((rows, n_out), F32),
        compiler_params=_params(("parallel",)),
        name="ada_mod",
    )(cs, w_ada, b_ada)


def _lower_bounds(lbl):
    out = []
    for d in range(2):
        l0, l1 = lbl[2 * d:2 * d + 1], lbl[2 * d + 1:2 * d + 2]
        m = jnp.maximum(l0, l1)
        e0, e1 = jnp.exp(l0 - m), jnp.exp(l1 - m)
        out.append(e0 / (e0 + e1))
    return out


def _prenorm(x_ref, sh_ref, sc_ref, g_ref, rows=slice(None)):
    return (_rms(x_ref[rows, :], g_ref[...]) * (1.0 + sc_ref[...]) + sh_ref[...]).astype(BF16)


def _gates(z, lb, k_ref, lf_ref, d, rows=slice(None)):
    half_t = 0.5 * jnp.tanh(0.5 * z)
    k_ref[d, rows, :] = ((1.0 - lb) * (0.5 - half_t)).astype(k_ref.dtype)
    lf_ref[d, rows, :] = jnp.log2(lb + (1.0 - lb) * (0.5 + half_t))


def _proj_lat_kernel(x_ref, sh_ref, sc_ref, g_ref, w_ref, lbl_ref, lng_ref, lnb_ref,
                     q_ref, k_ref, lf_ref, v_ref, sg_ref, u_ref, vn_ref, sga_ref, sgb_ref):
    tm = x_ref.shape[0]
    group = min(PROJ_ROWS, tm)
    groups = [slice(r0, r0 + group) for r0 in range(0, tm, group)]
    hbs = [_prenorm(x_ref, sh_ref, sc_ref, g_ref, rows) for rows in groups]
    lbs = _lower_bounds(lbl_ref[...])

    def columns(lo, width, epilogue):
        zs = [_dot(hb, w_ref[:, lo:lo + width]) for hb in hbs]
        for rows, z in zip(groups, zs):
            epilogue(rows, z)

    def silu_to(ref):
        def epilogue(rows, z):
            ref[rows, :] = (z * _sigmoid(z)).astype(ref.dtype)
        return epilogue

    def sigmoid_to(ref):
        def epilogue(rows, z):
            ref[rows, :] = _sigmoid(z).astype(ref.dtype)
        return epilogue

    def gelu_to_u(rows, z):
        u_ref[rows, :] = jax.nn.gelu(z).astype(u_ref.dtype)

    def gelu_layernorm_to_vn(rows, z):
        vv = jax.nn.gelu(z)
        vc = vv - jnp.mean(vv, axis=-1, keepdims=True)
        vn = vc * lax.rsqrt(jnp.mean(vc * vc, axis=-1, keepdims=True) + EPS)
        vn_ref[rows, :] = (vn * lng_ref[...] + lnb_ref[...]).astype(vn_ref.dtype)

    def cast_to_v(rows, z):
        v_ref[rows, :] = z.astype(v_ref.dtype)

    columns(0, HG_W, silu_to(q_ref))
    for d in range(2):
        columns((1 + d) * HG_W, HG_W, lambda rows, z, d=d: _gates(z, lbs[d], k_ref, lf_ref, d, rows))
    columns(4 * HG_W, HG_W, silu_to(sg_ref))
    columns(5 * HG_W, CM_W, gelu_to_u)
    columns(5 * HG_W + CM_W, CM_W, gelu_layernorm_to_vn)
    base = 5 * HG_W + 2 * CM_W
    columns(base, D_MODEL, sigmoid_to(sga_ref))
    columns(base + D_MODEL, D_MODEL, sigmoid_to(sgb_ref))
    columns(3 * HG_W, HG_W, cast_to_v)


def _mod_spec(rows_per_batch_tiles, col):
    return pl.BlockSpec((None, 1, D_MODEL), lambda i: (i // rows_per_batch_tiles, 0, col))


def _proj_lat(x2, mod3, g_pre, w_in_b, lbl, ln_g, ln_b, seq, tm):
    n = x2.shape[0]
    tpb = seq // tm
    row = lambda w: pl.BlockSpec((tm, w), lambda i: (i, 0))
    row2 = pl.BlockSpec((2, tm, HG_W), lambda i: (0, i, 0))
    full = lambda a: pl.BlockSpec(a.shape, lambda i: (0,) * a.ndim)
    outs = [
        (row(HG_W), jax.ShapeDtypeStruct((n, HG_W), BF16)),
        (row2, jax.ShapeDtypeStruct((2, n, HG_W), BF16)),
        (row2, jax.ShapeDtypeStruct((2, n, HG_W), F32)),
        (row(HG_W), jax.ShapeDtypeStruct((n, HG_W), BF16)),
        (row(HG_W), jax.ShapeDtypeStruct((n, HG_W), BF16)),
        (row(CM_W), jax.ShapeDtypeStruct((n, CM_W), BF16)),
        (row(CM_W), jax.ShapeDtypeStruct((n, CM_W), BF16)),
        (row(D_MODEL), jax.ShapeDtypeStruct((n, D_MODEL), BF16)),
        (row(D_MODEL), jax.ShapeDtypeStruct((n, D_MODEL), BF16)),
    ]
    return pl.pallas_call(
        _proj_lat_kernel,
        grid=(n // tm,),
        in_specs=[row(D_MODEL), _mod_spec(tpb, 0), _mod_spec(tpb, 1), full(g_pre), full(w_in_b),
                  full(lbl), full(ln_g), full(ln_b)],
        out_specs=[o[0] for o in outs],
        out_shape=[o[1] for o in outs],
        compiler_params=_params(("parallel",)),
        name="proj_lat",
    )(x2, mod3, mod3, g_pre, w_in_b, lbl, ln_g, ln_b)


def _proj_ctx_kernel(x_ref, sh_ref, sc_ref, g_ref, w_ref, lbl_ref, k_ref, lf_ref, v_ref):
    hb = _prenorm(x_ref, sh_ref, sc_ref, g_ref)
    lbs = _lower_bounds(lbl_ref[...])
    for d in range(2):
        _gates(_dot(hb, w_ref[:, d * HG_W:(d + 1) * HG_W]), lbs[d], k_ref, lf_ref, d)
    v_ref[...] = _dot(hb, w_ref[:, 2 * HG_W:3 * HG_W]).astype(v_ref.dtype)


def _proj_ctx(c2, mod3, ctx_row, g_pre, w_ctx_b, lbl, tm):
    n = c2.shape[0]
    row = lambda w: pl.BlockSpec((tm, w), lambda i: (i, 0))
    row2 = pl.BlockSpec((2, tm, HG_W), lambda i: (0, i, 0))
    full = lambda a: pl.BlockSpec(a.shape, lambda i: (0,) * a.ndim)
    mod = lambda col: pl.BlockSpec((None, 1, D_MODEL), lambda i: (ctx_row, 0, col))
    return pl.pallas_call(
        _proj_ctx_kernel,
        grid=(n // tm,),
        in_specs=[row(D_MODEL), mod(0), mod(1), full(g_pre), full(w_ctx_b), full(lbl)],
        out_specs=[row2, row2, row(HG_W)],
        out_shape=[jax.ShapeDtypeStruct((2, n, HG_W), BF16), jax.ShapeDtypeStruct((2, n, HG_W), F32),
                   jax.ShapeDtypeStruct((n, HG_W), BF16)],
        compiler_params=_params(("parallel",)),
        name="proj_ctx",
    )(c2, mod3, mod3, g_pre, w_ctx_b, lbl)


def _scan_tables():
    C = SCAN_CHUNK
    t = np.arange(C)
    lmats, lvls = [], []
    for d in range(2):
        p = t if d == 0 else C - 1 - t
        pt, ps = p[:, None], p[None, :]
        lmat = (ps <= pt).astype(np.float32)
        lmats.append(np.concatenate([lmat, lmat], axis=1))
        lvl = np.full((C, C), -1, np.int32)
        lvl[(pt // SUB == ps // SUB) & (ps <= pt)] = 0
        half, idx = SUB, 1
        while half < C:
            span = 2 * half
            lvl[(pt // span == ps // span) & ((pt // half) % 2 == 1) & ((ps // half) % 2 == 0)] = idx
            half, idx = span, idx + 1
        lvls.append(lvl)
    return jnp.asarray(np.stack(lmats), dtype=BF16), jnp.asarray(np.stack(lvls))


def _scan_cumsum(d, slot, lmat_ref, lf_ref, b_scr):
    lf = lf_ref[...]
    hi = lf.astype(BF16)
    lo = (lf - hi.astype(F32)).astype(BF16)
    b_scr[slot] = _dot(lmat_ref[d], jnp.concatenate([hi, lo], axis=0))


def _scan_head(d, slot, h, masks, k_ref, v_ref, st_ref, b_scr, q_ref=None, o_ref=None):
    C = SCAN_CHUNK
    sl = slice(h * HG_DK, (h + 1) * HG_DK)
    b = b_scr[slot, :, sl]

    def row(i):
        return b_scr[slot, pl.ds(i, 1), sl]

    b_last = row(C - 1 if d == 0 else 0)
    k = k_ref[:, sl]
    v = v_ref[:, sl]
    st = st_ref[slot, :, sl]
    if q_ref is not None:
        q = q_ref[:, sl]
        e0 = jnp.concatenate([b[m * SUB:(m + 1) * SUB] - row(m * SUB + SUB // 2 - 1 + d)
                              for m in range(C // SUB)], axis=0)
        factors = [(jnp.exp2(e0).astype(BF16), jnp.exp2(-e0).astype(BF16))]
        half = SUB
        while half < C:
            span = 2 * half
            pieces = []
            for m in range(C // span):
                ref = row(m * span + half - 1 + d)
                lo_half, hi_half = b[m * span:m * span + half], b[m * span + half:(m + 1) * span]
                pieces += [ref - lo_half, hi_half - ref] if d == 0 else [lo_half - ref, ref - hi_half]
            w = jnp.exp2(jnp.concatenate(pieces, axis=0)).astype(BF16)
            factors.append((w, w))
            half = span
        a = jnp.zeros((C, C), F32)
        for (wq, wk), mask in zip(factors, masks):
            a = jnp.where(mask, _dot_nt(q * wq, k * wk), a)
        qhat = q * jnp.exp2(b).astype(BF16)
        o = _dot(a.astype(BF16), v) + _dot_nt(qhat, st.astype(BF16))
        o_ref[:, sl] = o.astype(o_ref.dtype)
    khat = k * jnp.exp2(b_last - b).astype(BF16)
    st_ref[slot, :, sl] = jnp.exp2(b_last) * st + _dot_tn(v, khat)


def _scan_kernel(n_ctx_steps, lmat_ref, lvl_ref, q_f, k_f, lf_f, v_f, q_b, k_b, lf_b, v_b,
                 kc_f, lfc_f, vc_f, kc_b, lfc_b, vc_b, o_f, o_b, st_ref, b_scr):
    s = pl.program_id(1)
    n_seq = q_f.shape[0]

    @pl.when(s == 0)
    def _():
        st_ref[...] = jnp.zeros_like(st_ref)

    def step(refs, readout):
        chains = [(d, 2 * i + d, [r.at[i] if r is not None else None for r in refs[d]])
                  for i in range(n_seq) for d in range(2)]
        for d, slot, (k, lf, v, q, o) in chains:
            _scan_cumsum(d, slot, lmat_ref, lf, b_scr)
        masks = {d: [lvl_ref[d] == i for i in range(N_LEVELS)] for d in range(2)} if readout else {0: None, 1: None}
        for h in range(HG_HEADS):
            for d, slot, (k, lf, v, q, o) in chains:
                _scan_head(d, slot, h, masks[d], k, v, st_ref, b_scr, q, o)

    @pl.when(s < n_ctx_steps)
    def _():
        step([(kc_f, lfc_f, vc_f, None, None), (kc_b, lfc_b, vc_b, None, None)], False)

    @pl.when(s >= n_ctx_steps)
    def _():
        step([(k_f, lf_f, v_f, q_f, o_f), (k_b, lf_b, v_b, q_b, o_b)], True)


def _hgrn_scan(q, k2, lf2, v, kc2, lfc2, vc, batch, seq, ctx_len):
    C = SCAN_CHUNK
    n_lat, n_ctx = seq // C, ctx_len // C
    n_seq = 4 if batch % 4 == 0 else 2 if batch % 2 == 0 else 1
    lmat, lvl = _scan_tables()
    q, v = (a.reshape(batch, seq, HG_W) for a in (q, v))
    k2, lf2 = (a.reshape(2, batch, seq, HG_W) for a in (k2, lf2))
    vc = vc.reshape(batch, ctx_len, HG_W)
    kc2, lfc2 = (a.reshape(2, batch, ctx_len, HG_W) for a in (kc2, lfc2))

    def lat_blk(d):
        def blk(s):
            j = jnp.maximum(s - n_ctx, 0)
            return j if d == 0 else n_lat - 1 - j
        return blk

    def ctx_blk(d):
        def blk(s):
            i = jnp.minimum(s, n_ctx - 1)
            return i if d == 0 else n_ctx - 1 - i
        return blk

    def plain(blk):
        return pl.BlockSpec((n_seq, C, HG_W), lambda b, s: (b, blk(s), 0))

    def specs(blk_of, with_q):
        out = []
        for d in range(2):
            blk = blk_of(d)
            per_dir = pl.BlockSpec((None, n_seq, C, HG_W), lambda b, s, blk=blk, d=d: (d, b, blk(s), 0))
            out += ([plain(blk)] if with_q else []) + [per_dir, per_dir, plain(blk)]
        return out

    full = lambda a: pl.BlockSpec(a.shape, lambda b, s: (0,) * a.ndim)
    o_shape = jax.ShapeDtypeStruct((batch, seq, HG_W), BF16)
    n_chains = 2 * n_seq
    o_f, o_b = pl.pallas_call(
        functools.partial(_scan_kernel, n_ctx),
        grid=(batch // n_seq, n_ctx + n_lat),
        in_specs=[full(lmat), full(lvl)] + specs(lat_blk, True) + specs(ctx_blk, False),
        out_specs=[plain(lat_blk(d)) for d in range(2)],
        out_shape=[o_shape, o_shape],
        scratch_shapes=[pltpu.VMEM((n_chains, HG_DK, HG_W), F32), pltpu.VMEM((n_chains, C, HG_W), F32)],
        compiler_params=_params(("parallel", "arbitrary")),
        name="hgrn_scan",
    )(lmat, lvl, q, k2, lf2, v, q, k2, lf2, v, kc2, lfc2, vc, kc2, lfc2, vc)
    return o_f.reshape(batch * seq, HG_W), o_b.reshape(batch * seq, HG_W)


def _merge_kernel(of_ref, ob_ref, sg_ref, u_ref, vn_ref, sga_ref, sgb_ref, x_ref, gt1_ref, sh2_ref, sc2_ref,
                  gout_ref, ws_ref, bs_ref, wa_ref, wb_ref, wo_ref, gpost_ref, gffn_ref, wr_ref, br_ref,
                  x1_ref, h2_ref, cw_ref):
    tm = x_ref.shape[0]
    gout = gout_ref[...]
    gw = CM_W // CM_GROUPS
    group = min(MERGE_ROWS, tm)
    groups = [slice(r0, r0 + group) for r0 in range(0, tm, group)]

    def branch_inputs(rows):
        o = of_ref[rows, :].astype(F32) + ob_ref[rows, :].astype(F32)
        sg = sg_ref[rows, :].astype(F32)
        a = jnp.concatenate(
            [_rms(o[:, h * HG_DK:(h + 1) * HG_DK], gout) * sg[:, h * HG_DK:(h + 1) * HG_DK]
             for h in range(HG_HEADS)], axis=1).astype(BF16)
        vn = vn_ref[rows, :]
        z = jnp.concatenate(
            [jnp.concatenate([_dot(ws_ref[g], vn[c * CM_CHUNK:(c + 1) * CM_CHUNK, g * gw:(g + 1) * gw])
                              for g in range(CM_GROUPS)], axis=1) + bs_ref[...]
             for c in range(group // CM_CHUNK)], axis=0)
        return a, (u_ref[rows, :].astype(F32) * z).astype(BF16)

    ab = [branch_inputs(rows) for rows in groups]
    ys = [(sga_ref[rows, :].astype(F32) * _dot(a, wa_ref[...])
           + sgb_ref[rows, :].astype(F32) * _dot(bm, wb_ref[...])).astype(BF16)
          for rows, (a, bm) in zip(groups, ab)]
    yos = [_dot(y, wo_ref[...]) for y in ys]
    for rows, yo in zip(groups, yos):
        x1 = x_ref[rows, :] + gt1_ref[...] * _rms(yo, gpost_ref[...])
        x1_ref[rows, :] = x1
        h2 = (_rms(x1, gffn_ref[...]) * (1.0 + sc2_ref[...]) + sh2_ref[...]).astype(BF16)
        h2_ref[rows, :] = h2
        cw_ref[rows, :] = _route(_dot_nt(wr_ref[...], h2), br_ref[...])


def _merge(o_f, o_b, sg, u, vn, sga, sgb, x2, mod3, g_out, ws_b, bs_full, wa_b, wb_b, wo_b, g_post, g_ffn, wr_b,
           br_cols, seq, tm):
    n = x2.shape[0]
    tpb = seq // tm
    row = lambda w: pl.BlockSpec((tm, w), lambda i: (i, 0))
    full = lambda a: pl.BlockSpec(a.shape, lambda i: (0,) * a.ndim)
    return pl.pallas_call(
        _merge_kernel,
        grid=(n // tm,),
        in_specs=[row(HG_W), row(HG_W), row(HG_W), row(CM_W), row(CM_W),
                  row(D_MODEL), row(D_MODEL), row(D_MODEL), _mod_spec(tpb, 2), _mod_spec(tpb, 3),
                  _mod_spec(tpb, 4), full(g_out), full(ws_b), full(bs_full), full(wa_b), full(wb_b),
                  full(wo_b), full(g_post), full(g_ffn), full(wr_b), full(br_cols)],
        out_specs=[row(D_MODEL), row(D_MODEL), row(wr_b.shape[0])],
        out_shape=[jax.ShapeDtypeStruct((n, D_MODEL), F32), jax.ShapeDtypeStruct((n, D_MODEL), BF16),
                   jax.ShapeDtypeStruct((n, wr_b.shape[0]), F32)],
        compiler_params=_params(("parallel",)),
        name="merge",
    )(o_f, o_b, sg, u, vn, sga, sgb, x2, mod3, mod3, mod3, g_out, ws_b, bs_full, wa_b, wb_b, wo_b, g_post, g_ffn,
      wr_b, br_cols)


def _route(logits, br):
    tm = logits.shape[1]
    gsz = N_EXPERTS // N_GROUPS
    scores = _sigmoid(logits[:N_EXPERTS, :])
    sel = scores + jnp.concatenate([br] * (tm // br.shape[1]), axis=1)
    neg = -jnp.inf

    def first_max(x, ids, sentinel, axis):
        m = jnp.max(x, axis=axis, keepdims=True)
        return m, jnp.min(jnp.where(x == m, ids, sentinel), axis=axis, keepdims=True)

    sel3 = sel.reshape(N_GROUPS, gsz, tm)
    j3 = lax.broadcasted_iota(jnp.int32, sel3.shape, 1)
    m1, i1 = first_max(sel3, j3, gsz, 1)
    gscore = m1 + jnp.max(jnp.where(j3 == i1, neg, sel3), axis=1, keepdims=True)
    g3 = lax.broadcasted_iota(jnp.int32, gscore.shape, 0)
    keep = jnp.zeros(gscore.shape, F32)
    for _ in range(TOPK_GROUPS):
        _, gi = first_max(gscore, g3, N_GROUPS, 0)
        keep = jnp.where(g3 == gi, 1.0, keep)
        gscore = jnp.where(g3 == gi, neg, gscore)
    x = jnp.where(keep > 0.0, sel3, neg).reshape(N_EXPERTS, tm)
    e_i = lax.broadcasted_iota(jnp.int32, x.shape, 0)
    w = jnp.zeros(x.shape, F32)
    for _ in range(TOP_K):
        _, ei = first_max(x, e_i, N_EXPERTS, 0)
        w = jnp.where(e_i == ei, scores, w)
        x = jnp.where(e_i == ei, neg, x)
    w = w / jnp.sum(w, axis=0, keepdims=True) * ROUTED_SCALE
    return jnp.concatenate([w, jnp.zeros_like(w)], axis=0).T


MOE_TILE = 256
MOE_UNIT = BF16_SUBLANES
MOE_BLOCK = 512
GATHER_SLOTS = 3
EXPERT_GRID_STEP = 24
UNITS_PER_BLOCK = MOE_BLOCK // MOE_UNIT
ROW_CHUNK = 512
TILE_ROWS = -(-(MOE_TILE * TOP_K + N_EXPERTS * (MOE_UNIT - 1)) // ROW_CHUNK) * ROW_CHUNK
TILE_UNITS = TILE_ROWS // MOE_UNIT
N_CHUNKS = TILE_ROWS // ROW_CHUNK
FULL_CHUNKS = MOE_TILE * TOP_K // ROW_CHUNK
EAGER_CHUNKS = min(FULL_CHUNKS + 1, N_CHUNKS)
CHUNK_UNITS = ROW_CHUNK // MOE_UNIT
SORT_ROWS = 1024
KEY_W = 128
DIGIT_BITS = 6
DIGIT = 1 << DIGIT_BITS


def _swiglu_act(h, w_gu):
    gu = _dot(h, w_gu)
    de = gu.shape[1] // 2
    g = gu[:, :de]
    return g * _sigmoid(g) * gu[:, de:]


def _token_keys(cw, starts_row):
    t = cw.shape[0]
    routed = cw > 0.0
    t_i = lax.broadcasted_iota(jnp.int32, (t, t), 0)
    s_i = lax.broadcasted_iota(jnp.int32, (t, t), 1)
    rank = _dot((s_i < t_i).astype(BF16), routed.astype(BF16))
    pos = (starts_row + rank).astype(jnp.int32)
    lane = lax.broadcasted_iota(jnp.int32, cw.shape, 1)
    hi = jnp.where(routed, jnp.right_shift(pos, DIGIT_BITS), -1)
    lo = jnp.where(routed, jnp.bitwise_and(pos, DIGIT - 1), -1)
    key_hi = jnp.where(lane < N_EXPERTS, hi * DIGIT,
                       jnp.where(lane == N_EXPERTS, -DIGIT, jnp.where(lane == N_EXPERTS + 1, -1, 0)))
    key_lo = jnp.where(lane < N_EXPERTS, lo, 0)
    return jnp.concatenate([key_hi, key_lo], axis=1).astype(F32).astype(BF16)


def _segment_units(counts):
    return jnp.floor((counts + (MOE_UNIT - 1)) * (1.0 / MOE_UNIT))


def _dispatch_kernel(h_ref, cw_ref, digits_ref, xs_ref, cnt_ref):
    cw = cw_ref[...]
    t = cw.shape[0]
    routed = (cw > 0.0).astype(BF16)
    counts = _dot(jnp.ones((SUBLANES, t), BF16), routed)
    cnt_ref[...] = counts.astype(jnp.int32)
    units = _segment_units(counts)
    e_i = lax.broadcasted_iota(jnp.int32, (KEY_W, KEY_W), 0)
    f_i = lax.broadcasted_iota(jnp.int32, (KEY_W, KEY_W), 1)
    starts = _dot(units.astype(BF16), (e_i < f_i).astype(BF16)) * MOE_UNIT
    ends = starts + units * MOE_UNIT
    keys_t = _token_keys(cw, starts[:1]).astype(F32).T.astype(BF16)
    h = h_ref[...]
    used_rows = jnp.max(ends)

    def row_hits(r0, n):
        rows = slice(r0, r0 + n)
        lane = lax.broadcasted_iota(jnp.int32, (n, KEY_W), 1)
        r = (lax.broadcasted_iota(jnp.int32, (n, KEY_W), 0) + r0).astype(F32)
        in_seg = (r >= starts[:1]) & (r < ends[:1])
        rmap = jnp.where(lane < N_EXPERTS, in_seg.astype(F32), digits_ref[rows, :].astype(F32)).astype(BF16)
        return (_dot(jnp.concatenate([rmap, rmap], axis=1), keys_t) == 0.0).astype(BF16)

    def sort_rows(r0, n):
        xs_ref[r0:r0 + n, :] = _dot(row_hits(r0, n), h).astype(xs_ref.dtype)

    eager_rows = EAGER_CHUNKS * ROW_CHUNK
    pieces = [(r0, min(SORT_ROWS, eager_rows - r0)) for r0 in range(0, eager_rows, SORT_ROWS)]
    hits = [row_hits(r0, n) for r0, n in pieces]
    for (r0, n), hit in zip(pieces, hits):
        xs_ref[r0:r0 + n, :] = _dot(hit, h).astype(xs_ref.dtype)
    for c in range(EAGER_CHUNKS, N_CHUNKS):
        pl.when(used_rows > c * ROW_CHUNK)(functools.partial(sort_rows, c * ROW_CHUNK, ROW_CHUNK))

        @pl.when(used_rows <= c * ROW_CHUNK)
        def _(c=c):
            xs_ref[c * ROW_CHUNK:(c + 1) * ROW_CHUNK, :] = jnp.zeros((ROW_CHUNK, D_MODEL), xs_ref.dtype)


def _dispatch(h2, cw, digits):
    n = h2.shape[0]
    n_tiles = n // MOE_TILE
    return pl.pallas_call(
        _dispatch_kernel,
        grid=(n_tiles,),
        in_specs=[pl.BlockSpec((MOE_TILE, D_MODEL), lambda i: (i, 0)),
                  pl.BlockSpec((MOE_TILE, KEY_W), lambda i: (i, 0)),
                  pl.BlockSpec(digits.shape, lambda i: (0, 0))],
        out_specs=[pl.BlockSpec((TILE_ROWS, D_MODEL), lambda i: (i, 0)),
                   pl.BlockSpec((SUBLANES, KEY_W), lambda i: (i, 0))],
        out_shape=[jax.ShapeDtypeStruct((n_tiles * TILE_ROWS, D_MODEL), BF16),
                   jax.ShapeDtypeStruct((n_tiles * SUBLANES, KEY_W), jnp.int32)],
        compiler_params=_params(("parallel",)),
        name="moe_dispatch",
    )(h2, cw, digits)


def _unit_copy(src_hbm, unit, dst, slot, pos, sem):
    return pltpu.make_async_copy(
        src_hbm.at[pl.ds(pl.multiple_of(unit * MOE_UNIT, MOE_UNIT), MOE_UNIT)],
        dst.at[slot, pl.ds(pos * MOE_UNIT, MOE_UNIT)], sem.at[slot])


def _experts_kernel(be_ref, src_ref, nb_ref, wplan_ref, xs_hbm, wgu_hbm, wdn_hbm, ys_ref, xbuf, sem,
                    wgu_f, wdn_f, wsem, wgu_b, wdn_b):
    j = pl.program_id(0)
    nb = nb_ref[0]
    n_steps = pl.num_programs(0)

    def copies(blk, slot):
        return [_unit_copy(xs_hbm, src_ref[blk * UNITS_PER_BLOCK + u], xbuf, slot, u, sem)
                for u in range(UNITS_PER_BLOCK)]

    def fetch(blk, slot):
        for cp in copies(blk, slot):
            cp.start()

    def weight_copies(expert, slot):
        return [pltpu.make_async_copy(wgu_hbm.at[expert], wgu_f.at[slot], wsem.at[slot]),
                pltpu.make_async_copy(wdn_hbm.at[expert], wdn_f.at[slot], wsem.at[slot])]

    ahead = GATHER_SLOTS - 1

    @pl.when(j == 0)
    def _():
        for a in range(ahead):
            fetch(jnp.minimum(a, nb - 1), a)
        for cp in weight_copies(be_ref[0], 0):
            cp.start()

    @pl.when((j < nb) & (wplan_ref[j] == 1))
    def _():
        slot = wplan_ref[n_steps + j]
        for cp in weight_copies(be_ref[j], slot):
            cp.wait()
        nxt = wplan_ref[2 * n_steps + j]

        @pl.when(nxt >= 0)
        def _():
            for cp in weight_copies(nxt, 1 - slot):
                cp.start()

        wgu_b[...] = wgu_f[slot].astype(BF16)
        wdn_b[...] = wdn_f[slot].astype(BF16)

    @pl.when(j < nb)
    def _():
        slot = j % GATHER_SLOTS
        for cp in copies(j, slot):
            cp.wait()
        fetch(jnp.minimum(j + ahead, nb - 1), (j + ahead) % GATHER_SLOTS)

    def ffn(slot, n_rows):
        act = _swiglu_act(xbuf[slot, :n_rows, :], wgu_b[...])
        ys_ref[:n_rows, :] = _dot(act.astype(BF16), wdn_b[...]).astype(ys_ref.dtype)

    half_full = wplan_ref[3 * n_steps + j] <= UNITS_PER_BLOCK // 2

    @pl.when((j < nb) & jnp.logical_not(half_full))
    def _():
        ffn(j % GATHER_SLOTS, MOE_BLOCK)

    @pl.when((j < nb) & half_full)
    def _():
        ffn(j % GATHER_SLOTS, MOE_BLOCK // 2)
        ys_ref[MOE_BLOCK // 2:, :] = jnp.zeros((MOE_BLOCK // 2, D_MODEL), ys_ref.dtype)

    @pl.when(j == nb - 1)
    def _():
        for a in range(1, GATHER_SLOTS):
            for cp in copies(j, (j + a) % GATHER_SLOTS):
                cp.wait()

    @pl.when(j >= nb)
    def _():
        ys_ref[...] = jnp.zeros_like(ys_ref)


def _weight_plan(block_expert, block_units, n_blocks_used):
    nb = block_expert.shape[0]
    jb = jnp.arange(nb, dtype=jnp.int32)
    used = jb < n_blocks_used[0]
    first = used & ((jb == 0) | (block_expert != jnp.roll(block_expert, 1)))
    slot = (jnp.cumsum(first.astype(jnp.int32)) - 1) % 2
    first_at = jnp.where(first, jb, nb)
    nxt_first = jnp.min(jnp.where(first_at[None, :] > jb[:, None], first_at[None, :], nb), axis=1)
    nxt_expert = jnp.sum(jnp.where(jb[None, :] == nxt_first[:, None], block_expert[None, :], 0), axis=1)
    nxt = jnp.where(nxt_first < nb, nxt_expert, -1)
    return jnp.concatenate([first.astype(jnp.int32), slot.astype(jnp.int32), nxt.astype(jnp.int32),
                            block_units.astype(jnp.int32)])


def _experts(xs, block_expert, src_units, block_units, n_blocks_used, w_gu, w_dn):
    nb_max = block_expert.shape[0]
    any_spec = pl.BlockSpec(memory_space=pl.ANY)
    grid_spec = pltpu.PrefetchScalarGridSpec(
        num_scalar_prefetch=4,
        grid=(nb_max,),
        in_specs=[any_spec, any_spec, any_spec],
        out_specs=pl.BlockSpec((MOE_BLOCK, D_MODEL), lambda j, be, src, nb, wplan: (j, 0)),
        scratch_shapes=[pltpu.VMEM((GATHER_SLOTS, MOE_BLOCK, D_MODEL), BF16),
                        pltpu.SemaphoreType.DMA((GATHER_SLOTS,)),
                        pltpu.VMEM((2, D_MODEL, 2 * D_EXPERT), F32), pltpu.VMEM((2, D_EXPERT, D_MODEL), F32),
                        pltpu.SemaphoreType.DMA((2,)),
                        pltpu.VMEM((D_MODEL, 2 * D_EXPERT), BF16), pltpu.VMEM((D_EXPERT, D_MODEL), BF16)],
    )
    return pl.pallas_call(
        _experts_kernel,
        grid_spec=grid_spec,
        out_shape=jax.ShapeDtypeStruct((nb_max * MOE_BLOCK, D_MODEL), BF16),
        compiler_params=_params(("arbitrary",)),
        name="moe_experts",
    )(block_expert, src_units, n_blocks_used, _weight_plan(block_expert, block_units, n_blocks_used), xs, w_gu,
      w_dn)


def _combine_kernel(src_ref, used_ref, ys_hbm, cw_ref, h_ref, x1_ref, gt2_ref, gpost_ref, digits_t_ref,
                    wsgu_ref, wsdn_ref, o_ref, ybuf, sem, acc_ref):
    i = pl.program_id(0)

    def copies(tile, slot, c):
        return [_unit_copy(ys_hbm, src_ref[tile * TILE_UNITS + u], ybuf, slot, u, sem)
                for u in range(c * CHUNK_UNITS, (c + 1) * CHUNK_UNITS)]

    def chunk_used(tile, c):
        return used_ref[tile] > c * CHUNK_UNITS

    def for_used_chunks(tile, fn):
        for c in range(N_CHUNKS):
            if c < EAGER_CHUNKS:
                fn(c)
            else:
                pl.when(chunk_used(tile, c))(functools.partial(fn, c))

    def fetch(tile, slot):
        def start(c):
            for cp in copies(tile, slot, c):
                cp.start()

        for_used_chunks(tile, start)

    def wait_all(tile, slot):
        def wait(c):
            for cp in copies(tile, slot, c):
                cp.wait()

        for_used_chunks(tile, wait)

    @pl.when(i == 0)
    def _():
        fetch(0, 0)

    @pl.when(i + 1 < pl.num_programs(0))
    def _():
        fetch(i + 1, (i + 1) % 2)

    cw = cw_ref[...]
    t = cw.shape[0]
    routed = (cw > 0.0).astype(BF16)
    e_i = lax.broadcasted_iota(jnp.int32, (KEY_W, KEY_W), 0)
    f_i = lax.broadcasted_iota(jnp.int32, (KEY_W, KEY_W), 1)
    units = _segment_units(_dot_tn(routed, jnp.ones((t, KEY_W), BF16)))
    starts = _dot((f_i < e_i).astype(BF16), units.astype(BF16)) * MOE_UNIT
    ends = starts + units * MOE_UNIT
    units_row = _segment_units(_dot(jnp.ones((SUBLANES, t), BF16), routed))
    starts_row = _dot(units_row.astype(BF16), (e_i < f_i).astype(BF16)) * MOE_UNIT
    keys = _token_keys(cw, starts_row[:1])
    wb = cw.astype(BF16)

    f = _dot(_swiglu_act(h_ref[...], wsgu_ref[...]).astype(BF16), wsdn_ref[...])
    slot = i % 2
    reps = ROW_CHUNK // KEY_W
    starts_c = jnp.concatenate([starts] * reps, axis=1)
    ends_c = jnp.concatenate([ends] * reps, axis=1)
    sub = lax.broadcasted_iota(jnp.int32, (KEY_W, ROW_CHUNK), 0)

    wait_all(i, slot)

    def chunk_weights(c):
        rows = slice(c * ROW_CHUNK, (c + 1) * ROW_CHUNK)
        r = (lax.broadcasted_iota(jnp.int32, (KEY_W, ROW_CHUNK), 1) + c * ROW_CHUNK).astype(F32)
        in_seg = (r >= starts_c) & (r < ends_c)
        rmap_t = jnp.where(sub < N_EXPERTS, in_seg.astype(F32), digits_t_ref[:, rows].astype(F32)).astype(BF16)
        hit = _dot(keys, jnp.concatenate([rmap_t, rmap_t], axis=0)) == 0.0
        return jnp.where(hit, _dot(wb, rmap_t), 0.0).astype(BF16)

    def chunk_sum(c):
        return _dot(chunk_weights(c), ybuf[slot, c * ROW_CHUNK:(c + 1) * ROW_CHUNK, :])

    pw = jnp.concatenate([chunk_weights(c) for c in range(EAGER_CHUNKS)], axis=1)
    acc_ref[...] = f + _dot(pw, ybuf[slot, :EAGER_CHUNKS * ROW_CHUNK, :])
    for c in range(EAGER_CHUNKS, N_CHUNKS):
        @pl.when(chunk_used(i, c))
        def _(c=c):
            acc_ref[...] += chunk_sum(c)
    o_ref[...] = x1_ref[...] + gt2_ref[...] * _rms(acc_ref[...], gpost_ref[...])


def _combine(ys, src_units, used_units, cw, h2, x1, mod3, g_post, digits_t, wsgu_b, wsdn_b, seq):
    n = h2.shape[0]
    tpb = seq // MOE_TILE
    row = lambda w: pl.BlockSpec((MOE_TILE, w), lambda i, src, used: (i, 0))
    full = lambda a: pl.BlockSpec(a.shape, lambda i, src, used: (0,) * a.ndim)
    grid_spec = pltpu.PrefetchScalarGridSpec(
        num_scalar_prefetch=2,
        grid=(n // MOE_TILE,),
        in_specs=[pl.BlockSpec(memory_space=pl.ANY), row(KEY_W), row(D_MODEL), row(D_MODEL),
                  pl.BlockSpec((None, 1, D_MODEL), lambda i, src, used: (i // tpb, 0, 5)), full(g_post),
                  full(digits_t), full(wsgu_b), full(wsdn_b)],
        out_specs=row(D_MODEL),
        scratch_shapes=[pltpu.VMEM((2, TILE_ROWS, D_MODEL), BF16), pltpu.SemaphoreType.DMA((2,)),
                        pltpu.VMEM((MOE_TILE, D_MODEL), F32)],
    )
    return pl.pallas_call(
        _combine_kernel,
        grid_spec=grid_spec,
        out_shape=jax.ShapeDtypeStruct((n, D_MODEL), F32),
        compiler_params=_params(("arbitrary",)),
        name="moe_combine",
    )(src_units, used_units, ys, cw, h2, x1, mod3, g_post, digits_t, wsgu_b, wsdn_b)


def _row_digits():
    r = np.arange(TILE_ROWS)
    d = np.zeros((TILE_ROWS, KEY_W), np.float32)
    d[:, N_EXPERTS] = r // DIGIT
    d[:, N_EXPERTS + 1] = r % DIGIT
    return jnp.asarray(d, dtype=BF16)


def _moe_plan(counts, nb_max):
    n_tiles = counts.shape[0]
    s = (counts + (MOE_UNIT - 1)) // MOE_UNIT
    local = jnp.cumsum(s, axis=1) - s
    cs = jnp.cumsum(s, axis=0)
    per_expert = cs[-1]
    padded = (per_expert + UNITS_PER_BLOCK - 1) // UNITS_PER_BLOCK * UNITS_PER_BLOCK
    g_end = jnp.cumsum(padded)
    g_start = g_end - padded
    seg_start = g_start[None, :] + cs - s
    n_blocks_used = (g_end[-1] // UNITS_PER_BLOCK).astype(jnp.int32).reshape(1)
    jb = jnp.arange(nb_max, dtype=jnp.int32)
    one_e = ((jb[:, None] >= (g_start // UNITS_PER_BLOCK)[None, :])
             & (jb[:, None] < (g_end // UNITS_PER_BLOCK)[None, :])).astype(jnp.int32)
    pick_e = lambda table: jnp.sum(one_e[:, :, None] * table.T[None, :, :], axis=1)
    block_expert = jnp.where(jb < n_blocks_used[0], jnp.sum(one_e * jnp.arange(N_EXPERTS, dtype=jnp.int32), axis=1),
                             N_EXPERTS - 1).astype(jnp.int32)
    cs_b, s_b, local_b = pick_e(cs), pick_e(s), pick_e(local)
    q = (jb * UNITS_PER_BLOCK - jnp.sum(one_e * g_start[None, :], axis=1))[:, None] \
        + jnp.arange(UNITS_PER_BLOCK, dtype=jnp.int32)[None, :]
    tile = jnp.minimum(jnp.sum(cs_b[:, None, :] <= q[:, :, None], axis=2), n_tiles - 1)
    one_t = (tile[:, :, None] == jnp.arange(n_tiles, dtype=jnp.int32)).astype(jnp.int32)
    src = tile * TILE_UNITS + q + jnp.sum(one_t * (local_b - cs_b + s_b)[:, None, :], axis=2)
    valid = q < jnp.sum(one_e * per_expert[None, :], axis=1)[:, None]
    src_units = jnp.where(valid, src, 0).astype(jnp.int32).reshape(-1)
    block_units = jnp.sum(valid, axis=1)
    u = jnp.arange(TILE_UNITS, dtype=jnp.int32)
    seg_end = local + s
    eu = jnp.minimum(jnp.sum(seg_end[:, None, :] <= u[None, :, None], axis=2), N_EXPERTS - 1)
    one_u = (eu[:, :, None] == jnp.arange(N_EXPERTS, dtype=jnp.int32)).astype(jnp.int32)
    back = u[None, :] + jnp.sum(one_u * (seg_start - local)[:, None, :], axis=2)
    back_units = jnp.where(u[None, :] < seg_end[:, -1:], back, 0).astype(jnp.int32).reshape(-1)
    return block_expert, src_units, block_units, n_blocks_used, back_units, seg_end[:, -1].astype(jnp.int32)


def _tile(n, pref):
    t = pref
    while n % t:
        t //= 2
    return t


def kernel(x, c, ctx, c_ctx, w_ada, b_ada, g_pre_mix, g_post_mix, g_pre_ffn, g_post_ffn, w_in, lb_logits, g_hgrn_out, cm_ln_g, cm_ln_b, w_spatial, b_spatial, w_branch_a, w_branch_b, w_out, w_router, b_router, w_expert_gu, w_expert_down, w_shared_gu, w_shared_down):
    B, T, D = x.shape
    L = ctx.shape[1]
    assert D == D_MODEL and w_ada.shape[0] == 1 and T % SCAN_CHUNK == 0 and L % SCAN_CHUNK == 0
    assert T % MOE_TILE == 0 and MOE_TILE % CM_CHUNK == 0
    l = 0
    row = lambda a: a[l].reshape(1, -1)

    n_rows = -(-(B + 1) // BF16_SUBLANES) * BF16_SUBLANES
    cs = jnp.zeros((n_rows, D), F32).at[:B].set(c).at[B].set(c_ctx)
    mod3 = _ada_mod(cs, w_ada[l], row(b_ada)).reshape(n_rows, 1, 6 * D)

    w_in_b = w_in[l].astype(BF16)
    lbl = lb_logits[:, l:l + 2].reshape(4, HG_W)
    x2 = x.reshape(B * T, D)
    q, k2, lf2, v, sg, u, vn, sga, sgb = _proj_lat(
        x2, mod3, row(g_pre_mix), w_in_b, lbl, row(cm_ln_g), row(cm_ln_b), T, _tile(T, PROJ_TILE))
    kc2, lfc2, vc = _proj_ctx(ctx.reshape(B * L, D), mod3, B, row(g_pre_mix), w_in_b[:, HG_W:4 * HG_W], lbl,
                              _tile(B * L, CTX_TILE))

    o_f, o_b = _hgrn_scan(q, k2, lf2, v, kc2, lfc2, vc, B, T, L)

    bs_full = jnp.repeat(b_spatial[l], CM_W // CM_GROUPS, axis=1)
    x1, h2, cw = _merge(
        o_f, o_b, sg, u, vn, sga, sgb, x2, mod3, row(g_hgrn_out), w_spatial[l].astype(BF16), bs_full,
        w_branch_a[l].astype(BF16), w_branch_b[l].astype(BF16), w_out[l].astype(BF16), row(g_post_mix),
        row(g_pre_ffn), jnp.pad(w_router[l].T, ((0, KEY_W - N_EXPERTS), (0, 0))).astype(BF16),
        jnp.broadcast_to(b_router[l][:, None], (N_EXPERTS, LANES)), T, _tile(T, MERGE_TILE))

    n_tok = B * T
    n_tiles = n_tok // MOE_TILE
    digits = _row_digits()
    xs, cnt = _dispatch(h2, cw, digits)
    counts = cnt.reshape(n_tiles, SUBLANES, KEY_W)[:, 0, :N_EXPERTS]
    max_units = (n_tok * TOP_K + n_tiles * N_EXPERTS * (MOE_UNIT - 1)) // MOE_UNIT + N_EXPERTS * (UNITS_PER_BLOCK - 1)
    nb_max = -(-max_units // UNITS_PER_BLOCK)
    block_expert, src_units, block_units, n_blocks_used, back_units, tile_units = _moe_plan(counts, nb_max)
    wsgu_b, wsdn_b = w_shared_gu[l].astype(BF16), w_shared_down[l].astype(BF16)

    nb_min = -(-(n_tok * TOP_K) // MOE_BLOCK)
    grids = sorted({min(nb_max, g) for g in range(nb_min + EXPERT_GRID_STEP, nb_max + EXPERT_GRID_STEP,
                                                  EXPERT_GRID_STEP)})

    def experts_and_combine(nb_grid):
        def run():
            ys = _experts(xs, block_expert[:nb_grid], src_units[:nb_grid * UNITS_PER_BLOCK], block_units[:nb_grid],
                          n_blocks_used, w_expert_gu[l], w_expert_down[l])
            return _combine(ys, back_units, tile_units, cw, h2, x1, mod3, row(g_post_ffn), digits.T,
                            wsgu_b, wsdn_b, T)
        return run

    which = jnp.sum(n_blocks_used[0] > jnp.asarray(grids[:-1], jnp.int32))
    out = lax.switch(which, [experts_and_combine(g) for g in grids])
    return out.reshape(B, T, D)
```

```python
import functools

import numpy as np
import jax
import jax.numpy as jnp
from jax import lax
from jax.experimental import pallas as pl
from jax.experimental.pallas import tpu as pltpu

F32 = jnp.float32
BF16 = jnp.bfloat16

D_MODEL = 1024
EPS = 1e-6
HG_HEADS = 4
HG_DK = 128
HG_W = HG_HEADS * HG_DK
CM_W = 512
CM_CHUNK = 128
CM_GROUPS = 4
N_EXPERTS = 64
TOP_K = 8
N_GROUPS = 8
TOPK_GROUPS = 4
D_EXPERT = 256
ROUTED_SCALE = 2.5
LANES = 128
SUBLANES = 8
BF16_SUBLANES = 16
SCAN_CHUNK = 128
SUB = 16
N_LEVELS = (SCAN_CHUNK // SUB).bit_length()
PROJ_TILE, PROJ_ROWS = 1024, 256
CTX_TILE = 256
MERGE_TILE, MERGE_ROWS = 1024, 512
V7X_VMEM_BYTES = 64 * 1024 * 1024
VMEM_LIMIT = V7X_VMEM_BYTES - 8 * 1024 * 1024


def _params(sem):
    return pltpu.CompilerParams(dimension_semantics=sem, vmem_limit_bytes=VMEM_LIMIT)


def _dot(a, b):
    return jnp.dot(a, b, preferred_element_type=F32)


def _dot_nt(a, b):
    return lax.dot_general(a, b, (((1,), (1,)), ((), ())), preferred_element_type=F32)


def _dot_tn(a, b):
    return lax.dot_general(a, b, (((0,), (0,)), ((), ())), preferred_element_type=F32)


def _sigmoid(x):
    return 0.5 * jnp.tanh(0.5 * x) + 0.5


def _rms(x, g):
    return x * lax.rsqrt(jnp.mean(x * x, axis=-1, keepdims=True) + EPS) * g


def _ada_kernel(c_ref, w_ref, b_ref, o_ref):
    c = c_ref[...]
    s = c * _sigmoid(c)
    o_ref[...] = _dot(s.astype(BF16), w_ref[...].astype(BF16)) + b_ref[...]


def _ada_mod(cs, w_ada, b_ada):
    rows = cs.shape[0]
    n_out = w_ada.shape[1]
    return pl.pallas_call(
        _ada_kernel,
        grid=(n_out // D_MODEL,),
        in_specs=[
            pl.BlockSpec((rows, D_MODEL), lambda j: (0, 0)),
            pl.BlockSpec((D_MODEL, D_MODEL), lambda j: (0, j)),
            pl.BlockSpec((1, D_MODEL), lambda j: (0, j)),
        ],
        out_specs=pl.BlockSpec((rows, D_MODEL), lambda j: (0, j)),
        out_shape=jax.ShapeDtypeStruct((rows, n_out), F32),
        compiler_params=_params(("parallel",)),
        name="ada_mod",
    )(cs, w_ada, b_ada)


def _lower_bounds(lbl):
    out = []
    for d in range(2):
        l0, l1 = lbl[2 * d:2 * d + 1], lbl[2 * d + 1:2 * d + 2]
        m = jnp.maximum(l0, l1)
        e0, e1 = jnp.exp(l0 - m), jnp.exp(l1 - m)
        out.append(e0 / (e0 + e1))
    return out


def _prenorm(x_ref, sh_ref, sc_ref, g_ref, rows=slice(None)):
    return (_rms(x_ref[rows, :], g_ref[...]) * (1.0 + sc_ref[...]) + sh_ref[...]).astype(BF16)


def _gates(z, lb, k_ref, lf_ref, d, rows=slice(None)):
    half_t = 0.5 * jnp.tanh(0.5 * z)
    k_ref[d, rows, :] = ((1.0 - lb) * (0.5 - half_t)).astype(k_ref.dtype)
    lf_ref[d, rows, :] = jnp.log2(lb + (1.0 - lb) * (0.5 + half_t))


def _proj_lat_kernel(x_ref, sh_ref, sc_ref, g_ref, w_ref, lbl_ref, lng_ref, lnb_ref,
                     q_ref, k_ref, lf_ref, v_ref, sg_ref, u_ref, vn_ref, sga_ref, sgb_ref):
    tm = x_ref.shape[0]
    group = min(PROJ_ROWS, tm)
    groups = [slice(r0, r0 + group) for r0 in range(0, tm, group)]
    hbs = [_prenorm(x_ref, sh_ref, sc_ref, g_ref, rows) for rows in groups]
    lbs = _lower_bounds(lbl_ref[...])

    def columns(lo, width, epilogue):
        zs = [_dot(hb, w_ref[:, lo:lo + width]) for hb in hbs]
        for rows, z in zip(groups, zs):
            epilogue(rows, z)

    def silu_to(ref):
        def epilogue(rows, z):
            ref[rows, :] = (z * _sigmoid(z)).astype(ref.dtype)
        return epilogue

    def sigmoid_to(ref):
        def epilogue(rows, z):
            ref[rows, :] = _sigmoid(z).astype(ref.dtype)
        return epilogue

    def gelu_to_u(rows, z):
        u_ref[rows, :] = jax.nn.gelu(z).astype(u_ref.dtype)

    def gelu_layernorm_to_vn(rows, z):
        vv = jax.nn.gelu(z)
        vc = vv - jnp.mean(vv, axis=-1, keepdims=True)
        vn = vc * lax.rsqrt(jnp.mean(vc * vc, axis=-1, keepdims=True) + EPS)
        vn_ref[rows, :] = (vn * lng_ref[...] + lnb_ref[...]).astype(vn_ref.dtype)

    def cast_to_v(rows, z):
        v_ref[rows, :] = z.astype(v_ref.dtype)

    columns(0, HG_W, silu_to(q_ref))
    for d in range(2):
        columns((1 + d) * HG_W, HG_W, lambda rows, z, d=d: _gates(z, lbs[d], k_ref, lf_ref, d, rows))
    columns(4 * HG_W, HG_W, silu_to(sg_ref))
    columns(5 * HG_W, CM_W, gelu_to_u)
    columns(5 * HG_W + CM_W, CM_W, gelu_layernorm_to_vn)
    base = 5 * HG_W + 2 * CM_W
    columns(base, D_MODEL, sigmoid_to(sga_ref))
    columns(base + D_MODEL, D_MODEL, sigmoid_to(sgb_ref))
    columns(3 * HG_W, HG_W, cast_to_v)


def _mod_spec(rows_per_batch_tiles, col):
    return pl.BlockSpec((None, 1, D_MODEL), lambda i: (i // rows_per_batch_tiles, 0, col))


def _proj_lat(x2, mod3, g_pre, w_in_b, lbl, ln_g, ln_b, seq, tm):
    n = x2.shape[0]
    tpb = seq // tm
    row = lambda w: pl.BlockSpec((tm, w), lambda i: (i, 0))
    row2 = pl.BlockSpec((2, tm, HG_W), lambda i: (0, i, 0))
    full = lambda a: pl.BlockSpec(a.shape, lambda i: (0,) * a.ndim)
    outs = [
        (row(HG_W), jax.ShapeDtypeStruct((n, HG_W), BF16)),
        (row2, jax.ShapeDtypeStruct((2, n, HG_W), BF16)),
        (row2, jax.ShapeDtypeStruct((2, n, HG_W), F32)),
        (row(HG_W), jax.ShapeDtypeStruct((n, HG_W), BF16)),
        (row(HG_W), jax.ShapeDtypeStruct((n, HG_W), BF16)),
        (row(CM_W), jax.ShapeDtypeStruct((n, CM_W), BF16)),
        (row(CM_W), jax.ShapeDtypeStruct((n, CM_W), BF16)),
        (row(D_MODEL), jax.ShapeDtypeStruct((n, D_MODEL), BF16)),
        (row(D_MODEL), jax.ShapeDtypeStruct((n, D_MODEL), BF16)),
    ]
    return pl.pallas_call(
        _proj_lat_kernel,
        grid=(n // tm,),
        in_specs=[row(D_MODEL), _mod_spec(tpb, 0), _mod_spec(tpb, 1), full(g_pre),
                  pl.BlockSpec(w_in_b.shape, lambda i: (0, 0), pipeline_mode=pl.Buffered(1)),
                  full(lbl), full(ln_g), full(ln_b)],
        out_specs=[o[0] for o in outs],
        out_shape=[o[1] for o in outs],
        compiler_params=_params(("parallel",)),
        name="proj_lat",
    )(x2, mod3, mod3, g_pre, w_in_b, lbl, ln_g, ln_b)


def _proj_ctx_kernel(x_ref, sh_ref, sc_ref, g_ref, w_ref, lbl_ref, k_ref, lf_ref, v_ref):
    hb = _prenorm(x_ref, sh_ref, sc_ref, g_ref)
    lbs = _lower_bounds(lbl_ref[...])
    for d in range(2):
        _gates(_dot(hb, w_ref[:, d * HG_W:(d + 1) * HG_W]), lbs[d], k_ref, lf_ref, d)
    v_ref[...] = _dot(hb, w_ref[:, 2 * HG_W:3 * HG_W]).astype(v_ref.dtype)


def _proj_ctx(c2, mod3, ctx_row, g_pre, w_ctx_b, lbl, tm):
    n = c2.shape[0]
    row = lambda w: pl.BlockSpec((tm, w), lambda i: (i, 0))
    row2 = pl.BlockSpec((2, tm, HG_W), lambda i: (0, i, 0))
    full = lambda a: pl.BlockSpec(a.shape, lambda i: (0,) * a.ndim)
    mod = lambda col: pl.BlockSpec((None, 1, D_MODEL), lambda i: (ctx_row, 0, col))
    return pl.pallas_call(
        _proj_ctx_kernel,
        grid=(n // tm,),
        in_specs=[row(D_MODEL), mod(0), mod(1), full(g_pre), full(w_ctx_b), full(lbl)],
        out_specs=[row2, row2, row(HG_W)],
        out_shape=[jax.ShapeDtypeStruct((2, n, HG_W), BF16), jax.ShapeDtypeStruct((2, n, HG_W), F32),
                   jax.ShapeDtypeStruct((n, HG_W), BF16)],
        compiler_params=_params(("parallel",)),
        name="proj_ctx",
    )(c2, mod3, mod3, g_pre, w_ctx_b, lbl)


def _scan_tables():
    C = SCAN_CHUNK
    t = np.arange(C)
    lmats, lvls = [], []
    for d in range(2):
        p = t if d == 0 else C - 1 - t
        pt, ps = p[:, None], p[None, :]
        lmat = (ps <= pt).astype(np.float32)
        lmats.append(np.concatenate([lmat, lmat], axis=1))
        lvl = np.full((C, C), -1, np.int32)
        lvl[(pt // SUB == ps // SUB) & (ps <= pt)] = 0
        half, idx = SUB, 1
        while half < C:
            span = 2 * half
            lvl[(pt // span == ps // span) & ((pt // half) % 2 == 1) & ((ps // half) % 2 == 0)] = idx
            half, idx = span, idx + 1
        lvls.append(lvl)
    return jnp.asarray(np.stack(lmats), dtype=BF16), jnp.asarray(np.stack(lvls))


def _scan_cumsum(d, slot, lmat_ref, lf_ref, b_scr):
    lf = lf_ref[...]
    hi = lf.astype(BF16)
    lo = (lf - hi.astype(F32)).astype(BF16)
    b_scr[slot] = _dot(lmat_ref[d], jnp.concatenate([hi, lo], axis=0))


def _scan_head(d, slot, h, masks, k_ref, v_ref, st_ref, b_scr, q_ref=None, o_ref=None):
    C = SCAN_CHUNK
    sl = slice(h * HG_DK, (h + 1) * HG_DK)
    b = b_scr[slot, :, sl]

    def row(i):
        return b_scr[slot, pl.ds(i, 1), sl]

    b_last = row(C - 1 if d == 0 else 0)
    k = k_ref[:, sl]
    v = v_ref[:, sl]
    st = st_ref[slot, :, sl]
    if q_ref is not None:
        q = q_ref[:, sl]
        e0 = jnp.concatenate([b[m * SUB:(m + 1) * SUB] - row(m * SUB + SUB // 2 - 1 + d)
                              for m in range(C // SUB)], axis=0)
        factors = [(jnp.exp2(e0).astype(BF16), jnp.exp2(-e0).astype(BF16))]
        half = SUB
        while half < C:
            span = 2 * half
            pieces = []
            for m in range(C // span):
                ref = row(m * span + half - 1 + d)
                lo_half, hi_half = b[m * span:m * span + half], b[m * span + half:(m + 1) * span]
                pieces += [ref - lo_half, hi_half - ref] if d == 0 else [lo_half - ref, ref - hi_half]
            w = jnp.exp2(jnp.concatenate(pieces, axis=0)).astype(BF16)
            factors.append((w, w))
            half = span
        a = jnp.zeros((C, C), F32)
        for (wq, wk), mask in zip(factors, masks):
            a = jnp.where(mask, _dot_nt(q * wq, k * wk), a)
        qhat = q * jnp.exp2(b).astype(BF16)
        o = _dot(a.astype(BF16), v) + _dot_nt(qhat, st.astype(BF16))
        o_ref[:, sl] = o.astype(o_ref.dtype)
    khat = k * jnp.exp2(b_last - b).astype(BF16)
    st_ref[slot, :, sl] = jnp.exp2(b_last) * st + _dot_tn(v, khat)


def _scan_kernel(n_ctx_steps, lmat_ref, lvl_ref, q_f, k_f, lf_f, v_f, q_b, k_b, lf_b, v_b,
                 kc_f, lfc_f, vc_f, kc_b, lfc_b, vc_b, o_f, o_b, st_ref, b_scr):
    s = pl.program_id(1)
    n_seq = q_f.shape[0]

    @pl.when(s == 0)
    def _():
        st_ref[...] = jnp.zeros_like(st_ref)

    def step(refs, readout):
        chains = [(d, 2 * i + d, [r.at[i] if r is not None else None for r in refs[d]])
                  for i in range(n_seq) for d in range(2)]
        for d, slot, (k, lf, v, q, o) in chains:
            _scan_cumsum(d, slot, lmat_ref, lf, b_scr)
        masks = {d: [lvl_ref[d] == i for i in range(N_LEVELS)] for d in range(2)} if readout else {0: None, 1: None}
        for h in range(HG_HEADS):
            for d, slot, (k, lf, v, q, o) in chains:
                _scan_head(d, slot, h, masks[d], k, v, st_ref, b_scr, q, o)

    @pl.when(s < n_ctx_steps)
    def _():
        step([(kc_f, lfc_f, vc_f, None, None), (kc_b, lfc_b, vc_b, None, None)], False)

    @pl.when(s >= n_ctx_steps)
    def _():
        step([(k_f, lf_f, v_f, q_f, o_f), (k_b, lf_b, v_b, q_b, o_b)], True)


def _hgrn_scan(q, k2, lf2, v, kc2, lfc2, vc, batch, seq, ctx_len):
    C = SCAN_CHUNK
    n_lat, n_ctx = seq // C, ctx_len // C
    n_seq = 4 if batch % 4 == 0 else 2 if batch % 2 == 0 else 1
    lmat, lvl = _scan_tables()
    q, v = (a.reshape(batch, seq, HG_W) for a in (q, v))
    k2, lf2 = (a.reshape(2, batch, seq, HG_W) for a in (k2, lf2))
    vc = vc.reshape(batch, ctx_len, HG_W)
    kc2, lfc2 = (a.reshape(2, batch, ctx_len, HG_W) for a in (kc2, lfc2))

    def lat_blk(d):
        def blk(s):
            j = jnp.maximum(s - n_ctx, 0)
            return j if d == 0 else n_lat - 1 - j
        return blk

    def ctx_blk(d):
        def blk(s):
            i = jnp.minimum(s, n_ctx - 1)
            return i if d == 0 else n_ctx - 1 - i
        return blk

    def plain(blk):
        return pl.BlockSpec((n_seq, C, HG_W), lambda b, s: (b, blk(s), 0))

    def specs(blk_of, with_q):
        out = []
        for d in range(2):
            blk = blk_of(d)
            per_dir = pl.BlockSpec((None, n_seq, C, HG_W), lambda b, s, blk=blk, d=d: (d, b, blk(s), 0))
            out += ([plain(blk)] if with_q else []) + [per_dir, per_dir, plain(blk)]
        return out

    full = lambda a: pl.BlockSpec(a.shape, lambda b, s: (0,) * a.ndim)
    o_shape = jax.ShapeDtypeStruct((batch, seq, HG_W), BF16)
    n_chains = 2 * n_seq
    o_f, o_b = pl.pallas_call(
        functools.partial(_scan_kernel, n_ctx),
        grid=(batch // n_seq, n_ctx + n_lat),
        in_specs=[full(lmat), full(lvl)] + specs(lat_blk, True) + specs(ctx_blk, False),
        out_specs=[plain(lat_blk(d)) for d in range(2)],
        out_shape=[o_shape, o_shape],
        scratch_shapes=[pltpu.VMEM((n_chains, HG_DK, HG_W), F32), pltpu.VMEM((n_chains, C, HG_W), F32)],
        compiler_params=_params(("parallel", "arbitrary")),
        name="hgrn_scan",
    )(lmat, lvl, q, k2, lf2, v, q, k2, lf2, v, kc2, lfc2, vc, kc2, lfc2, vc)
    return o_f.reshape(batch * seq, HG_W), o_b.reshape(batch * seq, HG_W)


def _merge_kernel(of_ref, ob_ref, sg_ref, u_ref, vn_ref, sga_ref, sgb_ref, x_ref, gt1_ref, sh2_ref, sc2_ref,
                  gout_ref, ws_ref, bs_ref, wa_ref, wb_ref, wo_ref, gpost_ref, gffn_ref, wr_ref, br_ref,
                  x1_ref, h2_ref, cw_ref):
    tm = x_ref.shape[0]
    gout = gout_ref[...]
    gw = CM_W // CM_GROUPS
    group = min(MERGE_ROWS, tm)
    groups = [slice(r0, r0 + group) for r0 in range(0, tm, group)]

    def branch_inputs(rows):
        o = of_ref[rows, :].astype(F32) + ob_ref[rows, :].astype(F32)
        sg = sg_ref[rows, :].astype(F32)
        a = jnp.concatenate(
            [_rms(o[:, h * HG_DK:(h + 1) * HG_DK], gout) * sg[:, h * HG_DK:(h + 1) * HG_DK]
             for h in range(HG_HEADS)], axis=1).astype(BF16)
        vn = vn_ref[rows, :]
        z = jnp.concatenate(
            [jnp.concatenate([_dot(ws_ref[g], vn[c * CM_CHUNK:(c + 1) * CM_CHUNK, g * gw:(g + 1) * gw])
                              for g in range(CM_GROUPS)], axis=1) + bs_ref[...]
             for c in range(group // CM_CHUNK)], axis=0)
        return a, (u_ref[rows, :].astype(F32) * z).astype(BF16)

    ab = [branch_inputs(rows) for rows in groups]
    ys = [(sga_ref[rows, :].astype(F32) * _dot(a, wa_ref[...])
           + sgb_ref[rows, :].astype(F32) * _dot(bm, wb_ref[...])).astype(BF16)
          for rows, (a, bm) in zip(groups, ab)]
    yos = [_dot(y, wo_ref[...]) for y in ys]
    for rows, yo in zip(groups, yos):
        x1 = x_ref[rows, :] + gt1_ref[...] * _rms(yo, gpost_ref[...])
        x1_ref[rows, :] = x1
        h2 = (_rms(x1, gffn_ref[...]) * (1.0 + sc2_ref[...]) + sh2_ref[...]).astype(BF16)
        h2_ref[rows, :] = h2
        cw_ref[rows, :] = _route(_dot_nt(wr_ref[...], h2), br_ref[...])


def _merge(o_f, o_b, sg, u, vn, sga, sgb, x2, mod3, g_out, ws_b, bs_full, wa_b, wb_b, wo_b, g_post, g_ffn, wr_b,
           br_cols, seq, tm):
    n = x2.shape[0]
    tpb = seq // tm
    row = lambda w: pl.BlockSpec((tm, w), lambda i: (i, 0))
    full = lambda a: pl.BlockSpec(a.shape, lambda i: (0,) * a.ndim)
    return pl.pallas_call(
        _merge_kernel,
        grid=(n // tm,),
        in_specs=[row(HG_W), row(HG_W), row(HG_W), row(CM_W), row(CM_W),
                  row(D_MODEL), row(D_MODEL), row(D_MODEL), _mod_spec(tpb, 2), _mod_spec(tpb, 3),
                  _mod_spec(tpb, 4), full(g_out), full(ws_b), full(bs_full), full(wa_b), full(wb_b),
                  full(wo_b), full(g_post), full(g_ffn), full(wr_b), full(br_cols)],
        out_specs=[row(D_MODEL), row(D_MODEL), row(wr_b.shape[0])],
        out_shape=[jax.ShapeDtypeStruct((n, D_MODEL), F32), jax.ShapeDtypeStruct((n, D_MODEL), BF16),
                   jax.ShapeDtypeStruct((n, wr_b.shape[0]), F32)],
        compiler_params=_params(("parallel",)),
        name="merge",
    )(o_f, o_b, sg, u, vn, sga, sgb, x2, mod3, mod3, mod3, g_out, ws_b, bs_full, wa_b, wb_b, wo_b, g_post, g_ffn,
      wr_b, br_cols)


def _route(logits, br):
    tm = logits.shape[1]
    gsz = N_EXPERTS // N_GROUPS
    scores = _sigmoid(logits[:N_EXPERTS, :])
    sel = scores + jnp.concatenate([br] * (tm // br.shape[1]), axis=1)
    neg = -jnp.inf

    def first_max(x, ids, sentinel, axis):
        m = jnp.max(x, axis=axis, keepdims=True)
        return m, jnp.min(jnp.where(x == m, ids, sentinel), axis=axis, keepdims=True)

    sel3 = sel.reshape(N_GROUPS, gsz, tm)
    j3 = lax.broadcasted_iota(jnp.int32, sel3.shape, 1)
    m1, i1 = first_max(sel3, j3, gsz, 1)
    gscore = m1 + jnp.max(jnp.where(j3 == i1, neg, sel3), axis=1, keepdims=True)
    g3 = lax.broadcasted_iota(jnp.int32, gscore.shape, 0)
    keep = jnp.zeros(gscore.shape, F32)
    for _ in range(TOPK_GROUPS):
        _, gi = first_max(gscore, g3, N_GROUPS, 0)
        keep = jnp.where(g3 == gi, 1.0, keep)
        gscore = jnp.where(g3 == gi, neg, gscore)
    x = jnp.where(keep > 0.0, sel3, neg).reshape(N_EXPERTS, tm)
    e_i = lax.broadcasted_iota(jnp.int32, x.shape, 0)
    w = jnp.zeros(x.shape, F32)
    for _ in range(TOP_K):
        _, ei = first_max(x, e_i, N_EXPERTS, 0)
        w = jnp.where(e_i == ei, scores, w)
        x = jnp.where(e_i == ei, neg, x)
    w = w / jnp.sum(w, axis=0, keepdims=True) * ROUTED_SCALE
    return jnp.concatenate([w, jnp.zeros_like(w)], axis=0).T


MOE_TILE = 256
MOE_UNIT = BF16_SUBLANES
MOE_BLOCK = 512
GATHER_SLOTS = 3
EXPERT_GRID_STEP = 24
UNITS_PER_BLOCK = MOE_BLOCK // MOE_UNIT
ROW_CHUNK = 512
TILE_ROWS = -(-(MOE_TILE * TOP_K + N_EXPERTS * (MOE_UNIT - 1)) // ROW_CHUNK) * ROW_CHUNK
TILE_UNITS = TILE_ROWS // MOE_UNIT
N_CHUNKS = TILE_ROWS // ROW_CHUNK
FULL_CHUNKS = MOE_TILE * TOP_K // ROW_CHUNK
EAGER_CHUNKS = min(FULL_CHUNKS + 1, N_CHUNKS)
CHUNK_UNITS = ROW_CHUNK // MOE_UNIT
SORT_ROWS = 1024
KEY_W = 128
DIGIT_BITS = 6
DIGIT = 1 << DIGIT_BITS


def _swiglu_act(h, w_gu):
    gu = _dot(h, w_gu)
    de = gu.shape[1] // 2
    g = gu[:, :de]
    return g * _sigmoid(g) * gu[:, de:]


def _token_keys(cw, starts_row):
    t = cw.shape[0]
    routed = cw > 0.0
    t_i = lax.broadcasted_iota(jnp.int32, (t, t), 0)
    s_i = lax.broadcasted_iota(jnp.int32, (t, t), 1)
    rank = _dot((s_i < t_i).astype(BF16), routed.astype(BF16))
    pos = (starts_row + rank).astype(jnp.int32)
    lane = lax.broadcasted_iota(jnp.int32, cw.shape, 1)
    hi = jnp.where(routed, jnp.right_shift(pos, DIGIT_BITS), -1)
    lo = jnp.where(routed, jnp.bitwise_and(pos, DIGIT - 1), -1)
    key_hi = jnp.where(lane < N_EXPERTS, hi * DIGIT,
                       jnp.where(lane == N_EXPERTS, -DIGIT, jnp.where(lane == N_EXPERTS + 1, -1, 0)))
    key_lo = jnp.where(lane < N_EXPERTS, lo, 0)
    return jnp.concatenate([key_hi, key_lo], axis=1).astype(F32).astype(BF16)


def _segment_units(counts):
    return jnp.floor((counts + (MOE_UNIT - 1)) * (1.0 / MOE_UNIT))


def _dispatch_kernel(h_ref, cw_ref, digits_ref, xs_ref, cnt_ref):
    cw = cw_ref[...]
    t = cw.shape[0]
    routed = (cw > 0.0).astype(BF16)
    counts = _dot(jnp.ones((SUBLANES, t), BF16), routed)
    cnt_ref[...] = counts.astype(jnp.int32)
    units = _segment_units(counts)
    e_i = lax.broadcasted_iota(jnp.int32, (KEY_W, KEY_W), 0)
    f_i = lax.broadcasted_iota(jnp.int32, (KEY_W, KEY_W), 1)
    starts = _dot(units.astype(BF16), (e_i < f_i).astype(BF16)) * MOE_UNIT
    ends = starts + units * MOE_UNIT
    keys_t = _token_keys(cw, starts[:1]).astype(F32).T.astype(BF16)
    h = h_ref[...]
    used_rows = jnp.max(ends)

    def row_hits(r0, n):
        rows = slice(r0, r0 + n)
        lane = lax.broadcasted_iota(jnp.int32, (n, KEY_W), 1)
        r = (lax.broadcasted_iota(jnp.int32, (n, KEY_W), 0) + r0).astype(F32)
        in_seg = (r >= starts[:1]) & (r < ends[:1])
        rmap = jnp.where(lane < N_EXPERTS, in_seg.astype(F32), digits_ref[rows, :].astype(F32)).astype(BF16)
        return (_dot(jnp.concatenate([rmap, rmap], axis=1), keys_t) == 0.0).astype(BF16)

    def sort_rows(r0, n):
        xs_ref[r0:r0 + n, :] = _dot(row_hits(r0, n), h).astype(xs_ref.dtype)

    eager_rows = EAGER_CHUNKS * ROW_CHUNK
    pieces = [(r0, min(SORT_ROWS, eager_rows - r0)) for r0 in range(0, eager_rows, SORT_ROWS)]
    hits = [row_hits(r0, n) for r0, n in pieces]
    for (r0, n), hit in zip(pieces, hits):
        xs_ref[r0:r0 + n, :] = _dot(hit, h).astype(xs_ref.dtype)
    for c in range(EAGER_CHUNKS, N_CHUNKS):
        pl.when(used_rows > c * ROW_CHUNK)(functools.partial(sort_rows, c * ROW_CHUNK, ROW_CHUNK))

        @pl.when(used_rows <= c * ROW_CHUNK)
        def _(c=c):
            xs_ref[c * ROW_CHUNK:(c + 1) * ROW_CHUNK, :] = jnp.zeros((ROW_CHUNK, D_MODEL), xs_ref.dtype)


def _dispatch(h2, cw, digits):
    n = h2.shape[0]
    n_tiles = n // MOE_TILE
    return pl.pallas_call(
        _dispatch_kernel,
        grid=(n_tiles,),
        in_specs=[pl.BlockSpec((MOE_TILE, D_MODEL), lambda i: (i, 0)),
                  pl.BlockSpec((MOE_TILE, KEY_W), lambda i: (i, 0)),
                  pl.BlockSpec(digits.shape, lambda i: (0, 0))],
        out_specs=[pl.BlockSpec((TILE_ROWS, D_MODEL), lambda i: (i, 0)),
                   pl.BlockSpec((SUBLANES, KEY_W), lambda i: (i, 0))],
        out_shape=[jax.ShapeDtypeStruct((n_tiles * TILE_ROWS, D_MODEL), BF16),
                   jax.ShapeDtypeStruct((n_tiles * SUBLANES, KEY_W), jnp.int32)],
        compiler_params=_params(("parallel",)),
        name="moe_dispatch",
    )(h2, cw, digits)


def _unit_copy(src_hbm, unit, dst, slot, pos, sem):
    return pltpu.make_async_copy(
        src_hbm.at[pl.ds(pl.multiple_of(unit * MOE_UNIT, MOE_UNIT), MOE_UNIT)],
        dst.at[slot, pl.ds(pos * MOE_UNIT, MOE_UNIT)], sem.at[slot])


def _experts_kernel(be_ref, src_ref, nb_ref, wplan_ref, xs_hbm, wgu_hbm, wdn_hbm, ys_ref, xbuf, sem,
                    wgu_f, wdn_f, wsem, wgu_b, wdn_b):
    j = pl.program_id(0)
    nb = nb_ref[0]
    n_steps = pl.num_programs(0)

    def copies(blk, slot):
        return [_unit_copy(xs_hbm, src_ref[blk * UNITS_PER_BLOCK + u], xbuf, slot, u, sem)
                for u in range(UNITS_PER_BLOCK)]

    def fetch(blk, slot):
        for cp in copies(blk, slot):
            cp.start()

    def weight_copies(expert, slot):
        return [pltpu.make_async_copy(wgu_hbm.at[expert], wgu_f.at[slot], wsem.at[slot]),
                pltpu.make_async_copy(wdn_hbm.at[expert], wdn_f.at[slot], wsem.at[slot])]

    ahead = GATHER_SLOTS - 1

    @pl.when(j == 0)
    def _():
        for a in range(ahead):
            fetch(jnp.minimum(a, nb - 1), a)
        for cp in weight_copies(be_ref[0], 0):
            cp.start()

    @pl.when((j < nb) & (wplan_ref[j] == 1))
    def _():
        slot = wplan_ref[n_steps + j]
        for cp in weight_copies(be_ref[j], slot):
            cp.wait()
        nxt = wplan_ref[2 * n_steps + j]

        @pl.when(nxt >= 0)
        def _():
            for cp in weight_copies(nxt, 1 - slot):
                cp.start()

        wgu_b[...] = wgu_f[slot].astype(BF16)
        wdn_b[...] = wdn_f[slot].astype(BF16)

    @pl.when(j < nb)
    def _():
        slot = j % GATHER_SLOTS
        for cp in copies(j, slot):
            cp.wait()
        fetch(jnp.minimum(j + ahead, nb - 1), (j + ahead) % GATHER_SLOTS)

    def ffn(slot, n_rows):
        act = _swiglu_act(xbuf[slot, :n_rows, :], wgu_b[...])
        ys_ref[:n_rows, :] = _dot(act.astype(BF16), wdn_b[...]).astype(ys_ref.dtype)

    half_full = wplan_ref[3 * n_steps + j] <= UNITS_PER_BLOCK // 2

    @pl.when((j < nb) & jnp.logical_not(half_full))
    def _():
        ffn(j % GATHER_SLOTS, MOE_BLOCK)

    @pl.when((j < nb) & half_full)
    def _():
        ffn(j % GATHER_SLOTS, MOE_BLOCK // 2)
        ys_ref[MOE_BLOCK // 2:, :] = jnp.zeros((MOE_BLOCK // 2, D_MODEL), ys_ref.dtype)

    @pl.when(j == nb - 1)
    def _():
        for a in range(1, GATHER_SLOTS):
            for cp in copies(j, (j + a) % GATHER_SLOTS):
                cp.wait()

    @pl.when(j >= nb)
    def _():
        ys_ref[...] = jnp.zeros_like(ys_ref)


def _weight_plan(block_expert, block_units, n_blocks_used):
    nb = block_expert.shape[0]
    jb = jnp.arange(nb, dtype=jnp.int32)
    used = jb < n_blocks_used[0]
    first = used & ((jb == 0) | (block_expert != jnp.roll(block_expert, 1)))
    slot = (jnp.cumsum(first.astype(jnp.int32)) - 1) % 2
    first_at = jnp.where(first, jb, nb)
    nxt_first = jnp.min(jnp.where(first_at[None, :] > jb[:, None], first_at[None, :], nb), axis=1)
    nxt_expert = jnp.sum(jnp.where(jb[None, :] == nxt_first[:, None], block_expert[None, :], 0), axis=1)
    nxt = jnp.where(nxt_first < nb, nxt_expert, -1)
    return jnp.concatenate([first.astype(jnp.int32), slot.astype(jnp.int32), nxt.astype(jnp.int32),
                            block_units.astype(jnp.int32)])


def _experts(xs, block_expert, src_units, block_units, n_blocks_used, w_gu, w_dn):
    nb_max = block_expert.shape[0]
    any_spec = pl.BlockSpec(memory_space=pl.ANY)
    grid_spec = pltpu.PrefetchScalarGridSpec(
        num_scalar_prefetch=4,
        grid=(nb_max,),
        in_specs=[any_spec, any_spec, any_spec],
        out_specs=pl.BlockSpec((MOE_BLOCK, D_MODEL), lambda j, be, src, nb, wplan: (j, 0)),
        scratch_shapes=[pltpu.VMEM((GATHER_SLOTS, MOE_BLOCK, D_MODEL), BF16),
                        pltpu.SemaphoreType.DMA((GATHER_SLOTS,)),
                        pltpu.VMEM((2, D_MODEL, 2 * D_EXPERT), F32), pltpu.VMEM((2, D_EXPERT, D_MODEL), F32),
                        pltpu.SemaphoreType.DMA((2,)),
                        pltpu.VMEM((D_MODEL, 2 * D_EXPERT), BF16), pltpu.VMEM((D_EXPERT, D_MODEL), BF16)],
    )
    return pl.pallas_call(
        _experts_kernel,
        grid_spec=grid_spec,
        out_shape=jax.ShapeDtypeStruct((nb_max * MOE_BLOCK, D_MODEL), BF16),
        compiler_params=_params(("arbitrary",)),
        name="moe_experts",
    )(block_expert, src_units, n_blocks_used, _weight_plan(block_expert, block_units, n_blocks_used), xs, w_gu,
      w_dn)


def _combine_kernel(src_ref, used_ref, ys_hbm, cw_ref, h_ref, x1_ref, gt2_ref, gpost_ref, digits_t_ref,
                    wsgu_ref, wsdn_ref, o_ref, ybuf, sem, acc_ref):
    i = pl.program_id(0)

    def copies(tile, slot, c):
        return [_unit_copy(ys_hbm, src_ref[tile * TILE_UNITS + u], ybuf, slot, u, sem)
                for u in range(c * CHUNK_UNITS, (c + 1) * CHUNK_UNITS)]

    def chunk_used(tile, c):
        return used_ref[tile] > c * CHUNK_UNITS

    def for_used_chunks(tile, fn):
        for c in range(N_CHUNKS):
            if c < EAGER_CHUNKS:
                fn(c)
            else:
                pl.when(chunk_used(tile, c))(functools.partial(fn, c))

    def fetch(tile, slot):
        def start(c):
            for cp in copies(tile, slot, c):
                cp.start()

        for_used_chunks(tile, start)

    def wait_all(tile, slot):
        def wait(c):
            for cp in copies(tile, slot, c):
                cp.wait()

        for_used_chunks(tile, wait)

    @pl.when(i == 0)
    def _():
        fetch(0, 0)

    @pl.when(i + 1 < pl.num_programs(0))
    def _():
        fetch(i + 1, (i + 1) % 2)

    cw = cw_ref[...]
    t = cw.shape[0]
    routed = (cw > 0.0).astype(BF16)
    e_i = lax.broadcasted_iota(jnp.int32, (KEY_W, KEY_W), 0)
    f_i = lax.broadcasted_iota(jnp.int32, (KEY_W, KEY_W), 1)
    units = _segment_units(_dot_tn(routed, jnp.ones((t, KEY_W), BF16)))
    starts = _dot((f_i < e_i).astype(BF16), units.astype(BF16)) * MOE_UNIT
    ends = starts + units * MOE_UNIT
    units_row = _segment_units(_dot(jnp.ones((SUBLANES, t), BF16), routed))
    starts_row = _dot(units_row.astype(BF16), (e_i < f_i).astype(BF16)) * MOE_UNIT
    keys = _token_keys(cw, starts_row[:1])
    wb = cw.astype(BF16)

    f = _dot(_swiglu_act(h_ref[...], wsgu_ref[...]).astype(BF16), wsdn_ref[...])
    slot = i % 2
    reps = ROW_CHUNK // KEY_W
    starts_c = jnp.concatenate([starts] * reps, axis=1)
    ends_c = jnp.concatenate([ends] * reps, axis=1)
    sub = lax.broadcasted_iota(jnp.int32, (KEY_W, ROW_CHUNK), 0)

    wait_all(i, slot)

    def chunk_weights(c):
        rows = slice(c * ROW_CHUNK, (c + 1) * ROW_CHUNK)
        r = (lax.broadcasted_iota(jnp.int32, (KEY_W, ROW_CHUNK), 1) + c * ROW_CHUNK).astype(F32)
        in_seg = (r >= starts_c) & (r < ends_c)
        rmap_t = jnp.where(sub < N_EXPERTS, in_seg.astype(F32), digits_t_ref[:, rows].astype(F32)).astype(BF16)
        hit = _dot(keys, jnp.concatenate([rmap_t, rmap_t], axis=0)) == 0.0
        return jnp.where(hit, _dot(wb, rmap_t), 0.0).astype(BF16)

    def chunk_sum(c):
        return _dot(chunk_weights(c), ybuf[slot, c * ROW_CHUNK:(c + 1) * ROW_CHUNK, :])

    pw = jnp.concatenate([chunk_weights(c) for c in range(EAGER_CHUNKS)], axis=1)
    acc_ref[...] = f + _dot(pw, ybuf[slot, :EAGER_CHUNKS * ROW_CHUNK, :])
    for c in range(EAGER_CHUNKS, N_CHUNKS):
        @pl.when(chunk_used(i, c))
        def _(c=c):
            acc_ref[...] += chunk_sum(c)
    o_ref[...] = x1_ref[...] + gt2_ref[...] * _rms(acc_ref[...], gpost_ref[...])


def _combine(ys, src_units, used_units, cw, h2, x1, mod3, g_post, digits_t, wsgu_b, wsdn_b, seq):
    n = h2.shape[0]
    tpb = seq // MOE_TILE
    row = lambda w: pl.BlockSpec((MOE_TILE, w), lambda i, src, used: (i, 0))
    full = lambda a: pl.BlockSpec(a.shape, lambda i, src, used: (0,) * a.ndim)
    grid_spec = pltpu.PrefetchScalarGridSpec(
        num_scalar_prefetch=2,
        grid=(n // MOE_TILE,),
        in_specs=[pl.BlockSpec(memory_space=pl.ANY), row(KEY_W), row(D_MODEL), row(D_MODEL),
                  pl.BlockSpec((None, 1, D_MODEL), lambda i, src, used: (i // tpb, 0, 5)), full(g_post),
                  full(digits_t), full(wsgu_b), full(wsdn_b)],
        out_specs=row(D_MODEL),
        scratch_shapes=[pltpu.VMEM((2, TILE_ROWS, D_MODEL), BF16), pltpu.SemaphoreType.DMA((2,)),
                        pltpu.VMEM((MOE_TILE, D_MODEL), F32)],
    )
    return pl.pallas_call(
        _combine_kernel,
        grid_spec=grid_spec,
        out_shape=jax.ShapeDtypeStruct((n, D_MODEL), F32),
        compiler_params=_params(("arbitrary",)),
        name="moe_combine",
    )(src_units, used_units, ys, cw, h2, x1, mod3, g_post, digits_t, wsgu_b, wsdn_b)


def _row_digits():
    r = np.arange(TILE_ROWS)
    d = np.zeros((TILE_ROWS, KEY_W), np.float32)
    d[:, N_EXPERTS] = r // DIGIT
    d[:, N_EXPERTS + 1] = r % DIGIT
    return jnp.asarray(d, dtype=BF16)


def _moe_plan(counts, nb_max):
    n_tiles = counts.shape[0]
    s = (counts + (MOE_UNIT - 1)) // MOE_UNIT
    local = jnp.cumsum(s, axis=1) - s
    cs = jnp.cumsum(s, axis=0)
    per_expert = cs[-1]
    padded = (per_expert + UNITS_PER_BLOCK - 1) // UNITS_PER_BLOCK * UNITS_PER_BLOCK
    g_end = jnp.cumsum(padded)
    g_start = g_end - padded
    seg_start = g_start[None, :] + cs - s
    n_blocks_used = (g_end[-1] // UNITS_PER_BLOCK).astype(jnp.int32).reshape(1)
    jb = jnp.arange(nb_max, dtype=jnp.int32)
    one_e = ((jb[:, None] >= (g_start // UNITS_PER_BLOCK)[None, :])
             & (jb[:, None] < (g_end // UNITS_PER_BLOCK)[None, :])).astype(jnp.int32)
    pick_e = lambda table: jnp.sum(one_e[:, :, None] * table.T[None, :, :], axis=1)
    block_expert = jnp.where(jb < n_blocks_used[0], jnp.sum(one_e * jnp.arange(N_EXPERTS, dtype=jnp.int32), axis=1),
                             N_EXPERTS - 1).astype(jnp.int32)
    cs_b, s_b, local_b = pick_e(cs), pick_e(s), pick_e(local)
    q = (jb * UNITS_PER_BLOCK - jnp.sum(one_e * g_start[None, :], axis=1))[:, None] \
        + jnp.arange(UNITS_PER_BLOCK, dtype=jnp.int32)[None, :]
    tile = jnp.minimum(jnp.sum(cs_b[:, None, :] <= q[:, :, None], axis=2), n_tiles - 1)
    one_t = (tile[:, :, None] == jnp.arange(n_tiles, dtype=jnp.int32)).astype(jnp.int32)
    src = tile * TILE_UNITS + q + jnp.sum(one_t * (local_b - cs_b + s_b)[:, None, :], axis=2)
    valid = q < jnp.sum(one_e * per_expert[None, :], axis=1)[:, None]
    src_units = jnp.where(valid, src, 0).astype(jnp.int32).reshape(-1)
    block_units = jnp.sum(valid, axis=1)
    u = jnp.arange(TILE_UNITS, dtype=jnp.int32)
    seg_end = local + s
    eu = jnp.minimum(jnp.sum(seg_end[:, None, :] <= u[None, :, None], axis=2), N_EXPERTS - 1)
    one_u = (eu[:, :, None] == jnp.arange(N_EXPERTS, dtype=jnp.int32)).astype(jnp.int32)
    back = u[None, :] + jnp.sum(one_u * (seg_start - local)[:, None, :], axis=2)
    back_units = jnp.where(u[None, :] < seg_end[:, -1:], back, 0).astype(jnp.int32).reshape(-1)
    return block_expert, src_units, block_units, n_blocks_used, back_units, seg_end[:, -1].astype(jnp.int32)


def _tile(n, pref):
    t = pref
    while n % t:
        t //= 2
    return t


def kernel(x, c, ctx, c_ctx, w_ada, b_ada, g_pre_mix, g_post_mix, g_pre_ffn, g_post_ffn, w_in, lb_logits, g_hgrn_out, cm_ln_g, cm_ln_b, w_spatial, b_spatial, w_branch_a, w_branch_b, w_out, w_router, b_router, w_expert_gu, w_expert_down, w_shared_gu, w_shared_down):
    B, T, D = x.shape
    L = ctx.shape[1]
    assert D == D_MODEL and w_ada.shape[0] == 1 and T % SCAN_CHUNK == 0 and L % SCAN_CHUNK == 0
    assert T % MOE_TILE == 0 and MOE_TILE % CM_CHUNK == 0
    l = 0
    row = lambda a: a[l].reshape(1, -1)

    n_rows = -(-(B + 1) // BF16_SUBLANES) * BF16_SUBLANES
    cs = jnp.zeros((n_rows, D), F32).at[:B].set(c).at[B].set(c_ctx)
    mod3 = _ada_mod(cs, w_ada[l], row(b_ada)).reshape(n_rows, 1, 6 * D)

    w_in_b = w_in[l].astype(BF16)
    lbl = lb_logits[:, l:l + 2].reshape(4, HG_W)
    x2 = x.reshape(B * T, D)
    q, k2, lf2, v, sg, u, vn, sga, sgb = _proj_lat(
        x2, mod3, row(g_pre_mix), w_in_b, lbl, row(cm_ln_g), row(cm_ln_b), T, _tile(T, PROJ_TILE))
    kc2, lfc2, vc = _proj_ctx(ctx.reshape(B * L, D), mod3, B, row(g_pre_mix), w_in_b[:, HG_W:4 * HG_W], lbl,
                              _tile(B * L, CTX_TILE))

    o_f, o_b = _hgrn_scan(q, k2, lf2, v, kc2, lfc2, vc, B, T, L)

    bs_full = jnp.repeat(b_spatial[l], CM_W // CM_GROUPS, axis=1)
    x1, h2, cw = _merge(
        o_f, o_b, sg, u, vn, sga, sgb, x2, mod3, row(g_hgrn_out), w_spatial[l].astype(BF16), bs_full,
        w_branch_a[l].astype(BF16), w_branch_b[l].astype(BF16), w_out[l].astype(BF16), row(g_post_mix),
        row(g_pre_ffn), jnp.pad(w_router[l].T, ((0, KEY_W - N_EXPERTS), (0, 0))).astype(BF16),
        jnp.broadcast_to(b_router[l][:, None], (N_EXPERTS, LANES)), T, _tile(T, MERGE_TILE))

    n_tok = B * T
    n_tiles = n_tok // MOE_TILE
    digits = _row_digits()
    xs, cnt = _dispatch(h2, cw, digits)
    counts = cnt.reshape(n_tiles, SUBLANES, KEY_W)[:, 0, :N_EXPERTS]
    max_units = (n_tok * TOP_K + n_tiles * N_EXPERTS * (MOE_UNIT - 1)) // MOE_UNIT + N_EXPERTS * (UNITS_PER_BLOCK - 1)
    nb_max = -(-max_units // UNITS_PER_BLOCK)
    block_expert, src_units, block_units, n_blocks_used, back_units, tile_units = _moe_plan(counts, nb_max)
    wsgu_b, wsdn_b = w_shared_gu[l].astype(BF16), w_shared_down[l].astype(BF16)

    nb_min = -(-(n_tok * TOP_K) // MOE_BLOCK)
    grids = sorted({min(nb_max, g) for g in range(nb_min + EXPERT_GRID_STEP, nb_max + EXPERT_GRID_STEP,
                                                  EXPERT_GRID_STEP)})

    def experts_and_combine(nb_grid):
        def run():
            ys = _experts(xs, block_expert[:nb_grid], src_units[:nb_grid * UNITS_PER_BLOCK], block_units[:nb_grid],
                          n_blocks_used, w_expert_gu[l], w_expert_down[l])
            return _combine(ys, back_units, tile_units, cw, h2, x1, mod3, row(g_post_ffn), digits.T,
                            wsgu_b, wsdn_b, T)
        return run

    which = jnp.sum(n_blocks_used[0] > jnp.asarray(grids[:-1], jnp.int32))
    out = lax.switch(which, [experts_and_combine(g) for g in grids])
    return out.reshape(B, T, D)
```

```python
import functools

import numpy as np
import jax
import jax.numpy as jnp
from jax import lax
from jax.experimental import pallas as pl
from jax.experimental.pallas import tpu as pltpu

F32 = jnp.float32
BF16 = jnp.bfloat16

D_MODEL = 1024
EPS = 1e-6
HG_HEADS = 4
HG_DK = 128
HG_W = HG_HEADS * HG_DK
CM_W = 512
CM_CHUNK = 128
CM_GROUPS = 4
N_EXPERTS = 64
TOP_K = 8
N_GROUPS = 8
TOPK_GROUPS = 4
D_EXPERT = 256
ROUTED_SCALE = 2.5
LANES = 128
SUBLANES = 8
BF16_SUBLANES = 16
SCAN_CHUNK = 128
SUB = 16
N_LEVELS = (SCAN_CHUNK // SUB).bit_length()
PROJ_TILE, PROJ_ROWS = 1024, 256
CTX_TILE = 256
MERGE_TILE, MERGE_ROWS = 1024, 256
V7X_VMEM_BYTES = 64 * 1024 * 1024
VMEM_LIMIT = V7X_VMEM_BYTES - 8 * 1024 * 1024


def _params(sem):
    return pltpu.CompilerParams(dimension_semantics=sem, vmem_limit_bytes=VMEM_LIMIT)


def _dot(a, b):
    return jnp.dot(a, b, preferred_element_type=F32)


def _dot_nt(a, b):
    return lax.dot_general(a, b, (((1,), (1,)), ((), ())), preferred_element_type=F32)


def _dot_tn(a, b):
    return lax.dot_general(a, b, (((0,), (0,)), ((), ())), preferred_element_type=F32)


def _sigmoid(x):
    return 0.5 * jnp.tanh(0.5 * x) + 0.5


def _rms(x, g):
    return x * lax.rsqrt(jnp.mean(x * x, axis=-1, keepdims=True) + EPS) * g


def _ada_kernel(c_ref, w_ref, b_ref, o_ref):
    c = c_ref[...]
    s = c * _sigmoid(c)
    o_ref[...] = _dot(s.astype(BF16), w_ref[...].astype(BF16)) + b_ref[...]


def _ada_mod(cs, w_ada, b_ada):
    rows = cs.shape[0]
    n_out = w_ada.shape[1]
    return pl.pallas_call(
        _ada_kernel,
        grid=(n_out // D_MODEL,),
        in_specs=[
            pl.BlockSpec((rows, D_MODEL), lambda j: (0, 0)),
            pl.BlockSpec((D_MODEL, D_MODEL), lambda j: (0, j)),
            pl.BlockSpec((1, D_MODEL), lambda j: (0, j)),
        ],
        out_specs=pl.BlockSpec((rows, D_MODEL), lambda j: (0, j)),
        out_shape=jax.ShapeDtypeStruct((rows, n_out), F32),
        compiler_params=_params(("parallel",)),
        name="ada_mod",
    )(cs, w_ada, b_ada)


def _lower_bounds(lbl):
    out = []
    for d in range(2):
        l0, l1 = lbl[2 * d:2 * d + 1], lbl[2 * d + 1:2 * d + 2]
        m = jnp.maximum(l0, l1)
        e0, e1 = jnp.exp(l0 - m), jnp.exp(l1 - m)
        out.append(e0 / (e0 + e1))
    return out


def _prenorm(x_ref, sh_ref, sc_ref, g_ref, rows=slice(None)):
    return (_rms(x_ref[rows, :], g_ref[...]) * (1.0 + sc_ref[...]) + sh_ref[...]).astype(BF16)


def _gates(z, lb, k_ref, lf_ref, d, rows=slice(None)):
    half_t = 0.5 * jnp.tanh(0.5 * z)
    k_ref[d, rows, :] = ((1.0 - lb) * (0.5 - half_t)).astype(k_ref.dtype)
    lf_ref[d, rows, :] = jnp.log2(lb + (1.0 - lb) * (0.5 + half_t))


def _proj_lat_kernel(x_ref, sh_ref, sc_ref, g_ref, w_ref, lbl_ref, lng_ref, lnb_ref,
                     q_ref, k_ref, lf_ref, v_ref, sg_ref, u_ref, vn_ref, sga_ref, sgb_ref):
    tm = x_ref.shape[0]
    group = min(PROJ_ROWS, tm)
    groups = [slice(r0, r0 + group) for r0 in range(0, tm, group)]
    hbs = [_prenorm(x_ref, sh_ref, sc_ref, g_ref, rows) for rows in groups]
    lbs = _lower_bounds(lbl_ref[...])

    def columns(lo, width, epilogue):
        zs = [_dot(hb, w_ref[:, lo:lo + width]) for hb in hbs]
        for rows, z in zip(groups, zs):
            epilogue(rows, z)

    def silu_to(ref):
        def epilogue(rows, z):
            ref[rows, :] = (z * _sigmoid(z)).astype(ref.dtype)
        return epilogue

    def sigmoid_to(ref):
        def epilogue(rows, z):
            ref[rows, :] = _sigmoid(z).astype(ref.dtype)
        return epilogue

    def gelu_to_u(rows, z):
        u_ref[rows, :] = jax.nn.gelu(z).astype(u_ref.dtype)

    def gelu_layernorm_to_vn(rows, z):
        vv = jax.nn.gelu(z)
        vc = vv - jnp.mean(vv, axis=-1, keepdims=True)
        vn = vc * lax.rsqrt(jnp.mean(vc * vc, axis=-1, keepdims=True) + EPS)
        vn_ref[rows, :] = (vn * lng_ref[...] + lnb_ref[...]).astype(vn_ref.dtype)

    def cast_to_v(rows, z):
        v_ref[rows, :] = z.astype(v_ref.dtype)

    columns(0, HG_W, silu_to(q_ref))
    for d in range(2):
        columns((1 + d) * HG_W, HG_W, lambda rows, z, d=d: _gates(z, lbs[d], k_ref, lf_ref, d, rows))
    columns(4 * HG_W, HG_W, silu_to(sg_ref))
    columns(5 * HG_W, CM_W, gelu_to_u)
    columns(5 * HG_W + CM_W, CM_W, gelu_layernorm_to_vn)
    base = 5 * HG_W + 2 * CM_W
    columns(base, D_MODEL, sigmoid_to(sga_ref))
    columns(base + D_MODEL, D_MODEL, sigmoid_to(sgb_ref))
    columns(3 * HG_W, HG_W, cast_to_v)


def _mod_spec(rows_per_batch_tiles, col):
    return pl.BlockSpec((None, 1, D_MODEL), lambda i: (i // rows_per_batch_tiles, 0, col))


def _proj_lat(x2, mod3, g_pre, w_in_b, lbl, ln_g, ln_b, seq, tm):
    n = x2.shape[0]
    tpb = seq // tm
    row = lambda w: pl.BlockSpec((tm, w), lambda i: (i, 0))
    row2 = pl.BlockSpec((2, tm, HG_W), lambda i: (0, i, 0))
    full = lambda a: pl.BlockSpec(a.shape, lambda i: (0,) * a.ndim)
    outs = [
        (row(HG_W), jax.ShapeDtypeStruct((n, HG_W), BF16)),
        (row2, jax.ShapeDtypeStruct((2, n, HG_W), BF16)),
        (row2, jax.ShapeDtypeStruct((2, n, HG_W), F32)),
        (row(HG_W), jax.ShapeDtypeStruct((n, HG_W), BF16)),
        (row(HG_W), jax.ShapeDtypeStruct((n, HG_W), BF16)),
        (row(CM_W), jax.ShapeDtypeStruct((n, CM_W), BF16)),
        (row(CM_W), jax.ShapeDtypeStruct((n, CM_W), BF16)),
        (row(D_MODEL), jax.ShapeDtypeStruct((n, D_MODEL), BF16)),
        (row(D_MODEL), jax.ShapeDtypeStruct((n, D_MODEL), BF16)),
    ]
    return pl.pallas_call(
        _proj_lat_kernel,
        grid=(n // tm,),
        in_specs=[row(D_MODEL), _mod_spec(tpb, 0), _mod_spec(tpb, 1), full(g_pre),
                  pl.BlockSpec(w_in_b.shape, lambda i: (0, 0), pipeline_mode=pl.Buffered(1)),
                  full(lbl), full(ln_g), full(ln_b)],
        out_specs=[o[0] for o in outs],
        out_shape=[o[1] for o in outs],
        compiler_params=_params(("parallel",)),
        name="proj_lat",
    )(x2, mod3, mod3, g_pre, w_in_b, lbl, ln_g, ln_b)


def _proj_ctx_kernel(x_ref, sh_ref, sc_ref, g_ref, w_ref, lbl_ref, k_ref, lf_ref, v_ref):
    hb = _prenorm(x_ref, sh_ref, sc_ref, g_ref)
    lbs = _lower_bounds(lbl_ref[...])
    for d in range(2):
        _gates(_dot(hb, w_ref[:, d * HG_W:(d + 1) * HG_W]), lbs[d], k_ref, lf_ref, d)
    v_ref[...] = _dot(hb, w_ref[:, 2 * HG_W:3 * HG_W]).astype(v_ref.dtype)


def _proj_ctx(c2, mod3, ctx_row, g_pre, w_ctx_b, lbl, tm):
    n = c2.shape[0]
    row = lambda w: pl.BlockSpec((tm, w), lambda i: (i, 0))
    row2 = pl.BlockSpec((2, tm, HG_W), lambda i: (0, i, 0))
    full = lambda a: pl.BlockSpec(a.shape, lambda i: (0,) * a.ndim)
    mod = lambda col: pl.BlockSpec((None, 1, D_MODEL), lambda i: (ctx_row, 0, col))
    return pl.pallas_call(
        _proj_ctx_kernel,
        grid=(n // tm,),
        in_specs=[row(D_MODEL), mod(0), mod(1), full(g_pre), full(w_ctx_b), full(lbl)],
        out_specs=[row2, row2, row(HG_W)],
        out_shape=[jax.ShapeDtypeStruct((2, n, HG_W), BF16), jax.ShapeDtypeStruct((2, n, HG_W), F32),
                   jax.ShapeDtypeStruct((n, HG_W), BF16)],
        compiler_params=_params(("parallel",)),
        name="proj_ctx",
    )(c2, mod3, mod3, g_pre, w_ctx_b, lbl)


def _scan_tables():
    C = SCAN_CHUNK
    t = np.arange(C)
    lmats, lvls = [], []
    for d in range(2):
        p = t if d == 0 else C - 1 - t
        pt, ps = p[:, None], p[None, :]
        lmat = (ps <= pt).astype(np.float32)
        lmats.append(np.concatenate([lmat, lmat], axis=1))
        lvl = np.full((C, C), -1, np.int32)
        lvl[(pt // SUB == ps // SUB) & (ps <= pt)] = 0
        half, idx = SUB, 1
        while half < C:
            span = 2 * half
            lvl[(pt // span == ps // span) & ((pt // half) % 2 == 1) & ((ps // half) % 2 == 0)] = idx
            half, idx = span, idx + 1
        lvls.append(lvl)
    return jnp.asarray(np.stack(lmats), dtype=BF16), jnp.asarray(np.stack(lvls))


def _scan_cumsum(d, slot, lmat_ref, lf_ref, b_scr):
    lf = lf_ref[...]
    hi = lf.astype(BF16)
    lo = (lf - hi.astype(F32)).astype(BF16)
    b_scr[slot] = _dot(lmat_ref[d], jnp.concatenate([hi, lo], axis=0))


def _scan_head(d, slot, h, masks, k_ref, v_ref, st_ref, b_scr, q_ref=None, o_ref=None):
    C = SCAN_CHUNK
    sl = slice(h * HG_DK, (h + 1) * HG_DK)
    b = b_scr[slot, :, sl]

    def row(i):
        return b_scr[slot, pl.ds(i, 1), sl]

    b_last = row(C - 1 if d == 0 else 0)
    k = k_ref[:, sl]
    v = v_ref[:, sl]
    st = st_ref[slot, :, sl]
    if q_ref is not None:
        q = q_ref[:, sl]
        e0 = jnp.concatenate([b[m * SUB:(m + 1) * SUB] - row(m * SUB + SUB // 2 - 1 + d)
                              for m in range(C // SUB)], axis=0)
        factors = [(jnp.exp2(e0).astype(BF16), jnp.exp2(-e0).astype(BF16))]
        half = SUB
        while half < C:
            span = 2 * half
            pieces = []
            for m in range(C // span):
                ref = row(m * span + half - 1 + d)
                lo_half, hi_half = b[m * span:m * span + half], b[m * span + half:(m + 1) * span]
                pieces += [ref - lo_half, hi_half - ref] if d == 0 else [lo_half - ref, ref - hi_half]
            w = jnp.exp2(jnp.concatenate(pieces, axis=0)).astype(BF16)
            factors.append((w, w))
            half = span
        a = jnp.zeros((C, C), F32)
        for (wq, wk), mask in zip(factors, masks):
            a = jnp.where(mask, _dot_nt(q * wq, k * wk), a)
        qhat = q * jnp.exp2(b).astype(BF16)
        o = _dot(a.astype(BF16), v) + _dot_nt(qhat, st.astype(BF16))
        o_ref[:, sl] = o.astype(o_ref.dtype)
    khat = k * jnp.exp2(b_last - b).astype(BF16)
    st_ref[slot, :, sl] = jnp.exp2(b_last) * st + _dot_tn(v, khat)


def _scan_kernel(n_ctx_steps, lmat_ref, lvl_ref, q_f, k_f, lf_f, v_f, q_b, k_b, lf_b, v_b,
                 kc_f, lfc_f, vc_f, kc_b, lfc_b, vc_b, o_f, o_b, st_ref, b_scr):
    s = pl.program_id(1)
    n_seq = q_f.shape[0]

    @pl.when(s == 0)
    def _():
        st_ref[...] = jnp.zeros_like(st_ref)

    def step(refs, readout):
        chains = [(d, 2 * i + d, [r.at[i] if r is not None else None for r in refs[d]])
                  for i in range(n_seq) for d in range(2)]
        for d, slot, (k, lf, v, q, o) in chains:
            _scan_cumsum(d, slot, lmat_ref, lf, b_scr)
        masks = {d: [lvl_ref[d] == i for i in range(N_LEVELS)] for d in range(2)} if readout else {0: None, 1: None}
        for h in range(HG_HEADS):
            for d, slot, (k, lf, v, q, o) in chains:
                _scan_head(d, slot, h, masks[d], k, v, st_ref, b_scr, q, o)

    @pl.when(s < n_ctx_steps)
    def _():
        step([(kc_f, lfc_f, vc_f, None, None), (kc_b, lfc_b, vc_b, None, None)], False)

    @pl.when(s >= n_ctx_steps)
    def _():
        step([(k_f, lf_f, v_f, q_f, o_f), (k_b, lf_b, v_b, q_b, o_b)], True)


def _hgrn_scan(q, k2, lf2, v, kc2, lfc2, vc, batch, seq, ctx_len):
    C = SCAN_CHUNK
    n_lat, n_ctx = seq // C, ctx_len // C
    n_seq = 4 if batch % 4 == 0 else 2 if batch % 2 == 0 else 1
    lmat, lvl = _scan_tables()
    q, v = (a.reshape(batch, seq, HG_W) for a in (q, v))
    k2, lf2 = (a.reshape(2, batch, seq, HG_W) for a in (k2, lf2))
    vc = vc.reshape(batch, ctx_len, HG_W)
    kc2, lfc2 = (a.reshape(2, batch, ctx_len, HG_W) for a in (kc2, lfc2))

    def lat_blk(d):
        def blk(s):
            j = jnp.maximum(s - n_ctx, 0)
            return j if d == 0 else n_lat - 1 - j
        return blk

    def ctx_blk(d):
        def blk(s):
            i = jnp.minimum(s, n_ctx - 1)
            return i if d == 0 else n_ctx - 1 - i
        return blk

    def plain(blk):
        return pl.BlockSpec((n_seq, C, HG_W), lambda b, s: (b, blk(s), 0))

    def specs(blk_of, with_q):
        out = []
        for d in range(2):
            blk = blk_of(d)
            per_dir = pl.BlockSpec((None, n_seq, C, HG_W), lambda b, s, blk=blk, d=d: (d, b, blk(s), 0))
            out += ([plain(blk)] if with_q else []) + [per_dir, per_dir, plain(blk)]
        return out

    full = lambda a: pl.BlockSpec(a.shape, lambda b, s: (0,) * a.ndim)
    o_shape = jax.ShapeDtypeStruct((batch, seq, HG_W), BF16)
    n_chains = 2 * n_seq
    o_f, o_b = pl.pallas_call(
        functools.partial(_scan_kernel, n_ctx),
        grid=(batch // n_seq, n_ctx + n_lat),
        in_specs=[full(lmat), full(lvl)] + specs(lat_blk, True) + specs(ctx_blk, False),
        out_specs=[plain(lat_blk(d)) for d in range(2)],
        out_shape=[o_shape, o_shape],
        scratch_shapes=[pltpu.VMEM((n_chains, HG_DK, HG_W), F32), pltpu.VMEM((n_chains, C, HG_W), F32)],
        compiler_params=_params(("parallel", "arbitrary")),
        name="hgrn_scan",
    )(lmat, lvl, q, k2, lf2, v, q, k2, lf2, v, kc2, lfc2, vc, kc2, lfc2, vc)
    return o_f.reshape(batch * seq, HG_W), o_b.reshape(batch * seq, HG_W)


def _merge_kernel(of_ref, ob_ref, sg_ref, u_ref, vn_ref, sga_ref, sgb_ref, x_ref, gt1_ref, sh2_ref, sc2_ref,
                  gout_ref, ws_ref, bs_ref, wa_ref, wb_ref, wo_ref, gpost_ref, gffn_ref, wr_ref, br_ref,
                  x1_ref, h2_ref, cw_ref):
    tm = x_ref.shape[0]
    gout = gout_ref[...]
    gw = CM_W // CM_GROUPS
    group = min(MERGE_ROWS, tm)
    groups = [slice(r0, r0 + group) for r0 in range(0, tm, group)]

    def branch_inputs(rows):
        o = of_ref[rows, :].astype(F32) + ob_ref[rows, :].astype(F32)
        sg = sg_ref[rows, :].astype(F32)
        a = jnp.concatenate(
            [_rms(o[:, h * HG_DK:(h + 1) * HG_DK], gout) * sg[:, h * HG_DK:(h + 1) * HG_DK]
             for h in range(HG_HEADS)], axis=1).astype(BF16)
        vn = vn_ref[rows, :]
        z = jnp.concatenate(
            [jnp.concatenate([_dot(ws_ref[g], vn[c * CM_CHUNK:(c + 1) * CM_CHUNK, g * gw:(g + 1) * gw])
                              for g in range(CM_GROUPS)], axis=1) + bs_ref[...]
             for c in range(group // CM_CHUNK)], axis=0)
        return a, (u_ref[rows, :].astype(F32) * z).astype(BF16)

    ab = [branch_inputs(rows) for rows in groups]
    ys = [(sga_ref[rows, :].astype(F32) * _dot(a, wa_ref[...])
           + sgb_ref[rows, :].astype(F32) * _dot(bm, wb_ref[...])).astype(BF16)
          for rows, (a, bm) in zip(groups, ab)]
    yos = [_dot(y, wo_ref[...]) for y in ys]
    for rows, yo in zip(groups, yos):
        x1 = x_ref[rows, :] + gt1_ref[...] * _rms(yo, gpost_ref[...])
        x1_ref[rows, :] = x1
        h2 = (_rms(x1, gffn_ref[...]) * (1.0 + sc2_ref[...]) + sh2_ref[...]).astype(BF16)
        h2_ref[rows, :] = h2
        cw_ref[rows, :] = _route(_dot_nt(wr_ref[...], h2), br_ref[...])


def _merge(o_f, o_b, sg, u, vn, sga, sgb, x2, mod3, g_out, ws_b, bs_full, wa_b, wb_b, wo_b, g_post, g_ffn, wr_b,
           br_cols, seq, tm):
    n = x2.shape[0]
    tpb = seq // tm
    row = lambda w: pl.BlockSpec((tm, w), lambda i: (i, 0))
    full = lambda a: pl.BlockSpec(a.shape, lambda i: (0,) * a.ndim)
    return pl.pallas_call(
        _merge_kernel,
        grid=(n // tm,),
        in_specs=[row(HG_W), row(HG_W), row(HG_W), row(CM_W), row(CM_W),
                  row(D_MODEL), row(D_MODEL), row(D_MODEL), _mod_spec(tpb, 2), _mod_spec(tpb, 3),
                  _mod_spec(tpb, 4), full(g_out), full(ws_b), full(bs_full), full(wa_b), full(wb_b),
                  full(wo_b), full(g_post), full(g_ffn), full(wr_b), full(br_cols)],
        out_specs=[row(D_MODEL), row(D_MODEL), row(wr_b.shape[0])],
        out_shape=[jax.ShapeDtypeStruct((n, D_MODEL), F32), jax.ShapeDtypeStruct((n, D_MODEL), BF16),
                   jax.ShapeDtypeStruct((n, wr_b.shape[0]), F32)],
        compiler_params=_params(("parallel",)),
        name="merge",
    )(o_f, o_b, sg, u, vn, sga, sgb, x2, mod3, mod3, mod3, g_out, ws_b, bs_full, wa_b, wb_b, wo_b, g_post, g_ffn,
      wr_b, br_cols)


def _route(logits, br):
    tm = logits.shape[1]
    gsz = N_EXPERTS // N_GROUPS
    scores = _sigmoid(logits[:N_EXPERTS, :])
    sel = scores + jnp.concatenate([br] * (tm // br.shape[1]), axis=1)
    neg = -jnp.inf

    def first_max(x, ids, sentinel, axis):
        m = jnp.max(x, axis=axis, keepdims=True)
        return m, jnp.min(jnp.where(x == m, ids, sentinel), axis=axis, keepdims=True)

    sel3 = sel.reshape(N_GROUPS, gsz, tm)
    j3 = lax.broadcasted_iota(jnp.int32, sel3.shape, 1)
    m1, i1 = first_max(sel3, j3, gsz, 1)
    gscore = m1 + jnp.max(jnp.where(j3 == i1, neg, sel3), axis=1, keepdims=True)
    g3 = lax.broadcasted_iota(jnp.int32, gscore.shape, 0)
    keep = jnp.zeros(gscore.shape, F32)
    for _ in range(TOPK_GROUPS):
        _, gi = first_max(gscore, g3, N_GROUPS, 0)
        keep = jnp.where(g3 == gi, 1.0, keep)
        gscore = jnp.where(g3 == gi, neg, gscore)
    x = jnp.where(keep > 0.0, sel3, neg).reshape(N_EXPERTS, tm)
    e_i = lax.broadcasted_iota(jnp.int32, x.shape, 0)
    w = jnp.zeros(x.shape, F32)
    for _ in range(TOP_K):
        _, ei = first_max(x, e_i, N_EXPERTS, 0)
        w = jnp.where(e_i == ei, scores, w)
        x = jnp.where(e_i == ei, neg, x)
    w = w / jnp.sum(w, axis=0, keepdims=True) * ROUTED_SCALE
    return jnp.concatenate([w, jnp.zeros_like(w)], axis=0).T


MOE_TILE = 256
MOE_UNIT = BF16_SUBLANES
MOE_BLOCK = 512
GATHER_SLOTS = 3
EXPERT_GRID_STEP = 24
UNITS_PER_BLOCK = MOE_BLOCK // MOE_UNIT
ROW_CHUNK = 512
TILE_ROWS = -(-(MOE_TILE * TOP_K + N_EXPERTS * (MOE_UNIT - 1)) // ROW_CHUNK) * ROW_CHUNK
TILE_UNITS = TILE_ROWS // MOE_UNIT
N_CHUNKS = TILE_ROWS // ROW_CHUNK
FULL_CHUNKS = MOE_TILE * TOP_K // ROW_CHUNK
EAGER_CHUNKS = min(FULL_CHUNKS + 1, N_CHUNKS)
CHUNK_UNITS = ROW_CHUNK // MOE_UNIT
SORT_ROWS = 1024
KEY_W = 128
DIGIT_BITS = 6
DIGIT = 1 << DIGIT_BITS


def _swiglu_act(h, w_gu):
    gu = _dot(h, w_gu)
    de = gu.shape[1] // 2
    g = gu[:, :de]
    return g * _sigmoid(g) * gu[:, de:]


def _token_keys(cw, starts_row):
    t = cw.shape[0]
    routed = cw > 0.0
    t_i = lax.broadcasted_iota(jnp.int32, (t, t), 0)
    s_i = lax.broadcasted_iota(jnp.int32, (t, t), 1)
    rank = _dot((s_i < t_i).astype(BF16), routed.astype(BF16))
    pos = (starts_row + rank).astype(jnp.int32)
    lane = lax.broadcasted_iota(jnp.int32, cw.shape, 1)
    hi = jnp.where(routed, jnp.right_shift(pos, DIGIT_BITS), -1)
    lo = jnp.where(routed, jnp.bitwise_and(pos, DIGIT - 1), -1)
    key_hi = jnp.where(lane < N_EXPERTS, hi * DIGIT,
                       jnp.where(lane == N_EXPERTS, -DIGIT, jnp.where(lane == N_EXPERTS + 1, -1, 0)))
    key_lo = jnp.where(lane < N_EXPERTS, lo, 0)
    return jnp.concatenate([key_hi, key_lo], axis=1).astype(F32).astype(BF16)


def _segment_units(counts):
    return jnp.floor((counts + (MOE_UNIT - 1)) * (1.0 / MOE_UNIT))


def _dispatch_kernel(h_ref, cw_ref, digits_ref, xs_ref, cnt_ref):
    cw = cw_ref[...]
    t = cw.shape[0]
    routed = (cw > 0.0).astype(BF16)
    counts = _dot(jnp.ones((SUBLANES, t), BF16), routed)
    cnt_ref[...] = counts.astype(jnp.int32)
    units = _segment_units(counts)
    e_i = lax.broadcasted_iota(jnp.int32, (KEY_W, KEY_W), 0)
    f_i = lax.broadcasted_iota(jnp.int32, (KEY_W, KEY_W), 1)
    starts = _dot(units.astype(BF16), (e_i < f_i).astype(BF16)) * MOE_UNIT
    ends = starts + units * MOE_UNIT
    keys_t = _token_keys(cw, starts[:1]).astype(F32).T.astype(BF16)
    h = h_ref[...]
    used_rows = jnp.max(ends)

    def row_hits(r0, n):
        rows = slice(r0, r0 + n)
        lane = lax.broadcasted_iota(jnp.int32, (n, KEY_W), 1)
        r = (lax.broadcasted_iota(jnp.int32, (n, KEY_W), 0) + r0).astype(F32)
        in_seg = (r >= starts[:1]) & (r < ends[:1])
        rmap = jnp.where(lane < N_EXPERTS, in_seg.astype(F32), digits_ref[rows, :].astype(F32)).astype(BF16)
        return (_dot(jnp.concatenate([rmap, rmap], axis=1), keys_t) == 0.0).astype(BF16)

    def sort_rows(r0, n):
        xs_ref[r0:r0 + n, :] = _dot(row_hits(r0, n), h).astype(xs_ref.dtype)

    eager_rows = EAGER_CHUNKS * ROW_CHUNK
    pieces = [(r0, min(SORT_ROWS, eager_rows - r0)) for r0 in range(0, eager_rows, SORT_ROWS)]
    hits = [row_hits(r0, n) for r0, n in pieces]
    for (r0, n), hit in zip(pieces, hits):
        xs_ref[r0:r0 + n, :] = _dot(hit, h).astype(xs_ref.dtype)
    for c in range(EAGER_CHUNKS, N_CHUNKS):
        pl.when(used_rows > c * ROW_CHUNK)(functools.partial(sort_rows, c * ROW_CHUNK, ROW_CHUNK))

        @pl.when(used_rows <= c * ROW_CHUNK)
        def _(c=c):
            xs_ref[c * ROW_CHUNK:(c + 1) * ROW_CHUNK, :] = jnp.zeros((ROW_CHUNK, D_MODEL), xs_ref.dtype)


def _dispatch(h2, cw, digits):
    n = h2.shape[0]
    n_tiles = n // MOE_TILE
    return pl.pallas_call(
        _dispatch_kernel,
        grid=(n_tiles,),
        in_specs=[pl.BlockSpec((MOE_TILE, D_MODEL), lambda i: (i, 0)),
                  pl.BlockSpec((MOE_TILE, KEY_W), lambda i: (i, 0)),
                  pl.BlockSpec(digits.shape, lambda i: (0, 0))],
        out_specs=[pl.BlockSpec((TILE_ROWS, D_MODEL), lambda i: (i, 0)),
                   pl.BlockSpec((SUBLANES, KEY_W), lambda i: (i, 0))],
        out_shape=[jax.ShapeDtypeStruct((n_tiles * TILE_ROWS, D_MODEL), BF16),
                   jax.ShapeDtypeStruct((n_tiles * SUBLANES, KEY_W), jnp.int32)],
        compiler_params=_params(("parallel",)),
        name="moe_dispatch",
    )(h2, cw, digits)


def _unit_copy(src_hbm, unit, dst, slot, pos, sem):
    return pltpu.make_async_copy(
        src_hbm.at[pl.ds(pl.multiple_of(unit * MOE_UNIT, MOE_UNIT), MOE_UNIT)],
        dst.at[slot, pl.ds(pos * MOE_UNIT, MOE_UNIT)], sem.at[slot])


def _experts_kernel(be_ref, src_ref, nb_ref, wplan_ref, xs_hbm, wgu_hbm, wdn_hbm, ys_ref, xbuf, sem,
                    wgu_f, wdn_f, wsem, wgu_b, wdn_b):
    j = pl.program_id(0)
    nb = nb_ref[0]
    n_steps = pl.num_programs(0)

    def copies(blk, slot):
        return [_unit_copy(xs_hbm, src_ref[blk * UNITS_PER_BLOCK + u], xbuf, slot, u, sem)
                for u in range(UNITS_PER_BLOCK)]

    def fetch(blk, slot):
        for cp in copies(blk, slot):
            cp.start()

    def weight_copies(expert, slot):
        return [pltpu.make_async_copy(wgu_hbm.at[expert], wgu_f.at[slot], wsem.at[slot]),
                pltpu.make_async_copy(wdn_hbm.at[expert], wdn_f.at[slot], wsem.at[slot])]

    ahead = GATHER_SLOTS - 1

    @pl.when(j == 0)
    def _():
        for a in range(ahead):
            fetch(jnp.minimum(a, nb - 1), a)
        for cp in weight_copies(be_ref[0], 0):
            cp.start()

    @pl.when((j < nb) & (wplan_ref[j] == 1))
    def _():
        slot = wplan_ref[n_steps + j]
        for cp in weight_copies(be_ref[j], slot):
            cp.wait()
        nxt = wplan_ref[2 * n_steps + j]

        @pl.when(nxt >= 0)
        def _():
            for cp in weight_copies(nxt, 1 - slot):
                cp.start()

        wgu_b[...] = wgu_f[slot].astype(BF16)
        wdn_b[...] = wdn_f[slot].astype(BF16)

    @pl.when(j < nb)
    def _():
        slot = j % GATHER_SLOTS
        for cp in copies(j, slot):
            cp.wait()
        fetch(jnp.minimum(j + ahead, nb - 1), (j + ahead) % GATHER_SLOTS)

    def ffn(slot, n_rows):
        act = _swiglu_act(xbuf[slot, :n_rows, :], wgu_b[...])
        ys_ref[:n_rows, :] = _dot(act.astype(BF16), wdn_b[...]).astype(ys_ref.dtype)

    half_full = wplan_ref[3 * n_steps + j] <= UNITS_PER_BLOCK // 2

    @pl.when((j < nb) & jnp.logical_not(half_full))
    def _():
        ffn(j % GATHER_SLOTS, MOE_BLOCK)

    @pl.when((j < nb) & half_full)
    def _():
        ffn(j % GATHER_SLOTS, MOE_BLOCK // 2)
        ys_ref[MOE_BLOCK // 2:, :] = jnp.zeros((MOE_BLOCK // 2, D_MODEL), ys_ref.dtype)

    @pl.when(j == nb - 1)
    def _():
        for a in range(1, GATHER_SLOTS):
            for cp in copies(j, (j + a) % GATHER_SLOTS):
                cp.wait()

    @pl.when(j >= nb)
    def _():
        ys_ref[...] = jnp.zeros_like(ys_ref)


def _weight_plan(block_expert, block_units, n_blocks_used):
    nb = block_expert.shape[0]
    jb = jnp.arange(nb, dtype=jnp.int32)
    used = jb < n_blocks_used[0]
    first = used & ((jb == 0) | (block_expert != jnp.roll(block_expert, 1)))
    slot = (jnp.cumsum(first.astype(jnp.int32)) - 1) % 2
    first_at = jnp.where(first, jb, nb)
    nxt_first = jnp.min(jnp.where(first_at[None, :] > jb[:, None], first_at[None, :], nb), axis=1)
    nxt_expert = jnp.sum(jnp.where(jb[None, :] == nxt_first[:, None], block_expert[None, :], 0), axis=1)
    nxt = jnp.where(nxt_first < nb, nxt_expert, -1)
    return jnp.concatenate([first.astype(jnp.int32), slot.astype(jnp.int32), nxt.astype(jnp.int32),
                            block_units.astype(jnp.int32)])


def _experts(xs, block_expert, src_units, block_units, n_blocks_used, w_gu, w_dn):
    nb_max = block_expert.shape[0]
    any_spec = pl.BlockSpec(memory_space=pl.ANY)
    grid_spec = pltpu.PrefetchScalarGridSpec(
        num_scalar_prefetch=4,
        grid=(nb_max,),
        in_specs=[any_spec, any_spec, any_spec],
        out_specs=pl.BlockSpec((MOE_BLOCK, D_MODEL), lambda j, be, src, nb, wplan: (j, 0)),
        scratch_shapes=[pltpu.VMEM((GATHER_SLOTS, MOE_BLOCK, D_MODEL), BF16),
                        pltpu.SemaphoreType.DMA((GATHER_SLOTS,)),
                        pltpu.VMEM((2, D_MODEL, 2 * D_EXPERT), F32), pltpu.VMEM((2, D_EXPERT, D_MODEL), F32),
                        pltpu.SemaphoreType.DMA((2,)),
                        pltpu.VMEM((D_MODEL, 2 * D_EXPERT), BF16), pltpu.VMEM((D_EXPERT, D_MODEL), BF16)],
    )
    return pl.pallas_call(
        _experts_kernel,
        grid_spec=grid_spec,
        out_shape=jax.ShapeDtypeStruct((nb_max * MOE_BLOCK, D_MODEL), BF16),
        compiler_params=_params(("arbitrary",)),
        name="moe_experts",
    )(block_expert, src_units, n_blocks_used, _weight_plan(block_expert, block_units, n_blocks_used), xs, w_gu,
      w_dn)


def _combine_kernel(src_ref, used_ref, ys_hbm, cw_ref, h_ref, x1_ref, gt2_ref, gpost_ref, digits_t_ref,
                    wsgu_ref, wsdn_ref, o_ref, ybuf, sem, acc_ref):
    i = pl.program_id(0)

    def copies(tile, slot, c):
        return [_unit_copy(ys_hbm, src_ref[tile * TILE_UNITS + u], ybuf, slot, u, sem)
                for u in range(c * CHUNK_UNITS, (c + 1) * CHUNK_UNITS)]

    def chunk_used(tile, c):
        return used_ref[tile] > c * CHUNK_UNITS

    def for_used_chunks(tile, fn):
        for c in range(N_CHUNKS):
            if c < EAGER_CHUNKS:
                fn(c)
            else:
                pl.when(chunk_used(tile, c))(functools.partial(fn, c))

    def fetch(tile, slot):
        def start(c):
            for cp in copies(tile, slot, c):
                cp.start()

        for_used_chunks(tile, start)

    def wait_all(tile, slot):
        def wait(c):
            for cp in copies(tile, slot, c):
                cp.wait()

        for_used_chunks(tile, wait)

    @pl.when(i == 0)
    def _():
        fetch(0, 0)

    @pl.when(i + 1 < pl.num_programs(0))
    def _():
        fetch(i + 1, (i + 1) % 2)

    cw = cw_ref[...]
    t = cw.shape[0]
    routed = (cw > 0.0).astype(BF16)
    e_i = lax.broadcasted_iota(jnp.int32, (KEY_W, KEY_W), 0)
    f_i = lax.broadcasted_iota(jnp.int32, (KEY_W, KEY_W), 1)
    units = _segment_units(_dot_tn(routed, jnp.ones((t, KEY_W), BF16)))
    starts = _dot((f_i < e_i).astype(BF16), units.astype(BF16)) * MOE_UNIT
    ends = starts + units * MOE_UNIT
    units_row = _segment_units(_dot(jnp.ones((SUBLANES, t), BF16), routed))
    starts_row = _dot(units_row.astype(BF16), (e_i < f_i).astype(BF16)) * MOE_UNIT
    keys = _token_keys(cw, starts_row[:1])
    wb = cw.astype(BF16)

    f = _dot(_swiglu_act(h_ref[...], wsgu_ref[...]).astype(BF16), wsdn_ref[...])
    slot = i % 2
    reps = ROW_CHUNK // KEY_W
    starts_c = jnp.concatenate([starts] * reps, axis=1)
    ends_c = jnp.concatenate([ends] * reps, axis=1)
    sub = lax.broadcasted_iota(jnp.int32, (KEY_W, ROW_CHUNK), 0)

    wait_all(i, slot)

    def chunk_weights(c):
        rows = slice(c * ROW_CHUNK, (c + 1) * ROW_CHUNK)
        r = (lax.broadcasted_iota(jnp.int32, (KEY_W, ROW_CHUNK), 1) + c * ROW_CHUNK).astype(F32)
        in_seg = (r >= starts_c) & (r < ends_c)
        rmap_t = jnp.where(sub < N_EXPERTS, in_seg.astype(F32), digits_t_ref[:, rows].astype(F32)).astype(BF16)
        hit = _dot(keys, jnp.concatenate([rmap_t, rmap_t], axis=0)) == 0.0
        return jnp.where(hit, _dot(wb, rmap_t), 0.0).astype(BF16)

    def chunk_sum(c):
        return _dot(chunk_weights(c), ybuf[slot, c * ROW_CHUNK:(c + 1) * ROW_CHUNK, :])

    pw = jnp.concatenate([chunk_weights(c) for c in range(EAGER_CHUNKS)], axis=1)
    acc_ref[...] = f + _dot(pw, ybuf[slot, :EAGER_CHUNKS * ROW_CHUNK, :])
    for c in range(EAGER_CHUNKS, N_CHUNKS):
        @pl.when(chunk_used(i, c))
        def _(c=c):
            acc_ref[...] += chunk_sum(c)
    o_ref[...] = x1_ref[...] + gt2_ref[...] * _rms(acc_ref[...], gpost_ref[...])


def _combine(ys, src_units, used_units, cw, h2, x1, mod3, g_post, digits_t, wsgu_b, wsdn_b, seq):
    n = h2.shape[0]
    tpb = seq // MOE_TILE
    row = lambda w: pl.BlockSpec((MOE_TILE, w), lambda i, src, used: (i, 0))
    full = lambda a: pl.BlockSpec(a.shape, lambda i, src, used: (0,) * a.ndim)
    grid_spec = pltpu.PrefetchScalarGridSpec(
        num_scalar_prefetch=2,
        grid=(n // MOE_TILE,),
        in_specs=[pl.BlockSpec(memory_space=pl.ANY), row(KEY_W), row(D_MODEL), row(D_MODEL),
                  pl.BlockSpec((None, 1, D_MODEL), lambda i, src, used: (i // tpb, 0, 5)), full(g_post),
                  full(digits_t), full(wsgu_b), full(wsdn_b)],
        out_specs=row(D_MODEL),
        scratch_shapes=[pltpu.VMEM((2, TILE_ROWS, D_MODEL), BF16), pltpu.SemaphoreType.DMA((2,)),
                        pltpu.VMEM((MOE_TILE, D_MODEL), F32)],
    )
    return pl.pallas_call(
        _combine_kernel,
        grid_spec=grid_spec,
        out_shape=jax.ShapeDtypeStruct((n, D_MODEL), F32),
        compiler_params=_params(("arbitrary",)),
        name="moe_combine",
    )(src_units, used_units, ys, cw, h2, x1, mod3, g_post, digits_t, wsgu_b, wsdn_b)


def _row_digits():
    r = np.arange(TILE_ROWS)
    d = np.zeros((TILE_ROWS, KEY_W), np.float32)
    d[:, N_EXPERTS] = r // DIGIT
    d[:, N_EXPERTS + 1] = r % DIGIT
    return jnp.asarray(d, dtype=BF16)


def _moe_plan(counts, nb_max):
    n_tiles = counts.shape[0]
    s = (counts + (MOE_UNIT - 1)) // MOE_UNIT
    local = jnp.cumsum(s, axis=1) - s
    cs = jnp.cumsum(s, axis=0)
    per_expert = cs[-1]
    padded = (per_expert + UNITS_PER_BLOCK - 1) // UNITS_PER_BLOCK * UNITS_PER_BLOCK
    g_end = jnp.cumsum(padded)
    g_start = g_end - padded
    seg_start = g_start[None, :] + cs - s
    n_blocks_used = (g_end[-1] // UNITS_PER_BLOCK).astype(jnp.int32).reshape(1)
    jb = jnp.arange(nb_max, dtype=jnp.int32)
    one_e = ((jb[:, None] >= (g_start // UNITS_PER_BLOCK)[None, :])
             & (jb[:, None] < (g_end // UNITS_PER_BLOCK)[None, :])).astype(jnp.int32)
    pick_e = lambda table: jnp.sum(one_e[:, :, None] * table.T[None, :, :], axis=1)
    block_expert = jnp.where(jb < n_blocks_used[0], jnp.sum(one_e * jnp.arange(N_EXPERTS, dtype=jnp.int32), axis=1),
                             N_EXPERTS - 1).astype(jnp.int32)
    cs_b, s_b, local_b = pick_e(cs), pick_e(s), pick_e(local)
    q = (jb * UNITS_PER_BLOCK - jnp.sum(one_e * g_start[None, :], axis=1))[:, None] \
        + jnp.arange(UNITS_PER_BLOCK, dtype=jnp.int32)[None, :]
    tile = jnp.minimum(jnp.sum(cs_b[:, None, :] <= q[:, :, None], axis=2), n_tiles - 1)
    one_t = (tile[:, :, None] == jnp.arange(n_tiles, dtype=jnp.int32)).astype(jnp.int32)
    src = tile * TILE_UNITS + q + jnp.sum(one_t * (local_b - cs_b + s_b)[:, None, :], axis=2)
    valid = q < jnp.sum(one_e * per_expert[None, :], axis=1)[:, None]
    src_units = jnp.where(valid, src, 0).astype(jnp.int32).reshape(-1)
    block_units = jnp.sum(valid, axis=1)
    u = jnp.arange(TILE_UNITS, dtype=jnp.int32)
    seg_end = local + s
    eu = jnp.minimum(jnp.sum(seg_end[:, None, :] <= u[None, :, None], axis=2), N_EXPERTS - 1)
    one_u = (eu[:, :, None] == jnp.arange(N_EXPERTS, dtype=jnp.int32)).astype(jnp.int32)
    back = u[None, :] + jnp.sum(one_u * (seg_start - local)[:, None, :], axis=2)
    back_units = jnp.where(u[None, :] < seg_end[:, -1:], back, 0).astype(jnp.int32).reshape(-1)
    return block_expert, src_units, block_units, n_blocks_used, back_units, seg_end[:, -1].astype(jnp.int32)


def _tile(n, pref):
    t = pref
    while n % t:
        t //= 2
    return t


def kernel(x, c, ctx, c_ctx, w_ada, b_ada, g_pre_mix, g_post_mix, g_pre_ffn, g_post_ffn, w_in, lb_logits, g_hgrn_out, cm_ln_g, cm_ln_b, w_spatial, b_spatial, w_branch_a, w_branch_b, w_out, w_router, b_router, w_expert_gu, w_expert_down, w_shared_gu, w_shared_down):
    B, T, D = x.shape
    L = ctx.shape[1]
    assert D == D_MODEL and w_ada.shape[0] == 1 and T % SCAN_CHUNK == 0 and L % SCAN_CHUNK == 0
    assert T % MOE_TILE == 0 and MOE_TILE % CM_CHUNK == 0
    l = 0
    row = lambda a: a[l].reshape(1, -1)

    n_rows = -(-(B + 1) // BF16_SUBLANES) * BF16_SUBLANES
    cs = jnp.zeros((n_rows, D), F32).at[:B].set(c).at[B].set(c_ctx)
    mod3 = _ada_mod(cs, w_ada[l], row(b_ada)).reshape(n_rows, 1, 6 * D)

    w_in_b = w_in[l].astype(BF16)
    lbl = lb_logits[:, l:l + 2].reshape(4, HG_W)
    x2 = x.reshape(B * T, D)
    q, k2, lf2, v, sg, u, vn, sga, sgb = _proj_lat(
        x2, mod3, row(g_pre_mix), w_in_b, lbl, row(cm_ln_g), row(cm_ln_b), T, _tile(T, PROJ_TILE))
    kc2, lfc2, vc = _proj_ctx(ctx.reshape(B * L, D), mod3, B, row(g_pre_mix), w_in_b[:, HG_W:4 * HG_W], lbl,
                              _tile(B * L, CTX_TILE))

    o_f, o_b = _hgrn_scan(q, k2, lf2, v, kc2, lfc2, vc, B, T, L)

    bs_full = jnp.repeat(b_spatial[l], CM_W // CM_GROUPS, axis=1)
    x1, h2, cw = _merge(
        o_f, o_b, sg, u, vn, sga, sgb, x2, mod3, row(g_hgrn_out), w_spatial[l].astype(BF16), bs_full,
        w_branch_a[l].astype(BF16), w_branch_b[l].astype(BF16), w_out[l].astype(BF16), row(g_post_mix),
        row(g_pre_ffn), jnp.pad(w_router[l].T, ((0, KEY_W - N_EXPERTS), (0, 0))).astype(BF16),
        jnp.broadcast_to(b_router[l][:, None], (N_EXPERTS, LANES)), T, _tile(T, MERGE_TILE))

    n_tok = B * T
    n_tiles = n_tok // MOE_TILE
    digits = _row_digits()
    xs, cnt = _dispatch(h2, cw, digits)
    counts = cnt.reshape(n_tiles, SUBLANES, KEY_W)[:, 0, :N_EXPERTS]
    max_units = (n_tok * TOP_K + n_tiles * N_EXPERTS * (MOE_UNIT - 1)) // MOE_UNIT + N_EXPERTS * (UNITS_PER_BLOCK - 1)
    nb_max = -(-max_units // UNITS_PER_BLOCK)
    block_expert, src_units, block_units, n_blocks_used, back_units, tile_units = _moe_plan(counts, nb_max)
    wsgu_b, wsdn_b = w_shared_gu[l].astype(BF16), w_shared_down[l].astype(BF16)

    nb_min = -(-(n_tok * TOP_K) // MOE_BLOCK)
    grids = sorted({min(nb_max, g) for g in range(nb_min + EXPERT_GRID_STEP, nb_max + EXPERT_GRID_STEP,
                                                  EXPERT_GRID_STEP)})

    def experts_and_combine(nb_grid):
        def run():
            ys = _experts(xs, block_expert[:nb_grid], src_units[:nb_grid * UNITS_PER_BLOCK], block_units[:nb_grid],
                          n_blocks_used, w_expert_gu[l], w_expert_down[l])
            return _combine(ys, back_units, tile_units, cw, h2, x1, mod3, row(g_post_ffn), digits.T,
                            wsgu_b, wsdn_b, T)
        return run

    which = jnp.sum(n_blocks_used[0] > jnp.asarray(grids[:-1], jnp.int32))
    out = lax.switch(which, [experts_and_combine(g) for g in grids])
    return out.reshape(B, T, D)
```
